```python
import jax, jax.numpy as jnp
from jax import lax
import numpy as np

D_MODEL = 2048
BATCH = 16
SEQ = 2048
DEPTH = 1
DEC_BATCH = 8
DEC_SEQ = 16
PAST_LEN = 2048

CHUNK = 64
SWA_WIDTH = D_MODEL // 2
GLA_WIDTH = D_MODEL - SWA_WIDTH
SWA_HEAD_DIM = 64
SWA_HEADS = SWA_WIDTH // SWA_HEAD_DIM
SWA_KV_HEADS = max(1, SWA_HEADS // 8)
SWA_GROUP = SWA_HEADS // SWA_KV_HEADS
WINDOW = 128
WINDOW_CHUNKS = WINDOW // CHUNK
BAND = (WINDOW_CHUNKS + 1) * CHUNK
GLA_HEADS = 4
GLA_DV = GLA_WIDTH // GLA_HEADS
GLA_DK = GLA_DV // 2
GLA_GATE_RANK = 16
GLA_GATE_NORM = 16.0
GLA_BLOCK = 16
N_GROUPS = 4
EXPERTS_PER_GROUP = 8
N_EXPERTS = N_GROUPS * EXPERTS_PER_GROUP
TOP_K = 2
EXPERT_HIDDEN = D_MODEL // 4
MOE_BLOCK = 128
N_MOD = 6
EPS = 1e-6
NEG_INF = -1e30

IN_SIZES = (SWA_HEADS * SWA_HEAD_DIM, SWA_KV_HEADS * SWA_HEAD_DIM, SWA_KV_HEADS * SWA_HEAD_DIM,
            GLA_HEADS * GLA_DK, GLA_HEADS * GLA_DK, GLA_HEADS * GLA_DV, GLA_HEADS * GLA_DV,
            GLA_GATE_RANK)
D_IN = sum(IN_SIZES)
SPLIT_POINTS = tuple(int(s) for s in np.cumsum(IN_SIZES)[:-1])

kernel_name = 'hymba_swa_gla_hmoe_stream_step'


def rmsnorm(x, g):
    xf = x.astype(jnp.float32)
    y = xf * lax.rsqrt(jnp.mean(xf * xf, axis=-1, keepdims=True) + EPS)
    return y.astype(x.dtype) * g


def adaln(c, w_ada, b_ada):
    mod = jax.nn.silu(c) @ w_ada + b_ada
    return jnp.split(mod[:, None, :], N_MOD, axis=-1)


def modulate(h, shift, scale):
    return h * (1 + scale) + shift


def project_inputs(h, w_in, w_gate_up, b_gate):
    B, T, _ = h.shape
    qa, ka, va, qb, kb, vb, rb, ab = jnp.split(h @ w_in, SPLIT_POINTS, axis=-1)
    qa = qa.reshape(B, T, SWA_KV_HEADS, SWA_GROUP, SWA_HEAD_DIM)
    ka = ka.reshape(B, T, SWA_KV_HEADS, SWA_HEAD_DIM)
    va = va.reshape(B, T, SWA_KV_HEADS, SWA_HEAD_DIM)
    qb = qb.reshape(B, T, GLA_HEADS, GLA_DK) * (GLA_DK ** -0.5)
    kb = kb.reshape(B, T, GLA_HEADS, GLA_DK)
    vb = vb.reshape(B, T, GLA_HEADS, GLA_DV)
    lg = jax.nn.log_sigmoid((ab @ w_gate_up + b_gate).astype(jnp.float32)) / GLA_GATE_NORM
    lg = lg.reshape(B, T, GLA_HEADS, GLA_DK)
    return qa, ka, va, qb, kb, vb, rb, lg


def softmax_with_sink(s, sink):
    sink = jnp.broadcast_to(sink, s.shape[:-1] + (1,))
    return jax.nn.softmax(jnp.concatenate([s, sink], axis=-1), axis=-1)[..., :-1]


def swa_prompt(q, k, v, sinks):
    B, T = k.shape[:2]
    nc = T // CHUNK

    def band(a):
        pad = jnp.zeros((B, WINDOW) + a.shape[2:], a.dtype)
        ac = jnp.concatenate([pad, a], axis=1).reshape(B, nc + WINDOW_CHUNKS, CHUNK, *a.shape[2:])
        return jnp.concatenate([ac[:, i:i + nc] for i in range(WINDOW_CHUNKS + 1)], axis=2)

    kband, vband = band(k), band(v)
    qc = q.reshape(B, nc, CHUNK, SWA_KV_HEADS, SWA_GROUP, SWA_HEAD_DIM)
    s = jnp.einsum('bncjgd,bnmjd->bnjgcm', qc, kband).astype(jnp.float32) * (SWA_HEAD_DIM ** -0.5)
    key_chunk = jnp.arange(nc)[:, None] - WINDOW_CHUNKS + jnp.arange(BAND)[None, :] // CHUNK
    s = jnp.where((key_chunk >= 0)[None, :, None, None, None, :], s, NEG_INF)
    sink = sinks.astype(jnp.float32).reshape(SWA_KV_HEADS, SWA_GROUP)[None, None, :, :, None, None]
    p = softmax_with_sink(s, sink)
    o = jnp.einsum('bnjgcm,bnmjd->bncjgd', p.astype(v.dtype), vband)
    return o.reshape(B, T, SWA_WIDTH)


def swa_sample(q, k_new, v_new, k_past, v_past, sinks):
    B, T = k_new.shape[:2]
    k = jnp.concatenate([k_past.astype(k_new.dtype), k_new], axis=1)
    v = jnp.concatenate([v_past.astype(v_new.dtype), v_new], axis=1)
    s = jnp.einsum('btjgd,bsjd->bjgts', q, k).astype(jnp.float32) * (SWA_HEAD_DIM ** -0.5)
    sink = sinks.astype(jnp.float32).reshape(SWA_KV_HEADS, SWA_GROUP)[None, :, :, None, None]
    p = softmax_with_sink(s, sink)
    o = jnp.einsum('bjgts,bsjd->btjgd', p.astype(v.dtype), v)
    return o.reshape(B, T, SWA_WIDTH)


def gla_block(s, q, k, v, lg):
    q, k, v = q.astype(jnp.float32), k.astype(jnp.float32), v.astype(jnp.float32)
    L = q.shape[1]
    b = jnp.cumsum(lg, axis=1)
    qd = q * jnp.exp(b)
    kd = k * jnp.exp(-b)
    causal = jnp.tril(jnp.ones((L, L), bool))
    a = jnp.where(causal, jnp.einsum('blhk,bshk->bhls', qd, kd), 0.0)
    o = jnp.einsum('blhk,bhkv->blhv', qd, s) + jnp.einsum('bhls,bshv->blhv', a, v)
    b_last = b[:, -1:]
    s_new = (jnp.exp(b_last[:, 0])[..., None] * s
             + jnp.einsum('bshk,bshv->bhkv', k * jnp.exp(b_last - b), v))
    return s_new, o


def gla_prompt(q, k, v, lg):
    B, T = q.shape[:2]
    nb = T // GLA_BLOCK

    def blocks(a):
        return a.astype(jnp.float32).reshape(B, nb, GLA_BLOCK, *a.shape[2:]).swapaxes(0, 1)

    s0 = jnp.zeros((B, GLA_HEADS, GLA_DK, GLA_DV), jnp.float32)
    s_fin, o = lax.scan(lambda s, blk: gla_block(s, *blk), s0,
                        (blocks(q), blocks(k), blocks(v), blocks(lg)))
    return s_fin, o.swapaxes(0, 1).reshape(B, T, GLA_HEADS, GLA_DV)


def mixer_output(oa, ob, rb, g_head, w_out):
    B, T, _ = oa.shape
    ob = rmsnorm(ob.astype(oa.dtype), g_head).reshape(B, T, GLA_WIDTH) * jax.nn.silu(rb)
    return jnp.concatenate([oa, ob], axis=-1) @ w_out


def expert_dispatch(x2, e_idx, gates, w_eg, w_eu, w_ed):
    N, D = x2.shape
    A = N * TOP_K
    flat_e = e_idx.reshape(-1)
    flat_w = gates.reshape(-1)
    flat_tok = jnp.arange(A, dtype=jnp.int32) // TOP_K
    order = jnp.argsort(flat_e)
    se = flat_e[order]
    counts = jnp.bincount(flat_e, length=N_EXPERTS)
    starts = jnp.cumsum(counts) - counts
    padded = (counts + MOE_BLOCK - 1) // MOE_BLOCK * MOE_BLOCK
    pend = jnp.cumsum(padded)
    pstart = pend - padded
    dest = pstart[se] + jnp.arange(A) - starts[se]
    n_blocks = -(-(A + N_EXPERTS * (MOE_BLOCK - 1)) // MOE_BLOCK)
    n_slots = n_blocks * MOE_BLOCK
    slot_tok = jnp.zeros((n_slots,), jnp.int32).at[dest].set(flat_tok[order])
    slot_w = jnp.zeros((n_slots,), x2.dtype).at[dest].set(flat_w[order].astype(x2.dtype))
    block_e = jnp.minimum(jnp.searchsorted(pend, jnp.arange(n_blocks) * MOE_BLOCK, side='right'),
                          N_EXPERTS - 1)
    xb = x2[slot_tok].reshape(n_blocks, MOE_BLOCK, D)

    def expert_block(args):
        xblk, e = args
        return (jax.nn.silu(xblk @ w_eg[e]) * (xblk @ w_eu[e])) @ w_ed[e]

    yb = lax.map(expert_block, (xb, block_e))
    return jax.ops.segment_sum(yb.reshape(n_slots, D) * slot_w[:, None], slot_tok, num_segments=N)


def moe_ffn(h, w_rg, b_rg, w_re, b_re, w_eg, w_eu, w_ed):
    B, T, D = h.shape
    x2 = h.reshape(-1, D)
    N = x2.shape[0]
    lg_group = (x2 @ w_rg).astype(jnp.float32) + b_rg
    p_group = jax.nn.softmax(lg_group, axis=-1)
    _, g_sel = lax.top_k(lg_group, 1)
    p_sel = jnp.take_along_axis(p_group, g_sel, axis=-1)
    lg_exp = ((x2 @ w_re).astype(jnp.float32) + b_re).reshape(N, N_GROUPS, EXPERTS_PER_GROUP)
    lg_in = lg_exp[jnp.arange(N), g_sel[:, 0]]
    top_v, top_i = lax.top_k(lg_in, TOP_K)
    gates = jax.nn.softmax(top_v, axis=-1) * p_sel
    e_idx = (g_sel * EXPERTS_PER_GROUP + top_i).astype(jnp.int32)
    y = expert_dispatch(x2, e_idx, gates, w_eg, w_eu, w_ed)
    return y.reshape(B, T, D)


def encoder_layer(x, c, past_k, past_v, past_s, g_mix, g_ffn, w_ada, b_ada, w_in, sinks,
                  w_gate_up, b_gate, g_head, w_out, w_rg, b_rg, w_re, b_re, w_eg, w_eu, w_ed):
    sh1, sc1, gt1, sh2, sc2, gt2 = adaln(c, w_ada, b_ada)
    h = modulate(rmsnorm(x, g_mix), sh1, sc1)
    qa, ka, va, qb, kb, vb, rb, lg = project_inputs(h, w_in, w_gate_up, b_gate)
    if past_k is None:
        oa = swa_prompt(qa, ka, va, sinks)
        s_new, ob = gla_prompt(qb, kb, vb, lg)
        k_state, v_state = ka[:, -WINDOW:], va[:, -WINDOW:]
    else:
        oa = swa_sample(qa, ka, va, past_k, past_v, sinks)
        s_new, ob = gla_block(past_s.astype(jnp.float32), qb, kb, vb, lg)
        k_state, v_state = ka, va
    x = x + gt1 * mixer_output(oa, ob, rb, g_head, w_out)
    h = modulate(rmsnorm(x, g_ffn), sh2, sc2)
    x = x + gt2 * moe_ffn(h, w_rg, b_rg, w_re, b_re, w_eg, w_eu, w_ed)
    return x, k_state, v_state, s_new


def setup_inputs(seed: int = 0) -> dict:
    key = jax.random.key(seed)
    ks = jax.random.split(key, 26)
    D = D_MODEL

    def nrm(k, shape, scale):
        return scale * jax.random.normal(k, shape, jnp.float32)

    return {
        'x_prompt': nrm(ks[0], (BATCH, SEQ, D), 1.0),
        'x_sample': nrm(ks[1], (DEC_BATCH, DEC_SEQ, D), 1.0),
        'cache_swa_k': nrm(ks[2], (DEPTH, DEC_BATCH, min(WINDOW, PAST_LEN), SWA_KV_HEADS, SWA_HEAD_DIM), 1.0),
        'cache_swa_v': nrm(ks[3], (DEPTH, DEC_BATCH, min(WINDOW, PAST_LEN), SWA_KV_HEADS, SWA_HEAD_DIM), 1.0),
        'state_gla': nrm(ks[4], (DEPTH, DEC_BATCH, GLA_HEADS, GLA_DK, GLA_DV), 1.0),
        'c_prompt': nrm(ks[5], (BATCH, D), 1.0),
        'c_sample': nrm(ks[6], (DEC_BATCH, D), 1.0),
        'g_mix_norm': 1.0 + nrm(ks[7], (DEPTH, D), 0.02),
        'g_ffn_norm': 1.0 + nrm(ks[8], (DEPTH, D), 0.02),
        'w_ada': nrm(ks[9], (DEPTH, D, N_MOD * D), 0.3 * D ** -0.5),
        'b_ada': nrm(ks[10], (DEPTH, N_MOD * D), 0.02),
        'w_in': nrm(ks[11], (DEPTH, D, D_IN), D ** -0.5),
        'attn_sinks': nrm(ks[12], (DEPTH, SWA_HEADS), 0.5),
        'w_gla_gate': nrm(ks[13], (DEPTH, GLA_GATE_RANK, GLA_HEADS * GLA_DK), GLA_GATE_RANK ** -0.5),
        'b_gla_gate': nrm(ks[14], (DEPTH, GLA_HEADS * GLA_DK), 0.1),
        'g_gla_norm': 1.0 + nrm(ks[15], (DEPTH, GLA_DV), 0.02),
        'w_out': nrm(ks[16], (DEPTH, D, D), D ** -0.5),
        'w_router_group': nrm(ks[17], (DEPTH, D, N_GROUPS), D ** -0.5),
        'b_router_group': nrm(ks[18], (DEPTH, N_GROUPS), 0.01),
        'w_router_expert': nrm(ks[19], (DEPTH, D, N_EXPERTS), D ** -0.5),
        'b_router_expert': nrm(ks[20], (DEPTH, N_EXPERTS), 0.01),
        'w_expert_gate': nrm(ks[21], (DEPTH, N_EXPERTS, D, EXPERT_HIDDEN), D ** -0.5),
        'w_expert_up': nrm(ks[22], (DEPTH, N_EXPERTS, D, EXPERT_HIDDEN), D ** -0.5),
        'w_expert_down': nrm(ks[23], (DEPTH, N_EXPERTS, EXPERT_HIDDEN, D), EXPERT_HIDDEN ** -0.5),
        'g_final': 1.0 + nrm(ks[24], (D,), 0.02),
    }


def reference(x_prompt, x_sample, cache_swa_k, cache_swa_v, state_gla, c_prompt, c_sample,
              g_mix_norm, g_ffn_norm, w_ada, b_ada, w_in, attn_sinks, w_gla_gate, b_gla_gate,
              g_gla_norm, w_out, w_router_group, b_router_group, w_router_expert, b_router_expert,
              w_expert_gate, w_expert_up, w_expert_down, g_final):
    xp, xs = x_prompt, x_sample
    kp, vp, sp, ksm, vsm, ssm = [], [], [], [], [], []
    for l in range(DEPTH):
        wl = (g_mix_norm[l], g_ffn_norm[l], w_ada[l], b_ada[l], w_in[l], attn_sinks[l],
              w_gla_gate[l], b_gla_gate[l], g_gla_norm[l], w_out[l], w_router_group[l],
              b_router_group[l], w_router_expert[l], b_router_expert[l], w_expert_gate[l],
              w_expert_up[l], w_expert_down[l])
        xp, k_, v_, s_ = encoder_layer(xp, c_prompt, None, None, None, *wl)
        kp.append(k_)
        vp.append(v_)
        sp.append(s_)
        xs, k_, v_, s_ = encoder_layer(xs, c_sample, cache_swa_k[l], cache_swa_v[l], state_gla[l], *wl)
        ksm.append(k_)
        vsm.append(v_)
        ssm.append(s_)
    y_prompt = rmsnorm(xp, g_final)
    y_sample = rmsnorm(xs, g_final)
    return (y_prompt, y_sample, jnp.stack(kp), jnp.stack(vp), jnp.stack(sp),
            jnp.stack(ksm), jnp.stack(vsm), jnp.stack(ssm))
```

```python
import functools

import jax
import jax.numpy as jnp
from jax import lax
from jax.experimental import pallas as pl
from jax.experimental.pallas import tpu as pltpu

f32 = jnp.float32
bf16 = jnp.bfloat16

D_MODEL = 2048
N_MOD = 6
EPS = 1e-6
NEG_INF = -1e30

SWA_HEAD_DIM = 64
SWA_KV_HEADS = 2
SWA_GROUP = 8
SWA_WIDTH = SWA_KV_HEADS * SWA_GROUP * SWA_HEAD_DIM
KV_WIDTH = SWA_KV_HEADS * SWA_HEAD_DIM
WINDOW = 128
CHUNK = 64

GLA_HEADS = 4
GLA_DK = 128
GLA_DV = 256
GLA_QK_WIDTH = GLA_HEADS * GLA_DK
GLA_V_WIDTH = GLA_HEADS * GLA_DV
GLA_GATE_RANK = 16
GLA_GATE_NORM = 16.0

N_GROUPS = 4
EXPERTS_PER_GROUP = 8
N_EXPERTS = N_GROUPS * EXPERTS_PER_GROUP
TOP_K = 2
EXPERT_HIDDEN = D_MODEL // 4

_C_Q = 0
_C_K = _C_Q + SWA_WIDTH
_C_V = _C_K + KV_WIDTH
_C_GQ = _C_V + KV_WIDTH
_C_GK = _C_GQ + GLA_QK_WIDTH
_C_GV = _C_GK + GLA_QK_WIDTH
_C_R = _C_GV + GLA_V_WIDTH
_C_AB = _C_R + GLA_V_WIDTH

LANES = 128
VMEM_LIMIT = 56 * 1024 * 1024

ROUTE_E0, ROUTE_E1, ROUTE_G0, ROUTE_G1 = 0, 1, 2, 3


def _dot(a, b):
    return jnp.dot(a, b, preferred_element_type=f32)


def _dot_nt(a, b):
    return lax.dot_general(a, b, (((1,), (1,)), ((), ())), preferred_element_type=f32)


def _dot_tn(a, b):
    return lax.dot_general(a, b, (((0,), (0,)), ((), ())), preferred_element_type=f32)


def _silu(x):
    return x / (1.0 + jnp.exp(-x))


def _rows(ref):
    v = ref[...]
    return v.reshape(v.shape[-2:])


def _rms(x):
    return x * lax.rsqrt(jnp.mean(x * x, axis=-1, keepdims=True) + EPS)


def _resident(shape):
    return pl.BlockSpec(shape, lambda *_: (0,) * len(shape), pipeline_mode=pl.Buffered(1))


def _mod_spec(comp, tiles_per_seq, n_tiles):
    return pl.BlockSpec((1, 1, D_MODEL),
                        lambda i: ((jnp.minimum(i, n_tiles - 1) // tiles_per_seq) * N_MOD + comp, 0, 0))


def _adaln_kernel(c_ref, w_ref, b_ref, o_ref):
    a = _silu(c_ref[...]).astype(bf16)
    o_ref[...] = _dot(a, w_ref[...].astype(bf16)) + b_ref[...]


def _adaln(c_all, w_ada, b_ada, tn=1024):
    r = c_all.shape[0]
    n = w_ada.shape[1]
    return pl.pallas_call(
        _adaln_kernel,
        out_shape=jax.ShapeDtypeStruct((r, n), f32),
        grid=(n // tn,),
        in_specs=[pl.BlockSpec((r, D_MODEL), lambda j: (0, 0)),
                  pl.BlockSpec((D_MODEL, tn), lambda j: (0, j)),
                  pl.BlockSpec((1, tn), lambda j: (0, j))],
        out_specs=pl.BlockSpec((r, tn), lambda j: (0, j)),
        compiler_params=pltpu.CompilerParams(dimension_semantics=("arbitrary",),
                                             vmem_limit_bytes=VMEM_LIMIT),
        name="adaln",
    )(c_all, w_ada, b_ada.reshape(1, n))


def _proj_kernel(x_ref, sh_ref, sc_ref, g_ref, w_ref, wab_ref, wg_ref, bg_ref,
                 q_ref, k_ref, v_ref, gq_ref, gk_ref, gv_ref, r_ref, lg_ref):
    h = (_rms(x_ref[...]) * g_ref[...]) * (1.0 + _rows(sc_ref)) + _rows(sh_ref)
    hb = h.astype(bf16)
    q_ref[...] = _dot(hb, w_ref[:, _C_Q:_C_K]).astype(bf16)
    kv = _dot(hb, w_ref[:, _C_K:_C_GQ])
    k_ref[...] = kv[:, :KV_WIDTH]
    v_ref[...] = kv[:, KV_WIDTH:]
    gq_ref[...] = _dot(hb, w_ref[:, _C_GQ:_C_GK]).astype(bf16)
    gk_ref[...] = _dot(hb, w_ref[:, _C_GK:_C_GV]).astype(bf16)
    gv_ref[...] = _dot(hb, w_ref[:, _C_GV:_C_R]).astype(bf16)
    r_ref[...] = _dot(hb, w_ref[:, _C_R:_C_AB]).astype(bf16)
    ab = _dot(hb, wab_ref[...])
    z = _dot(ab.astype(bf16), wg_ref[...]) + bg_ref[...]
    log_sig = jnp.minimum(z, 0.0) - jnp.log1p(jnp.exp(-jnp.abs(z)))
    lg_ref[...] = log_sig / GLA_GATE_NORM


def _proj(x2d, sh, sc, g_mix, w_main, w_ab, w_gate, b_gate, *, tm, mod_specs):
    n = x2d.shape[0]
    row = lambda w: pl.BlockSpec((tm, w), lambda i: (i, 0))
    outs = [(SWA_WIDTH, bf16), (KV_WIDTH, f32), (KV_WIDTH, f32), (GLA_QK_WIDTH, bf16),
            (GLA_QK_WIDTH, bf16), (GLA_V_WIDTH, bf16), (GLA_V_WIDTH, bf16), (GLA_QK_WIDTH, f32)]
    return pl.pallas_call(
        _proj_kernel,
        out_shape=[jax.ShapeDtypeStruct((n, w), dt) for w, dt in outs],
        grid=(n // tm,),
        in_specs=[row(D_MODEL), mod_specs[0], mod_specs[1],
                  _resident((1, D_MODEL)), _resident(w_main.shape), _resident(w_ab.shape),
                  _resident(w_gate.shape), _resident((1, GLA_QK_WIDTH))],
        out_specs=[row(w) for w, _ in outs],
        compiler_params=pltpu.CompilerParams(dimension_semantics=("parallel",),
                                             vmem_limit_bytes=VMEM_LIMIT),
        name="proj",
    )(x2d, sh, sc, g_mix.reshape(1, D_MODEL), w_main, w_ab, w_gate, b_gate.reshape(1, -1))


def _swa_kernel(q_ref, kp_ref, vp_ref, kc_ref, vc_ref, sink_ref, o_ref, *, tq, banded):
    u = pl.program_id(1)
    kc, vc = kc_ref[...], vc_ref[...]
    if tq < WINDOW:
        pad = jnp.zeros((WINDOW - tq, KV_WIDTH), f32)
        kc = jnp.concatenate([kc, pad], axis=0)
        vc = jnp.concatenate([vc, pad], axis=0)
    k_all = jnp.concatenate([kp_ref[...], kc], axis=0).astype(bf16)
    v_all = jnp.concatenate([vp_ref[...], vc], axis=0).astype(bf16)
    rows, cols = SWA_GROUP * tq, 2 * WINDOW
    col = lax.broadcasted_iota(jnp.int32, (rows, cols), 1)
    if banded:
        q_chunk = (lax.broadcasted_iota(jnp.int32, (rows, cols), 0) % tq) // CHUNK
        k_chunk = col // CHUNK
        valid = (k_chunk >= q_chunk) & (k_chunk <= q_chunk + WINDOW // CHUNK)
        valid = valid & ((col >= WINDOW) | (u > 0))
    else:
        valid = col < WINDOW + tq
    q = q_ref[...]
    outs = []
    for j in range(SWA_KV_HEADS):
        heads = [q[:, (j * SWA_GROUP + g) * SWA_HEAD_DIM:(j * SWA_GROUP + g + 1) * SWA_HEAD_DIM]
                 for g in range(SWA_GROUP)]
        qs = jnp.concatenate(heads, axis=0)
        kj = k_all[:, j * SWA_HEAD_DIM:(j + 1) * SWA_HEAD_DIM]
        vj = v_all[:, j * SWA_HEAD_DIM:(j + 1) * SWA_HEAD_DIM]
        s = _dot_nt(qs, kj) * (SWA_HEAD_DIM ** -0.5)
        s = jnp.where(valid, s, NEG_INF)
        sink = sink_ref[j]
        m = jnp.maximum(jnp.max(s, axis=1, keepdims=True), sink)
        p = jnp.exp(s - m)
        den = jnp.sum(p, axis=1, keepdims=True) + jnp.exp(sink - m)
        o = _dot(p.astype(bf16), vj) / den
        outs.append(jnp.concatenate([o[g * tq:(g + 1) * tq] for g in range(SWA_GROUP)], axis=1))
    o_ref[...] = jnp.concatenate(outs, axis=1).astype(bf16)


def _swa(q, k_prev, v_prev, k_cur, v_cur, sinks, *, n_seq, tq, tiles, banded):
    sink_rows = jnp.repeat(sinks.astype(f32).reshape(SWA_KV_HEADS, SWA_GROUP), tq, axis=1)
    sink_rows = sink_rows.reshape(SWA_KV_HEADS, SWA_GROUP * tq, 1)
    cur = lambda w: pl.BlockSpec((tq, w), lambda b, u: (b * tiles + u, 0))
    if banded:
        prev = pl.BlockSpec((WINDOW, KV_WIDTH), lambda b, u: (b * tiles + jnp.maximum(u - 1, 0), 0))
    else:
        prev = pl.BlockSpec((WINDOW, KV_WIDTH), lambda b, u: (b, 0))
    return pl.pallas_call(
        functools.partial(_swa_kernel, tq=tq, banded=banded),
        out_shape=jax.ShapeDtypeStruct(q.shape, bf16),
        grid=(n_seq, tiles),
        in_specs=[cur(SWA_WIDTH), prev, prev, cur(KV_WIDTH), cur(KV_WIDTH),
                  pl.BlockSpec(sink_rows.shape, lambda b, u: (0, 0, 0))],
        out_specs=cur(SWA_WIDTH),
        compiler_params=pltpu.CompilerParams(dimension_semantics=("parallel", "arbitrary"),
                                             vmem_limit_bytes=VMEM_LIMIT),
        name="swa",
    )(q, k_prev, v_prev, k_cur, v_cur, sink_rows)


def _gla_kernel(q_ref, k_ref, v_ref, lg_ref, r_ref, gh_ref, s0_ref, o_ref, s_ref, *, c):
    @pl.when(pl.program_id(1) == 0)
    def _():
        s_ref[...] = s0_ref[...]

    row = lax.broadcasted_iota(jnp.int32, (c, GLA_DK), 0)
    causal = (lax.broadcasted_iota(jnp.int32, (c, c), 0) >= lax.broadcasted_iota(jnp.int32, (c, c), 1))
    for h in range(GLA_HEADS):
        ks = slice(h * GLA_DK, (h + 1) * GLA_DK)
        vs = slice(h * GLA_DV, (h + 1) * GLA_DV)
        b = lg_ref[:, ks]
        step = 1
        while step < c:
            b = b + jnp.where(row >= step, pltpu.roll(b, step, 0), 0.0)
            step *= 2
        b_last = b[c - 1:c, :]
        q = q_ref[:, ks].astype(f32) * (GLA_DK ** -0.5)
        k = k_ref[:, ks].astype(f32)
        v = v_ref[:, vs]
        qd = (q * jnp.exp(b)).astype(bf16)
        kd = (k * jnp.exp(-b)).astype(bf16)
        kl = (k * jnp.exp(b_last - b)).astype(bf16)
        a = jnp.where(causal, _dot_nt(qd, kd), 0.0)
        s = s_ref[0, h]
        o = _dot(qd, s.astype(bf16)) + _dot(a.astype(bf16), v)
        decay = jnp.broadcast_to(jnp.exp(b_last), (GLA_DK, GLA_DK)).T
        s_ref[0, h] = s * jnp.concatenate([decay, decay], axis=1) + _dot_tn(kl, v)
        on = _rms(o) * gh_ref[...]
        o_ref[:, vs] = (on * _silu(r_ref[:, vs].astype(f32))).astype(bf16)


def _gla(gq, gk, gv, lg, r, g_head, s0, *, n_seq, c, chunks):
    blk = lambda w: pl.BlockSpec((c, w), lambda b, t: (b * chunks + t, 0))
    state = pl.BlockSpec((1, GLA_HEADS, GLA_DK, GLA_DV), lambda b, t: (b, 0, 0, 0))
    return pl.pallas_call(
        functools.partial(_gla_kernel, c=c),
        out_shape=[jax.ShapeDtypeStruct(gv.shape, bf16),
                   jax.ShapeDtypeStruct((n_seq, GLA_HEADS, GLA_DK, GLA_DV), f32)],
        grid=(n_seq, chunks),
        in_specs=[blk(GLA_QK_WIDTH), blk(GLA_QK_WIDTH), blk(GLA_V_WIDTH), blk(GLA_QK_WIDTH),
                  blk(GLA_V_WIDTH), pl.BlockSpec((1, GLA_DV), lambda b, t: (0, 0)), state],
        out_specs=[blk(GLA_V_WIDTH), state],
        compiler_params=pltpu.CompilerParams(dimension_semantics=("parallel", "arbitrary"),
                                             vmem_limit_bytes=VMEM_LIMIT),
        name="gla",
    )(gq, gk, gv, lg, r, g_head.reshape(1, GLA_DV), s0)


def _outproj_kernel(*refs, n_alias):
    (oa_ref, ob_ref, x_ref, gt_ref, sh_ref, sc_ref, g_ref, w_ref, wr_ref, br_ref) = refs[:10]
    x1_ref, h2_ref, route_ref = refs[10 + n_alias:]
    mix = _dot(oa_ref[...], w_ref[:SWA_WIDTH, :]) + _dot(ob_ref[...], w_ref[SWA_WIDTH:, :])
    x1 = x_ref[...] + _rows(gt_ref) * mix
    x1_ref[...] = x1
    h2 = (_rms(x1) * g_ref[...]) * (1.0 + _rows(sc_ref)) + _rows(sh_ref)
    h2_ref[...] = h2
    hi = h2.astype(bf16)
    lo = (h2 - hi.astype(f32)).astype(bf16)
    r1 = _dot(hi, wr_ref[...])
    logits = r1[:, :LANES] + r1[:, LANES:] + _dot(lo, wr_ref[:, :LANES]) + br_ref[...]
    lane = lax.broadcasted_iota(jnp.int32, logits.shape, 1)
    lane_f = lane.astype(f32)
    neg = float("-inf")
    first = lambda hit: jnp.min(jnp.where(hit, lane_f, float(LANES)), axis=1, keepdims=True)
    lg_g = jnp.where(lane < N_GROUPS, logits, neg)
    g_max = jnp.max(lg_g, axis=1, keepdims=True)
    g_sel = first(lg_g == g_max)
    p_sel = 1.0 / jnp.sum(jnp.exp(lg_g - g_max), axis=1, keepdims=True)
    e_lane = lane - N_GROUPS
    in_group = (e_lane >= 0) & (e_lane < N_EXPERTS) & ((e_lane // EXPERTS_PER_GROUP).astype(f32) == g_sel)
    lg_e = jnp.where(in_group, logits, neg)
    v1 = jnp.max(lg_e, axis=1, keepdims=True)
    i1 = first(lg_e == v1)
    lg_e2 = jnp.where(lane_f == i1, neg, lg_e)
    v2 = jnp.max(lg_e2, axis=1, keepdims=True)
    i2 = first(lg_e2 == v2)
    e = jnp.exp(v2 - v1)
    g1 = p_sel / (1.0 + e)
    g2 = p_sel * e / (1.0 + e)
    rec = jnp.where(lane == ROUTE_E0, i1 - N_GROUPS, 0.0)
    rec = jnp.where(lane == ROUTE_E1, i2 - N_GROUPS, rec)
    rec = jnp.where(lane == ROUTE_G0, g1, rec)
    rec = jnp.where(lane == ROUTE_G1, g2, rec)
    route_ref[...] = rec


def _outproj(oa, ob, x2d, gt, sh, sc, g_ffn, w_out, w_route, b_route, *, tm, mod_specs,
             n_total, row0, bufs=None, fill_steps=0):
    n = x2d.shape[0]
    tiles = n // tm
    row = lambda w: pl.BlockSpec((tm, w), lambda i: (jnp.minimum(i, tiles - 1), 0))
    out_row = lambda w: pl.BlockSpec((tm, w), lambda i: (row0 // tm + i, 0))
    widths = (D_MODEL, D_MODEL, LANES)
    alias_in = list(bufs) if bufs is not None else []
    n_in = 10
    return pl.pallas_call(
        functools.partial(_outproj_kernel, n_alias=len(alias_in)),
        out_shape=[jax.ShapeDtypeStruct((n_total, w), f32) for w in widths],
        grid=(tiles + fill_steps,),
        in_specs=[row(SWA_WIDTH), row(GLA_V_WIDTH), row(D_MODEL), mod_specs[0], mod_specs[1],
                  mod_specs[2], _resident((1, D_MODEL)), _resident(w_out.shape),
                  _resident(w_route.shape), _resident((1, LANES))]
                 + [pl.BlockSpec(memory_space=pl.ANY)] * len(alias_in),
        out_specs=[out_row(w) for w in widths],
        input_output_aliases={n_in + a: a for a in range(len(alias_in))},
        compiler_params=pltpu.CompilerParams(dimension_semantics=("parallel",),
                                             vmem_limit_bytes=VMEM_LIMIT),
        name="outproj",
    )(oa, ob, x2d, gt, sh, sc, g_ffn.reshape(1, D_MODEL), w_out, w_route, b_route, *alias_in)


def _gather_rows(idx_ref, base, src_hbm, dst, sem, n):
    def body(r, carry):
        t = idx_ref[base + r]
        pltpu.make_async_copy(src_hbm.at[pl.ds(t, 1), :], dst.at[pl.ds(r, 1), :], sem).start()
        return carry
    lax.fori_loop(0, n, body, 0)


def _expert_kernel(be_ref, tok_ref, h2_hbm, wg_ref, wu_ref, wd_ref, y_ref,
                   xbuf, wg_b, wu_b, wd_b, sem, *, tb):
    i = pl.program_id(0)
    slot = i % 2

    @pl.when(i == 0)
    def _():
        _gather_rows(tok_ref, 0, h2_hbm, xbuf.at[0], sem.at[0], tb)

    @pl.when(i + 1 < pl.num_programs(0))
    def _():
        _gather_rows(tok_ref, (i + 1) * tb, h2_hbm, xbuf.at[1 - slot], sem.at[1 - slot], tb)

    @pl.when((i == 0) | (be_ref[i] != be_ref[jnp.maximum(i - 1, 0)]))
    def _():
        wg_b[...] = wg_ref[0].astype(bf16)
        wu_b[...] = wu_ref[0].astype(bf16)
        wd_b[...] = wd_ref[0].astype(bf16)

    pltpu.make_async_copy(h2_hbm.at[pl.ds(0, tb), :], xbuf.at[slot], sem.at[slot]).wait()
    x = xbuf[slot].astype(bf16)
    g = _dot(x, wg_b[...])
    u = _dot(x, wu_b[...])
    y_ref[...] = _dot((_silu(g) * u).astype(bf16), wd_b[...])


def _experts(block_e, slot_tok, h2, w_eg, w_eu, w_ed, *, tb):
    n_blocks = block_e.shape[0]
    wspec = lambda shape: pl.BlockSpec((1,) + shape, lambda i, be, tok: (be[i], 0, 0))
    return pl.pallas_call(
        functools.partial(_expert_kernel, tb=tb),
        out_shape=jax.ShapeDtypeStruct((n_blocks * tb, D_MODEL), f32),
        grid_spec=pltpu.PrefetchScalarGridSpec(
            num_scalar_prefetch=2,
            grid=(n_blocks,),
            in_specs=[pl.BlockSpec(memory_space=pl.ANY),
                      wspec((D_MODEL, EXPERT_HIDDEN)), wspec((D_MODEL, EXPERT_HIDDEN)),
                      wspec((EXPERT_HIDDEN, D_MODEL))],
            out_specs=pl.BlockSpec((tb, D_MODEL), lambda i, be, tok: (i, 0)),
            scratch_shapes=[pltpu.VMEM((2, tb, D_MODEL), f32),
                            pltpu.VMEM((D_MODEL, EXPERT_HIDDEN), bf16),
                            pltpu.VMEM((D_MODEL, EXPERT_HIDDEN), bf16),
                            pltpu.VMEM((EXPERT_HIDDEN, D_MODEL), bf16),
                            pltpu.SemaphoreType.DMA((2,))]),
        compiler_params=pltpu.CompilerParams(dimension_semantics=("arbitrary",),
                                             vmem_limit_bytes=VMEM_LIMIT),
        name="experts",
    )(block_e, slot_tok, h2, w_eg, w_eu, w_ed)


def _combine_kernel(pos_ref, yb_hbm, x1_ref, route_ref, gt_ref, gf_ref, y_ref, ybuf, sem,
                    *, tm, blk0):
    i = pl.program_id(0)
    slot = i % 2

    def start(blk, s):
        for k in range(TOP_K):
            _gather_rows(pos_ref, (k * pl.num_programs(0) + blk) * tm, yb_hbm,
                         ybuf.at[s, k], sem.at[s], tm)

    @pl.when(i == 0)
    def _():
        start(0, 0)

    @pl.when(i + 1 < pl.num_programs(0))
    def _():
        start(i + 1, 1 - slot)

    for k in range(TOP_K):
        pltpu.make_async_copy(yb_hbm.at[pl.ds(0, tm), :], ybuf.at[slot, k], sem.at[slot]).wait()
    route = route_ref[...]
    moe = ybuf[slot, 0] * route[:, ROUTE_G0:ROUTE_G0 + 1] + ybuf[slot, 1] * route[:, ROUTE_G1:ROUTE_G1 + 1]
    x2 = x1_ref[...] + _rows(gt_ref) * moe
    y_ref[...] = _rms(x2) * gf_ref[...]


def _combine(pos_km, yb, x1, route, gt, g_final, *, tm, n, row0, gt_spec):
    blk0 = row0 // tm
    return pl.pallas_call(
        functools.partial(_combine_kernel, tm=tm, blk0=blk0),
        out_shape=jax.ShapeDtypeStruct((n, D_MODEL), f32),
        grid_spec=pltpu.PrefetchScalarGridSpec(
            num_scalar_prefetch=1,
            grid=(n // tm,),
            in_specs=[pl.BlockSpec(memory_space=pl.ANY),
                      pl.BlockSpec((tm, D_MODEL), lambda i, pos: (blk0 + i, 0)),
                      pl.BlockSpec((tm, LANES), lambda i, pos: (blk0 + i, 0)),
                      gt_spec,
                      pl.BlockSpec((1, D_MODEL), lambda i, pos: (0, 0))],
            out_specs=pl.BlockSpec((tm, D_MODEL), lambda i, pos: (i, 0)),
            scratch_shapes=[pltpu.VMEM((2, TOP_K, tm, D_MODEL), f32),
                            pltpu.SemaphoreType.DMA((2,))]),
        compiler_params=pltpu.CompilerParams(dimension_semantics=("arbitrary",),
                                             vmem_limit_bytes=VMEM_LIMIT),
        name="combine",
    )(pos_km, yb, x1, route, gt, g_final.reshape(1, D_MODEL))


def _dispatch_tables(e_idx, tb):
    n = e_idx.shape[0]
    a = n * TOP_K
    flat_e = e_idx.reshape(-1)
    order = jnp.argsort(flat_e)
    se = flat_e[order]
    counts = jnp.bincount(flat_e, length=N_EXPERTS)
    starts = jnp.cumsum(counts) - counts
    padded = (counts + tb - 1) // tb * tb
    pend = jnp.cumsum(padded)
    pstart = pend - padded
    dest = (pstart[se] + jnp.arange(a) - starts[se]).astype(jnp.int32)
    n_blocks = -(-(a + N_EXPERTS * (tb - 1)) // tb)
    slot_tok = jnp.zeros((n_blocks * tb,), jnp.int32).at[dest].set((order // TOP_K).astype(jnp.int32))
    pos = jnp.zeros((a,), jnp.int32).at[order].set(dest)
    block_e = jnp.minimum(jnp.searchsorted(pend, jnp.arange(n_blocks) * tb, side="right"),
                          N_EXPERTS - 1).astype(jnp.int32)
    return block_e, slot_tok, pos.reshape(n, TOP_K)


def kernel(x_prompt, x_sample, cache_swa_k, cache_swa_v, state_gla, c_prompt, c_sample, g_mix_norm, g_ffn_norm, w_ada, b_ada, w_in, attn_sinks, w_gla_gate, b_gla_gate, g_gla_norm, w_out, w_router_group, b_router_group, w_router_expert, b_router_expert, w_expert_gate, w_expert_up, w_expert_down, g_final):
    depth = w_in.shape[0]
    assert depth == 1
    bp, t, d = x_prompt.shape
    bs, ts, _ = x_sample.shape
    n_p, n_s = bp * t, bs * ts
    n_tot = n_p + n_s
    tm = 512
    to = 256
    tb = 256
    tc = 256
    gla_c = 128

    w_in0 = w_in[0]
    w_main = w_in0[:, :_C_AB].astype(bf16)
    w_ab = jnp.pad(w_in0[:, _C_AB:], ((0, 0), (0, LANES - GLA_GATE_RANK))).astype(bf16)
    w_gate = jnp.pad(w_gla_gate[0], ((0, LANES - GLA_GATE_RANK), (0, 0))).astype(bf16)
    w_out_b = w_out[0].astype(bf16)
    n_r = N_GROUPS + N_EXPERTS
    w_r = jnp.pad(jnp.concatenate([w_router_group[0], w_router_expert[0]], axis=1),
                  ((0, 0), (0, LANES - n_r)))
    w_r_hi = w_r.astype(bf16)
    w_r_lo = (w_r - w_r_hi.astype(f32)).astype(bf16)
    w_route = jnp.concatenate([w_r_hi, w_r_lo], axis=1)
    b_route = jnp.pad(jnp.concatenate([b_router_group[0], b_router_expert[0]]),
                      (0, LANES - n_r)).reshape(1, LANES)

    c_all = jnp.concatenate([c_prompt, c_sample], axis=0)
    mod = _adaln(c_all, w_ada[0], b_ada[0])
    mod_tab = mod.reshape((bp + bs) * N_MOD, 1, d)
    mod_s = jnp.repeat(mod[bp:].reshape(bs, N_MOD, d), ts, axis=0)
    mod_s = [mod_s[:, m] for m in range(N_MOD)]
    pmod = lambda comp, tile: _mod_spec(comp, t // tile, n_p // tile)
    smod = pl.BlockSpec((n_s, d), lambda i: (0, 0))

    xp = x_prompt.reshape(n_p, d)
    xs = x_sample.reshape(n_s, d)
    proj_w = (g_mix_norm[0], w_main, w_ab, w_gate, b_gla_gate[0])
    qp, kp, vp, gqp, gkp, gvp, rp, lgp = _proj(xp, mod_tab, mod_tab, *proj_w, tm=tm,
                                               mod_specs=(pmod(0, tm), pmod(1, tm)))
    qs, ks, vs, gqs, gks, gvs, rs, lgs = _proj(xs, mod_s[0], mod_s[1], *proj_w, tm=n_s,
                                               mod_specs=(smod, smod))

    sinks = attn_sinks[0]
    oap = _swa(qp, kp, vp, kp, vp, sinks, n_seq=bp, tq=WINDOW, tiles=t // WINDOW, banded=True)
    ck = cache_swa_k[0].reshape(bs * WINDOW, KV_WIDTH)
    cv = cache_swa_v[0].reshape(bs * WINDOW, KV_WIDTH)
    oas = _swa(qs, ck, cv, ks, vs, sinks, n_seq=bs, tq=ts, tiles=1, banded=False)
    s_zero = jnp.zeros((bp, GLA_HEADS, GLA_DK, GLA_DV), f32)
    obp, sp = _gla(gqp, gkp, gvp, lgp, rp, g_gla_norm[0], s_zero, n_seq=bp, c=gla_c, chunks=t // gla_c)
    obs, ss = _gla(gqs, gks, gvs, lgs, rs, g_gla_norm[0], state_gla[0], n_seq=bs, c=ts, chunks=1)

    out_w = (g_ffn_norm[0], w_out_b, w_route, b_route)
    bufs = _outproj(oap, obp, xp, mod_tab, mod_tab, mod_tab, *out_w, tm=to,
                    mod_specs=(pmod(2, to), pmod(3, to), pmod(4, to)), n_total=n_tot, row0=0,
                    fill_steps=-(-n_s // to))
    x1, h2, route = _outproj(oas, obs, xs, mod_s[2], mod_s[3], mod_s[4], *out_w, tm=n_s,
                             mod_specs=(smod, smod, smod), n_total=n_tot, row0=n_p, bufs=bufs)

    e_idx = route[:, ROUTE_E0:ROUTE_E1 + 1].astype(jnp.int32)
    block_e, slot_tok, pos = _dispatch_tables(e_idx, tb)
    yb = _experts(block_e, slot_tok, h2, w_expert_gate[0], w_expert_up[0], w_expert_down[0], tb=tb)

    pos_p = pos[:n_p].T.reshape(-1)
    pos_s = pos[n_p:].T.reshape(-1)
    gt_p = pl.BlockSpec((1, 1, d), lambda i, p: ((i // (t // tc)) * N_MOD + 5, 0, 0))
    gt_s = pl.BlockSpec((n_s, d), lambda i, p: (0, 0))
    y_p = _combine(pos_p, yb, x1, route, mod_tab, g_final, tm=tc, n=n_p, row0=0, gt_spec=gt_p)
    y_s = _combine(pos_s, yb, x1, route, mod_s[5], g_final, tm=n_s, n=n_s, row0=n_p, gt_spec=gt_s)

    kv_shape = (SWA_KV_HEADS, SWA_HEAD_DIM)
    k_state_p = kp.reshape(bp, t, *kv_shape)[:, -WINDOW:][None]
    v_state_p = vp.reshape(bp, t, *kv_shape)[:, -WINDOW:][None]
    return (y_p.reshape(bp, t, d), y_s.reshape(bs, ts, d), k_state_p, v_state_p, sp[None],
            ks.reshape(bs, ts, *kv_shape)[None], vs.reshape(bs, ts, *kv_shape)[None], ss[None])
```

```python
import functools

import jax
import jax.numpy as jnp
from jax import lax
from jax.experimental import pallas as pl
from jax.experimental.pallas import tpu as pltpu

f32 = jnp.float32
bf16 = jnp.bfloat16
u32 = jnp.uint32

D_MODEL = 2048
N_MOD = 6
EPS = 1e-6
NEG_INF = -1e30

SWA_HEAD_DIM = 64
SWA_KV_HEADS = 2
SWA_GROUP = 8
SWA_WIDTH = SWA_KV_HEADS * SWA_GROUP * SWA_HEAD_DIM
KV_WIDTH = SWA_KV_HEADS * SWA_HEAD_DIM
WINDOW = 128
CHUNK = 64

GLA_HEADS = 4
GLA_DK = 128
GLA_DV = 256
GLA_QK_WIDTH = GLA_HEADS * GLA_DK
GLA_V_WIDTH = GLA_HEADS * GLA_DV
GLA_GATE_RANK = 16
GLA_GATE_NORM = 16.0

N_GROUPS = 4
EXPERTS_PER_GROUP = 8
N_EXPERTS = N_GROUPS * EXPERTS_PER_GROUP
TOP_K = 2
EXPERT_HIDDEN = D_MODEL // 4

_C_Q = 0
_C_K = _C_Q + SWA_WIDTH
_C_V = _C_K + KV_WIDTH
_C_GQ = _C_V + KV_WIDTH
_C_GK = _C_GQ + GLA_QK_WIDTH
_C_GV = _C_GK + GLA_QK_WIDTH
_C_R = _C_GV + GLA_V_WIDTH
_C_AB = _C_R + GLA_V_WIDTH

LANES = 128
VMEM_LIMIT = 56 * 1024 * 1024

ROUTE_E0, ROUTE_E1, ROUTE_G0, ROUTE_G1 = 0, 1, 2, 3


def _dot(a, b):
    return jnp.dot(a, b, preferred_element_type=f32)


def _dot_nt(a, b):
    return lax.dot_general(a, b, (((1,), (1,)), ((), ())), preferred_element_type=f32)


def _dot_tn(a, b):
    return lax.dot_general(a, b, (((0,), (0,)), ((), ())), preferred_element_type=f32)


def _silu(x):
    return x / (1.0 + jnp.exp(-x))


def _rows(ref):
    v = ref[...]
    return v.reshape(v.shape[-2:])


def _rms(x):
    return x * lax.rsqrt(jnp.mean(x * x, axis=-1, keepdims=True) + EPS)


def _resident(shape):
    return pl.BlockSpec(shape, lambda *_: (0,) * len(shape), pipeline_mode=pl.Buffered(1))


def _mod_spec(comp, tiles_per_seq, n_tiles):
    return pl.BlockSpec((1, 1, D_MODEL),
                        lambda i: ((jnp.minimum(i, n_tiles - 1) // tiles_per_seq) * N_MOD + comp, 0, 0))


def _adaln_kernel(c_ref, w_ref, b_ref, o_ref):
    a = _silu(c_ref[...]).astype(bf16)
    o_ref[...] = _dot(a, w_ref[...].astype(bf16)) + b_ref[...]


def _adaln(c_all, w_ada, b_ada, tn=1024):
    r = c_all.shape[0]
    n = w_ada.shape[1]
    return pl.pallas_call(
        _adaln_kernel,
        out_shape=jax.ShapeDtypeStruct((r, n), f32),
        grid=(n // tn,),
        in_specs=[pl.BlockSpec((r, D_MODEL), lambda j: (0, 0)),
                  pl.BlockSpec((D_MODEL, tn), lambda j: (0, j)),
                  pl.BlockSpec((1, tn), lambda j: (0, j))],
        out_specs=pl.BlockSpec((r, tn), lambda j: (0, j)),
        compiler_params=pltpu.CompilerParams(dimension_semantics=("arbitrary",),
                                             vmem_limit_bytes=VMEM_LIMIT),
        name="adaln",
    )(c_all, w_ada, b_ada.reshape(1, n))


def _proj_kernel(x_ref, sh_ref, sc_ref, g_ref, w_ref, wab_ref, wg_ref, bg_ref,
                 q_ref, k_ref, v_ref, gq_ref, gk_ref, gv_ref, r_ref, lg_ref):
    h = (_rms(x_ref[...]) * g_ref[...]) * (1.0 + _rows(sc_ref)) + _rows(sh_ref)
    hb = h.astype(bf16)
    q_ref[...] = _dot(hb, w_ref[:, _C_Q:_C_K]).astype(bf16)
    kv = _dot(hb, w_ref[:, _C_K:_C_GQ])
    k_ref[...] = kv[:, :KV_WIDTH]
    v_ref[...] = kv[:, KV_WIDTH:]
    gq_ref[...] = _dot(hb, w_ref[:, _C_GQ:_C_GK]).astype(bf16)
    gk_ref[...] = _dot(hb, w_ref[:, _C_GK:_C_GV]).astype(bf16)
    gv_ref[...] = _dot(hb, w_ref[:, _C_GV:_C_R]).astype(bf16)
    r_ref[...] = _dot(hb, w_ref[:, _C_R:_C_AB]).astype(bf16)
    ab = _dot(hb, wab_ref[...])
    z = _dot(ab.astype(bf16), wg_ref[...]) + bg_ref[...]
    log_sig = jnp.minimum(z, 0.0) - jnp.log1p(jnp.exp(-jnp.abs(z)))
    lg_ref[...] = log_sig / GLA_GATE_NORM


def _proj(x2d, sh, sc, g_mix, w_main, w_ab, w_gate, b_gate, *, tm, mod_specs):
    n = x2d.shape[0]
    row = lambda w: pl.BlockSpec((tm, w), lambda i: (i, 0))
    outs = [(SWA_WIDTH, bf16), (KV_WIDTH, f32), (KV_WIDTH, f32), (GLA_QK_WIDTH, bf16),
            (GLA_QK_WIDTH, bf16), (GLA_V_WIDTH, bf16), (GLA_V_WIDTH, bf16), (GLA_QK_WIDTH, f32)]
    return pl.pallas_call(
        _proj_kernel,
        out_shape=[jax.ShapeDtypeStruct((n, w), dt) for w, dt in outs],
        grid=(n // tm,),
        in_specs=[row(D_MODEL), mod_specs[0], mod_specs[1],
                  _resident((1, D_MODEL)), _resident(w_main.shape), _resident(w_ab.shape),
                  _resident(w_gate.shape), _resident((1, GLA_QK_WIDTH))],
        out_specs=[row(w) for w, _ in outs],
        compiler_params=pltpu.CompilerParams(dimension_semantics=("parallel",),
                                             vmem_limit_bytes=VMEM_LIMIT),
        name="proj",
    )(x2d, sh, sc, g_mix.reshape(1, D_MODEL), w_main, w_ab, w_gate, b_gate.reshape(1, -1))


def _swa_kernel(q_ref, kp_ref, vp_ref, kc_ref, vc_ref, sink_ref, o_ref, *, tq, banded):
    u = pl.program_id(1)
    kc, vc = kc_ref[...], vc_ref[...]
    if tq < WINDOW:
        pad = jnp.zeros((WINDOW - tq, KV_WIDTH), f32)
        kc = jnp.concatenate([kc, pad], axis=0)
        vc = jnp.concatenate([vc, pad], axis=0)
    k_all = jnp.concatenate([kp_ref[...], kc], axis=0).astype(bf16)
    v_all = jnp.concatenate([vp_ref[...], vc], axis=0).astype(bf16)
    rows, cols = SWA_GROUP * tq, 2 * WINDOW
    col = lax.broadcasted_iota(jnp.int32, (rows, cols), 1)
    if banded:
        q_chunk = (lax.broadcasted_iota(jnp.int32, (rows, cols), 0) % tq) // CHUNK
        k_chunk = col // CHUNK
        valid = (k_chunk >= q_chunk) & (k_chunk <= q_chunk + WINDOW // CHUNK)
        valid = valid & ((col >= WINDOW) | (u > 0))
    else:
        valid = col < WINDOW + tq
    q = q_ref[...]
    outs = []
    for j in range(SWA_KV_HEADS):
        heads = [q[:, (j * SWA_GROUP + g) * SWA_HEAD_DIM:(j * SWA_GROUP + g + 1) * SWA_HEAD_DIM]
                 for g in range(SWA_GROUP)]
        qs = jnp.concatenate(heads, axis=0)
        kj = k_all[:, j * SWA_HEAD_DIM:(j + 1) * SWA_HEAD_DIM]
        vj = v_all[:, j * SWA_HEAD_DIM:(j + 1) * SWA_HEAD_DIM]
        s = _dot_nt(qs, kj) * (SWA_HEAD_DIM ** -0.5)
        s = jnp.where(valid, s, NEG_INF)
        sink = sink_ref[j]
        m = jnp.maximum(jnp.max(s, axis=1, keepdims=True), sink)
        p = jnp.exp(s - m)
        den = jnp.sum(p, axis=1, keepdims=True) + jnp.exp(sink - m)
        o = _dot(p.astype(bf16), vj) / den
        outs.append(jnp.concatenate([o[g * tq:(g + 1) * tq] for g in range(SWA_GROUP)], axis=1))
    o_ref[...] = jnp.concatenate(outs, axis=1).astype(bf16)


def _swa(q, k_prev, v_prev, k_cur, v_cur, sinks, *, n_seq, tq, tiles, banded):
    sink_rows = jnp.repeat(sinks.astype(f32).reshape(SWA_KV_HEADS, SWA_GROUP), tq, axis=1)
    sink_rows = sink_rows.reshape(SWA_KV_HEADS, SWA_GROUP * tq, 1)
    cur = lambda w: pl.BlockSpec((tq, w), lambda b, u: (b * tiles + u, 0))
    if banded:
        prev = pl.BlockSpec((WINDOW, KV_WIDTH), lambda b, u: (b * tiles + jnp.maximum(u - 1, 0), 0))
    else:
        prev = pl.BlockSpec((WINDOW, KV_WIDTH), lambda b, u: (b, 0))
    return pl.pallas_call(
        functools.partial(_swa_kernel, tq=tq, banded=banded),
        out_shape=jax.ShapeDtypeStruct(q.shape, bf16),
        grid=(n_seq, tiles),
        in_specs=[cur(SWA_WIDTH), prev, prev, cur(KV_WIDTH), cur(KV_WIDTH),
                  pl.BlockSpec(sink_rows.shape, lambda b, u: (0, 0, 0))],
        out_specs=cur(SWA_WIDTH),
        compiler_params=pltpu.CompilerParams(dimension_semantics=("parallel", "arbitrary"),
                                             vmem_limit_bytes=VMEM_LIMIT),
        name="swa",
    )(q, k_prev, v_prev, k_cur, v_cur, sink_rows)


def _gla_kernel(q_ref, k_ref, v_ref, lg_ref, r_ref, gh_ref, s0_ref, o_ref, s_ref, *, c):
    @pl.when(pl.program_id(1) == 0)
    def _():
        s_ref[...] = s0_ref[...]

    row = lax.broadcasted_iota(jnp.int32, (c, GLA_DK), 0)
    causal = (lax.broadcasted_iota(jnp.int32, (c, c), 0) >= lax.broadcasted_iota(jnp.int32, (c, c), 1))
    for h in range(GLA_HEADS):
        ks = slice(h * GLA_DK, (h + 1) * GLA_DK)
        vs = slice(h * GLA_DV, (h + 1) * GLA_DV)
        b = lg_ref[:, ks]
        step = 1
        while step < c:
            b = b + jnp.where(row >= step, pltpu.roll(b, step, 0), 0.0)
            step *= 2
        b_last = b[c - 1:c, :]
        q = q_ref[:, ks].astype(f32) * (GLA_DK ** -0.5)
        k = k_ref[:, ks].astype(f32)
        v = v_ref[:, vs]
        qd = (q * jnp.exp(b)).astype(bf16)
        kd = (k * jnp.exp(-b)).astype(bf16)
        kl = (k * jnp.exp(b_last - b)).astype(bf16)
        a = jnp.where(causal, _dot_nt(qd, kd), 0.0)
        s = s_ref[0, h]
        o = _dot(qd, s.astype(bf16)) + _dot(a.astype(bf16), v)
        decay = jnp.broadcast_to(jnp.exp(b_last), (GLA_DK, GLA_DK)).T
        s_ref[0, h] = s * jnp.concatenate([decay, decay], axis=1) + _dot_tn(kl, v)
        on = _rms(o) * gh_ref[...]
        o_ref[:, vs] = (on * _silu(r_ref[:, vs].astype(f32))).astype(bf16)


def _gla(gq, gk, gv, lg, r, g_head, s0, *, n_seq, c, chunks):
    blk = lambda w: pl.BlockSpec((c, w), lambda b, t: (b * chunks + t, 0))
    state = pl.BlockSpec((1, GLA_HEADS, GLA_DK, GLA_DV), lambda b, t: (b, 0, 0, 0))
    return pl.pallas_call(
        functools.partial(_gla_kernel, c=c),
        out_shape=[jax.ShapeDtypeStruct(gv.shape, bf16),
                   jax.ShapeDtypeStruct((n_seq, GLA_HEADS, GLA_DK, GLA_DV), f32)],
        grid=(n_seq, chunks),
        in_specs=[blk(GLA_QK_WIDTH), blk(GLA_QK_WIDTH), blk(GLA_V_WIDTH), blk(GLA_QK_WIDTH),
                  blk(GLA_V_WIDTH), pl.BlockSpec((1, GLA_DV), lambda b, t: (0, 0)), state],
        out_specs=[blk(GLA_V_WIDTH), state],
        compiler_params=pltpu.CompilerParams(dimension_semantics=("parallel", "arbitrary"),
                                             vmem_limit_bytes=VMEM_LIMIT),
        name="gla",
    )(gq, gk, gv, lg, r, g_head.reshape(1, GLA_DV), s0)


def _pack_pair(hb):
    w = hb.shape[1] // 2
    a = lax.bitcast_convert_type(hb[:, :w].astype(f32), u32)
    b = lax.bitcast_convert_type(hb[:, w:].astype(f32), u32)
    return a | (b >> 16)


def _unpack_pair(p):
    a = lax.bitcast_convert_type(p & jnp.uint32(0xFFFF0000), f32)
    b = lax.bitcast_convert_type(p << 16, f32)
    return a, b


def _outproj_kernel(*refs, n_alias):
    (oa_ref, ob_ref, x_ref, gt_ref, sh_ref, sc_ref, g_ref, w_ref, wr_ref, br_ref) = refs[:10]
    x1_ref, h2_ref, route_ref = refs[10 + n_alias:]
    mix = _dot(oa_ref[...], w_ref[:SWA_WIDTH, :]) + _dot(ob_ref[...], w_ref[SWA_WIDTH:, :])
    x1 = x_ref[...] + _rows(gt_ref) * mix
    x1_ref[...] = x1
    h2 = (_rms(x1) * g_ref[...]) * (1.0 + _rows(sc_ref)) + _rows(sh_ref)
    hi = h2.astype(bf16)
    h2_ref[...] = _pack_pair(hi)
    lo = (h2 - hi.astype(f32)).astype(bf16)
    r1 = _dot(hi, wr_ref[...])
    logits = r1[:, :LANES] + r1[:, LANES:] + _dot(lo, wr_ref[:, :LANES]) + br_ref[...]
    lane = lax.broadcasted_iota(jnp.int32, logits.shape, 1)
    lane_f = lane.astype(f32)
    neg = float("-inf")
    first = lambda hit: jnp.min(jnp.where(hit, lane_f, float(LANES)), axis=1, keepdims=True)
    lg_g = jnp.where(lane < N_GROUPS, logits, neg)
    g_max = jnp.max(lg_g, axis=1, keepdims=True)
    g_sel = first(lg_g == g_max)
    p_sel = 1.0 / jnp.sum(jnp.exp(lg_g - g_max), axis=1, keepdims=True)
    e_lane = lane - N_GROUPS
    in_group = (e_lane >= 0) & (e_lane < N_EXPERTS) & ((e_lane // EXPERTS_PER_GROUP).astype(f32) == g_sel)
    lg_e = jnp.where(in_group, logits, neg)
    v1 = jnp.max(lg_e, axis=1, keepdims=True)
    i1 = first(lg_e == v1)
    lg_e2 = jnp.where(lane_f == i1, neg, lg_e)
    v2 = jnp.max(lg_e2, axis=1, keepdims=True)
    i2 = first(lg_e2 == v2)
    e = jnp.exp(v2 - v1)
    g1 = p_sel / (1.0 + e)
    g2 = p_sel * e / (1.0 + e)
    rec = jnp.where(lane == ROUTE_E0, i1 - N_GROUPS, 0.0)
    rec = jnp.where(lane == ROUTE_E1, i2 - N_GROUPS, rec)
    rec = jnp.where(lane == ROUTE_G0, g1, rec)
    rec = jnp.where(lane == ROUTE_G1, g2, rec)
    route_ref[...] = rec


def _outproj(oa, ob, x2d, gt, sh, sc, g_ffn, w_out, w_route, b_route, *, tm, mod_specs,
             n_total, row0, bufs=None, fill_steps=0):
    n = x2d.shape[0]
    tiles = n // tm
    row = lambda w: pl.BlockSpec((tm, w), lambda i: (jnp.minimum(i, tiles - 1), 0))
    out_row = lambda w: pl.BlockSpec((tm, w), lambda i: (row0 // tm + i, 0))
    outs = ((D_MODEL, f32), (D_MODEL // 2, u32), (LANES, f32))
    alias_in = list(bufs) if bufs is not None else []
    n_in = 10
    return pl.pallas_call(
        functools.partial(_outproj_kernel, n_alias=len(alias_in)),
        out_shape=[jax.ShapeDtypeStruct((n_total, w), dt) for w, dt in outs],
        grid=(tiles + fill_steps,),
        in_specs=[row(SWA_WIDTH), row(GLA_V_WIDTH), row(D_MODEL), mod_specs[0], mod_specs[1],
                  mod_specs[2], _resident((1, D_MODEL)), _resident(w_out.shape),
                  _resident(w_route.shape), _resident((1, LANES))]
                 + [pl.BlockSpec(memory_space=pl.ANY)] * len(alias_in),
        out_specs=[out_row(w) for w, _ in outs],
        input_output_aliases={n_in + a: a for a in range(len(alias_in))},
        compiler_params=pltpu.CompilerParams(dimension_semantics=("parallel",),
                                             vmem_limit_bytes=VMEM_LIMIT),
        name="outproj",
    )(oa, ob, x2d, gt, sh, sc, g_ffn.reshape(1, D_MODEL), w_out, w_route, b_route, *alias_in)


def _rank_kernel(route_ref, pos_ref, pend_ref, *, n_chunks, tb):
    n_e = N_EXPERTS
    expert = lax.broadcasted_iota(jnp.int32, (n_e, LANES), 0)
    expert_f = expert.astype(f32)
    earlier = (lax.broadcasted_iota(jnp.int32, (LANES, LANES), 0)
               < lax.broadcasted_iota(jnp.int32, (LANES, LANES), 1)).astype(bf16)

    def onehots(c):
        rec = route_ref[pl.ds(pl.multiple_of(c * LANES, LANES), LANES), :].T
        return expert_f == rec[ROUTE_E0:ROUTE_E0 + 1, :], expert_f == rec[ROUTE_E1:ROUTE_E1 + 1, :]

    def count(c, cnt):
        h0, h1 = onehots(c)
        return cnt + jnp.sum((h0 | h1).astype(f32), axis=1, keepdims=True)

    cnt = lax.fori_loop(0, n_chunks, count, jnp.zeros((n_e, 1), f32))
    padded = jnp.floor((cnt + (tb - 1.0)) / tb) * tb
    end = jnp.broadcast_to(padded, (n_e, LANES))
    step = 1
    while step < n_e:
        end = end + jnp.where(expert >= step, pltpu.roll(end, step, 0), 0.0)
        step *= 2
    pend_ref[...] = end.astype(jnp.int32)

    def rank(c, base):
        h0, h1 = onehots(c)
        both = h0 | h1
        off = _dot(both.astype(bf16), earlier) + base
        pos_ref[0, pl.ds(c, 1), :] = jnp.sum(jnp.where(h0, off, 0.0), axis=0, keepdims=True).astype(jnp.int32)
        pos_ref[1, pl.ds(c, 1), :] = jnp.sum(jnp.where(h1, off, 0.0), axis=0, keepdims=True).astype(jnp.int32)
        return base + jnp.sum(both.astype(f32), axis=1, keepdims=True)

    lax.fori_loop(0, n_chunks, rank, (end - padded)[:, 0:1])


def _rank(route, *, tb):
    n = route.shape[0]
    n_chunks = n // LANES
    return pl.pallas_call(
        functools.partial(_rank_kernel, n_chunks=n_chunks, tb=tb),
        out_shape=[jax.ShapeDtypeStruct((TOP_K, n_chunks, LANES), jnp.int32),
                   jax.ShapeDtypeStruct((N_EXPERTS, LANES), jnp.int32)],
        in_specs=[_resident(route.shape)],
        compiler_params=pltpu.CompilerParams(vmem_limit_bytes=VMEM_LIMIT),
        name="rank",
    )(route)


def _dispatch_kernel(pos_ref, pend_ref, h2_ref, xs_hbm, stage, zbuf, sem, zsem, *, tm, tb, n_tot, steps):
    i = pl.program_id(0)
    slot = i % 2

    def wait_slot(s):
        for _ in range(TOP_K):
            pltpu.make_async_copy(stage.at[s], xs_hbm.at[pl.ds(0, tm), :], sem.at[s]).wait()

    @pl.when(i == 0)
    def _():
        zbuf[...] = jnp.zeros(zbuf.shape, zbuf.dtype)

        def fill(e, start):
            end = pend_ref[e]
            prev = jnp.where(e > 0, pend_ref[jnp.maximum(e - 1, 0)], 0)

            @pl.when(end > prev)
            def _():
                first = pl.multiple_of(end - tb, tb)
                cp = pltpu.make_async_copy(zbuf, xs_hbm.at[pl.ds(first, tb), :], zsem)
                if start:
                    cp.start()
                else:
                    cp.wait()

        def fill_unused(b, start):
            cp = pltpu.make_async_copy(zbuf, xs_hbm.at[pl.ds(pl.multiple_of(b * tb, tb), tb), :], zsem)
            cp.start() if start else cp.wait()

        first_unused = pend_ref[N_EXPERTS - 1] // tb
        n_blocks = xs_hbm.shape[0] // tb
        lax.fori_loop(0, N_EXPERTS, lambda e, c: (fill(e, True), c)[1], 0)
        lax.fori_loop(first_unused, n_blocks, lambda b, c: (fill_unused(b, True), c)[1], 0)
        lax.fori_loop(0, N_EXPERTS, lambda e, c: (fill(e, False), c)[1], 0)
        lax.fori_loop(first_unused, n_blocks, lambda b, c: (fill_unused(b, False), c)[1], 0)

    @pl.when(i >= 2)
    def _():
        wait_slot(slot)

    stage[slot] = h2_ref[...]

    def scatter(r, carry):
        for k in range(TOP_K):
            d = pos_ref[k * n_tot + i * tm + r]
            pltpu.make_async_copy(stage.at[slot, pl.ds(r, 1), :], xs_hbm.at[pl.ds(d, 1), :],
                                  sem.at[slot]).start()
        return carry

    lax.fori_loop(0, tm, scatter, 0, unroll=8)

    @pl.when(i == steps - 1)
    def _():
        wait_slot(slot)
        if steps > 1:
            wait_slot(1 - slot)


def _dispatch(pos_flat, pend, h2p, *, tm, tb, n_blocks):
    n_tot, w = h2p.shape
    steps = n_tot // tm
    return pl.pallas_call(
        functools.partial(_dispatch_kernel, tm=tm, tb=tb, n_tot=n_tot, steps=steps),
        out_shape=jax.ShapeDtypeStruct((n_blocks * tb, w), u32),
        grid_spec=pltpu.PrefetchScalarGridSpec(
            num_scalar_prefetch=2,
            grid=(steps,),
            in_specs=[pl.BlockSpec((tm, w), lambda i, pos, pend: (i, 0))],
            out_specs=pl.BlockSpec(memory_space=pl.ANY),
            scratch_shapes=[pltpu.VMEM((2, tm, w), u32), pltpu.VMEM((tb, w), u32),
                            pltpu.SemaphoreType.DMA((2,)), pltpu.SemaphoreType.DMA(())]),
        compiler_params=pltpu.CompilerParams(dimension_semantics=("arbitrary",),
                                             vmem_limit_bytes=VMEM_LIMIT),
        name="dispatch",
    )(pos_flat, pend, h2p)


def _expert_kernel(be_ref, nu_ref, xs_ref, wg_ref, wu_ref, wd_ref, y_ref, wg_b, wu_b, wd_b):
    i = pl.program_id(0)
    used = i < nu_ref[0]

    @pl.when(used & ((i == 0) | (be_ref[i] != be_ref[jnp.maximum(i - 1, 0)])))
    def _():
        wg_b[...] = wg_ref[0].astype(bf16)
        wu_b[...] = wu_ref[0].astype(bf16)
        wd_b[...] = wd_ref[0].astype(bf16)

    @pl.when(used)
    def _():
        a, b = _unpack_pair(xs_ref[...])
        x = jnp.concatenate([a.astype(bf16), b.astype(bf16)], axis=1)
        g = _dot(x, wg_b[...])
        u = _dot(x, wu_b[...])
        y = _dot((_silu(g) * u).astype(bf16), wd_b[...])
        y_ref[...] = _pack_pair(y.astype(bf16))

    @pl.when(jnp.logical_not(used))
    def _():
        y_ref[...] = jnp.zeros(y_ref.shape, y_ref.dtype)


def _experts(block_e, n_used, xs, w_eg, w_eu, w_ed, *, tb):
    n_blocks = block_e.shape[0]
    w = xs.shape[1]
    wspec = lambda shape: pl.BlockSpec((1,) + shape, lambda i, be, nu: (be[i], 0, 0))
    return pl.pallas_call(
        _expert_kernel,
        out_shape=jax.ShapeDtypeStruct((n_blocks * tb, w), u32),
        grid_spec=pltpu.PrefetchScalarGridSpec(
            num_scalar_prefetch=2,
            grid=(n_blocks,),
            in_specs=[pl.BlockSpec((tb, w), lambda i, be, nu: (jnp.minimum(i, nu[0] - 1), 0)),
                      wspec((D_MODEL, EXPERT_HIDDEN)), wspec((D_MODEL, EXPERT_HIDDEN)),
                      wspec((EXPERT_HIDDEN, D_MODEL))],
            out_specs=pl.BlockSpec((tb, w), lambda i, be, nu: (i, 0)),
            scratch_shapes=[pltpu.VMEM((D_MODEL, EXPERT_HIDDEN), bf16),
                            pltpu.VMEM((D_MODEL, EXPERT_HIDDEN), bf16),
                            pltpu.VMEM((EXPERT_HIDDEN, D_MODEL), bf16)]),
        compiler_params=pltpu.CompilerParams(dimension_semantics=("arbitrary",),
                                             vmem_limit_bytes=VMEM_LIMIT),
        name="experts",
    )(block_e, n_used, xs, w_eg, w_eu, w_ed)


def _gather_rows(idx_ref, base, src_hbm, dst, sem, n):
    def body(r, carry):
        t = idx_ref[base + r]
        pltpu.make_async_copy(src_hbm.at[pl.ds(t, 1), :], dst.at[pl.ds(r, 1), :], sem).start()
        return carry
    lax.fori_loop(0, n, body, 0, unroll=8)


def _combine_kernel(pos_ref, yb_hbm, x1_ref, route_ref, gt_ref, gf_ref, y_ref, ybuf, sem, *, tm):
    i = pl.program_id(0)
    slot = i % 2

    def start(blk, s):
        for k in range(TOP_K):
            _gather_rows(pos_ref, (k * pl.num_programs(0) + blk) * tm, yb_hbm,
                         ybuf.at[s, k], sem.at[s], tm)

    @pl.when(i == 0)
    def _():
        start(0, 0)

    @pl.when(i + 1 < pl.num_programs(0))
    def _():
        start(i + 1, 1 - slot)

    for k in range(TOP_K):
        pltpu.make_async_copy(yb_hbm.at[pl.ds(0, tm), :], ybuf.at[slot, k], sem.at[slot]).wait()
    route = route_ref[...]
    a0, b0 = _unpack_pair(ybuf[slot, 0])
    a1, b1 = _unpack_pair(ybuf[slot, 1])
    g0 = route[:, ROUTE_G0:ROUTE_G0 + 1]
    g1 = route[:, ROUTE_G1:ROUTE_G1 + 1]
    moe = jnp.concatenate([a0 * g0 + a1 * g1, b0 * g0 + b1 * g1], axis=1)
    x2 = x1_ref[...] + _rows(gt_ref) * moe
    y_ref[...] = _rms(x2) * gf_ref[...]


def _combine(pos_km, yb, x1, route, gt, g_final, *, tm, n, row0, gt_spec):
    blk0 = row0 // tm
    w = yb.shape[1]
    return pl.pallas_call(
        functools.partial(_combine_kernel, tm=tm),
        out_shape=jax.ShapeDtypeStruct((n, D_MODEL), f32),
        grid_spec=pltpu.PrefetchScalarGridSpec(
            num_scalar_prefetch=1,
            grid=(n // tm,),
            in_specs=[pl.BlockSpec(memory_space=pl.ANY),
                      pl.BlockSpec((tm, D_MODEL), lambda i, pos: (blk0 + i, 0)),
                      pl.BlockSpec((tm, LANES), lambda i, pos: (blk0 + i, 0)),
                      gt_spec,
                      pl.BlockSpec((1, D_MODEL), lambda i, pos: (0, 0))],
            out_specs=pl.BlockSpec((tm, D_MODEL), lambda i, pos: (i, 0)),
            scratch_shapes=[pltpu.VMEM((2, TOP_K, tm, w), u32),
                            pltpu.SemaphoreType.DMA((2,))]),
        compiler_params=pltpu.CompilerParams(dimension_semantics=("arbitrary",),
                                             vmem_limit_bytes=VMEM_LIMIT),
        name="combine",
    )(pos_km, yb, x1, route, gt, g_final.reshape(1, D_MODEL))


def kernel(x_prompt, x_sample, cache_swa_k, cache_swa_v, state_gla, c_prompt, c_sample, g_mix_norm, g_ffn_norm, w_ada, b_ada, w_in, attn_sinks, w_gla_gate, b_gla_gate, g_gla_norm, w_out, w_router_group, b_router_group, w_router_expert, b_router_expert, w_expert_gate, w_expert_up, w_expert_down, g_final):
    depth = w_in.shape[0]
    assert depth == 1
    bp, t, d = x_prompt.shape
    bs, ts, _ = x_sample.shape
    n_p, n_s = bp * t, bs * ts
    n_tot = n_p + n_s
    assert n_tot % LANES == 0
    tm = 512
    to = 256
    tb = 256
    tc = 256
    td = LANES
    gla_c = 128

    w_in0 = w_in[0]
    w_main = w_in0[:, :_C_AB].astype(bf16)
    w_ab = jnp.pad(w_in0[:, _C_AB:], ((0, 0), (0, LANES - GLA_GATE_RANK))).astype(bf16)
    w_gate = jnp.pad(w_gla_gate[0], ((0, LANES - GLA_GATE_RANK), (0, 0))).astype(bf16)
    w_out_b = w_out[0].astype(bf16)
    n_r = N_GROUPS + N_EXPERTS
    w_r = jnp.pad(jnp.concatenate([w_router_group[0], w_router_expert[0]], axis=1),
                  ((0, 0), (0, LANES - n_r)))
    w_r_hi = w_r.astype(bf16)
    w_r_lo = (w_r - w_r_hi.astype(f32)).astype(bf16)
    w_route = jnp.concatenate([w_r_hi, w_r_lo], axis=1)
    b_route = jnp.pad(jnp.concatenate([b_router_group[0], b_router_expert[0]]),
                      (0, LANES - n_r)).reshape(1, LANES)

    c_all = jnp.concatenate([c_prompt, c_sample], axis=0)
    mod = _adaln(c_all, w_ada[0], b_ada[0])
    mod_tab = mod.reshape((bp + bs) * N_MOD, 1, d)
    mod_s = jnp.repeat(mod[bp:].reshape(bs, N_MOD, d), ts, axis=0)
    mod_s = [mod_s[:, m] for m in range(N_MOD)]
    pmod = lambda comp, tile: _mod_spec(comp, t // tile, n_p // tile)
    smod = pl.BlockSpec((n_s, d), lambda i: (0, 0))

    xp = x_prompt.reshape(n_p, d)
    xs = x_sample.reshape(n_s, d)
    proj_w = (g_mix_norm[0], w_main, w_ab, w_gate, b_gla_gate[0])
    qp, kp, vp, gqp, gkp, gvp, rp, lgp = _proj(xp, mod_tab, mod_tab, *proj_w, tm=tm,
                                               mod_specs=(pmod(0, tm), pmod(1, tm)))
    qs, ks, vs, gqs, gks, gvs, rs, lgs = _proj(xs, mod_s[0], mod_s[1], *proj_w, tm=n_s,
                                               mod_specs=(smod, smod))

    sinks = attn_sinks[0]
    oap = _swa(qp, kp, vp, kp, vp, sinks, n_seq=bp, tq=WINDOW, tiles=t // WINDOW, banded=True)
    ck = cache_swa_k[0].reshape(bs * WINDOW, KV_WIDTH)
    cv = cache_swa_v[0].reshape(bs * WINDOW, KV_WIDTH)
    oas = _swa(qs, ck, cv, ks, vs, sinks, n_seq=bs, tq=ts, tiles=1, banded=False)
    s_zero = jnp.zeros((bp, GLA_HEADS, GLA_DK, GLA_DV), f32)
    obp, sp = _gla(gqp, gkp, gvp, lgp, rp, g_gla_norm[0], s_zero, n_seq=bp, c=gla_c, chunks=t // gla_c)
    obs, ss = _gla(gqs, gks, gvs, lgs, rs, g_gla_norm[0], state_gla[0], n_seq=bs, c=ts, chunks=1)

    out_w = (g_ffn_norm[0], w_out_b, w_route, b_route)
    bufs = _outproj(oap, obp, xp, mod_tab, mod_tab, mod_tab, *out_w, tm=to,
                    mod_specs=(pmod(2, to), pmod(3, to), pmod(4, to)), n_total=n_tot, row0=0,
                    fill_steps=-(-n_s // to))
    x1, h2p, route = _outproj(oas, obs, xs, mod_s[2], mod_s[3], mod_s[4], *out_w, tm=n_s,
                              mod_specs=(smod, smod, smod), n_total=n_tot, row0=n_p, bufs=bufs)

    n_blocks = -(-(n_tot * TOP_K + N_EXPERTS * (tb - 1)) // tb)
    pos, pend_tab = _rank(route, tb=tb)
    pend = pend_tab[:, 0]
    block_e = jnp.minimum(jnp.sum(pend[None, :] <= (jnp.arange(n_blocks, dtype=jnp.int32) * tb)[:, None],
                                  axis=1), N_EXPERTS - 1).astype(jnp.int32)
    n_used = pend[N_EXPERTS - 1:] // tb
    xsort = _dispatch(pos.reshape(-1), pend, h2p, tm=td, tb=tb, n_blocks=n_blocks)
    yb = _experts(block_e, n_used, xsort, w_expert_gate[0], w_expert_up[0], w_expert_down[0], tb=tb)

    pos = pos.reshape(TOP_K, n_tot)
    pos_p = pos[:, :n_p].reshape(-1)
    pos_s = pos[:, n_p:].reshape(-1)
    gt_p = pl.BlockSpec((1, 1, d), lambda i, p: ((i // (t // tc)) * N_MOD + 5, 0, 0))
    gt_s = pl.BlockSpec((n_s, d), lambda i, p: (0, 0))
    y_p = _combine(pos_p, yb, x1, route, mod_tab, g_final, tm=tc, n=n_p, row0=0, gt_spec=gt_p)
    y_s = _combine(pos_s, yb, x1, route, mod_s[5], g_final, tm=n_s, n=n_s, row0=n_p, gt_spec=gt_s)

    kv_shape = (SWA_KV_HEADS, SWA_HEAD_DIM)
    k_state_p = kp.reshape(bp, t, *kv_shape)[:, -WINDOW:][None]
    v_state_p = vp.reshape(bp, t, *kv_shape)[:, -WINDOW:][None]
    return (y_p.reshape(bp, t, d), y_s.reshape(bs, ts, d), k_state_p, v_state_p, sp[None],
            ks.reshape(bs, ts, *kv_shape)[None], vs.reshape(bs, ts, *kv_shape)[None], ss[None])
```

```python
import functools

import jax
import jax.numpy as jnp
from jax import lax
from jax.experimental import pallas as pl
from jax.experimental.pallas import tpu as pltpu

f32 = jnp.float32
bf16 = jnp.bfloat16
u32 = jnp.uint32

D_MODEL = 2048
N_MOD = 6
EPS = 1e-6
NEG_INF = -1e30

SWA_HEAD_DIM = 64
SWA_KV_HEADS = 2
SWA_GROUP = 8
SWA_WIDTH = SWA_KV_HEADS * SWA_GROUP * SWA_HEAD_DIM
KV_WIDTH = SWA_KV_HEADS * SWA_HEAD_DIM
WINDOW = 128
CHUNK = 64

GLA_HEADS = 4
GLA_DK = 128
GLA_DV = 256
GLA_QK_WIDTH = GLA_HEADS * GLA_DK
GLA_V_WIDTH = GLA_HEADS * GLA_DV
GLA_GATE_RANK = 16
GLA_GATE_NORM = 16.0

N_GROUPS = 4
EXPERTS_PER_GROUP = 8
N_EXPERTS = N_GROUPS * EXPERTS_PER_GROUP
TOP_K = 2
EXPERT_HIDDEN = D_MODEL // 4

_C_Q = 0
_C_K = _C_Q + SWA_WIDTH
_C_V = _C_K + KV_WIDTH
_C_GQ = _C_V + KV_WIDTH
_C_GK = _C_GQ + GLA_QK_WIDTH
_C_GV = _C_GK + GLA_QK_WIDTH
_C_R = _C_GV + GLA_V_WIDTH
_C_AB = _C_R + GLA_V_WIDTH

LANES = 128
VMEM_LIMIT = 56 * 1024 * 1024

ROUTE_E0, ROUTE_E1, ROUTE_G0, ROUTE_G1 = 0, 1, 2, 3


def _dot(a, b):
    return jnp.dot(a, b, preferred_element_type=f32)


def _dot_nt(a, b):
    return lax.dot_general(a, b, (((1,), (1,)), ((), ())), preferred_element_type=f32)


def _dot_tn(a, b):
    return lax.dot_general(a, b, (((0,), (0,)), ((), ())), preferred_element_type=f32)


def _silu(x):
    return x / (1.0 + jnp.exp(-x))


def _rows(ref):
    v = ref[...]
    return v.reshape(v.shape[-2:])


def _rms(x):
    return x * lax.rsqrt(jnp.mean(x * x, axis=-1, keepdims=True) + EPS)


def _resident(shape):
    return pl.BlockSpec(shape, lambda *_: (0,) * len(shape), pipeline_mode=pl.Buffered(1))


def _mod_spec(comp, tiles_per_seq, n_tiles):
    return pl.BlockSpec((1, 1, D_MODEL),
                        lambda i: ((jnp.minimum(i, n_tiles - 1) // tiles_per_seq) * N_MOD + comp, 0, 0))


def _adaln_kernel(c_ref, w_ref, b_ref, o_ref):
    a = _silu(c_ref[...]).astype(bf16)
    o_ref[...] = _dot(a, w_ref[...].astype(bf16)) + b_ref[...]


def _adaln(c_all, w_ada, b_ada, tn=1024):
    r = c_all.shape[0]
    n = w_ada.shape[1]
    return pl.pallas_call(
        _adaln_kernel,
        out_shape=jax.ShapeDtypeStruct((r, n), f32),
        grid=(n // tn,),
        in_specs=[pl.BlockSpec((r, D_MODEL), lambda j: (0, 0)),
                  pl.BlockSpec((D_MODEL, tn), lambda j: (0, j)),
                  pl.BlockSpec((1, tn), lambda j: (0, j))],
        out_specs=pl.BlockSpec((r, tn), lambda j: (0, j)),
        compiler_params=pltpu.CompilerParams(dimension_semantics=("arbitrary",),
                                             vmem_limit_bytes=VMEM_LIMIT),
        name="adaln",
    )(c_all, w_ada, b_ada.reshape(1, n))


def _proj_kernel(x_ref, sh_ref, sc_ref, g_ref, wq_ref, w_ref, wab_ref, wg_ref, bg_ref,
                 q_ref, k_ref, v_ref, gq_ref, gk_ref, gv_ref, r_ref, lg_ref, *, q_transposed):
    h = (_rms(x_ref[...]) * g_ref[...]) * (1.0 + _rows(sc_ref)) + _rows(sh_ref)
    hb = h.astype(bf16)
    if q_transposed:
        q_ref[...] = (_dot_nt(wq_ref[...], hb) * (SWA_HEAD_DIM ** -0.5)).astype(bf16)
    else:
        q_ref[...] = _dot(hb, wq_ref[...]).astype(bf16)
    col = lambda c: c - _C_K
    kv = _dot(hb, w_ref[:, col(_C_K):col(_C_GQ)])
    k_ref[...] = kv[:, :KV_WIDTH]
    v_ref[...] = kv[:, KV_WIDTH:]
    gq_ref[...] = _dot(hb, w_ref[:, col(_C_GQ):col(_C_GK)]).astype(bf16)
    gk_ref[...] = _dot(hb, w_ref[:, col(_C_GK):col(_C_GV)]).astype(bf16)
    gv_ref[...] = _dot(hb, w_ref[:, col(_C_GV):col(_C_R)]).astype(bf16)
    r_ref[...] = _dot(hb, w_ref[:, col(_C_R):col(_C_AB)]).astype(bf16)
    ab = _dot(hb, wab_ref[...])
    z = _dot(ab.astype(bf16), wg_ref[...]) + bg_ref[...]
    log_sig = jnp.minimum(z, 0.0) - jnp.log1p(jnp.exp(-jnp.abs(z)))
    lg_ref[...] = log_sig / GLA_GATE_NORM


def _proj(x2d, sh, sc, g_mix, w_q, w_rest, w_ab, w_gate, b_gate, *, tm, mod_specs, q_transposed):
    n = x2d.shape[0]
    row = lambda w: pl.BlockSpec((tm, w), lambda i: (i, 0))
    outs = [(KV_WIDTH, f32), (KV_WIDTH, f32), (GLA_QK_WIDTH, bf16),
            (GLA_QK_WIDTH, bf16), (GLA_V_WIDTH, bf16), (GLA_V_WIDTH, bf16), (GLA_QK_WIDTH, f32)]
    if q_transposed:
        q_shape, q_spec = (SWA_WIDTH, n), pl.BlockSpec((SWA_WIDTH, tm), lambda i: (0, i))
    else:
        q_shape, q_spec = (n, SWA_WIDTH), row(SWA_WIDTH)
    return pl.pallas_call(
        functools.partial(_proj_kernel, q_transposed=q_transposed),
        out_shape=[jax.ShapeDtypeStruct(q_shape, bf16)]
                  + [jax.ShapeDtypeStruct((n, w), dt) for w, dt in outs],
        grid=(n // tm,),
        in_specs=[row(D_MODEL), mod_specs[0], mod_specs[1],
                  _resident((1, D_MODEL)), _resident(w_q.shape), _resident(w_rest.shape),
                  _resident(w_ab.shape), _resident(w_gate.shape), _resident((1, GLA_QK_WIDTH))],
        out_specs=[q_spec] + [row(w) for w, _ in outs],
        compiler_params=pltpu.CompilerParams(dimension_semantics=("parallel",),
                                             vmem_limit_bytes=VMEM_LIMIT),
        name="proj",
    )(x2d, sh, sc, g_mix.reshape(1, D_MODEL), w_q, w_rest, w_ab, w_gate, b_gate.reshape(1, -1))


def _swa_kernel(q_ref, kp_ref, vp_ref, kc_ref, vc_ref, sink_ref, o_ref, *, tq, banded):
    u = pl.program_id(1)
    kc, vc = kc_ref[...], vc_ref[...]
    if tq < WINDOW:
        pad = jnp.zeros((WINDOW - tq, KV_WIDTH), f32)
        kc = jnp.concatenate([kc, pad], axis=0)
        vc = jnp.concatenate([vc, pad], axis=0)
    k_all = jnp.concatenate([kp_ref[...], kc], axis=0).astype(bf16)
    v_all = jnp.concatenate([vp_ref[...], vc], axis=0).astype(bf16)
    rows, cols = SWA_GROUP * tq, 2 * WINDOW
    col = lax.broadcasted_iota(jnp.int32, (rows, cols), 1)
    if banded:
        q_chunk = (lax.broadcasted_iota(jnp.int32, (rows, cols), 0) % tq) // CHUNK
        k_chunk = col // CHUNK
        valid = (k_chunk >= q_chunk) & (k_chunk <= q_chunk + WINDOW // CHUNK)
        valid = valid & ((col >= WINDOW) | (u > 0))
    else:
        valid = col < WINDOW + tq
    q = q_ref[...]
    outs = []
    for j in range(SWA_KV_HEADS):
        heads = [q[:, (j * SWA_GROUP + g) * SWA_HEAD_DIM:(j * SWA_GROUP + g + 1) * SWA_HEAD_DIM]
                 for g in range(SWA_GROUP)]
        qs = jnp.concatenate(heads, axis=0)
        kj = k_all[:, j * SWA_HEAD_DIM:(j + 1) * SWA_HEAD_DIM]
        vj = v_all[:, j * SWA_HEAD_DIM:(j + 1) * SWA_HEAD_DIM]
        s = _dot_nt(qs, kj) * (SWA_HEAD_DIM ** -0.5)
        s = jnp.where(valid, s, NEG_INF)
        sink = sink_ref[j]
        m = jnp.maximum(jnp.max(s, axis=1, keepdims=True), sink)
        p = jnp.exp(s - m)
        den = jnp.sum(p, axis=1, keepdims=True) + jnp.exp(sink - m)
        o = _dot(p.astype(bf16), vj) / den
        outs.append(jnp.concatenate([o[g * tq:(g + 1) * tq] for g in range(SWA_GROUP)], axis=1))
    o_ref[...] = jnp.concatenate(outs, axis=1).astype(bf16)


def _swa(q, k_prev, v_prev, k_cur, v_cur, sinks, *, n_seq, tq, tiles, banded):
    sink_rows = jnp.repeat(sinks.astype(f32).reshape(SWA_KV_HEADS, SWA_GROUP), tq, axis=1)
    sink_rows = sink_rows.reshape(SWA_KV_HEADS, SWA_GROUP * tq, 1)
    cur = lambda w: pl.BlockSpec((tq, w), lambda b, u: (b * tiles + u, 0))
    if banded:
        prev = pl.BlockSpec((WINDOW, KV_WIDTH), lambda b, u: (b * tiles + jnp.maximum(u - 1, 0), 0))
    else:
        prev = pl.BlockSpec((WINDOW, KV_WIDTH), lambda b, u: (b, 0))
    return pl.pallas_call(
        functools.partial(_swa_kernel, tq=tq, banded=banded),
        out_shape=jax.ShapeDtypeStruct(q.shape, bf16),
        grid=(n_seq, tiles),
        in_specs=[cur(SWA_WIDTH), prev, prev, cur(KV_WIDTH), cur(KV_WIDTH),
                  pl.BlockSpec(sink_rows.shape, lambda b, u: (0, 0, 0))],
        out_specs=cur(SWA_WIDTH),
        compiler_params=pltpu.CompilerParams(dimension_semantics=("parallel", "arbitrary"),
                                             vmem_limit_bytes=VMEM_LIMIT),
        name="swa",
    )(q, k_prev, v_prev, k_cur, v_cur, sink_rows)


def _swa_t_kernel(q_ref, kp_ref, vp_ref, kc_ref, vc_ref, sink_ref, o_ref):
    u = pl.program_id(1)
    tq = WINDOW
    k_all = jnp.concatenate([kp_ref[...], kc_ref[...]], axis=0).astype(bf16)
    v_all = jnp.concatenate([vp_ref[...], vc_ref[...]], axis=0).astype(bf16)
    rows, cols = 2 * WINDOW, SWA_GROUP * tq
    key = lax.broadcasted_iota(jnp.int32, (rows, cols), 0)
    q_chunk = (lax.broadcasted_iota(jnp.int32, (rows, cols), 1) % tq) // CHUNK
    k_chunk = key // CHUNK
    valid = (k_chunk >= q_chunk) & (k_chunk <= q_chunk + WINDOW // CHUNK)
    valid = valid & ((key >= WINDOW) | (u > 0))
    for j in range(SWA_KV_HEADS):
        head = lambda g: slice((j * SWA_GROUP + g) * SWA_HEAD_DIM, (j * SWA_GROUP + g + 1) * SWA_HEAD_DIM)
        qs = jnp.concatenate([q_ref[head(g), :] for g in range(SWA_GROUP)], axis=1)
        kj = k_all[:, j * SWA_HEAD_DIM:(j + 1) * SWA_HEAD_DIM]
        vj = v_all[:, j * SWA_HEAD_DIM:(j + 1) * SWA_HEAD_DIM]
        s = jnp.where(valid, _dot(kj, qs), NEG_INF)
        sink = sink_ref[j]
        m = jnp.maximum(jnp.max(s, axis=0, keepdims=True), sink)
        p = jnp.exp(s - m)
        den = jnp.sum(p, axis=0, keepdims=True) + jnp.exp(sink - m)
        o = _dot_tn(vj, p.astype(bf16)) / den
        for g in range(SWA_GROUP):
            o_ref[head(g), :] = o[:, g * tq:(g + 1) * tq].astype(bf16)


def _swa_t(q_t, k, v, sinks, *, n_seq, tiles):
    tq = WINDOW
    sink_cols = jnp.repeat(sinks.astype(f32).reshape(SWA_KV_HEADS, SWA_GROUP), tq, axis=1)
    sink_cols = sink_cols.reshape(SWA_KV_HEADS, 1, SWA_GROUP * tq)
    qspec = pl.BlockSpec((SWA_WIDTH, tq), lambda b, u: (0, b * tiles + u))
    cur = pl.BlockSpec((tq, KV_WIDTH), lambda b, u: (b * tiles + u, 0))
    prev = pl.BlockSpec((WINDOW, KV_WIDTH), lambda b, u: (b * tiles + jnp.maximum(u - 1, 0), 0))
    return pl.pallas_call(
        _swa_t_kernel,
        out_shape=jax.ShapeDtypeStruct(q_t.shape, bf16),
        grid=(n_seq, tiles),
        in_specs=[qspec, prev, prev, cur, cur, pl.BlockSpec(sink_cols.shape, lambda b, u: (0, 0, 0))],
        out_specs=qspec,
        compiler_params=pltpu.CompilerParams(dimension_semantics=("parallel", "arbitrary"),
                                             vmem_limit_bytes=VMEM_LIMIT),
        name="swa_t",
    )(q_t, k, v, k, v, sink_cols)


def _gla_kernel(q_ref, k_ref, v_ref, lg_ref, r_ref, gh_ref, s0_ref, o_ref, s_ref, *, c):
    @pl.when(pl.program_id(1) == 0)
    def _():
        s_ref[...] = s0_ref[...]

    row = lax.broadcasted_iota(jnp.int32, (c, GLA_DK), 0)
    causal = (lax.broadcasted_iota(jnp.int32, (c, c), 0) >= lax.broadcasted_iota(jnp.int32, (c, c), 1))
    for h in range(GLA_HEADS):
        ks = slice(h * GLA_DK, (h + 1) * GLA_DK)
        vs = slice(h * GLA_DV, (h + 1) * GLA_DV)
        b = lg_ref[:, ks]
        step = 1
        while step < c:
            b = b + jnp.where(row >= step, pltpu.roll(b, step, 0), 0.0)
            step *= 2
        b_last = b[c - 1:c, :]
        q = q_ref[:, ks].astype(f32) * (GLA_DK ** -0.5)
        k = k_ref[:, ks].astype(f32)
        v = v_ref[:, vs]
        qd = (q * jnp.exp(b)).astype(bf16)
        kd = (k * jnp.exp(-b)).astype(bf16)
        kl = (k * jnp.exp(b_last - b)).astype(bf16)
        a = jnp.where(causal, _dot_nt(qd, kd), 0.0)
        s = s_ref[0, h]
        o = _dot(qd, s.astype(bf16)) + _dot(a.astype(bf16), v)
        decay = jnp.broadcast_to(jnp.exp(b_last), (GLA_DK, GLA_DK)).T
        s_ref[0, h] = s * jnp.concatenate([decay, decay], axis=1) + _dot_tn(kl, v)
        on = _rms(o) * gh_ref[...]
        o_ref[:, vs] = (on * _silu(r_ref[:, vs].astype(f32))).astype(bf16)


def _gla(gq, gk, gv, lg, r, g_head, s0, *, n_seq, c, chunks):
    blk = lambda w: pl.BlockSpec((c, w), lambda b, t: (b * chunks + t, 0))
    state = pl.BlockSpec((1, GLA_HEADS, GLA_DK, GLA_DV), lambda b, t: (b, 0, 0, 0))
    return pl.pallas_call(
        functools.partial(_gla_kernel, c=c),
        out_shape=[jax.ShapeDtypeStruct(gv.shape, bf16),
                   jax.ShapeDtypeStruct((n_seq, GLA_HEADS, GLA_DK, GLA_DV), f32)],
        grid=(n_seq, chunks),
        in_specs=[blk(GLA_QK_WIDTH), blk(GLA_QK_WIDTH), blk(GLA_V_WIDTH), blk(GLA_QK_WIDTH),
                  blk(GLA_V_WIDTH), pl.BlockSpec((1, GLA_DV), lambda b, t: (0, 0)), state],
        out_specs=[blk(GLA_V_WIDTH), state],
        compiler_params=pltpu.CompilerParams(dimension_semantics=("parallel", "arbitrary"),
                                             vmem_limit_bytes=VMEM_LIMIT),
        name="gla",
    )(gq, gk, gv, lg, r, g_head.reshape(1, GLA_DV), s0)


def _pack_pair(hb):
    w = hb.shape[1] // 2
    a = lax.bitcast_convert_type(hb[:, :w].astype(f32), u32)
    b = lax.bitcast_convert_type(hb[:, w:].astype(f32), u32)
    return a | (b >> 16)


def _unpack_pair(p):
    a = lax.bitcast_convert_type(p & jnp.uint32(0xFFFF0000), f32)
    b = lax.bitcast_convert_type(p << 16, f32)
    return a, b


def _outproj_kernel(*refs, n_alias, oa_transposed):
    (oa_ref, ob_ref, x_ref, gt_ref, sh_ref, sc_ref, g_ref, w_ref, wr_ref, br_ref) = refs[:10]
    x1_ref, h2_ref, route_ref = refs[10 + n_alias:]
    oa_dot = _dot_tn if oa_transposed else _dot
    mix = oa_dot(oa_ref[...], w_ref[:SWA_WIDTH, :]) + _dot(ob_ref[...], w_ref[SWA_WIDTH:, :])
    x1 = x_ref[...] + _rows(gt_ref) * mix
    x1_ref[...] = x1
    h2 = (_rms(x1) * g_ref[...]) * (1.0 + _rows(sc_ref)) + _rows(sh_ref)
    hi = h2.astype(bf16)
    h2_ref[...] = _pack_pair(hi)
    lo = (h2 - hi.astype(f32)).astype(bf16)
    r1 = _dot(hi, wr_ref[...])
    logits = r1[:, :LANES] + r1[:, LANES:] + _dot(lo, wr_ref[:, :LANES]) + br_ref[...]
    lane = lax.broadcasted_iota(jnp.int32, logits.shape, 1)
    lane_f = lane.astype(f32)
    neg = float("-inf")
    first = lambda hit: jnp.min(jnp.where(hit, lane_f, float(LANES)), axis=1, keepdims=True)
    lg_g = jnp.where(lane < N_GROUPS, logits, neg)
    g_max = jnp.max(lg_g, axis=1, keepdims=True)
    g_sel = first(lg_g == g_max)
    p_sel = 1.0 / jnp.sum(jnp.exp(lg_g - g_max), axis=1, keepdims=True)
    e_lane = lane - N_GROUPS
    in_group = (e_lane >= 0) & (e_lane < N_EXPERTS) & ((e_lane // EXPERTS_PER_GROUP).astype(f32) == g_sel)
    lg_e = jnp.where(in_group, logits, neg)
    v1 = jnp.max(lg_e, axis=1, keepdims=True)
    i1 = first(lg_e == v1)
    lg_e2 = jnp.where(lane_f == i1, neg, lg_e)
    v2 = jnp.max(lg_e2, axis=1, keepdims=True)
    i2 = first(lg_e2 == v2)
    e = jnp.exp(v2 - v1)
    g1 = p_sel / (1.0 + e)
    g2 = p_sel * e / (1.0 + e)
    rec = jnp.where(lane == ROUTE_E0, i1 - N_GROUPS, 0.0)
    rec = jnp.where(lane == ROUTE_E1, i2 - N_GROUPS, rec)
    rec = jnp.where(lane == ROUTE_G0, g1, rec)
    rec = jnp.where(lane == ROUTE_G1, g2, rec)
    route_ref[...] = rec


def _outproj(oa, ob, x2d, gt, sh, sc, g_ffn, w_out, w_route, b_route, *, tm, mod_specs,
             n_total, row0, bufs=None, fill_steps=0, oa_transposed=False):
    n = x2d.shape[0]
    tiles = n // tm
    row = lambda w: pl.BlockSpec((tm, w), lambda i: (jnp.minimum(i, tiles - 1), 0))
    oa_spec = row(SWA_WIDTH)
    if oa_transposed:
        oa_spec = pl.BlockSpec((SWA_WIDTH, tm), lambda i: (0, jnp.minimum(i, tiles - 1)))
    out_row = lambda w: pl.BlockSpec((tm, w), lambda i: (row0 // tm + i, 0))
    outs = ((D_MODEL, f32), (D_MODEL // 2, u32), (LANES, f32))
    alias_in = list(bufs) if bufs is not None else []
    n_in = 10
    return pl.pallas_call(
        functools.partial(_outproj_kernel, n_alias=len(alias_in), oa_transposed=oa_transposed),
        out_shape=[jax.ShapeDtypeStruct((n_total, w), dt) for w, dt in outs],
        grid=(tiles + fill_steps,),
        in_specs=[oa_spec, row(GLA_V_WIDTH), row(D_MODEL), mod_specs[0], mod_specs[1],
                  mod_specs[2], _resident((1, D_MODEL)), _resident(w_out.shape),
                  _resident(w_route.shape), _resident((1, LANES))]
                 + [pl.BlockSpec(memory_space=pl.ANY)] * len(alias_in),
        out_specs=[out_row(w) for w, _ in outs],
        input_output_aliases={n_in + a: a for a in range(len(alias_in))},
        compiler_params=pltpu.CompilerParams(dimension_semantics=("parallel",),
                                             vmem_limit_bytes=VMEM_LIMIT),
        name="outproj",
    )(oa, ob, x2d, gt, sh, sc, g_ffn.reshape(1, D_MODEL), w_out, w_route, b_route, *alias_in)


def _rank_kernel(route_ref, pos_ref, pend_ref, *, n_chunks, tb):
    n_e = N_EXPERTS
    expert = lax.broadcasted_iota(jnp.int32, (n_e, LANES), 0)
    expert_f = expert.astype(f32)
    earlier = (lax.broadcasted_iota(jnp.int32, (LANES, LANES), 0)
               < lax.broadcasted_iota(jnp.int32, (LANES, LANES), 1)).astype(bf16)

    def onehots(c):
        rec = route_ref[pl.ds(pl.multiple_of(c * LANES, LANES), LANES), :].T
        return expert_f == rec[ROUTE_E0:ROUTE_E0 + 1, :], expert_f == rec[ROUTE_E1:ROUTE_E1 + 1, :]

    def count(c, cnt):
        h0, h1 = onehots(c)
        return cnt + jnp.sum((h0 | h1).astype(f32), axis=1, keepdims=True)

    cnt = lax.fori_loop(0, n_chunks, count, jnp.zeros((n_e, 1), f32))
    padded = jnp.floor((cnt + (tb - 1.0)) / tb) * tb
    end = jnp.broadcast_to(padded, (n_e, LANES))
    step = 1
    while step < n_e:
        end = end + jnp.where(expert >= step, pltpu.roll(end, step, 0), 0.0)
        step *= 2
    pend_ref[...] = end.astype(jnp.int32)

    def rank(c, base):
        h0, h1 = onehots(c)
        both = h0 | h1
        off = _dot(both.astype(bf16), earlier) + base
        pos_ref[0, pl.ds(c, 1), :] = jnp.sum(jnp.where(h0, off, 0.0), axis=0, keepdims=True).astype(jnp.int32)
        pos_ref[1, pl.ds(c, 1), :] = jnp.sum(jnp.where(h1, off, 0.0), axis=0, keepdims=True).astype(jnp.int32)
        return base + jnp.sum(both.astype(f32), axis=1, keepdims=True)

    lax.fori_loop(0, n_chunks, rank, (end - padded)[:, 0:1])


def _rank(route, *, tb):
    n = route.shape[0]
    n_chunks = n // LANES
    return pl.pallas_call(
        functools.partial(_rank_kernel, n_chunks=n_chunks, tb=tb),
        out_shape=[jax.ShapeDtypeStruct((TOP_K, n_chunks, LANES), jnp.int32),
                   jax.ShapeDtypeStruct((N_EXPERTS, LANES), jnp.int32)],
        in_specs=[_resident(route.shape)],
        compiler_params=pltpu.CompilerParams(vmem_limit_bytes=VMEM_LIMIT),
        name="rank",
    )(route)


def _dispatch_kernel(pos_ref, pend_ref, h2_ref, xs_hbm, stage, zbuf, sem, zsem, *, tm, tb, n_tot, steps):
    i = pl.program_id(0)
    slot = i % 2

    def wait_slot(s):
        for _ in range(TOP_K):
            pltpu.make_async_copy(stage.at[s], xs_hbm.at[pl.ds(0, tm), :], sem.at[s]).wait()

    @pl.when(i == 0)
    def _():
        zbuf[...] = jnp.zeros(zbuf.shape, zbuf.dtype)

        def fill(e, start):
            end = pend_ref[e]
            prev = jnp.where(e > 0, pend_ref[jnp.maximum(e - 1, 0)], 0)

            @pl.when(end > prev)
            def _():
                first = pl.multiple_of(end - tb, tb)
                cp = pltpu.make_async_copy(zbuf, xs_hbm.at[pl.ds(first, tb), :], zsem)
                if start:
                    cp.start()
                else:
                    cp.wait()

        def fill_unused(b, start):
            cp = pltpu.make_async_copy(zbuf, xs_hbm.at[pl.ds(pl.multiple_of(b * tb, tb), tb), :], zsem)
            cp.start() if start else cp.wait()

        first_unused = pend_ref[N_EXPERTS - 1] // tb
        n_blocks = xs_hbm.shape[0] // tb
        lax.fori_loop(0, N_EXPERTS, lambda e, c: (fill(e, True), c)[1], 0)
        lax.fori_loop(first_unused, n_blocks, lambda b, c: (fill_unused(b, True), c)[1], 0)
        lax.fori_loop(0, N_EXPERTS, lambda e, c: (fill(e, False), c)[1], 0)
        lax.fori_loop(first_unused, n_blocks, lambda b, c: (fill_unused(b, False), c)[1], 0)

    @pl.when(i >= 2)
    def _():
        wait_slot(slot)

    stage[slot] = h2_ref[...]

    def scatter(r, carry):
        for k in range(TOP_K):
            d = pos_ref[k * n_tot + i * tm + r]
            pltpu.make_async_copy(stage.at[slot, pl.ds(r, 1), :], xs_hbm.at[pl.ds(d, 1), :],
                                  sem.at[slot]).start()
        return carry

    lax.fori_loop(0, tm, scatter, 0, unroll=8)

    @pl.when(i == steps - 1)
    def _():
        wait_slot(slot)
        if steps > 1:
            wait_slot(1 - slot)


def _dispatch(pos_flat, pend, h2p, *, tm, tb, n_blocks):
    n_tot, w = h2p.shape
    steps = n_tot // tm
    return pl.pallas_call(
        functools.partial(_dispatch_kernel, tm=tm, tb=tb, n_tot=n_tot, steps=steps),
        out_shape=jax.ShapeDtypeStruct((n_blocks * tb, w), u32),
        grid_spec=pltpu.PrefetchScalarGridSpec(
            num_scalar_prefetch=2,
            grid=(steps,),
            in_specs=[pl.BlockSpec((tm, w), lambda i, pos, pend: (i, 0))],
            out_specs=pl.BlockSpec(memory_space=pl.ANY),
            scratch_shapes=[pltpu.VMEM((2, tm, w), u32), pltpu.VMEM((tb, w), u32),
                            pltpu.SemaphoreType.DMA((2,)), pltpu.SemaphoreType.DMA(())]),
        compiler_params=pltpu.CompilerParams(dimension_semantics=("arbitrary",),
                                             vmem_limit_bytes=VMEM_LIMIT),
        name="dispatch",
    )(pos_flat, pend, h2p)


def _expert_kernel(be_ref, nu_ref, xs_ref, wg_ref, wu_ref, wd_ref, y_ref, wg_b, wu_b, wd_b):
    i = pl.program_id(0)
    used = i < nu_ref[0]

    @pl.when(used & ((i == 0) | (be_ref[i] != be_ref[jnp.maximum(i - 1, 0)])))
    def _():
        wg_b[...] = wg_ref[0].astype(bf16)
        wu_b[...] = wu_ref[0].astype(bf16)
        wd_b[...] = wd_ref[0].astype(bf16)

    @pl.when(used)
    def _():
        a, b = _unpack_pair(xs_ref[...])
        x = jnp.concatenate([a.astype(bf16), b.astype(bf16)], axis=1)
        g = _dot(x, wg_b[...])
        u = _dot(x, wu_b[...])
        y = _dot((_silu(g) * u).astype(bf16), wd_b[...])
        y_ref[...] = _pack_pair(y.astype(bf16))

    @pl.when(jnp.logical_not(used))
    def _():
        y_ref[...] = jnp.zeros(y_ref.shape, y_ref.dtype)


def _experts(block_e, n_used, xs, w_eg, w_eu, w_ed, *, tb):
    n_blocks = block_e.shape[0]
    w = xs.shape[1]
    wspec = lambda shape: pl.BlockSpec((1,) + shape, lambda i, be, nu: (be[i], 0, 0))
    return pl.pallas_call(
        _expert_kernel,
        out_shape=jax.ShapeDtypeStruct((n_blocks * tb, w), u32),
        grid_spec=pltpu.PrefetchScalarGridSpec(
            num_scalar_prefetch=2,
            grid=(n_blocks,),
            in_specs=[pl.BlockSpec((tb, w), lambda i, be, nu: (jnp.minimum(i, nu[0] - 1), 0)),
                      wspec((D_MODEL, EXPERT_HIDDEN)), wspec((D_MODEL, EXPERT_HIDDEN)),
                      wspec((EXPERT_HIDDEN, D_MODEL))],
            out_specs=pl.BlockSpec((tb, w), lambda i, be, nu: (i, 0)),
            scratch_shapes=[pltpu.VMEM((D_MODEL, EXPERT_HIDDEN), bf16),
                            pltpu.VMEM((D_MODEL, EXPERT_HIDDEN), bf16),
                            pltpu.VMEM((EXPERT_HIDDEN, D_MODEL), bf16)]),
        compiler_params=pltpu.CompilerParams(dimension_semantics=("arbitrary",),
                                             vmem_limit_bytes=VMEM_LIMIT),
        name="experts",
    )(block_e, n_used, xs, w_eg, w_eu, w_ed)


def _gather_rows(idx_ref, base, src_hbm, dst, sem, n):
    def body(r, carry):
        t = idx_ref[base + r]
        pltpu.make_async_copy(src_hbm.at[pl.ds(t, 1), :], dst.at[pl.ds(r, 1), :], sem).start()
        return carry
    lax.fori_loop(0, n, body, 0, unroll=8)


def _combine_kernel(pos_ref, yb_hbm, x1_ref, route_ref, gt_ref, gf_ref, y_ref, ybuf, sem, *, tm):
    i = pl.program_id(0)
    slot = i % 2

    def start(blk, s):
        for k in range(TOP_K):
            _gather_rows(pos_ref, (k * pl.num_programs(0) + blk) * tm, yb_hbm,
                         ybuf.at[s, k], sem.at[s], tm)

    @pl.when(i == 0)
    def _():
        start(0, 0)

    @pl.when(i + 1 < pl.num_programs(0))
    def _():
        start(i + 1, 1 - slot)

    for k in range(TOP_K):
        pltpu.make_async_copy(yb_hbm.at[pl.ds(0, tm), :], ybuf.at[slot, k], sem.at[slot]).wait()
    route = route_ref[...]
    a0, b0 = _unpack_pair(ybuf[slot, 0])
    a1, b1 = _unpack_pair(ybuf[slot, 1])
    g0 = route[:, ROUTE_G0:ROUTE_G0 + 1]
    g1 = route[:, ROUTE_G1:ROUTE_G1 + 1]
    moe = jnp.concatenate([a0 * g0 + a1 * g1, b0 * g0 + b1 * g1], axis=1)
    x2 = x1_ref[...] + _rows(gt_ref) * moe
    y_ref[...] = _rms(x2) * gf_ref[...]


def _combine(pos_km, yb, x1, route, gt, g_final, *, tm, n, row0, gt_spec):
    blk0 = row0 // tm
    w = yb.shape[1]
    return pl.pallas_call(
        functools.partial(_combine_kernel, tm=tm),
        out_shape=jax.ShapeDtypeStruct((n, D_MODEL), f32),
        grid_spec=pltpu.PrefetchScalarGridSpec(
            num_scalar_prefetch=1,
            grid=(n // tm,),
            in_specs=[pl.BlockSpec(memory_space=pl.ANY),
                      pl.BlockSpec((tm, D_MODEL), lambda i, pos: (blk0 + i, 0)),
                      pl.BlockSpec((tm, LANES), lambda i, pos: (blk0 + i, 0)),
                      gt_spec,
                      pl.BlockSpec((1, D_MODEL), lambda i, pos: (0, 0))],
            out_specs=pl.BlockSpec((tm, D_MODEL), lambda i, pos: (i, 0)),
            scratch_shapes=[pltpu.VMEM((2, TOP_K, tm, w), u32),
                            pltpu.SemaphoreType.DMA((2,))]),
        compiler_params=pltpu.CompilerParams(dimension_semantics=("arbitrary",),
                                             vmem_limit_bytes=VMEM_LIMIT),
        name="combine",
    )(pos_km, yb, x1, route, gt, g_final.reshape(1, D_MODEL))


def kernel(x_prompt, x_sample, cache_swa_k, cache_swa_v, state_gla, c_prompt, c_sample, g_mix_norm, g_ffn_norm, w_ada, b_ada, w_in, attn_sinks, w_gla_gate, b_gla_gate, g_gla_norm, w_out, w_router_group, b_router_group, w_router_expert, b_router_expert, w_expert_gate, w_expert_up, w_expert_down, g_final):
    depth = w_in.shape[0]
    assert depth == 1
    bp, t, d = x_prompt.shape
    bs, ts, _ = x_sample.shape
    n_p, n_s = bp * t, bs * ts
    n_tot = n_p + n_s
    assert n_tot % LANES == 0
    tm = 512
    to = 256
    tb = 256
    tc = 256
    td = LANES
    gla_c = 128

    w_in0 = w_in[0]
    w_q = w_in0[:, :_C_K].astype(bf16)
    w_q_t = w_q.T
    w_rest = w_in0[:, _C_K:_C_AB].astype(bf16)
    w_ab = jnp.pad(w_in0[:, _C_AB:], ((0, 0), (0, LANES - GLA_GATE_RANK))).astype(bf16)
    w_gate = jnp.pad(w_gla_gate[0], ((0, LANES - GLA_GATE_RANK), (0, 0))).astype(bf16)
    w_out_b = w_out[0].astype(bf16)
    n_r = N_GROUPS + N_EXPERTS
    w_r = jnp.pad(jnp.concatenate([w_router_group[0], w_router_expert[0]], axis=1),
                  ((0, 0), (0, LANES - n_r)))
    w_r_hi = w_r.astype(bf16)
    w_r_lo = (w_r - w_r_hi.astype(f32)).astype(bf16)
    w_route = jnp.concatenate([w_r_hi, w_r_lo], axis=1)
    b_route = jnp.pad(jnp.concatenate([b_router_group[0], b_router_expert[0]]),
                      (0, LANES - n_r)).reshape(1, LANES)

    c_all = jnp.concatenate([c_prompt, c_sample], axis=0)
    mod = _adaln(c_all, w_ada[0], b_ada[0])
    mod_tab = mod.reshape((bp + bs) * N_MOD, 1, d)
    mod_s = jnp.repeat(mod[bp:].reshape(bs, N_MOD, d), ts, axis=0)
    mod_s = [mod_s[:, m] for m in range(N_MOD)]
    pmod = lambda comp, tile: _mod_spec(comp, t // tile, n_p // tile)
    smod = pl.BlockSpec((n_s, d), lambda i: (0, 0))

    xp = x_prompt.reshape(n_p, d)
    xs = x_sample.reshape(n_s, d)
    proj_w = (w_rest, w_ab, w_gate, b_gla_gate[0])
    qp, kp, vp, gqp, gkp, gvp, rp, lgp = _proj(xp, mod_tab, mod_tab, g_mix_norm[0], w_q_t, *proj_w, tm=tm,
                                               mod_specs=(pmod(0, tm), pmod(1, tm)), q_transposed=True)
    qs, ks, vs, gqs, gks, gvs, rs, lgs = _proj(xs, mod_s[0], mod_s[1], g_mix_norm[0], w_q, *proj_w, tm=n_s,
                                               mod_specs=(smod, smod), q_transposed=False)

    sinks = attn_sinks[0]
    oap = _swa_t(qp, kp, vp, sinks, n_seq=bp, tiles=t // WINDOW)
    ck = cache_swa_k[0].reshape(bs * WINDOW, KV_WIDTH)
    cv = cache_swa_v[0].reshape(bs * WINDOW, KV_WIDTH)
    oas = _swa(qs, ck, cv, ks, vs, sinks, n_seq=bs, tq=ts, tiles=1, banded=False)
    s_zero = jnp.zeros((bp, GLA_HEADS, GLA_DK, GLA_DV), f32)
    obp, sp = _gla(gqp, gkp, gvp, lgp, rp, g_gla_norm[0], s_zero, n_seq=bp, c=gla_c, chunks=t // gla_c)
    obs, ss = _gla(gqs, gks, gvs, lgs, rs, g_gla_norm[0], state_gla[0], n_seq=bs, c=ts, chunks=1)

    out_w = (g_ffn_norm[0], w_out_b, w_route, b_route)
    bufs = _outproj(oap, obp, xp, mod_tab, mod_tab, mod_tab, *out_w, tm=to,
                    mod_specs=(pmod(2, to), pmod(3, to), pmod(4, to)), n_total=n_tot, row0=0,
                    fill_steps=-(-n_s // to), oa_transposed=True)
    x1, h2p, route = _outproj(oas, obs, xs, mod_s[2], mod_s[3], mod_s[4], *out_w, tm=n_s,
                              mod_specs=(smod, smod, smod), n_total=n_tot, row0=n_p, bufs=bufs)

    n_blocks = -(-(n_tot * TOP_K + N_EXPERTS * (tb - 1)) // tb)
    pos, pend_tab = _rank(route, tb=tb)
    pend = pend_tab[:, 0]
    block_e = jnp.minimum(jnp.sum(pend[None, :] <= (jnp.arange(n_blocks, dtype=jnp.int32) * tb)[:, None],
                                  axis=1), N_EXPERTS - 1).astype(jnp.int32)
    n_used = pend[N_EXPERTS - 1:] // tb
    xsort = _dispatch(pos.reshape(-1), pend, h2p, tm=td, tb=tb, n_blocks=n_blocks)
    yb = _experts(block_e, n_used, xsort, w_expert_gate[0], w_expert_up[0], w_expert_down[0], tb=tb)

    pos = pos.reshape(TOP_K, n_tot)
    pos_p = pos[:, :n_p].reshape(-1)
    pos_s = pos[:, n_p:].reshape(-1)
    gt_p = pl.BlockSpec((1, 1, d), lambda i, p: ((i // (t // tc)) * N_MOD + 5, 0, 0))
    gt_s = pl.BlockSpec((n_s, d), lambda i, p: (0, 0))
    y_p = _combine(pos_p, yb, x1, route, mod_tab, g_final, tm=tc, n=n_p, row0=0, gt_spec=gt_p)
    y_s = _combine(pos_s, yb, x1, route, mod_s[5], g_final, tm=n_s, n=n_s, row0=n_p, gt_spec=gt_s)

    kv_shape = (SWA_KV_HEADS, SWA_HEAD_DIM)
    k_state_p = kp.reshape(bp, t, *kv_shape)[:, -WINDOW:][None]
    v_state_p = vp.reshape(bp, t, *kv_shape)[:, -WINDOW:][None]
    return (y_p.reshape(bp, t, d), y_s.reshape(bs, ts, d), k_state_p, v_state_p, sp[None],
            ks.reshape(bs, ts, *kv_shape)[None], vs.reshape(bs, ts, *kv_shape)[None], ss[None])
```

```python
import functools

import jax
import jax.numpy as jnp
from jax import lax
from jax.experimental import pallas as pl
from jax.experimental.pallas import tpu as pltpu

f32 = jnp.float32
bf16 = jnp.bfloat16
u32 = jnp.uint32

D_MODEL = 2048
N_MOD = 6
EPS = 1e-6
NEG_INF = -1e30

SWA_HEAD_DIM = 64
SWA_KV_HEADS = 2
SWA_GROUP = 8
SWA_WIDTH = SWA_KV_HEADS * SWA_GROUP * SWA_HEAD_DIM
KV_WIDTH = SWA_KV_HEADS * SWA_HEAD_DIM
WINDOW = 128
CHUNK = 64

GLA_HEADS = 4
GLA_DK = 128
GLA_DV = 256
GLA_QK_WIDTH = GLA_HEADS * GLA_DK
GLA_V_WIDTH = GLA_HEADS * GLA_DV
GLA_GATE_RANK = 16
GLA_GATE_NORM = 16.0

N_GROUPS = 4
EXPERTS_PER_GROUP = 8
N_EXPERTS = N_GROUPS * EXPERTS_PER_GROUP
TOP_K = 2
EXPERT_HIDDEN = D_MODEL // 4

_C_Q = 0
_C_K = _C_Q + SWA_WIDTH
_C_V = _C_K + KV_WIDTH
_C_GQ = _C_V + KV_WIDTH
_C_GK = _C_GQ + GLA_QK_WIDTH
_C_GV = _C_GK + GLA_QK_WIDTH
_C_R = _C_GV + GLA_V_WIDTH
_C_AB = _C_R + GLA_V_WIDTH

LANES = 128
SUBLANES = 8
PACKED_WIDTH = D_MODEL // 2
assert PACKED_WIDTH == SUBLANES * LANES
VMEM_LIMIT = 56 * 1024 * 1024

ROUTE_E0, ROUTE_E1, ROUTE_G0, ROUTE_G1 = 0, 1, 2, 3


def _dot(a, b):
    return jnp.dot(a, b, preferred_element_type=f32)


def _dot_nt(a, b):
    return lax.dot_general(a, b, (((1,), (1,)), ((), ())), preferred_element_type=f32)


def _dot_tn(a, b):
    return lax.dot_general(a, b, (((0,), (0,)), ((), ())), preferred_element_type=f32)


def _silu(x):
    return x / (1.0 + jnp.exp(-x))


def _rows(ref):
    v = ref[...]
    return v.reshape(v.shape[-2:])


def _rms(x):
    return x * lax.rsqrt(jnp.mean(x * x, axis=-1, keepdims=True) + EPS)


def _resident(shape):
    return pl.BlockSpec(shape, lambda *_: (0,) * len(shape), pipeline_mode=pl.Buffered(1))


def _mod_spec(comp, tiles_per_seq, n_tiles):
    return pl.BlockSpec((1, 1, D_MODEL),
                        lambda i: ((jnp.minimum(i, n_tiles - 1) // tiles_per_seq) * N_MOD + comp, 0, 0))


def _adaln_kernel(c_ref, w_ref, b_ref, o_ref):
    a = _silu(c_ref[...]).astype(bf16)
    o_ref[...] = _dot(a, w_ref[...].astype(bf16)) + b_ref[...]


def _adaln(c_all, w_ada, b_ada, tn=1024):
    r = c_all.shape[0]
    n = w_ada.shape[1]
    return pl.pallas_call(
        _adaln_kernel,
        out_shape=jax.ShapeDtypeStruct((r, n), f32),
        grid=(n // tn,),
        in_specs=[pl.BlockSpec((r, D_MODEL), lambda j: (0, 0)),
                  pl.BlockSpec((D_MODEL, tn), lambda j: (0, j)),
                  pl.BlockSpec((1, tn), lambda j: (0, j))],
        out_specs=pl.BlockSpec((r, tn), lambda j: (0, j)),
        compiler_params=pltpu.CompilerParams(dimension_semantics=("arbitrary",),
                                             vmem_limit_bytes=VMEM_LIMIT),
        name="adaln",
    )(c_all, w_ada, b_ada.reshape(1, n))


def _proj_kernel(x_ref, sh_ref, sc_ref, g_ref, wq_ref, w_ref, wab_ref, wg_ref, bg_ref,
                 q_ref, k_ref, v_ref, gq_ref, gk_ref, gv_ref, r_ref, lg_ref, *, q_transposed):
    h = (_rms(x_ref[...]) * g_ref[...]) * (1.0 + _rows(sc_ref)) + _rows(sh_ref)
    hb = h.astype(bf16)
    if q_transposed:
        q_ref[...] = (_dot_nt(wq_ref[...], hb) * (SWA_HEAD_DIM ** -0.5)).astype(bf16)
    else:
        q_ref[...] = _dot(hb, wq_ref[...]).astype(bf16)
    col = lambda c: c - _C_K
    kv = _dot(hb, w_ref[:, col(_C_K):col(_C_GQ)])
    k_ref[...] = kv[:, :KV_WIDTH]
    v_ref[...] = kv[:, KV_WIDTH:]
    gq_ref[...] = _dot(hb, w_ref[:, col(_C_GQ):col(_C_GK)]).astype(bf16)
    gk_ref[...] = _dot(hb, w_ref[:, col(_C_GK):col(_C_GV)]).astype(bf16)
    gv_ref[...] = _dot(hb, w_ref[:, col(_C_GV):col(_C_R)]).astype(bf16)
    r_ref[...] = _dot(hb, w_ref[:, col(_C_R):col(_C_AB)]).astype(bf16)
    ab = _dot(hb, wab_ref[...])
    z = _dot(ab.astype(bf16), wg_ref[...]) + bg_ref[...]
    log_sig = jnp.minimum(z, 0.0) - jnp.log1p(jnp.exp(-jnp.abs(z)))
    lg_ref[...] = log_sig / GLA_GATE_NORM


def _proj(x2d, sh, sc, g_mix, w_q, w_rest, w_ab, w_gate, b_gate, *, tm, mod_specs, q_transposed):
    n = x2d.shape[0]
    row = lambda w: pl.BlockSpec((tm, w), lambda i: (i, 0))
    outs = [(KV_WIDTH, f32), (KV_WIDTH, f32), (GLA_QK_WIDTH, bf16),
            (GLA_QK_WIDTH, bf16), (GLA_V_WIDTH, bf16), (GLA_V_WIDTH, bf16), (GLA_QK_WIDTH, f32)]
    if q_transposed:
        q_shape, q_spec = (SWA_WIDTH, n), pl.BlockSpec((SWA_WIDTH, tm), lambda i: (0, i))
    else:
        q_shape, q_spec = (n, SWA_WIDTH), row(SWA_WIDTH)
    return pl.pallas_call(
        functools.partial(_proj_kernel, q_transposed=q_transposed),
        out_shape=[jax.ShapeDtypeStruct(q_shape, bf16)]
                  + [jax.ShapeDtypeStruct((n, w), dt) for w, dt in outs],
        grid=(n // tm,),
        in_specs=[row(D_MODEL), mod_specs[0], mod_specs[1],
                  _resident((1, D_MODEL)), _resident(w_q.shape), _resident(w_rest.shape),
                  _resident(w_ab.shape), _resident(w_gate.shape), _resident((1, GLA_QK_WIDTH))],
        out_specs=[q_spec] + [row(w) for w, _ in outs],
        compiler_params=pltpu.CompilerParams(dimension_semantics=("parallel",),
                                             vmem_limit_bytes=VMEM_LIMIT),
        name="proj",
    )(x2d, sh, sc, g_mix.reshape(1, D_MODEL), w_q, w_rest, w_ab, w_gate, b_gate.reshape(1, -1))


def _swa_kernel(q_ref, kp_ref, vp_ref, kc_ref, vc_ref, sink_ref, o_ref, *, tq, banded):
    u = pl.program_id(1)
    kc, vc = kc_ref[...], vc_ref[...]
    if tq < WINDOW:
        pad = jnp.zeros((WINDOW - tq, KV_WIDTH), f32)
        kc = jnp.concatenate([kc, pad], axis=0)
        vc = jnp.concatenate([vc, pad], axis=0)
    k_all = jnp.concatenate([kp_ref[...], kc], axis=0).astype(bf16)
    v_all = jnp.concatenate([vp_ref[...], vc], axis=0).astype(bf16)
    rows, cols = SWA_GROUP * tq, 2 * WINDOW
    col = lax.broadcasted_iota(jnp.int32, (rows, cols), 1)
    if banded:
        q_chunk = (lax.broadcasted_iota(jnp.int32, (rows, cols), 0) % tq) // CHUNK
        k_chunk = col // CHUNK
        valid = (k_chunk >= q_chunk) & (k_chunk <= q_chunk + WINDOW // CHUNK)
        valid = valid & ((col >= WINDOW) | (u > 0))
    else:
        valid = col < WINDOW + tq
    q = q_ref[...]
    outs = []
    for j in range(SWA_KV_HEADS):
        heads = [q[:, (j * SWA_GROUP + g) * SWA_HEAD_DIM:(j * SWA_GROUP + g + 1) * SWA_HEAD_DIM]
                 for g in range(SWA_GROUP)]
        qs = jnp.concatenate(heads, axis=0)
        kj = k_all[:, j * SWA_HEAD_DIM:(j + 1) * SWA_HEAD_DIM]
        vj = v_all[:, j * SWA_HEAD_DIM:(j + 1) * SWA_HEAD_DIM]
        s = _dot_nt(qs, kj) * (SWA_HEAD_DIM ** -0.5)
        s = jnp.where(valid, s, NEG_INF)
        sink = sink_ref[j]
        m = jnp.maximum(jnp.max(s, axis=1, keepdims=True), sink)
        p = jnp.exp(s - m)
        den = jnp.sum(p, axis=1, keepdims=True) + jnp.exp(sink - m)
        o = _dot(p.astype(bf16), vj) / den
        outs.append(jnp.concatenate([o[g * tq:(g + 1) * tq] for g in range(SWA_GROUP)], axis=1))
    o_ref[...] = jnp.concatenate(outs, axis=1).astype(bf16)


def _swa(q, k_prev, v_prev, k_cur, v_cur, sinks, *, n_seq, tq, tiles, banded):
    sink_rows = jnp.repeat(sinks.astype(f32).reshape(SWA_KV_HEADS, SWA_GROUP), tq, axis=1)
    sink_rows = sink_rows.reshape(SWA_KV_HEADS, SWA_GROUP * tq, 1)
    cur = lambda w: pl.BlockSpec((tq, w), lambda b, u: (b * tiles + u, 0))
    if banded:
        prev = pl.BlockSpec((WINDOW, KV_WIDTH), lambda b, u: (b * tiles + jnp.maximum(u - 1, 0), 0))
    else:
        prev = pl.BlockSpec((WINDOW, KV_WIDTH), lambda b, u: (b, 0))
    return pl.pallas_call(
        functools.partial(_swa_kernel, tq=tq, banded=banded),
        out_shape=jax.ShapeDtypeStruct(q.shape, bf16),
        grid=(n_seq, tiles),
        in_specs=[cur(SWA_WIDTH), prev, prev, cur(KV_WIDTH), cur(KV_WIDTH),
                  pl.BlockSpec(sink_rows.shape, lambda b, u: (0, 0, 0))],
        out_specs=cur(SWA_WIDTH),
        compiler_params=pltpu.CompilerParams(dimension_semantics=("parallel", "arbitrary"),
                                             vmem_limit_bytes=VMEM_LIMIT),
        name="swa",
    )(q, k_prev, v_prev, k_cur, v_cur, sink_rows)


def _swa_t_kernel(q_ref, kp_ref, vp_ref, kc_ref, vc_ref, sink_ref, o_ref):
    u = pl.program_id(1)
    tq = WINDOW
    k_all = jnp.concatenate([kp_ref[...], kc_ref[...]], axis=0).astype(bf16)
    v_all = jnp.concatenate([vp_ref[...], vc_ref[...]], axis=0).astype(bf16)
    rows, cols = 2 * WINDOW, SWA_GROUP * tq
    key = lax.broadcasted_iota(jnp.int32, (rows, cols), 0)
    q_chunk = (lax.broadcasted_iota(jnp.int32, (rows, cols), 1) % tq) // CHUNK
    k_chunk = key // CHUNK
    valid = (k_chunk >= q_chunk) & (k_chunk <= q_chunk + WINDOW // CHUNK)
    valid = valid & ((key >= WINDOW) | (u > 0))
    for j in range(SWA_KV_HEADS):
        head = lambda g: slice((j * SWA_GROUP + g) * SWA_HEAD_DIM, (j * SWA_GROUP + g + 1) * SWA_HEAD_DIM)
        qs = jnp.concatenate([q_ref[head(g), :] for g in range(SWA_GROUP)], axis=1)
        kj = k_all[:, j * SWA_HEAD_DIM:(j + 1) * SWA_HEAD_DIM]
        vj = v_all[:, j * SWA_HEAD_DIM:(j + 1) * SWA_HEAD_DIM]
        s = jnp.where(valid, _dot(kj, qs), NEG_INF)
        sink = sink_ref[j]
        m = jnp.maximum(jnp.max(s, axis=0, keepdims=True), sink)
        p = jnp.exp(s - m)
        den = jnp.sum(p, axis=0, keepdims=True) + jnp.exp(sink - m)
        o = _dot_tn(vj, p.astype(bf16)) / den
        for g in range(SWA_GROUP):
            o_ref[head(g), :] = o[:, g * tq:(g + 1) * tq].astype(bf16)


def _swa_t(q_t, k, v, sinks, *, n_seq, tiles):
    tq = WINDOW
    sink_cols = jnp.repeat(sinks.astype(f32).reshape(SWA_KV_HEADS, SWA_GROUP), tq, axis=1)
    sink_cols = sink_cols.reshape(SWA_KV_HEADS, 1, SWA_GROUP * tq)
    qspec = pl.BlockSpec((SWA_WIDTH, tq), lambda b, u: (0, b * tiles + u))
    cur = pl.BlockSpec((tq, KV_WIDTH), lambda b, u: (b * tiles + u, 0))
    prev = pl.BlockSpec((WINDOW, KV_WIDTH), lambda b, u: (b * tiles + jnp.maximum(u - 1, 0), 0))
    return pl.pallas_call(
        _swa_t_kernel,
        out_shape=jax.ShapeDtypeStruct(q_t.shape, bf16),
        grid=(n_seq, tiles),
        in_specs=[qspec, prev, prev, cur, cur, pl.BlockSpec(sink_cols.shape, lambda b, u: (0, 0, 0))],
        out_specs=qspec,
        compiler_params=pltpu.CompilerParams(dimension_semantics=("parallel", "arbitrary"),
                                             vmem_limit_bytes=VMEM_LIMIT),
        name="swa_t",
    )(q_t, k, v, k, v, sink_cols)


def _gla_kernel(q_ref, k_ref, v_ref, lg_ref, r_ref, gh_ref, s0_ref, o_ref, s_ref, *, c):
    @pl.when(pl.program_id(1) == 0)
    def _():
        s_ref[...] = s0_ref[...]

    row = lax.broadcasted_iota(jnp.int32, (c, GLA_DK), 0)
    causal = (lax.broadcasted_iota(jnp.int32, (c, c), 0) >= lax.broadcasted_iota(jnp.int32, (c, c), 1))
    for h in range(GLA_HEADS):
        ks = slice(h * GLA_DK, (h + 1) * GLA_DK)
        vs = slice(h * GLA_DV, (h + 1) * GLA_DV)
        b = lg_ref[:, ks]
        step = 1
        while step < c:
            b = b + jnp.where(row >= step, pltpu.roll(b, step, 0), 0.0)
            step *= 2
        b_last = b[c - 1:c, :]
        q = q_ref[:, ks].astype(f32) * (GLA_DK ** -0.5)
        k = k_ref[:, ks].astype(f32)
        v = v_ref[:, vs]
        qd = (q * jnp.exp(b)).astype(bf16)
        kd = (k * jnp.exp(-b)).astype(bf16)
        kl = (k * jnp.exp(b_last - b)).astype(bf16)
        a = jnp.where(causal, _dot_nt(qd, kd), 0.0)
        s = s_ref[0, h]
        o = _dot(qd, s.astype(bf16)) + _dot(a.astype(bf16), v)
        decay = jnp.broadcast_to(jnp.exp(b_last), (GLA_DK, GLA_DK)).T
        s_ref[0, h] = s * jnp.concatenate([decay, decay], axis=1) + _dot_tn(kl, v)
        on = _rms(o) * gh_ref[...]
        o_ref[:, vs] = (on * _silu(r_ref[:, vs].astype(f32))).astype(bf16)


def _gla(gq, gk, gv, lg, r, g_head, s0, *, n_seq, c, chunks):
    blk = lambda w: pl.BlockSpec((c, w), lambda b, t: (b * chunks + t, 0))
    state = pl.BlockSpec((1, GLA_HEADS, GLA_DK, GLA_DV), lambda b, t: (b, 0, 0, 0))
    return pl.pallas_call(
        functools.partial(_gla_kernel, c=c),
        out_shape=[jax.ShapeDtypeStruct(gv.shape, bf16),
                   jax.ShapeDtypeStruct((n_seq, GLA_HEADS, GLA_DK, GLA_DV), f32)],
        grid=(n_seq, chunks),
        in_specs=[blk(GLA_QK_WIDTH), blk(GLA_QK_WIDTH), blk(GLA_V_WIDTH), blk(GLA_QK_WIDTH),
                  blk(GLA_V_WIDTH), pl.BlockSpec((1, GLA_DV), lambda b, t: (0, 0)), state],
        out_specs=[blk(GLA_V_WIDTH), state],
        compiler_params=pltpu.CompilerParams(dimension_semantics=("parallel", "arbitrary"),
                                             vmem_limit_bytes=VMEM_LIMIT),
        name="gla",
    )(gq, gk, gv, lg, r, g_head.reshape(1, GLA_DV), s0)


def _pack_pair(hb):
    w = hb.shape[1] // 2
    a = lax.bitcast_convert_type(hb[:, :w].astype(f32), u32)
    b = lax.bitcast_convert_type(hb[:, w:].astype(f32), u32)
    return a | (b >> 16)


def _unpack_pair(p):
    a = lax.bitcast_convert_type(p & jnp.uint32(0xFFFF0000), f32)
    b = lax.bitcast_convert_type(p << 16, f32)
    return a, b


def _load_row_tiles(ref, n):
    return jnp.concatenate([ref[pl.ds(s, n, stride=SUBLANES), :] for s in range(SUBLANES)], axis=1)


def _store_row_tiles(ref, val):
    n = val.shape[0]
    for s in range(SUBLANES):
        ref[pl.ds(s, n, stride=SUBLANES), :] = val[:, s * LANES:(s + 1) * LANES]


def _row_tile(ref, r):
    return ref.at[pl.ds(pl.multiple_of(r * SUBLANES, SUBLANES), SUBLANES), :]


def _outproj_kernel(*refs, n_alias, oa_transposed):
    (oa_ref, ob_ref, x_ref, gt_ref, sh_ref, sc_ref, g_ref, w_ref, wr_ref, br_ref) = refs[:10]
    x1_ref, h2_ref, route_ref = refs[10 + n_alias:]
    oa_dot = _dot_tn if oa_transposed else _dot
    mix = oa_dot(oa_ref[...], w_ref[:SWA_WIDTH, :]) + _dot(ob_ref[...], w_ref[SWA_WIDTH:, :])
    x1 = x_ref[...] + _rows(gt_ref) * mix
    x1_ref[...] = x1
    h2 = (_rms(x1) * g_ref[...]) * (1.0 + _rows(sc_ref)) + _rows(sh_ref)
    hi = h2.astype(bf16)
    _store_row_tiles(h2_ref, _pack_pair(hi))
    lo = (h2 - hi.astype(f32)).astype(bf16)
    r1 = _dot(hi, wr_ref[...])
    logits = r1[:, :LANES] + r1[:, LANES:] + _dot(lo, wr_ref[:, :LANES]) + br_ref[...]
    lane = lax.broadcasted_iota(jnp.int32, logits.shape, 1)
    lane_f = lane.astype(f32)
    neg = float("-inf")
    first = lambda hit: jnp.min(jnp.where(hit, lane_f, float(LANES)), axis=1, keepdims=True)
    lg_g = jnp.where(lane < N_GROUPS, logits, neg)
    g_max = jnp.max(lg_g, axis=1, keepdims=True)
    g_sel = first(lg_g == g_max)
    p_sel = 1.0 / jnp.sum(jnp.exp(lg_g - g_max), axis=1, keepdims=True)
    e_lane = lane - N_GROUPS
    in_group = (e_lane >= 0) & (e_lane < N_EXPERTS) & ((e_lane // EXPERTS_PER_GROUP).astype(f32) == g_sel)
    lg_e = jnp.where(in_group, logits, neg)
    v1 = jnp.max(lg_e, axis=1, keepdims=True)
    i1 = first(lg_e == v1)
    lg_e2 = jnp.where(lane_f == i1, neg, lg_e)
    v2 = jnp.max(lg_e2, axis=1, keepdims=True)
    i2 = first(lg_e2 == v2)
    e = jnp.exp(v2 - v1)
    g1 = p_sel / (1.0 + e)
    g2 = p_sel * e / (1.0 + e)
    rec = jnp.where(lane == ROUTE_E0, i1 - N_GROUPS, 0.0)
    rec = jnp.where(lane == ROUTE_E1, i2 - N_GROUPS, rec)
    rec = jnp.where(lane == ROUTE_G0, g1, rec)
    rec = jnp.where(lane == ROUTE_G1, g2, rec)
    route_ref[...] = rec


def _outproj(oa, ob, x2d, gt, sh, sc, g_ffn, w_out, w_route, b_route, *, tm, mod_specs,
             n_total, row0, bufs=None, fill_steps=0, oa_transposed=False):
    n = x2d.shape[0]
    tiles = n // tm
    row = lambda w: pl.BlockSpec((tm, w), lambda i: (jnp.minimum(i, tiles - 1), 0))
    oa_spec = row(SWA_WIDTH)
    if oa_transposed:
        oa_spec = pl.BlockSpec((SWA_WIDTH, tm), lambda i: (0, jnp.minimum(i, tiles - 1)))
    out_row = lambda w: pl.BlockSpec((tm, w), lambda i: (row0 // tm + i, 0))
    out_tiles = pl.BlockSpec((tm * SUBLANES, LANES), lambda i: (row0 // tm + i, 0))
    alias_in = list(bufs) if bufs is not None else []
    n_in = 10
    return pl.pallas_call(
        functools.partial(_outproj_kernel, n_alias=len(alias_in), oa_transposed=oa_transposed),
        out_shape=[jax.ShapeDtypeStruct((n_total, D_MODEL), f32),
                   jax.ShapeDtypeStruct((n_total * SUBLANES, LANES), u32),
                   jax.ShapeDtypeStruct((n_total, LANES), f32)],
        grid=(tiles + fill_steps,),
        in_specs=[oa_spec, row(GLA_V_WIDTH), row(D_MODEL), mod_specs[0], mod_specs[1],
                  mod_specs[2], _resident((1, D_MODEL)), _resident(w_out.shape),
                  _resident(w_route.shape), _resident((1, LANES))]
                 + [pl.BlockSpec(memory_space=pl.ANY)] * len(alias_in),
        out_specs=[out_row(D_MODEL), out_tiles, out_row(LANES)],
        input_output_aliases={n_in + a: a for a in range(len(alias_in))},
        compiler_params=pltpu.CompilerParams(dimension_semantics=("parallel",),
                                             vmem_limit_bytes=VMEM_LIMIT),
        name="outproj",
    )(oa, ob, x2d, gt, sh, sc, g_ffn.reshape(1, D_MODEL), w_out, w_route, b_route, *alias_in)


def _rank_kernel(route_ref, pos_ref, pend_ref, *, n_chunks, tb):
    n_e = N_EXPERTS
    expert = lax.broadcasted_iota(jnp.int32, (n_e, LANES), 0)
    expert_f = expert.astype(f32)
    earlier = (lax.broadcasted_iota(jnp.int32, (LANES, LANES), 0)
               < lax.broadcasted_iota(jnp.int32, (LANES, LANES), 1)).astype(bf16)

    def onehots(c):
        rec = route_ref[pl.ds(pl.multiple_of(c * LANES, LANES), LANES), :].T
        return expert_f == rec[ROUTE_E0:ROUTE_E0 + 1, :], expert_f == rec[ROUTE_E1:ROUTE_E1 + 1, :]

    def count(c, cnt):
        h0, h1 = onehots(c)
        return cnt + jnp.sum((h0 | h1).astype(f32), axis=1, keepdims=True)

    cnt = lax.fori_loop(0, n_chunks, count, jnp.zeros((n_e, 1), f32))
    padded = jnp.floor((cnt + (tb - 1.0)) / tb) * tb
    end = jnp.broadcast_to(padded, (n_e, LANES))
    step = 1
    while step < n_e:
        end = end + jnp.where(expert >= step, pltpu.roll(end, step, 0), 0.0)
        step *= 2
    pend_ref[...] = end.astype(jnp.int32)

    def rank(c, base):
        h0, h1 = onehots(c)
        both = h0 | h1
        off = _dot(both.astype(bf16), earlier) + base
        pos_ref[0, pl.ds(c, 1), :] = jnp.sum(jnp.where(h0, off, 0.0), axis=0, keepdims=True).astype(jnp.int32)
        pos_ref[1, pl.ds(c, 1), :] = jnp.sum(jnp.where(h1, off, 0.0), axis=0, keepdims=True).astype(jnp.int32)
        return base + jnp.sum(both.astype(f32), axis=1, keepdims=True)

    lax.fori_loop(0, n_chunks, rank, (end - padded)[:, 0:1])


def _rank(route, *, tb):
    n = route.shape[0]
    n_chunks = n // LANES
    return pl.pallas_call(
        functools.partial(_rank_kernel, n_chunks=n_chunks, tb=tb),
        out_shape=[jax.ShapeDtypeStruct((TOP_K, n_chunks, LANES), jnp.int32),
                   jax.ShapeDtypeStruct((N_EXPERTS, LANES), jnp.int32)],
        in_specs=[_resident(route.shape)],
        compiler_params=pltpu.CompilerParams(vmem_limit_bytes=VMEM_LIMIT),
        name="rank",
    )(route)


def _dispatch_kernel(pos_ref, pend_ref, h2_ref, xs_hbm, stage, zbuf, sem, zsem, *, tm, tb, n_tot, steps):
    i = pl.program_id(0)
    slot = i % 2

    blk = tb * SUBLANES

    def wait_slot(s):
        for _ in range(TOP_K):
            pltpu.make_async_copy(stage.at[s], xs_hbm.at[pl.ds(0, tm * SUBLANES), :], sem.at[s]).wait()

    @pl.when(i == 0)
    def _():
        zbuf[...] = jnp.zeros(zbuf.shape, zbuf.dtype)

        def fill_block(b, start):
            cp = pltpu.make_async_copy(zbuf, xs_hbm.at[pl.ds(pl.multiple_of(b * blk, blk), blk), :], zsem)
            cp.start() if start else cp.wait()

        def fill(e, start):
            end = pend_ref[e]
            prev = jnp.where(e > 0, pend_ref[jnp.maximum(e - 1, 0)], 0)

            @pl.when(end > prev)
            def _():
                fill_block(end // tb - 1, start)

        fill_unused = fill_block

        first_unused = pend_ref[N_EXPERTS - 1] // tb
        n_blocks = xs_hbm.shape[0] // blk
        lax.fori_loop(0, N_EXPERTS, lambda e, c: (fill(e, True), c)[1], 0)
        lax.fori_loop(first_unused, n_blocks, lambda b, c: (fill_unused(b, True), c)[1], 0)
        lax.fori_loop(0, N_EXPERTS, lambda e, c: (fill(e, False), c)[1], 0)
        lax.fori_loop(first_unused, n_blocks, lambda b, c: (fill_unused(b, False), c)[1], 0)

    @pl.when(i >= 2)
    def _():
        wait_slot(slot)

    stage[slot] = h2_ref[...]

    def scatter(r, carry):
        for k in range(TOP_K):
            d = pos_ref[k * n_tot + i * tm + r]
            pltpu.make_async_copy(_row_tile(stage.at[slot], r), _row_tile(xs_hbm, d),
                                  sem.at[slot]).start()
        return carry

    lax.fori_loop(0, tm, scatter, 0, unroll=8)

    @pl.when(i == steps - 1)
    def _():
        wait_slot(slot)
        if steps > 1:
            wait_slot(1 - slot)


def _dispatch(pos_flat, pend, h2p, *, tm, tb, n_blocks):
    n_tot = h2p.shape[0] // SUBLANES
    steps = n_tot // tm
    return pl.pallas_call(
        functools.partial(_dispatch_kernel, tm=tm, tb=tb, n_tot=n_tot, steps=steps),
        out_shape=jax.ShapeDtypeStruct((n_blocks * tb * SUBLANES, LANES), u32),
        grid_spec=pltpu.PrefetchScalarGridSpec(
            num_scalar_prefetch=2,
            grid=(steps,),
            in_specs=[pl.BlockSpec((tm * SUBLANES, LANES), lambda i, pos, pend: (i, 0))],
            out_specs=pl.BlockSpec(memory_space=pl.ANY),
            scratch_shapes=[pltpu.VMEM((2, tm * SUBLANES, LANES), u32),
                            pltpu.VMEM((tb * SUBLANES, LANES), u32),
                            pltpu.SemaphoreType.DMA((2,)), pltpu.SemaphoreType.DMA(())]),
        compiler_params=pltpu.CompilerParams(dimension_semantics=("arbitrary",),
                                             vmem_limit_bytes=VMEM_LIMIT),
        name="dispatch",
    )(pos_flat, pend, h2p)


def _expert_kernel(be_ref, nu_ref, xs_ref, wg_ref, wu_ref, wd_ref, y_ref, wg_b, wu_b, wd_b, *, tb):
    i = pl.program_id(0)
    used = i < nu_ref[0]

    @pl.when(used & ((i == 0) | (be_ref[i] != be_ref[jnp.maximum(i - 1, 0)])))
    def _():
        wg_b[...] = wg_ref[0].astype(bf16)
        wu_b[...] = wu_ref[0].astype(bf16)
        wd_b[...] = wd_ref[0].astype(bf16)

    @pl.when(used)
    def _():
        a, b = _unpack_pair(_load_row_tiles(xs_ref, tb))
        x = jnp.concatenate([a.astype(bf16), b.astype(bf16)], axis=1)
        g = _dot(x, wg_b[...])
        u = _dot(x, wu_b[...])
        y = _dot((_silu(g) * u).astype(bf16), wd_b[...])
        _store_row_tiles(y_ref, _pack_pair(y.astype(bf16)))

    @pl.when(jnp.logical_not(used))
    def _():
        y_ref[...] = jnp.zeros(y_ref.shape, y_ref.dtype)


def _experts(block_e, n_used, xs, w_eg, w_eu, w_ed, *, tb):
    n_blocks = block_e.shape[0]
    blk = (tb * SUBLANES, LANES)
    wspec = lambda shape: pl.BlockSpec((1,) + shape, lambda i, be, nu: (be[i], 0, 0))
    return pl.pallas_call(
        functools.partial(_expert_kernel, tb=tb),
        out_shape=jax.ShapeDtypeStruct(xs.shape, u32),
        grid_spec=pltpu.PrefetchScalarGridSpec(
            num_scalar_prefetch=2,
            grid=(n_blocks,),
            in_specs=[pl.BlockSpec(blk, lambda i, be, nu: (jnp.minimum(i, nu[0] - 1), 0)),
                      wspec((D_MODEL, EXPERT_HIDDEN)), wspec((D_MODEL, EXPERT_HIDDEN)),
                      wspec((EXPERT_HIDDEN, D_MODEL))],
            out_specs=pl.BlockSpec(blk, lambda i, be, nu: (i, 0)),
            scratch_shapes=[pltpu.VMEM((D_MODEL, EXPERT_HIDDEN), bf16),
                            pltpu.VMEM((D_MODEL, EXPERT_HIDDEN), bf16),
                            pltpu.VMEM((EXPERT_HIDDEN, D_MODEL), bf16)]),
        compiler_params=pltpu.CompilerParams(dimension_semantics=("arbitrary",),
                                             vmem_limit_bytes=VMEM_LIMIT),
        name="experts",
    )(block_e, n_used, xs, w_eg, w_eu, w_ed)


def _gather_rows(idx_ref, base, src_hbm, dst, sem, n):
    def body(r, carry):
        t = idx_ref[base + r]
        pltpu.make_async_copy(_row_tile(src_hbm, t), _row_tile(dst, r), sem).start()
        return carry
    lax.fori_loop(0, n, body, 0, unroll=8)


def _combine_kernel(pos_ref, yb_hbm, x1_ref, route_ref, gt_ref, gf_ref, y_ref, ybuf, sem, *, tm):
    i = pl.program_id(0)
    slot = i % 2

    def start(blk, s):
        for k in range(TOP_K):
            _gather_rows(pos_ref, (k * pl.num_programs(0) + blk) * tm, yb_hbm,
                         ybuf.at[s, k], sem.at[s], tm)

    @pl.when(i == 0)
    def _():
        start(0, 0)

    @pl.when(i + 1 < pl.num_programs(0))
    def _():
        start(i + 1, 1 - slot)

    for k in range(TOP_K):
        pltpu.make_async_copy(yb_hbm.at[pl.ds(0, tm * SUBLANES), :], ybuf.at[slot, k], sem.at[slot]).wait()
    route = route_ref[...]
    a0, b0 = _unpack_pair(_load_row_tiles(ybuf.at[slot, 0], tm))
    a1, b1 = _unpack_pair(_load_row_tiles(ybuf.at[slot, 1], tm))
    g0 = route[:, ROUTE_G0:ROUTE_G0 + 1]
    g1 = route[:, ROUTE_G1:ROUTE_G1 + 1]
    moe = jnp.concatenate([a0 * g0 + a1 * g1, b0 * g0 + b1 * g1], axis=1)
    x2 = x1_ref[...] + _rows(gt_ref) * moe
    y_ref[...] = _rms(x2) * gf_ref[...]


def _combine(pos_km, yb, x1, route, gt, g_final, *, tm, n, row0, gt_spec):
    blk0 = row0 // tm
    return pl.pallas_call(
        functools.partial(_combine_kernel, tm=tm),
        out_shape=jax.ShapeDtypeStruct((n, D_MODEL), f32),
        grid_spec=pltpu.PrefetchScalarGridSpec(
            num_scalar_prefetch=1,
            grid=(n // tm,),
            in_specs=[pl.BlockSpec(memory_space=pl.ANY),
                      pl.BlockSpec((tm, D_MODEL), lambda i, pos: (blk0 + i, 0)),
                      pl.BlockSpec((tm, LANES), lambda i, pos: (blk0 + i, 0)),
                      gt_spec,
                      pl.BlockSpec((1, D_MODEL), lambda i, pos: (0, 0))],
            out_specs=pl.BlockSpec((tm, D_MODEL), lambda i, pos: (i, 0)),
            scratch_shapes=[pltpu.VMEM((2, TOP_K, tm * SUBLANES, LANES), u32),
                            pltpu.SemaphoreType.DMA((2,))]),
        compiler_params=pltpu.CompilerParams(dimension_semantics=("arbitrary",),
                                             vmem_limit_bytes=VMEM_LIMIT),
        name="combine",
    )(pos_km, yb, x1, route, gt, g_final.reshape(1, D_MODEL))


def kernel(x_prompt, x_sample, cache_swa_k, cache_swa_v, state_gla, c_prompt, c_sample, g_mix_norm, g_ffn_norm, w_ada, b_ada, w_in, attn_sinks, w_gla_gate, b_gla_gate, g_gla_norm, w_out, w_router_group, b_router_group, w_router_expert, b_router_expert, w_expert_gate, w_expert_up, w_expert_down, g_final):
    depth = w_in.shape[0]
    assert depth == 1
    bp, t, d = x_prompt.shape
    bs, ts, _ = x_sample.shape
    n_p, n_s = bp * t, bs * ts
    n_tot = n_p + n_s
    assert n_tot % LANES == 0
    tm = 512
    to = 256
    tb = 256
    tc = 256
    td = LANES
    gla_c = 128

    w_in0 = w_in[0]
    w_q = w_in0[:, :_C_K].astype(bf16)
    w_q_t = w_q.T
    w_rest = w_in0[:, _C_K:_C_AB].astype(bf16)
    w_ab = jnp.pad(w_in0[:, _C_AB:], ((0, 0), (0, LANES - GLA_GATE_RANK))).astype(bf16)
    w_gate = jnp.pad(w_gla_gate[0], ((0, LANES - GLA_GATE_RANK), (0, 0))).astype(bf16)
    w_out_b = w_out[0].astype(bf16)
    n_r = N_GROUPS + N_EXPERTS
    w_r = jnp.pad(jnp.concatenate([w_router_group[0], w_router_expert[0]], axis=1),
                  ((0, 0), (0, LANES - n_r)))
    w_r_hi = w_r.astype(bf16)
    w_r_lo = (w_r - w_r_hi.astype(f32)).astype(bf16)
    w_route = jnp.concatenate([w_r_hi, w_r_lo], axis=1)
    b_route = jnp.pad(jnp.concatenate([b_router_group[0], b_router_expert[0]]),
                      (0, LANES - n_r)).reshape(1, LANES)

    c_all = jnp.concatenate([c_prompt, c_sample], axis=0)
    mod = _adaln(c_all, w_ada[0], b_ada[0])
    mod_tab = mod.reshape((bp + bs) * N_MOD, 1, d)
    mod_s = jnp.repeat(mod[bp:].reshape(bs, N_MOD, d), ts, axis=0)
    mod_s = [mod_s[:, m] for m in range(N_MOD)]
    pmod = lambda comp, tile: _mod_spec(comp, t // tile, n_p // tile)
    smod = pl.BlockSpec((n_s, d), lambda i: (0, 0))

    xp = x_prompt.reshape(n_p, d)
    xs = x_sample.reshape(n_s, d)
    proj_w = (w_rest, w_ab, w_gate, b_gla_gate[0])
    qp, kp, vp, gqp, gkp, gvp, rp, lgp = _proj(xp, mod_tab, mod_tab, g_mix_norm[0], w_q_t, *proj_w, tm=tm,
                                               mod_specs=(pmod(0, tm), pmod(1, tm)), q_transposed=True)
    qs, ks, vs, gqs, gks, gvs, rs, lgs = _proj(xs, mod_s[0], mod_s[1], g_mix_norm[0], w_q, *proj_w, tm=n_s,
                                               mod_specs=(smod, smod), q_transposed=False)

    sinks = attn_sinks[0]
    oap = _swa_t(qp, kp, vp, sinks, n_seq=bp, tiles=t // WINDOW)
    ck = cache_swa_k[0].reshape(bs * WINDOW, KV_WIDTH)
    cv = cache_swa_v[0].reshape(bs * WINDOW, KV_WIDTH)
    oas = _swa(qs, ck, cv, ks, vs, sinks, n_seq=bs, tq=ts, tiles=1, banded=False)
    s_zero = jnp.zeros((bp, GLA_HEADS, GLA_DK, GLA_DV), f32)
    obp, sp = _gla(gqp, gkp, gvp, lgp, rp, g_gla_norm[0], s_zero, n_seq=bp, c=gla_c, chunks=t // gla_c)
    obs, ss = _gla(gqs, gks, gvs, lgs, rs, g_gla_norm[0], state_gla[0], n_seq=bs, c=ts, chunks=1)

    out_w = (g_ffn_norm[0], w_out_b, w_route, b_route)
    bufs = _outproj(oap, obp, xp, mod_tab, mod_tab, mod_tab, *out_w, tm=to,
                    mod_specs=(pmod(2, to), pmod(3, to), pmod(4, to)), n_total=n_tot, row0=0,
                    fill_steps=-(-n_s // to), oa_transposed=True)
    x1, h2p, route = _outproj(oas, obs, xs, mod_s[2], mod_s[3], mod_s[4], *out_w, tm=n_s,
                              mod_specs=(smod, smod, smod), n_total=n_tot, row0=n_p, bufs=bufs)

    n_blocks = -(-(n_tot * TOP_K + N_EXPERTS * (tb - 1)) // tb)
    pos, pend_tab = _rank(route, tb=tb)
    pend = pend_tab[:, 0]
    block_e = jnp.minimum(jnp.sum(pend[None, :] <= (jnp.arange(n_blocks, dtype=jnp.int32) * tb)[:, None],
                                  axis=1), N_EXPERTS - 1).astype(jnp.int32)
    n_used = pend[N_EXPERTS - 1:] // tb
    xsort = _dispatch(pos.reshape(-1), pend, h2p, tm=td, tb=tb, n_blocks=n_blocks)
    yb = _experts(block_e, n_used, xsort, w_expert_gate[0], w_expert_up[0], w_expert_down[0], tb=tb)

    pos = pos.reshape(TOP_K, n_tot)
    pos_p = pos[:, :n_p].reshape(-1)
    pos_s = pos[:, n_p:].reshape(-1)
    gt_p = pl.BlockSpec((1, 1, d), lambda i, p: ((i // (t // tc)) * N_MOD + 5, 0, 0))
    gt_s = pl.BlockSpec((n_s, d), lambda i, p: (0, 0))
    y_p = _combine(pos_p, yb, x1, route, mod_tab, g_final, tm=tc, n=n_p, row0=0, gt_spec=gt_p)
    y_s = _combine(pos_s, yb, x1, route, mod_s[5], g_final, tm=n_s, n=n_s, row0=n_p, gt_spec=gt_s)

    kv_shape = (SWA_KV_HEADS, SWA_HEAD_DIM)
    k_state_p = kp.reshape(bp, t, *kv_shape)[:, -WINDOW:][None]
    v_state_p = vp.reshape(bp, t, *kv_shape)[:, -WINDOW:][None]
    return (y_p.reshape(bp, t, d), y_s.reshape(bs, ts, d), k_state_p, v_state_p, sp[None],
            ks.reshape(bs, ts, *kv_shape)[None], vs.reshape(bs, ts, *kv_shape)[None], ss[None])
```

```python
import functools

import jax
import jax.numpy as jnp
from jax import lax
from jax.experimental import pallas as pl
from jax.experimental.pallas import tpu as pltpu

f32 = jnp.float32
bf16 = jnp.bfloat16
u32 = jnp.uint32

D_MODEL = 2048
N_MOD = 6
EPS = 1e-6
NEG_INF = -1e30

SWA_HEAD_DIM = 64
SWA_KV_HEADS = 2
SWA_GROUP = 8
SWA_WIDTH = SWA_KV_HEADS * SWA_GROUP * SWA_HEAD_DIM
KV_WIDTH = SWA_KV_HEADS * SWA_HEAD_DIM
WINDOW = 128
CHUNK = 64

GLA_HEADS = 4
GLA_DK = 128
GLA_DV = 256
GLA_QK_WIDTH = GLA_HEADS * GLA_DK
GLA_V_WIDTH = GLA_HEADS * GLA_DV
GLA_GATE_RANK = 16
GLA_GATE_NORM = 16.0

N_GROUPS = 4
EXPERTS_PER_GROUP = 8
N_EXPERTS = N_GROUPS * EXPERTS_PER_GROUP
TOP_K = 2
EXPERT_HIDDEN = D_MODEL // 4

_C_Q = 0
_C_K = _C_Q + SWA_WIDTH
_C_V = _C_K + KV_WIDTH
_C_GQ = _C_V + KV_WIDTH
_C_GK = _C_GQ + GLA_QK_WIDTH
_C_GV = _C_GK + GLA_QK_WIDTH
_C_R = _C_GV + GLA_V_WIDTH
_C_AB = _C_R + GLA_V_WIDTH

LANES = 128
SUBLANES = 8
PACKED_WIDTH = D_MODEL // 2
assert PACKED_WIDTH == SUBLANES * LANES
VMEM_LIMIT = 56 * 1024 * 1024

ROUTE_E0, ROUTE_E1, ROUTE_G0, ROUTE_G1 = 0, 1, 2, 3


def _dot(a, b):
    return jnp.dot(a, b, preferred_element_type=f32)


def _dot_nt(a, b):
    return lax.dot_general(a, b, (((1,), (1,)), ((), ())), preferred_element_type=f32)


def _dot_tn(a, b):
    return lax.dot_general(a, b, (((0,), (0,)), ((), ())), preferred_element_type=f32)


def _silu(x):
    return x / (1.0 + jnp.exp(-x))


def _rows(ref):
    v = ref[...]
    return v.reshape(v.shape[-2:])


def _rms(x):
    return x * lax.rsqrt(jnp.mean(x * x, axis=-1, keepdims=True) + EPS)


def _resident(shape):
    return pl.BlockSpec(shape, lambda *_: (0,) * len(shape), pipeline_mode=pl.Buffered(1))


def _mod_spec(comp, tiles_per_seq, n_tiles):
    return pl.BlockSpec((1, 1, D_MODEL),
                        lambda i: ((jnp.minimum(i, n_tiles - 1) // tiles_per_seq) * N_MOD + comp, 0, 0))


def _adaln_kernel(c_ref, w_ref, b_ref, o_ref):
    a = _silu(c_ref[...]).astype(bf16)
    o_ref[...] = _dot(a, w_ref[...].astype(bf16)) + b_ref[...]


def _adaln(c_all, w_ada, b_ada, tn=1024):
    r = c_all.shape[0]
    n = w_ada.shape[1]
    return pl.pallas_call(
        _adaln_kernel,
        out_shape=jax.ShapeDtypeStruct((r, n), f32),
        grid=(n // tn,),
        in_specs=[pl.BlockSpec((r, D_MODEL), lambda j: (0, 0)),
                  pl.BlockSpec((D_MODEL, tn), lambda j: (0, j)),
                  pl.BlockSpec((1, tn), lambda j: (0, j))],
        out_specs=pl.BlockSpec((r, tn), lambda j: (0, j)),
        compiler_params=pltpu.CompilerParams(dimension_semantics=("arbitrary",),
                                             vmem_limit_bytes=VMEM_LIMIT),
        name="adaln",
    )(c_all, w_ada, b_ada.reshape(1, n))


def _proj_kernel(x_ref, sh_ref, sc_ref, g_ref, wq_ref, w_ref, wab_ref, wg_ref, bg_ref,
                 q_ref, k_ref, v_ref, gq_ref, gk_ref, gv_ref, r_ref, lg_ref, *, q_transposed):
    h = (_rms(x_ref[...]) * g_ref[...]) * (1.0 + _rows(sc_ref)) + _rows(sh_ref)
    hb = h.astype(bf16)
    if q_transposed:
        q_ref[...] = (_dot_nt(wq_ref[...], hb) * (SWA_HEAD_DIM ** -0.5)).astype(bf16)
    else:
        q_ref[...] = _dot(hb, wq_ref[...]).astype(bf16)
    col = lambda c: c - _C_K
    kv = _dot(hb, w_ref[:, col(_C_K):col(_C_GQ)])
    k_ref[...] = kv[:, :KV_WIDTH]
    v_ref[...] = kv[:, KV_WIDTH:]
    gq_ref[...] = _dot(hb, w_ref[:, col(_C_GQ):col(_C_GK)]).astype(bf16)
    gk_ref[...] = _dot(hb, w_ref[:, col(_C_GK):col(_C_GV)]).astype(bf16)
    gv_ref[...] = _dot(hb, w_ref[:, col(_C_GV):col(_C_R)]).astype(bf16)
    r_ref[...] = _dot(hb, w_ref[:, col(_C_R):col(_C_AB)]).astype(bf16)
    ab = _dot(hb, wab_ref[...])
    z = _dot(ab.astype(bf16), wg_ref[...]) + bg_ref[...]
    log_sig = jnp.minimum(z, 0.0) - jnp.log1p(jnp.exp(-jnp.abs(z)))
    lg_ref[...] = log_sig / GLA_GATE_NORM


def _proj(x2d, sh, sc, g_mix, w_q, w_rest, w_ab, w_gate, b_gate, *, tm, mod_specs, q_transposed):
    n = x2d.shape[0]
    row = lambda w: pl.BlockSpec((tm, w), lambda i: (i, 0))
    outs = [(KV_WIDTH, f32), (KV_WIDTH, f32), (GLA_QK_WIDTH, bf16),
            (GLA_QK_WIDTH, bf16), (GLA_V_WIDTH, bf16), (GLA_V_WIDTH, bf16), (GLA_QK_WIDTH, f32)]
    if q_transposed:
        q_shape, q_spec = (SWA_WIDTH, n), pl.BlockSpec((SWA_WIDTH, tm), lambda i: (0, i))
    else:
        q_shape, q_spec = (n, SWA_WIDTH), row(SWA_WIDTH)
    return pl.pallas_call(
        functools.partial(_proj_kernel, q_transposed=q_transposed),
        out_shape=[jax.ShapeDtypeStruct(q_shape, bf16)]
                  + [jax.ShapeDtypeStruct((n, w), dt) for w, dt in outs],
        grid=(n // tm,),
        in_specs=[row(D_MODEL), mod_specs[0], mod_specs[1],
                  _resident((1, D_MODEL)), _resident(w_q.shape), _resident(w_rest.shape),
                  _resident(w_ab.shape), _resident(w_gate.shape), _resident((1, GLA_QK_WIDTH))],
        out_specs=[q_spec] + [row(w) for w, _ in outs],
        compiler_params=pltpu.CompilerParams(dimension_semantics=("parallel",),
                                             vmem_limit_bytes=VMEM_LIMIT),
        name="proj",
    )(x2d, sh, sc, g_mix.reshape(1, D_MODEL), w_q, w_rest, w_ab, w_gate, b_gate.reshape(1, -1))


def _swa_kernel(q_ref, kp_ref, vp_ref, kc_ref, vc_ref, sink_ref, o_ref, *, tq):
    pad = jnp.zeros((WINDOW - tq, KV_WIDTH), f32)
    k_all = jnp.concatenate([kp_ref[...], kc_ref[...], pad], axis=0).astype(bf16)
    v_all = jnp.concatenate([vp_ref[...], vc_ref[...], pad], axis=0).astype(bf16)
    rows, cols = SWA_GROUP * tq, 2 * WINDOW
    valid = lax.broadcasted_iota(jnp.int32, (rows, cols), 1) < WINDOW + tq
    q = q_ref[...]
    outs = []
    for j in range(SWA_KV_HEADS):
        heads = [q[:, (j * SWA_GROUP + g) * SWA_HEAD_DIM:(j * SWA_GROUP + g + 1) * SWA_HEAD_DIM]
                 for g in range(SWA_GROUP)]
        qs = jnp.concatenate(heads, axis=0)
        kj = k_all[:, j * SWA_HEAD_DIM:(j + 1) * SWA_HEAD_DIM]
        vj = v_all[:, j * SWA_HEAD_DIM:(j + 1) * SWA_HEAD_DIM]
        s = _dot_nt(qs, kj) * (SWA_HEAD_DIM ** -0.5)
        s = jnp.where(valid, s, NEG_INF)
        sink = sink_ref[j]
        m = jnp.maximum(jnp.max(s, axis=1, keepdims=True), sink)
        p = jnp.exp(s - m)
        den = jnp.sum(p, axis=1, keepdims=True) + jnp.exp(sink - m)
        o = _dot(p.astype(bf16), vj) / den
        outs.append(jnp.concatenate([o[g * tq:(g + 1) * tq] for g in range(SWA_GROUP)], axis=1))
    o_ref[...] = jnp.concatenate(outs, axis=1).astype(bf16)


def _swa(q, k_past, v_past, k_new, v_new, sinks, *, n_seq, tq):
    sink_rows = jnp.repeat(sinks.astype(f32).reshape(SWA_KV_HEADS, SWA_GROUP), tq, axis=1)
    sink_rows = sink_rows.reshape(SWA_KV_HEADS, SWA_GROUP * tq, 1)
    new = lambda w: pl.BlockSpec((tq, w), lambda b: (b, 0))
    past = pl.BlockSpec((WINDOW, KV_WIDTH), lambda b: (b, 0))
    return pl.pallas_call(
        functools.partial(_swa_kernel, tq=tq),
        out_shape=jax.ShapeDtypeStruct(q.shape, bf16),
        grid=(n_seq,),
        in_specs=[new(SWA_WIDTH), past, past, new(KV_WIDTH), new(KV_WIDTH),
                  pl.BlockSpec(sink_rows.shape, lambda b: (0, 0, 0))],
        out_specs=new(SWA_WIDTH),
        compiler_params=pltpu.CompilerParams(dimension_semantics=("parallel",),
                                             vmem_limit_bytes=VMEM_LIMIT),
        name="swa",
    )(q, k_past, v_past, k_new, v_new, sink_rows)


def _swa_t_kernel(q_ref, kp_ref, vp_ref, kc_ref, vc_ref, sink_ref, bias_ref, o_ref):
    tq = WINDOW
    k_all = jnp.concatenate([kp_ref[...], kc_ref[...]], axis=0).astype(bf16)
    v_all = jnp.concatenate([vp_ref[...], vc_ref[...]], axis=0).astype(bf16)
    for j in range(SWA_KV_HEADS):
        head = lambda g: slice((j * SWA_GROUP + g) * SWA_HEAD_DIM, (j * SWA_GROUP + g + 1) * SWA_HEAD_DIM)
        qs = jnp.concatenate([q_ref[head(g), :] for g in range(SWA_GROUP)], axis=1)
        kj = k_all[:, j * SWA_HEAD_DIM:(j + 1) * SWA_HEAD_DIM]
        vj = v_all[:, j * SWA_HEAD_DIM:(j + 1) * SWA_HEAD_DIM]
        s = _dot(kj, qs) + bias_ref[0]
        sink = sink_ref[j]
        m = jnp.maximum(jnp.max(s, axis=0, keepdims=True), sink)
        p = jnp.exp(s - m)
        den = jnp.sum(p, axis=0, keepdims=True) + jnp.exp(sink - m)
        o = _dot_tn(vj, p.astype(bf16)) / den
        for g in range(SWA_GROUP):
            o_ref[head(g), :] = o[:, g * tq:(g + 1) * tq].astype(bf16)


def _swa_t(q_t, k, v, sinks, *, n_seq, tiles):
    tq = WINDOW
    sink_cols = jnp.repeat(sinks.astype(f32).reshape(SWA_KV_HEADS, SWA_GROUP), tq, axis=1)
    sink_cols = sink_cols.reshape(SWA_KV_HEADS, 1, SWA_GROUP * tq)
    shape = (2 * WINDOW, SWA_GROUP * tq)
    key = lax.broadcasted_iota(jnp.int32, shape, 0)
    q_chunk = (lax.broadcasted_iota(jnp.int32, shape, 1) % tq) // CHUNK
    band = (key // CHUNK >= q_chunk) & (key // CHUNK <= q_chunk + WINDOW // CHUNK)
    bias = jnp.where(jnp.stack([band & (key >= WINDOW), band]), 0.0, NEG_INF).astype(f32)
    bias_spec = pl.BlockSpec((1,) + shape, lambda b, u: (jnp.minimum(u, 1), 0, 0))
    qspec = pl.BlockSpec((SWA_WIDTH, tq), lambda b, u: (0, b * tiles + u))
    cur = pl.BlockSpec((tq, KV_WIDTH), lambda b, u: (b * tiles + u, 0))
    prev = pl.BlockSpec((WINDOW, KV_WIDTH), lambda b, u: (b * tiles + jnp.maximum(u - 1, 0), 0))
    return pl.pallas_call(
        _swa_t_kernel,
        out_shape=jax.ShapeDtypeStruct(q_t.shape, bf16),
        grid=(n_seq, tiles),
        in_specs=[qspec, prev, prev, cur, cur, pl.BlockSpec(sink_cols.shape, lambda b, u: (0, 0, 0)),
                  bias_spec],
        out_specs=qspec,
        compiler_params=pltpu.CompilerParams(dimension_semantics=("parallel", "arbitrary"),
                                             vmem_limit_bytes=VMEM_LIMIT),
        name="swa_t",
    )(q_t, k, v, k, v, sink_cols, bias)


def _gla_kernel(q_ref, k_ref, v_ref, lg_ref, r_ref, gh_ref, s0_ref, o_ref, s_ref, *, c, n_sub):
    @pl.when(pl.program_id(1) == 0)
    def _():
        s_ref[...] = s0_ref[...]

    row = lax.broadcasted_iota(jnp.int32, (c, GLA_DK), 0)
    causal = (lax.broadcasted_iota(jnp.int32, (c, c), 0) >= lax.broadcasted_iota(jnp.int32, (c, c), 1))
    for sub in range(n_sub):
        ts = slice(sub * c, (sub + 1) * c)
        for h in range(GLA_HEADS):
            ks = slice(h * GLA_DK, (h + 1) * GLA_DK)
            vs = slice(h * GLA_DV, (h + 1) * GLA_DV)
            b = lg_ref[ts, ks]
            step = 1
            while step < c:
                b = b + jnp.where(row >= step, pltpu.roll(b, step, 0), 0.0)
                step *= 2
            b_last = b[c - 1:c, :]
            q = q_ref[ts, ks].astype(f32) * (GLA_DK ** -0.5)
            k = k_ref[ts, ks].astype(f32)
            v = v_ref[ts, vs]
            qd = (q * jnp.exp(b)).astype(bf16)
            kd = (k * jnp.exp(-b)).astype(bf16)
            kl = (k * jnp.exp(b_last - b)).astype(bf16)
            a = jnp.where(causal, _dot_nt(qd, kd), 0.0)
            s = s_ref[0, h]
            o = _dot(qd, s.astype(bf16)) + _dot(a.astype(bf16), v)
            decay = jnp.broadcast_to(jnp.exp(b_last), (GLA_DK, GLA_DK)).T
            s_ref[0, h] = s * jnp.concatenate([decay, decay], axis=1) + _dot_tn(kl, v)
            on = _rms(o) * gh_ref[...]
            o_ref[ts, vs] = (on * _silu(r_ref[ts, vs].astype(f32))).astype(bf16)


def _gla(gq, gk, gv, lg, r, g_head, s0, *, n_seq, c, n_sub, steps):
    rows = c * n_sub
    blk = lambda w: pl.BlockSpec((rows, w), lambda b, t: (b * steps + t, 0))
    state = pl.BlockSpec((1, GLA_HEADS, GLA_DK, GLA_DV), lambda b, t: (b, 0, 0, 0))
    return pl.pallas_call(
        functools.partial(_gla_kernel, c=c, n_sub=n_sub),
        out_shape=[jax.ShapeDtypeStruct(gv.shape, bf16),
                   jax.ShapeDtypeStruct((n_seq, GLA_HEADS, GLA_DK, GLA_DV), f32)],
        grid=(n_seq, steps),
        in_specs=[blk(GLA_QK_WIDTH), blk(GLA_QK_WIDTH), blk(GLA_V_WIDTH), blk(GLA_QK_WIDTH),
                  blk(GLA_V_WIDTH), pl.BlockSpec((1, GLA_DV), lambda b, t: (0, 0)), state],
        out_specs=[blk(GLA_V_WIDTH), state],
        compiler_params=pltpu.CompilerParams(dimension_semantics=("parallel", "arbitrary"),
                                             vmem_limit_bytes=VMEM_LIMIT),
        name="gla",
    )(gq, gk, gv, lg, r, g_head.reshape(1, GLA_DV), s0)


def _pack_pair(hb):
    w = hb.shape[1] // 2
    a = lax.bitcast_convert_type(hb[:, :w].astype(f32), u32)
    b = lax.bitcast_convert_type(hb[:, w:].astype(f32), u32)
    return a | (b >> 16)


def _unpack_pair(p):
    a = lax.bitcast_convert_type(p & jnp.uint32(0xFFFF0000), f32)
    b = lax.bitcast_convert_type(p << 16, f32)
    return a, b


def _load_row_tiles(ref, n):
    return jnp.concatenate([ref[pl.ds(s, n, stride=SUBLANES), :] for s in range(SUBLANES)], axis=1)


def _store_row_tiles(ref, val):
    n = val.shape[0]
    for s in range(SUBLANES):
        ref[pl.ds(s, n, stride=SUBLANES), :] = val[:, s * LANES:(s + 1) * LANES]


def _row_tile(ref, r):
    return ref.at[pl.ds(pl.multiple_of(r * SUBLANES, SUBLANES), SUBLANES), :]


def _outproj_kernel(*refs, n_alias, oa_transposed):
    (oa_ref, ob_ref, x_ref, gt_ref, sh_ref, sc_ref, g_ref, w_ref, wr_ref, br_ref) = refs[:10]
    x1_ref, h2_ref, route_ref = refs[10 + n_alias:]
    oa_dot = _dot_tn if oa_transposed else _dot
    mix = oa_dot(oa_ref[...], w_ref[:SWA_WIDTH, :]) + _dot(ob_ref[...], w_ref[SWA_WIDTH:, :])
    x1 = x_ref[...] + _rows(gt_ref) * mix
    x1_ref[...] = x1
    h2 = (_rms(x1) * g_ref[...]) * (1.0 + _rows(sc_ref)) + _rows(sh_ref)
    hi = h2.astype(bf16)
    _store_row_tiles(h2_ref, _pack_pair(hi))
    lo = (h2 - hi.astype(f32)).astype(bf16)
    r1 = _dot(hi, wr_ref[...])
    logits = r1[:, :LANES] + r1[:, LANES:] + _dot(lo, wr_ref[:, :LANES]) + br_ref[...]
    lane = lax.broadcasted_iota(jnp.int32, logits.shape, 1)
    lane_f = lane.astype(f32)
    neg = float("-inf")
    first = lambda hit: jnp.min(jnp.where(hit, lane_f, float(LANES)), axis=1, keepdims=True)
    lg_g = jnp.where(lane < N_GROUPS, logits, neg)
    g_max = jnp.max(lg_g, axis=1, keepdims=True)
    g_sel = first(lg_g == g_max)
    p_sel = 1.0 / jnp.sum(jnp.exp(lg_g - g_max), axis=1, keepdims=True)
    e_lane = lane - N_GROUPS
    in_group = (e_lane >= 0) & (e_lane < N_EXPERTS) & ((e_lane // EXPERTS_PER_GROUP).astype(f32) == g_sel)
    lg_e = jnp.where(in_group, logits, neg)
    v1 = jnp.max(lg_e, axis=1, keepdims=True)
    i1 = first(lg_e == v1)
    lg_e2 = jnp.where(lane_f == i1, neg, lg_e)
    v2 = jnp.max(lg_e2, axis=1, keepdims=True)
    i2 = first(lg_e2 == v2)
    e = jnp.exp(v2 - v1)
    g1 = p_sel / (1.0 + e)
    g2 = p_sel * e / (1.0 + e)
    rec = jnp.where(lane == ROUTE_E0, i1 - N_GROUPS, 0.0)
    rec = jnp.where(lane == ROUTE_E1, i2 - N_GROUPS, rec)
    rec = jnp.where(lane == ROUTE_G0, g1, rec)
    rec = jnp.where(lane == ROUTE_G1, g2, rec)
    route_ref[...] = rec


def _outproj(oa, ob, x2d, gt, sh, sc, g_ffn, w_out, w_route, b_route, *, tm, mod_specs,
             n_total, row0, bufs=None, fill_steps=0, oa_transposed=False):
    n = x2d.shape[0]
    tiles = n // tm
    row = lambda w: pl.BlockSpec((tm, w), lambda i: (jnp.minimum(i, tiles - 1), 0))
    oa_spec = row(SWA_WIDTH)
    if oa_transposed:
        oa_spec = pl.BlockSpec((SWA_WIDTH, tm), lambda i: (0, jnp.minimum(i, tiles - 1)))
    out_row = lambda w: pl.BlockSpec((tm, w), lambda i: (row0 // tm + i, 0))
    out_tiles = pl.BlockSpec((tm * SUBLANES, LANES), lambda i: (row0 // tm + i, 0))
    alias_in = list(bufs) if bufs is not None else []
    n_in = 10
    return pl.pallas_call(
        functools.partial(_outproj_kernel, n_alias=len(alias_in), oa_transposed=oa_transposed),
        out_shape=[jax.ShapeDtypeStruct((n_total, D_MODEL), f32),
                   jax.ShapeDtypeStruct((n_total * SUBLANES, LANES), u32),
                   jax.ShapeDtypeStruct((n_total, LANES), f32)],
        grid=(tiles + fill_steps,),
        in_specs=[oa_spec, row(GLA_V_WIDTH), row(D_MODEL), mod_specs[0], mod_specs[1],
                  mod_specs[2], _resident((1, D_MODEL)), _resident(w_out.shape),
                  _resident(w_route.shape), _resident((1, LANES))]
                 + [pl.BlockSpec(memory_space=pl.ANY)] * len(alias_in),
        out_specs=[out_row(D_MODEL), out_tiles, out_row(LANES)],
        input_output_aliases={n_in + a: a for a in range(len(alias_in))},
        compiler_params=pltpu.CompilerParams(dimension_semantics=("parallel",),
                                             vmem_limit_bytes=VMEM_LIMIT),
        name="outproj",
    )(oa, ob, x2d, gt, sh, sc, g_ffn.reshape(1, D_MODEL), w_out, w_route, b_route, *alias_in)


def _rank_kernel(route_ref, pos_ref, pend_ref, rec_s, *, n_chunks, tb):
    n_e = N_EXPERTS
    expert = lax.broadcasted_iota(jnp.int32, (n_e, LANES), 0)
    expert_f = expert.astype(f32)
    earlier = (lax.broadcasted_iota(jnp.int32, (LANES, LANES), 0)
               < lax.broadcasted_iota(jnp.int32, (LANES, LANES), 1)).astype(bf16)

    def onehots(c):
        rec = rec_s[c]
        return expert_f == rec[ROUTE_E0:ROUTE_E0 + 1, :], expert_f == rec[ROUTE_E1:ROUTE_E1 + 1, :]

    def count(c, cnt):
        rec_s[c] = route_ref[pl.ds(pl.multiple_of(c * LANES, LANES), LANES), :].T[:SUBLANES, :]
        h0, h1 = onehots(c)
        return cnt + jnp.sum((h0 | h1).astype(f32), axis=1, keepdims=True)

    cnt = lax.fori_loop(0, n_chunks, count, jnp.zeros((n_e, 1), f32))
    padded = jnp.floor((cnt + (tb - 1.0)) / tb) * tb
    end = jnp.broadcast_to(padded, (n_e, LANES))
    step = 1
    while step < n_e:
        end = end + jnp.where(expert >= step, pltpu.roll(end, step, 0), 0.0)
        step *= 2
    pend_ref[...] = end.astype(jnp.int32)

    def rank(c, base):
        h0, h1 = onehots(c)
        both = h0 | h1
        off = _dot(both.astype(bf16), earlier) + base
        pos_ref[0, pl.ds(c, 1), :] = jnp.sum(jnp.where(h0, off, 0.0), axis=0, keepdims=True).astype(jnp.int32)
        pos_ref[1, pl.ds(c, 1), :] = jnp.sum(jnp.where(h1, off, 0.0), axis=0, keepdims=True).astype(jnp.int32)
        return base + jnp.sum(both.astype(f32), axis=1, keepdims=True)

    lax.fori_loop(0, n_chunks, rank, (end - padded)[:, 0:1])


def _rank(route, *, tb):
    n = route.shape[0]
    n_chunks = n // LANES
    return pl.pallas_call(
        functools.partial(_rank_kernel, n_chunks=n_chunks, tb=tb),
        out_shape=[jax.ShapeDtypeStruct((TOP_K, n_chunks, LANES), jnp.int32),
                   jax.ShapeDtypeStruct((N_EXPERTS, LANES), jnp.int32)],
        in_specs=[_resident(route.shape)],
        scratch_shapes=[pltpu.VMEM((n_chunks, SUBLANES, LANES), f32)],
        compiler_params=pltpu.CompilerParams(vmem_limit_bytes=VMEM_LIMIT),
        name="rank",
    )(route)


def _dispatch_kernel(pos_ref, pend_ref, h2_ref, xs_hbm, stage, zbuf, sem, zsem, *, tm, tb, n_tot, steps):
    i = pl.program_id(0)
    slot = i % 2

    blk = tb * SUBLANES

    def wait_slot(s):
        for _ in range(TOP_K):
            pltpu.make_async_copy(stage.at[s], xs_hbm.at[pl.ds(0, tm * SUBLANES), :], sem.at[s]).wait()

    @pl.when(i == 0)
    def _():
        zbuf[...] = jnp.zeros(zbuf.shape, zbuf.dtype)

        def fill_block(b, start):
            cp = pltpu.make_async_copy(zbuf, xs_hbm.at[pl.ds(pl.multiple_of(b * blk, blk), blk), :], zsem)
            cp.start() if start else cp.wait()

        def fill(e, start):
            end = pend_ref[e]
            prev = jnp.where(e > 0, pend_ref[jnp.maximum(e - 1, 0)], 0)

            @pl.when(end > prev)
            def _():
                fill_block(end // tb - 1, start)

        fill_unused = fill_block

        first_unused = pend_ref[N_EXPERTS - 1] // tb
        n_blocks = xs_hbm.shape[0] // blk
        lax.fori_loop(0, N_EXPERTS, lambda e, c: (fill(e, True), c)[1], 0)
        lax.fori_loop(first_unused, n_blocks, lambda b, c: (fill_unused(b, True), c)[1], 0)
        lax.fori_loop(0, N_EXPERTS, lambda e, c: (fill(e, False), c)[1], 0)
        lax.fori_loop(first_unused, n_blocks, lambda b, c: (fill_unused(b, False), c)[1], 0)

    @pl.when(i >= 2)
    def _():
        wait_slot(slot)

    stage[slot] = h2_ref[...]

    def scatter(r, carry):
        for k in range(TOP_K):
            d = pos_ref[k * n_tot + i * tm + r]
            pltpu.make_async_copy(_row_tile(stage.at[slot], r), _row_tile(xs_hbm, d),
                                  sem.at[slot]).start()
        return carry

    lax.fori_loop(0, tm, scatter, 0, unroll=8)

    @pl.when(i == steps - 1)
    def _():
        wait_slot(slot)
        if steps > 1:
            wait_slot(1 - slot)


def _dispatch(pos_flat, pend, h2p, *, tm, tb, n_blocks):
    n_tot = h2p.shape[0] // SUBLANES
    steps = n_tot // tm
    return pl.pallas_call(
        functools.partial(_dispatch_kernel, tm=tm, tb=tb, n_tot=n_tot, steps=steps),
        out_shape=jax.ShapeDtypeStruct((n_blocks * tb * SUBLANES, LANES), u32),
        grid_spec=pltpu.PrefetchScalarGridSpec(
            num_scalar_prefetch=2,
            grid=(steps,),
            in_specs=[pl.BlockSpec((tm * SUBLANES, LANES), lambda i, pos, pend: (i, 0))],
            out_specs=pl.BlockSpec(memory_space=pl.ANY),
            scratch_shapes=[pltpu.VMEM((2, tm * SUBLANES, LANES), u32),
                            pltpu.VMEM((tb * SUBLANES, LANES), u32),
                            pltpu.SemaphoreType.DMA((2,)), pltpu.SemaphoreType.DMA(())]),
        compiler_params=pltpu.CompilerParams(dimension_semantics=("arbitrary",),
                                             vmem_limit_bytes=VMEM_LIMIT),
        name="dispatch",
    )(pos_flat, pend, h2p)


def _expert_kernel(be_ref, nu_ref, xs_ref, wg_ref, wu_ref, wd_ref, y_ref, wg_b, wu_b, wd_b, *, tb):
    i = pl.program_id(0)
    used = i < nu_ref[0]

    @pl.when(used & ((i == 0) | (be_ref[i] != be_ref[jnp.maximum(i - 1, 0)])))
    def _():
        wg_b[...] = wg_ref[0].astype(bf16)
        wu_b[...] = wu_ref[0].astype(bf16)
        wd_b[...] = wd_ref[0].astype(bf16)

    @pl.when(used)
    def _():
        a, b = _unpack_pair(_load_row_tiles(xs_ref, tb))
        x = jnp.concatenate([a.astype(bf16), b.astype(bf16)], axis=1)
        g = _dot(x, wg_b[...])
        u = _dot(x, wu_b[...])
        y = _dot((_silu(g) * u).astype(bf16), wd_b[...])
        _store_row_tiles(y_ref, _pack_pair(y.astype(bf16)))

    @pl.when(jnp.logical_not(used))
    def _():
        y_ref[...] = jnp.zeros(y_ref.shape, y_ref.dtype)


def _experts(block_e, n_used, xs, w_eg, w_eu, w_ed, *, tb):
    n_blocks = block_e.shape[0]
    blk = (tb * SUBLANES, LANES)
    wspec = lambda shape: pl.BlockSpec((1,) + shape, lambda i, be, nu: (be[i], 0, 0))
    return pl.pallas_call(
        functools.partial(_expert_kernel, tb=tb),
        out_shape=jax.ShapeDtypeStruct(xs.shape, u32),
        grid_spec=pltpu.PrefetchScalarGridSpec(
            num_scalar_prefetch=2,
            grid=(n_blocks,),
            in_specs=[pl.BlockSpec(blk, lambda i, be, nu: (jnp.minimum(i, nu[0] - 1), 0)),
                      wspec((D_MODEL, EXPERT_HIDDEN)), wspec((D_MODEL, EXPERT_HIDDEN)),
                      wspec((EXPERT_HIDDEN, D_MODEL))],
            out_specs=pl.BlockSpec(blk, lambda i, be, nu: (i, 0)),
            scratch_shapes=[pltpu.VMEM((D_MODEL, EXPERT_HIDDEN), bf16),
                            pltpu.VMEM((D_MODEL, EXPERT_HIDDEN), bf16),
                            pltpu.VMEM((EXPERT_HIDDEN, D_MODEL), bf16)]),
        compiler_params=pltpu.CompilerParams(dimension_semantics=("arbitrary",),
                                             vmem_limit_bytes=VMEM_LIMIT),
        name="experts",
    )(block_e, n_used, xs, w_eg, w_eu, w_ed)


def _gather_rows(idx_ref, base, src_hbm, dst, sem, n):
    def body(r, carry):
        t = idx_ref[base + r]
        pltpu.make_async_copy(_row_tile(src_hbm, t), _row_tile(dst, r), sem).start()
        return carry
    lax.fori_loop(0, n, body, 0, unroll=8)


def _combine_kernel(pos_ref, yb_hbm, x1_ref, route_ref, gt_ref, gf_ref, y_ref, ybuf, sem, *, tm):
    i = pl.program_id(0)
    slot = i % 2

    def start(blk, s):
        for k in range(TOP_K):
            _gather_rows(pos_ref, (k * pl.num_programs(0) + blk) * tm, yb_hbm,
                         ybuf.at[s, k], sem.at[s], tm)

    @pl.when(i == 0)
    def _():
        start(0, 0)

    @pl.when(i + 1 < pl.num_programs(0))
    def _():
        start(i + 1, 1 - slot)

    for k in range(TOP_K):
        pltpu.make_async_copy(yb_hbm.at[pl.ds(0, tm * SUBLANES), :], ybuf.at[slot, k], sem.at[slot]).wait()
    route = route_ref[...]
    a0, b0 = _unpack_pair(_load_row_tiles(ybuf.at[slot, 0], tm))
    a1, b1 = _unpack_pair(_load_row_tiles(ybuf.at[slot, 1], tm))
    g0 = route[:, ROUTE_G0:ROUTE_G0 + 1]
    g1 = route[:, ROUTE_G1:ROUTE_G1 + 1]
    moe = jnp.concatenate([a0 * g0 + a1 * g1, b0 * g0 + b1 * g1], axis=1)
    x2 = x1_ref[...] + _rows(gt_ref) * moe
    y_ref[...] = _rms(x2) * gf_ref[...]


def _combine(pos_km, yb, x1, route, gt, g_final, *, tm, n, row0, gt_spec):
    blk0 = row0 // tm
    return pl.pallas_call(
        functools.partial(_combine_kernel, tm=tm),
        out_shape=jax.ShapeDtypeStruct((n, D_MODEL), f32),
        grid_spec=pltpu.PrefetchScalarGridSpec(
            num_scalar_prefetch=1,
            grid=(n // tm,),
            in_specs=[pl.BlockSpec(memory_space=pl.ANY),
                      pl.BlockSpec((tm, D_MODEL), lambda i, pos: (blk0 + i, 0)),
                      pl.BlockSpec((tm, LANES), lambda i, pos: (blk0 + i, 0)),
                      gt_spec,
                      pl.BlockSpec((1, D_MODEL), lambda i, pos: (0, 0))],
            out_specs=pl.BlockSpec((tm, D_MODEL), lambda i, pos: (i, 0)),
            scratch_shapes=[pltpu.VMEM((2, TOP_K, tm * SUBLANES, LANES), u32),
                            pltpu.SemaphoreType.DMA((2,))]),
        compiler_params=pltpu.CompilerParams(dimension_semantics=("arbitrary",),
                                             vmem_limit_bytes=VMEM_LIMIT),
        name="combine",
    )(pos_km, yb, x1, route, gt, g_final.reshape(1, D_MODEL))


def kernel(x_prompt, x_sample, cache_swa_k, cache_swa_v, state_gla, c_prompt, c_sample, g_mix_norm, g_ffn_norm, w_ada, b_ada, w_in, attn_sinks, w_gla_gate, b_gla_gate, g_gla_norm, w_out, w_router_group, b_router_group, w_router_expert, b_router_expert, w_expert_gate, w_expert_up, w_expert_down, g_final):
    depth = w_in.shape[0]
    assert depth == 1
    bp, t, d = x_prompt.shape
    bs, ts, _ = x_sample.shape
    n_p, n_s = bp * t, bs * ts
    n_tot = n_p + n_s
    assert n_tot % LANES == 0
    tm = 512
    to = 256
    tb = 512
    tc = 256
    td = LANES
    gla_c = 128
    gla_sub = 2

    w_in0 = w_in[0]
    w_q = w_in0[:, :_C_K].astype(bf16)
    w_q_t = w_q.T
    w_rest = w_in0[:, _C_K:_C_AB].astype(bf16)
    w_ab = jnp.pad(w_in0[:, _C_AB:], ((0, 0), (0, LANES - GLA_GATE_RANK))).astype(bf16)
    w_gate = jnp.pad(w_gla_gate[0], ((0, LANES - GLA_GATE_RANK), (0, 0))).astype(bf16)
    w_out_b = w_out[0].astype(bf16)
    n_r = N_GROUPS + N_EXPERTS
    w_r = jnp.pad(jnp.concatenate([w_router_group[0], w_router_expert[0]], axis=1),
                  ((0, 0), (0, LANES - n_r)))
    w_r_hi = w_r.astype(bf16)
    w_r_lo = (w_r - w_r_hi.astype(f32)).astype(bf16)
    w_route = jnp.concatenate([w_r_hi, w_r_lo], axis=1)
    b_route = jnp.pad(jnp.concatenate([b_router_group[0], b_router_expert[0]]),
                      (0, LANES - n_r)).reshape(1, LANES)

    c_all = jnp.concatenate([c_prompt, c_sample], axis=0)
    mod = _adaln(c_all, w_ada[0], b_ada[0])
    mod_tab = mod.reshape((bp + bs) * N_MOD, 1, d)
    mod_s = jnp.repeat(mod[bp:].reshape(bs, N_MOD, d), ts, axis=0)
    mod_s = [mod_s[:, m] for m in range(N_MOD)]
    pmod = lambda comp, tile: _mod_spec(comp, t // tile, n_p // tile)
    smod = pl.BlockSpec((n_s, d), lambda i: (0, 0))

    xp = x_prompt.reshape(n_p, d)
    xs = x_sample.reshape(n_s, d)
    proj_w = (w_rest, w_ab, w_gate, b_gla_gate[0])
    qp, kp, vp, gqp, gkp, gvp, rp, lgp = _proj(xp, mod_tab, mod_tab, g_mix_norm[0], w_q_t, *proj_w, tm=tm,
                                               mod_specs=(pmod(0, tm), pmod(1, tm)), q_transposed=True)
    qs, ks, vs, gqs, gks, gvs, rs, lgs = _proj(xs, mod_s[0], mod_s[1], g_mix_norm[0], w_q, *proj_w, tm=n_s,
                                               mod_specs=(smod, smod), q_transposed=False)

    sinks = attn_sinks[0]
    oap = _swa_t(qp, kp, vp, sinks, n_seq=bp, tiles=t // WINDOW)
    ck = cache_swa_k[0].reshape(bs * WINDOW, KV_WIDTH)
    cv = cache_swa_v[0].reshape(bs * WINDOW, KV_WIDTH)
    oas = _swa(qs, ck, cv, ks, vs, sinks, n_seq=bs, tq=ts)
    s_zero = jnp.zeros((bp, GLA_HEADS, GLA_DK, GLA_DV), f32)
    obp, sp = _gla(gqp, gkp, gvp, lgp, rp, g_gla_norm[0], s_zero, n_seq=bp, c=gla_c, n_sub=gla_sub,
                   steps=t // (gla_c * gla_sub))
    obs, ss = _gla(gqs, gks, gvs, lgs, rs, g_gla_norm[0], state_gla[0], n_seq=bs, c=ts, n_sub=1, steps=1)

    out_w = (g_ffn_norm[0], w_out_b, w_route, b_route)
    bufs = _outproj(oap, obp, xp, mod_tab, mod_tab, mod_tab, *out_w, tm=to,
                    mod_specs=(pmod(2, to), pmod(3, to), pmod(4, to)), n_total=n_tot, row0=0,
                    fill_steps=-(-n_s // to), oa_transposed=True)
    x1, h2p, route = _outproj(oas, obs, xs, mod_s[2], mod_s[3], mod_s[4], *out_w, tm=n_s,
                              mod_specs=(smod, smod, smod), n_total=n_tot, row0=n_p, bufs=bufs)

    n_blocks = -(-(n_tot * TOP_K + N_EXPERTS * (tb - 1)) // tb)
    pos, pend_tab = _rank(route, tb=tb)
    pend = pend_tab[:, 0]
    block_e = jnp.minimum(jnp.sum(pend[None, :] <= (jnp.arange(n_blocks, dtype=jnp.int32) * tb)[:, None],
                                  axis=1), N_EXPERTS - 1).astype(jnp.int32)
    n_used = pend[N_EXPERTS - 1:] // tb
    xsort = _dispatch(pos.reshape(-1), pend, h2p, tm=td, tb=tb, n_blocks=n_blocks)
    yb = _experts(block_e, n_used, xsort, w_expert_gate[0], w_expert_up[0], w_expert_down[0], tb=tb)

    pos = pos.reshape(TOP_K, n_tot)
    pos_p = pos[:, :n_p].reshape(-1)
    pos_s = pos[:, n_p:].reshape(-1)
    gt_p = pl.BlockSpec((1, 1, d), lambda i, p: ((i // (t // tc)) * N_MOD + 5, 0, 0))
    gt_s = pl.BlockSpec((n_s, d), lambda i, p: (0, 0))
    y_p = _combine(pos_p, yb, x1, route, mod_tab, g_final, tm=tc, n=n_p, row0=0, gt_spec=gt_p)
    y_s = _combine(pos_s, yb, x1, route, mod_s[5], g_final, tm=n_s, n=n_s, row0=n_p, gt_spec=gt_s)

    kv_shape = (SWA_KV_HEADS, SWA_HEAD_DIM)
    k_state_p = kp.reshape(bp, t, *kv_shape)[:, -WINDOW:][None]
    v_state_p = vp.reshape(bp, t, *kv_shape)[:, -WINDOW:][None]
    return (y_p.reshape(bp, t, d), y_s.reshape(bs, ts, d), k_state_p, v_state_p, sp[None],
            ks.reshape(bs, ts, *kv_shape)[None], vs.reshape(bs, ts, *kv_shape)[None], ss[None])
```

```python
import functools

import jax
import jax.numpy as jnp
from jax import lax
from jax.experimental import pallas as pl
from jax.experimental.pallas import tpu as pltpu

f32 = jnp.float32
bf16 = jnp.bfloat16
u32 = jnp.uint32

D_MODEL = 2048
N_MOD = 6
EPS = 1e-6
NEG_INF = -1e30

SWA_HEAD_DIM = 64
SWA_KV_HEADS = 2
SWA_GROUP = 8
SWA_WIDTH = SWA_KV_HEADS * SWA_GROUP * SWA_HEAD_DIM
KV_WIDTH = SWA_KV_HEADS * SWA_HEAD_DIM
WINDOW = 128
CHUNK = 64

GLA_HEADS = 4
GLA_DK = 128
GLA_DV = 256
GLA_QK_WIDTH = GLA_HEADS * GLA_DK
GLA_V_WIDTH = GLA_HEADS * GLA_DV
GLA_GATE_RANK = 16
GLA_GATE_NORM = 16.0

N_GROUPS = 4
EXPERTS_PER_GROUP = 8
N_EXPERTS = N_GROUPS * EXPERTS_PER_GROUP
TOP_K = 2
EXPERT_HIDDEN = D_MODEL // 4

_C_Q = 0
_C_K = _C_Q + SWA_WIDTH
_C_V = _C_K + KV_WIDTH
_C_GQ = _C_V + KV_WIDTH
_C_GK = _C_GQ + GLA_QK_WIDTH
_C_GV = _C_GK + GLA_QK_WIDTH
_C_R = _C_GV + GLA_V_WIDTH
_C_AB = _C_R + GLA_V_WIDTH

LANES = 128
SUBLANES = 8
PACKED_WIDTH = D_MODEL // 2
assert PACKED_WIDTH == SUBLANES * LANES
VMEM_LIMIT = 56 * 1024 * 1024

ROUTE_E0, ROUTE_E1, ROUTE_G0, ROUTE_G1 = 0, 1, 2, 3


def _dot(a, b):
    return jnp.dot(a, b, preferred_element_type=f32)


def _dot_nt(a, b):
    return lax.dot_general(a, b, (((1,), (1,)), ((), ())), preferred_element_type=f32)


def _dot_tn(a, b):
    return lax.dot_general(a, b, (((0,), (0,)), ((), ())), preferred_element_type=f32)


def _silu(x):
    return x / (1.0 + jnp.exp(-x))


def _rows(ref):
    v = ref[...]
    return v.reshape(v.shape[-2:])


def _rms(x):
    return x * lax.rsqrt(jnp.mean(x * x, axis=-1, keepdims=True) + EPS)


def _resident(shape):
    return pl.BlockSpec(shape, lambda *_: (0,) * len(shape), pipeline_mode=pl.Buffered(1))


def _mod_spec(comp, tiles_per_seq, n_tiles, lag=0):
    def index(i):
        tile = jnp.minimum(jnp.maximum(i - lag, 0), n_tiles - 1)
        return ((tile // tiles_per_seq) * N_MOD + comp, 0, 0)
    return pl.BlockSpec((1, 1, D_MODEL), index)


def _adaln_kernel(c_ref, w_ref, b_ref, o_ref):
    a = _silu(c_ref[...]).astype(bf16)
    o_ref[...] = _dot(a, w_ref[...].astype(bf16)) + b_ref[...]


def _adaln(c_all, w_ada, b_ada, tn=1024):
    r = c_all.shape[0]
    n = w_ada.shape[1]
    return pl.pallas_call(
        _adaln_kernel,
        out_shape=jax.ShapeDtypeStruct((r, n), f32),
        grid=(n // tn,),
        in_specs=[pl.BlockSpec((r, D_MODEL), lambda j: (0, 0)),
                  pl.BlockSpec((D_MODEL, tn), lambda j: (0, j)),
                  pl.BlockSpec((1, tn), lambda j: (0, j))],
        out_specs=pl.BlockSpec((r, tn), lambda j: (0, j)),
        compiler_params=pltpu.CompilerParams(dimension_semantics=("arbitrary",),
                                             vmem_limit_bytes=VMEM_LIMIT),
        name="adaln",
    )(c_all, w_ada, b_ada.reshape(1, n))


def _proj_kernel(x_ref, sh_ref, sc_ref, g_ref, wq_ref, w_ref, wab_ref, wg_ref, bg_ref,
                 q_ref, k_ref, v_ref, gq_ref, gk_ref, gv_ref, r_ref, lg_ref, *, q_transposed):
    h = (_rms(x_ref[...]) * g_ref[...]) * (1.0 + _rows(sc_ref)) + _rows(sh_ref)
    hb = h.astype(bf16)
    if q_transposed:
        q_ref[...] = (_dot_nt(wq_ref[...], hb) * (SWA_HEAD_DIM ** -0.5)).astype(bf16)
    else:
        q_ref[...] = _dot(hb, wq_ref[...]).astype(bf16)
    col = lambda c: c - _C_K
    kv = _dot(hb, w_ref[:, col(_C_K):col(_C_GQ)])
    k_ref[...] = kv[:, :KV_WIDTH]
    v_ref[...] = kv[:, KV_WIDTH:]
    gq_ref[...] = _dot(hb, w_ref[:, col(_C_GQ):col(_C_GK)]).astype(bf16)
    gk_ref[...] = _dot(hb, w_ref[:, col(_C_GK):col(_C_GV)]).astype(bf16)
    gv_ref[...] = _dot(hb, w_ref[:, col(_C_GV):col(_C_R)]).astype(bf16)
    r_ref[...] = _dot(hb, w_ref[:, col(_C_R):col(_C_AB)]).astype(bf16)
    ab = _dot(hb, wab_ref[...])
    z = _dot(ab.astype(bf16), wg_ref[...]) + bg_ref[...]
    log_sig = jnp.minimum(z, 0.0) - jnp.log1p(jnp.exp(-jnp.abs(z)))
    lg_ref[...] = log_sig / GLA_GATE_NORM


def _proj(x2d, sh, sc, g_mix, w_q, w_rest, w_ab, w_gate, b_gate, *, tm, mod_specs, q_transposed):
    n = x2d.shape[0]
    row = lambda w: pl.BlockSpec((tm, w), lambda i: (i, 0))
    outs = [(KV_WIDTH, f32), (KV_WIDTH, f32), (GLA_QK_WIDTH, bf16),
            (GLA_QK_WIDTH, bf16), (GLA_V_WIDTH, bf16), (GLA_V_WIDTH, bf16), (GLA_QK_WIDTH, f32)]
    if q_transposed:
        q_shape, q_spec = (SWA_WIDTH, n), pl.BlockSpec((SWA_WIDTH, tm), lambda i: (0, i))
    else:
        q_shape, q_spec = (n, SWA_WIDTH), row(SWA_WIDTH)
    return pl.pallas_call(
        functools.partial(_proj_kernel, q_transposed=q_transposed),
        out_shape=[jax.ShapeDtypeStruct(q_shape, bf16)]
                  + [jax.ShapeDtypeStruct((n, w), dt) for w, dt in outs],
        grid=(n // tm,),
        in_specs=[row(D_MODEL), mod_specs[0], mod_specs[1],
                  _resident((1, D_MODEL)), _resident(w_q.shape), _resident(w_rest.shape),
                  _resident(w_ab.shape), _resident(w_gate.shape), _resident((1, GLA_QK_WIDTH))],
        out_specs=[q_spec] + [row(w) for w, _ in outs],
        compiler_params=pltpu.CompilerParams(dimension_semantics=("parallel",),
                                             vmem_limit_bytes=VMEM_LIMIT),
        name="proj",
    )(x2d, sh, sc, g_mix.reshape(1, D_MODEL), w_q, w_rest, w_ab, w_gate, b_gate.reshape(1, -1))


def _swa_kernel(q_ref, kp_ref, vp_ref, kc_ref, vc_ref, sink_ref, o_ref, *, tq):
    pad = jnp.zeros((WINDOW - tq, KV_WIDTH), f32)
    k_all = jnp.concatenate([kp_ref[...], kc_ref[...], pad], axis=0).astype(bf16)
    v_all = jnp.concatenate([vp_ref[...], vc_ref[...], pad], axis=0).astype(bf16)
    rows, cols = SWA_GROUP * tq, 2 * WINDOW
    valid = lax.broadcasted_iota(jnp.int32, (rows, cols), 1) < WINDOW + tq
    q = q_ref[...]
    outs = []
    for j in range(SWA_KV_HEADS):
        heads = [q[:, (j * SWA_GROUP + g) * SWA_HEAD_DIM:(j * SWA_GROUP + g + 1) * SWA_HEAD_DIM]
                 for g in range(SWA_GROUP)]
        qs = jnp.concatenate(heads, axis=0)
        kj = k_all[:, j * SWA_HEAD_DIM:(j + 1) * SWA_HEAD_DIM]
        vj = v_all[:, j * SWA_HEAD_DIM:(j + 1) * SWA_HEAD_DIM]
        s = _dot_nt(qs, kj) * (SWA_HEAD_DIM ** -0.5)
        s = jnp.where(valid, s, NEG_INF)
        sink = sink_ref[j]
        m = jnp.maximum(jnp.max(s, axis=1, keepdims=True), sink)
        p = jnp.exp(s - m)
        den = jnp.sum(p, axis=1, keepdims=True) + jnp.exp(sink - m)
        o = _dot(p.astype(bf16), vj) / den
        outs.append(jnp.concatenate([o[g * tq:(g + 1) * tq] for g in range(SWA_GROUP)], axis=1))
    o_ref[...] = jnp.concatenate(outs, axis=1).astype(bf16)


def _swa(q, k_past, v_past, k_new, v_new, sinks, *, n_seq, tq):
    sink_rows = jnp.repeat(sinks.astype(f32).reshape(SWA_KV_HEADS, SWA_GROUP), tq, axis=1)
    sink_rows = sink_rows.reshape(SWA_KV_HEADS, SWA_GROUP * tq, 1)
    new = lambda w: pl.BlockSpec((tq, w), lambda b: (b, 0))
    past = pl.BlockSpec((WINDOW, KV_WIDTH), lambda b: (b, 0))
    return pl.pallas_call(
        functools.partial(_swa_kernel, tq=tq),
        out_shape=jax.ShapeDtypeStruct(q.shape, bf16),
        grid=(n_seq,),
        in_specs=[new(SWA_WIDTH), past, past, new(KV_WIDTH), new(KV_WIDTH),
                  pl.BlockSpec(sink_rows.shape, lambda b: (0, 0, 0))],
        out_specs=new(SWA_WIDTH),
        compiler_params=pltpu.CompilerParams(dimension_semantics=("parallel",),
                                             vmem_limit_bytes=VMEM_LIMIT),
        name="swa",
    )(q, k_past, v_past, k_new, v_new, sink_rows)


def _swa_t_kernel(q_ref, kp_ref, vp_ref, kc_ref, vc_ref, sink_ref, bias0_ref, bias_ref, o_ref, *, n_sub):
    tq = WINDOW
    k_all = jnp.concatenate([kp_ref[...], kc_ref[...]], axis=0).astype(bf16)
    v_all = jnp.concatenate([vp_ref[...], vc_ref[...]], axis=0).astype(bf16)
    for sub in range(n_sub):
        keys = slice(sub * tq, sub * tq + 2 * WINDOW)
        toks = slice(sub * tq, (sub + 1) * tq)
        bias = (bias0_ref if sub == 0 else bias_ref)[0]
        for j in range(SWA_KV_HEADS):
            head = lambda g: slice((j * SWA_GROUP + g) * SWA_HEAD_DIM, (j * SWA_GROUP + g + 1) * SWA_HEAD_DIM)
            qs = jnp.concatenate([q_ref[head(g), toks] for g in range(SWA_GROUP)], axis=1)
            kj = k_all[keys, j * SWA_HEAD_DIM:(j + 1) * SWA_HEAD_DIM]
            vj = v_all[keys, j * SWA_HEAD_DIM:(j + 1) * SWA_HEAD_DIM]
            s = _dot(kj, qs) + bias
            sink = sink_ref[j]
            m = jnp.maximum(jnp.max(s, axis=0, keepdims=True), sink)
            p = jnp.exp(s - m)
            den = jnp.sum(p, axis=0, keepdims=True) + jnp.exp(sink - m)
            o = _dot_tn(vj, p.astype(bf16)) / den
            for g in range(SWA_GROUP):
                o_ref[head(g), toks] = o[:, g * tq:(g + 1) * tq].astype(bf16)


def _swa_t(q_t, k, v, sinks, *, n_seq, tiles, n_sub):
    tq = WINDOW
    steps = tiles // n_sub
    sink_cols = jnp.repeat(sinks.astype(f32).reshape(SWA_KV_HEADS, SWA_GROUP), tq, axis=1)
    sink_cols = sink_cols.reshape(SWA_KV_HEADS, 1, SWA_GROUP * tq)
    shape = (2 * WINDOW, SWA_GROUP * tq)
    key = lax.broadcasted_iota(jnp.int32, shape, 0)
    q_chunk = (lax.broadcasted_iota(jnp.int32, shape, 1) % tq) // CHUNK
    band = (key // CHUNK >= q_chunk) & (key // CHUNK <= q_chunk + WINDOW // CHUNK)
    bias = jnp.where(jnp.stack([band & (key >= WINDOW), band]), 0.0, NEG_INF).astype(f32)
    bias0_spec = pl.BlockSpec((1,) + shape, lambda b, u: (jnp.minimum(u, 1), 0, 0))
    bias_spec = pl.BlockSpec((1,) + shape, lambda b, u: (1, 0, 0))
    qspec = pl.BlockSpec((SWA_WIDTH, n_sub * tq), lambda b, u: (0, b * steps + u))
    cur = pl.BlockSpec((n_sub * tq, KV_WIDTH), lambda b, u: (b * steps + u, 0))
    prev = pl.BlockSpec((WINDOW, KV_WIDTH), lambda b, u: (b * tiles + jnp.maximum(n_sub * u - 1, 0), 0))
    return pl.pallas_call(
        functools.partial(_swa_t_kernel, n_sub=n_sub),
        out_shape=jax.ShapeDtypeStruct(q_t.shape, bf16),
        grid=(n_seq, steps),
        in_specs=[qspec, prev, prev, cur, cur, pl.BlockSpec(sink_cols.shape, lambda b, u: (0, 0, 0)),
                  bias0_spec, bias_spec],
        out_specs=qspec,
        compiler_params=pltpu.CompilerParams(dimension_semantics=("parallel", "arbitrary"),
                                             vmem_limit_bytes=VMEM_LIMIT),
        name="swa_t",
    )(q_t, k, v, k, v, sink_cols, bias, bias)


def _gla_kernel(q_ref, k_ref, v_ref, lg_ref, r_ref, gh_ref, s0_ref, o_ref, s_ref, *, c, n_sub):
    @pl.when(pl.program_id(1) == 0)
    def _():
        s_ref[...] = s0_ref[...]

    row = lax.broadcasted_iota(jnp.int32, (c, GLA_DK), 0)
    causal = (lax.broadcasted_iota(jnp.int32, (c, c), 0) >= lax.broadcasted_iota(jnp.int32, (c, c), 1))
    for sub in range(n_sub):
        ts = slice(sub * c, (sub + 1) * c)
        for h in range(GLA_HEADS):
            ks = slice(h * GLA_DK, (h + 1) * GLA_DK)
            vs = slice(h * GLA_DV, (h + 1) * GLA_DV)
            b = lg_ref[ts, ks]
            step = 1
            while step < c:
                b = b + jnp.where(row >= step, pltpu.roll(b, step, 0), 0.0)
                step *= 2
            b_last = b[c - 1:c, :]
            q = q_ref[ts, ks].astype(f32) * (GLA_DK ** -0.5)
            k = k_ref[ts, ks].astype(f32)
            v = v_ref[ts, vs]
            qd = (q * jnp.exp(b)).astype(bf16)
            kd = (k * jnp.exp(-b)).astype(bf16)
            kl = (k * jnp.exp(b_last - b)).astype(bf16)
            a = jnp.where(causal, _dot_nt(qd, kd), 0.0)
            s = s_ref[0, h]
            o = _dot(qd, s.astype(bf16)) + _dot(a.astype(bf16), v)
            decay = jnp.broadcast_to(jnp.exp(b_last), (GLA_DK, GLA_DK)).T
            s_ref[0, h] = s * jnp.concatenate([decay, decay], axis=1) + _dot_tn(kl, v)
            on = _rms(o) * gh_ref[...]
            o_ref[ts, vs] = (on * _silu(r_ref[ts, vs].astype(f32))).astype(bf16)


def _gla(gq, gk, gv, lg, r, g_head, s0, *, n_seq, c, n_sub, steps):
    rows = c * n_sub
    blk = lambda w: pl.BlockSpec((rows, w), lambda b, t: (b * steps + t, 0))
    state = pl.BlockSpec((1, GLA_HEADS, GLA_DK, GLA_DV), lambda b, t: (b, 0, 0, 0))
    return pl.pallas_call(
        functools.partial(_gla_kernel, c=c, n_sub=n_sub),
        out_shape=[jax.ShapeDtypeStruct(gv.shape, bf16),
                   jax.ShapeDtypeStruct((n_seq, GLA_HEADS, GLA_DK, GLA_DV), f32)],
        grid=(n_seq, steps),
        in_specs=[blk(GLA_QK_WIDTH), blk(GLA_QK_WIDTH), blk(GLA_V_WIDTH), blk(GLA_QK_WIDTH),
                  blk(GLA_V_WIDTH), pl.BlockSpec((1, GLA_DV), lambda b, t: (0, 0)), state],
        out_specs=[blk(GLA_V_WIDTH), state],
        compiler_params=pltpu.CompilerParams(dimension_semantics=("parallel", "arbitrary"),
                                             vmem_limit_bytes=VMEM_LIMIT),
        name="gla",
    )(gq, gk, gv, lg, r, g_head.reshape(1, GLA_DV), s0)


def _pack_pair(hb):
    w = hb.shape[1] // 2
    a = lax.bitcast_convert_type(hb[:, :w].astype(f32), u32)
    b = lax.bitcast_convert_type(hb[:, w:].astype(f32), u32)
    return a | (b >> 16)


def _unpack_pair(p):
    a = lax.bitcast_convert_type(p & jnp.uint32(0xFFFF0000), f32)
    b = lax.bitcast_convert_type(p << 16, f32)
    return a, b


def _load_row_tiles(ref, n):
    return jnp.concatenate([ref[pl.ds(s, n, stride=SUBLANES), :] for s in range(SUBLANES)], axis=1)


def _store_row_tiles(ref, val):
    n = val.shape[0]
    for s in range(SUBLANES):
        ref[pl.ds(s, n, stride=SUBLANES), :] = val[:, s * LANES:(s + 1) * LANES]


def _row_tile(ref, r):
    return ref.at[pl.ds(pl.multiple_of(r * SUBLANES, SUBLANES), SUBLANES), :]


def _outproj_kernel(*refs, n_alias, oa_transposed):
    (oa_ref, ob_ref, x_ref, gt_ref, sh_ref, sc_ref, g_ref, w_ref, wr_ref, br_ref) = refs[:10]
    x1_ref, h2_ref, route_ref, x1_s = refs[10 + n_alias:]

    @pl.when(pl.program_id(0) == 0)
    def _():
        x1_s[...] = jnp.zeros(x1_s.shape, x1_s.dtype)

    x1_prev = x1_s[...]
    oa_dot = _dot_tn if oa_transposed else _dot
    mix = oa_dot(oa_ref[...], w_ref[:SWA_WIDTH, :]) + _dot(ob_ref[...], w_ref[SWA_WIDTH:, :])
    x1 = x_ref[...] + _rows(gt_ref) * mix
    x1_ref[...] = x1
    x1_s[...] = x1
    h2 = (_rms(x1_prev) * g_ref[...]) * (1.0 + _rows(sc_ref)) + _rows(sh_ref)
    hi = h2.astype(bf16)
    _store_row_tiles(h2_ref, _pack_pair(hi))
    lo = (h2 - hi.astype(f32)).astype(bf16)
    r1 = _dot(hi, wr_ref[...])
    logits = r1[:, :LANES] + r1[:, LANES:] + _dot(lo, wr_ref[:, :LANES]) + br_ref[...]
    lane = lax.broadcasted_iota(jnp.int32, logits.shape, 1)
    lane_f = lane.astype(f32)
    neg = float("-inf")
    first = lambda hit: jnp.min(jnp.where(hit, lane_f, float(LANES)), axis=1, keepdims=True)
    lg_g = jnp.where(lane < N_GROUPS, logits, neg)
    g_max = jnp.max(lg_g, axis=1, keepdims=True)
    g_sel = first(lg_g == g_max)
    p_sel = 1.0 / jnp.sum(jnp.exp(lg_g - g_max), axis=1, keepdims=True)
    e_lane = lane - N_GROUPS
    in_group = (e_lane >= 0) & (e_lane < N_EXPERTS) & ((e_lane // EXPERTS_PER_GROUP).astype(f32) == g_sel)
    lg_e = jnp.where(in_group, logits, neg)
    v1 = jnp.max(lg_e, axis=1, keepdims=True)
    i1 = first(lg_e == v1)
    lg_e2 = jnp.where(lane_f == i1, neg, lg_e)
    v2 = jnp.max(lg_e2, axis=1, keepdims=True)
    i2 = first(lg_e2 == v2)
    e = jnp.exp(v2 - v1)
    g1 = p_sel / (1.0 + e)
    g2 = p_sel * e / (1.0 + e)
    rec = jnp.where(lane == ROUTE_E0, i1 - N_GROUPS, 0.0)
    rec = jnp.where(lane == ROUTE_E1, i2 - N_GROUPS, rec)
    rec = jnp.where(lane == ROUTE_G0, g1, rec)
    rec = jnp.where(lane == ROUTE_G1, g2, rec)
    route_ref[...] = rec


def _outproj(oa, ob, x2d, gt, sh, sc, g_ffn, w_out, w_route, b_route, *, tm, mod_specs,
             n_total, row0, bufs=None, fill_steps=0, oa_transposed=False):
    n = x2d.shape[0]
    tiles = n // tm
    blocks = tiles + fill_steps
    row = lambda w: pl.BlockSpec((tm, w), lambda i: (jnp.minimum(i, tiles - 1), 0))
    oa_spec = row(SWA_WIDTH)
    if oa_transposed:
        oa_spec = pl.BlockSpec((SWA_WIDTH, tm), lambda i: (0, jnp.minimum(i, tiles - 1)))
    head_blk = lambda i: row0 // tm + jnp.minimum(i, blocks - 1)
    tail_blk = lambda i: row0 // tm + jnp.maximum(i - 1, 0)
    out_x1 = pl.BlockSpec((tm, D_MODEL), lambda i: (head_blk(i), 0))
    out_tiles = pl.BlockSpec((tm * SUBLANES, LANES), lambda i: (tail_blk(i), 0))
    out_route = pl.BlockSpec((tm, LANES), lambda i: (tail_blk(i), 0))
    alias_in = list(bufs) if bufs is not None else []
    n_in = 10
    return pl.pallas_call(
        functools.partial(_outproj_kernel, n_alias=len(alias_in), oa_transposed=oa_transposed),
        out_shape=[jax.ShapeDtypeStruct((n_total, D_MODEL), f32),
                   jax.ShapeDtypeStruct((n_total * SUBLANES, LANES), u32),
                   jax.ShapeDtypeStruct((n_total, LANES), f32)],
        grid=(blocks + 1,),
        in_specs=[oa_spec, row(GLA_V_WIDTH), row(D_MODEL), mod_specs[0], mod_specs[1],
                  mod_specs[2], _resident((1, D_MODEL)), _resident(w_out.shape),
                  _resident(w_route.shape), _resident((1, LANES))]
                 + [pl.BlockSpec(memory_space=pl.ANY)] * len(alias_in),
        out_specs=[out_x1, out_tiles, out_route],
        scratch_shapes=[pltpu.VMEM((tm, D_MODEL), f32)],
        input_output_aliases={n_in + a: a for a in range(len(alias_in))},
        compiler_params=pltpu.CompilerParams(dimension_semantics=("arbitrary",),
                                             vmem_limit_bytes=VMEM_LIMIT),
        name="outproj",
    )(oa, ob, x2d, gt, sh, sc, g_ffn.reshape(1, D_MODEL), w_out, w_route, b_route, *alias_in)


def _rank_kernel(route_ref, pos_ref, pend_ref, rec_s, *, n_chunks, tb):
    n_e = N_EXPERTS
    expert = lax.broadcasted_iota(jnp.int32, (n_e, LANES), 0)
    expert_f = expert.astype(f32)
    earlier = (lax.broadcasted_iota(jnp.int32, (LANES, LANES), 0)
               < lax.broadcasted_iota(jnp.int32, (LANES, LANES), 1)).astype(bf16)

    def onehots(c):
        rec = rec_s[c]
        return expert_f == rec[ROUTE_E0:ROUTE_E0 + 1, :], expert_f == rec[ROUTE_E1:ROUTE_E1 + 1, :]

    def count(c, cnt):
        rec_s[c] = route_ref[pl.ds(pl.multiple_of(c * LANES, LANES), LANES), :].T[:SUBLANES, :]
        h0, h1 = onehots(c)
        return cnt + jnp.sum((h0 | h1).astype(f32), axis=1, keepdims=True)

    cnt = lax.fori_loop(0, n_chunks, count, jnp.zeros((n_e, 1), f32))
    padded = jnp.floor((cnt + (tb - 1.0)) / tb) * tb
    end = jnp.broadcast_to(padded, (n_e, LANES))
    step = 1
    while step < n_e:
        end = end + jnp.where(expert >= step, pltpu.roll(end, step, 0), 0.0)
        step *= 2
    pend_ref[...] = end.astype(jnp.int32)

    def rank(c, base):
        h0, h1 = onehots(c)
        both = h0 | h1
        off = _dot(both.astype(bf16), earlier) + base
        pos_ref[0, pl.ds(c, 1), :] = jnp.sum(jnp.where(h0, off, 0.0), axis=0, keepdims=True).astype(jnp.int32)
        pos_ref[1, pl.ds(c, 1), :] = jnp.sum(jnp.where(h1, off, 0.0), axis=0, keepdims=True).astype(jnp.int32)
        return base + jnp.sum(both.astype(f32), axis=1, keepdims=True)

    lax.fori_loop(0, n_chunks, rank, (end - padded)[:, 0:1])


def _rank(route, *, tb):
    n = route.shape[0]
    n_chunks = n // LANES
    return pl.pallas_call(
        functools.partial(_rank_kernel, n_chunks=n_chunks, tb=tb),
        out_shape=[jax.ShapeDtypeStruct((TOP_K, n_chunks, LANES), jnp.int32),
                   jax.ShapeDtypeStruct((N_EXPERTS, LANES), jnp.int32)],
        in_specs=[_resident(route.shape)],
        scratch_shapes=[pltpu.VMEM((n_chunks, SUBLANES, LANES), f32)],
        compiler_params=pltpu.CompilerParams(vmem_limit_bytes=VMEM_LIMIT),
        name="rank",
    )(route)


def _dispatch_kernel(pos_ref, pend_ref, h2_ref, xs_hbm, stage, zbuf, sem, zsem, *, tm, tb, n_tot, steps):
    i = pl.program_id(0)
    slot = i % 2

    blk = tb * SUBLANES

    def wait_slot(s):
        for _ in range(TOP_K):
            pltpu.make_async_copy(stage.at[s], xs_hbm.at[pl.ds(0, tm * SUBLANES), :], sem.at[s]).wait()

    @pl.when(i == 0)
    def _():
        zbuf[...] = jnp.zeros(zbuf.shape, zbuf.dtype)

        def fill_block(b, start):
            cp = pltpu.make_async_copy(zbuf, xs_hbm.at[pl.ds(pl.multiple_of(b * blk, blk), blk), :], zsem)
            cp.start() if start else cp.wait()

        def fill(e, start):
            end = pend_ref[e]
            prev = jnp.where(e > 0, pend_ref[jnp.maximum(e - 1, 0)], 0)

            @pl.when(end > prev)
            def _():
                fill_block(end // tb - 1, start)

        fill_unused = fill_block

        first_unused = pend_ref[N_EXPERTS - 1] // tb
        n_blocks = xs_hbm.shape[0] // blk
        lax.fori_loop(0, N_EXPERTS, lambda e, c: (fill(e, True), c)[1], 0)
        lax.fori_loop(first_unused, n_blocks, lambda b, c: (fill_unused(b, True), c)[1], 0)
        lax.fori_loop(0, N_EXPERTS, lambda e, c: (fill(e, False), c)[1], 0)
        lax.fori_loop(first_unused, n_blocks, lambda b, c: (fill_unused(b, False), c)[1], 0)

    @pl.when(i >= 2)
    def _():
        wait_slot(slot)

    stage[slot] = h2_ref[...]

    def scatter(r, carry):
        for k in range(TOP_K):
            d = pos_ref[k * n_tot + i * tm + r]
            pltpu.make_async_copy(_row_tile(stage.at[slot], r), _row_tile(xs_hbm, d),
                                  sem.at[slot]).start()
        return carry

    lax.fori_loop(0, tm, scatter, 0, unroll=8)

    @pl.when(i == steps - 1)
    def _():
        wait_slot(slot)
        if steps > 1:
            wait_slot(1 - slot)


def _dispatch(pos_flat, pend, h2p, *, tm, tb, n_blocks):
    n_tot = h2p.shape[0] // SUBLANES
    steps = n_tot // tm
    return pl.pallas_call(
        functools.partial(_dispatch_kernel, tm=tm, tb=tb, n_tot=n_tot, steps=steps),
        out_shape=jax.ShapeDtypeStruct((n_blocks * tb * SUBLANES, LANES), u32),
        grid_spec=pltpu.PrefetchScalarGridSpec(
            num_scalar_prefetch=2,
            grid=(steps,),
            in_specs=[pl.BlockSpec((tm * SUBLANES, LANES), lambda i, pos, pend: (i, 0))],
            out_specs=pl.BlockSpec(memory_space=pl.ANY),
            scratch_shapes=[pltpu.VMEM((2, tm * SUBLANES, LANES), u32),
                            pltpu.VMEM((tb * SUBLANES, LANES), u32),
                            pltpu.SemaphoreType.DMA((2,)), pltpu.SemaphoreType.DMA(())]),
        compiler_params=pltpu.CompilerParams(dimension_semantics=("arbitrary",),
                                             vmem_limit_bytes=VMEM_LIMIT),
        name="dispatch",
    )(pos_flat, pend, h2p)


def _expert_kernel(be_ref, nu_ref, xs_ref, wg_ref, wu_ref, wd_ref, y_ref, wg_b, wu_b, wd_b, *, tb):
    i = pl.program_id(0)
    used = i < nu_ref[0]

    @pl.when(used & ((i == 0) | (be_ref[i] != be_ref[jnp.maximum(i - 1, 0)])))
    def _():
        wg_b[...] = wg_ref[0].astype(bf16)
        wu_b[...] = wu_ref[0].astype(bf16)
        wd_b[...] = wd_ref[0].astype(bf16)

    @pl.when(used)
    def _():
        a, b = _unpack_pair(_load_row_tiles(xs_ref, tb))
        x = jnp.concatenate([a.astype(bf16), b.astype(bf16)], axis=1)
        g = _dot(x, wg_b[...])
        u = _dot(x, wu_b[...])
        y = _dot((_silu(g) * u).astype(bf16), wd_b[...])
        _store_row_tiles(y_ref, _pack_pair(y.astype(bf16)))

    @pl.when(jnp.logical_not(used))
    def _():
        y_ref[...] = jnp.zeros(y_ref.shape, y_ref.dtype)


def _experts(block_e, n_used, xs, w_eg, w_eu, w_ed, *, tb):
    n_blocks = block_e.shape[0]
    blk = (tb * SUBLANES, LANES)
    wspec = lambda shape: pl.BlockSpec((1,) + shape, lambda i, be, nu: (be[i], 0, 0))
    return pl.pallas_call(
        functools.partial(_expert_kernel, tb=tb),
        out_shape=jax.ShapeDtypeStruct(xs.shape, u32),
        grid_spec=pltpu.PrefetchScalarGridSpec(
            num_scalar_prefetch=2,
            grid=(n_blocks,),
            in_specs=[pl.BlockSpec(blk, lambda i, be, nu: (jnp.minimum(i, nu[0] - 1), 0)),
                      wspec((D_MODEL, EXPERT_HIDDEN)), wspec((D_MODEL, EXPERT_HIDDEN)),
                      wspec((EXPERT_HIDDEN, D_MODEL))],
            out_specs=pl.BlockSpec(blk, lambda i, be, nu: (i, 0)),
            scratch_shapes=[pltpu.VMEM((D_MODEL, EXPERT_HIDDEN), bf16),
                            pltpu.VMEM((D_MODEL, EXPERT_HIDDEN), bf16),
                            pltpu.VMEM((EXPERT_HIDDEN, D_MODEL), bf16)]),
        compiler_params=pltpu.CompilerParams(dimension_semantics=("arbitrary",),
                                             vmem_limit_bytes=VMEM_LIMIT),
        name="experts",
    )(block_e, n_used, xs, w_eg, w_eu, w_ed)


def _gather_rows(idx_ref, base, src_hbm, dst, sem, n):
    def body(r, carry):
        t = idx_ref[base + r]
        pltpu.make_async_copy(_row_tile(src_hbm, t), _row_tile(dst, r), sem).start()
        return carry
    lax.fori_loop(0, n, body, 0, unroll=8)


def _combine_kernel(pos_ref, yb_hbm, x1_ref, route_ref, gt_ref, gf_ref, y_ref, ybuf, sem, *, tm):
    i = pl.program_id(0)
    slot = i % 2

    def start(blk, s):
        for k in range(TOP_K):
            _gather_rows(pos_ref, (k * pl.num_programs(0) + blk) * tm, yb_hbm,
                         ybuf.at[s, k], sem.at[s], tm)

    @pl.when(i == 0)
    def _():
        start(0, 0)

    @pl.when(i + 1 < pl.num_programs(0))
    def _():
        start(i + 1, 1 - slot)

    for k in range(TOP_K):
        pltpu.make_async_copy(yb_hbm.at[pl.ds(0, tm * SUBLANES), :], ybuf.at[slot, k], sem.at[slot]).wait()
    route = route_ref[...]
    a0, b0 = _unpack_pair(_load_row_tiles(ybuf.at[slot, 0], tm))
    a1, b1 = _unpack_pair(_load_row_tiles(ybuf.at[slot, 1], tm))
    g0 = route[:, ROUTE_G0:ROUTE_G0 + 1]
    g1 = route[:, ROUTE_G1:ROUTE_G1 + 1]
    moe = jnp.concatenate([a0 * g0 + a1 * g1, b0 * g0 + b1 * g1], axis=1)
    x2 = x1_ref[...] + _rows(gt_ref) * moe
    y_ref[...] = _rms(x2) * gf_ref[...]


def _combine(pos_km, yb, x1, route, gt, g_final, *, tm, n, row0, gt_spec):
    blk0 = row0 // tm
    return pl.pallas_call(
        functools.partial(_combine_kernel, tm=tm),
        out_shape=jax.ShapeDtypeStruct((n, D_MODEL), f32),
        grid_spec=pltpu.PrefetchScalarGridSpec(
            num_scalar_prefetch=1,
            grid=(n // tm,),
            in_specs=[pl.BlockSpec(memory_space=pl.ANY),
                      pl.BlockSpec((tm, D_MODEL), lambda i, pos: (blk0 + i, 0)),
                      pl.BlockSpec((tm, LANES), lambda i, pos: (blk0 + i, 0)),
                      gt_spec,
                      pl.BlockSpec((1, D_MODEL), lambda i, pos: (0, 0))],
            out_specs=pl.BlockSpec((tm, D_MODEL), lambda i, pos: (i, 0)),
            scratch_shapes=[pltpu.VMEM((2, TOP_K, tm * SUBLANES, LANES), u32),
                            pltpu.SemaphoreType.DMA((2,))]),
        compiler_params=pltpu.CompilerParams(dimension_semantics=("arbitrary",),
                                             vmem_limit_bytes=VMEM_LIMIT),
        name="combine",
    )(pos_km, yb, x1, route, gt, g_final.reshape(1, D_MODEL))


def kernel(x_prompt, x_sample, cache_swa_k, cache_swa_v, state_gla, c_prompt, c_sample, g_mix_norm, g_ffn_norm, w_ada, b_ada, w_in, attn_sinks, w_gla_gate, b_gla_gate, g_gla_norm, w_out, w_router_group, b_router_group, w_router_expert, b_router_expert, w_expert_gate, w_expert_up, w_expert_down, g_final):
    depth = w_in.shape[0]
    assert depth == 1
    bp, t, d = x_prompt.shape
    bs, ts, _ = x_sample.shape
    n_p, n_s = bp * t, bs * ts
    n_tot = n_p + n_s
    assert n_tot % LANES == 0
    tm = 512
    to = 256
    tb = 512
    tc = 256
    td = LANES
    gla_c = 128
    gla_sub = 4
    swa_sub = 2

    w_in0 = w_in[0]
    w_q = w_in0[:, :_C_K].astype(bf16)
    w_q_t = w_q.T
    w_rest = w_in0[:, _C_K:_C_AB].astype(bf16)
    w_ab = jnp.pad(w_in0[:, _C_AB:], ((0, 0), (0, LANES - GLA_GATE_RANK))).astype(bf16)
    w_gate = jnp.pad(w_gla_gate[0], ((0, LANES - GLA_GATE_RANK), (0, 0))).astype(bf16)
    w_out_b = w_out[0].astype(bf16)
    n_r = N_GROUPS + N_EXPERTS
    w_r = jnp.pad(jnp.concatenate([w_router_group[0], w_router_expert[0]], axis=1),
                  ((0, 0), (0, LANES - n_r)))
    w_r_hi = w_r.astype(bf16)
    w_r_lo = (w_r - w_r_hi.astype(f32)).astype(bf16)
    w_route = jnp.concatenate([w_r_hi, w_r_lo], axis=1)
    b_route = jnp.pad(jnp.concatenate([b_router_group[0], b_router_expert[0]]),
                      (0, LANES - n_r)).reshape(1, LANES)

    c_all = jnp.concatenate([c_prompt, c_sample], axis=0)
    mod = _adaln(c_all, w_ada[0], b_ada[0])
    mod_tab = mod.reshape((bp + bs) * N_MOD, 1, d)
    mod_s = jnp.repeat(mod[bp:].reshape(bs, N_MOD, d), ts, axis=0)
    mod_s = [mod_s[:, m] for m in range(N_MOD)]
    pmod = lambda comp, tile, lag=0: _mod_spec(comp, t // tile, n_p // tile, lag)
    smod = pl.BlockSpec((n_s, d), lambda i: (0, 0))

    xp = x_prompt.reshape(n_p, d)
    xs = x_sample.reshape(n_s, d)
    proj_w = (w_rest, w_ab, w_gate, b_gla_gate[0])
    qp, kp, vp, gqp, gkp, gvp, rp, lgp = _proj(xp, mod_tab, mod_tab, g_mix_norm[0], w_q_t, *proj_w, tm=tm,
                                               mod_specs=(pmod(0, tm), pmod(1, tm)), q_transposed=True)
    qs, ks, vs, gqs, gks, gvs, rs, lgs = _proj(xs, mod_s[0], mod_s[1], g_mix_norm[0], w_q, *proj_w, tm=n_s,
                                               mod_specs=(smod, smod), q_transposed=False)

    sinks = attn_sinks[0]
    oap = _swa_t(qp, kp, vp, sinks, n_seq=bp, tiles=t // WINDOW, n_sub=swa_sub)
    ck = cache_swa_k[0].reshape(bs * WINDOW, KV_WIDTH)
    cv = cache_swa_v[0].reshape(bs * WINDOW, KV_WIDTH)
    oas = _swa(qs, ck, cv, ks, vs, sinks, n_seq=bs, tq=ts)
    s_zero = jnp.zeros((bp, GLA_HEADS, GLA_DK, GLA_DV), f32)
    obp, sp = _gla(gqp, gkp, gvp, lgp, rp, g_gla_norm[0], s_zero, n_seq=bp, c=gla_c, n_sub=gla_sub,
                   steps=t // (gla_c * gla_sub))
    obs, ss = _gla(gqs, gks, gvs, lgs, rs, g_gla_norm[0], state_gla[0], n_seq=bs, c=ts, n_sub=1, steps=1)

    out_w = (g_ffn_norm[0], w_out_b, w_route, b_route)
    bufs = _outproj(oap, obp, xp, mod_tab, mod_tab, mod_tab, *out_w, tm=to,
                    mod_specs=(pmod(2, to), pmod(3, to, 1), pmod(4, to, 1)), n_total=n_tot, row0=0,
                    fill_steps=-(-n_s // to), oa_transposed=True)
    x1, h2p, route = _outproj(oas, obs, xs, mod_s[2], mod_s[3], mod_s[4], *out_w, tm=n_s,
                              mod_specs=(smod, smod, smod), n_total=n_tot, row0=n_p, bufs=bufs)

    n_blocks = -(-(n_tot * TOP_K + N_EXPERTS * (tb - 1)) // tb)
    pos, pend_tab = _rank(route, tb=tb)
    pend = pend_tab[:, 0]
    block_e = jnp.minimum(jnp.sum(pend[None, :] <= (jnp.arange(n_blocks, dtype=jnp.int32) * tb)[:, None],
                                  axis=1), N_EXPERTS - 1).astype(jnp.int32)
    n_used = pend[N_EXPERTS - 1:] // tb
    xsort = _dispatch(pos.reshape(-1), pend, h2p, tm=td, tb=tb, n_blocks=n_blocks)
    yb = _experts(block_e, n_used, xsort, w_expert_gate[0], w_expert_up[0], w_expert_down[0], tb=tb)

    pos = pos.reshape(TOP_K, n_tot)
    pos_p = pos[:, :n_p].reshape(-1)
    pos_s = pos[:, n_p:].reshape(-1)
    gt_p = pl.BlockSpec((1, 1, d), lambda i, p: ((i // (t // tc)) * N_MOD + 5, 0, 0))
    gt_s = pl.BlockSpec((n_s, d), lambda i, p: (0, 0))
    y_p = _combine(pos_p, yb, x1, route, mod_tab, g_final, tm=tc, n=n_p, row0=0, gt_spec=gt_p)
    y_s = _combine(pos_s, yb, x1, route, mod_s[5], g_final, tm=n_s, n=n_s, row0=n_p, gt_spec=gt_s)

    kv_shape = (SWA_KV_HEADS, SWA_HEAD_DIM)
    k_state_p = kp.reshape(bp, t, *kv_shape)[:, -WINDOW:][None]
    v_state_p = vp.reshape(bp, t, *kv_shape)[:, -WINDOW:][None]
    return (y_p.reshape(bp, t, d), y_s.reshape(bs, ts, d), k_state_p, v_state_p, sp[None],
            ks.reshape(bs, ts, *kv_shape)[None], vs.reshape(bs, ts, *kv_shape)[None], ss[None])
```

```python
import functools

import jax
import jax.numpy as jnp
from jax import lax
from jax.experimental import pallas as pl
from jax.experimental.pallas import tpu as pltpu

f32 = jnp.float32
bf16 = jnp.bfloat16
u32 = jnp.uint32

D_MODEL = 2048
N_MOD = 6
EPS = 1e-6
NEG_INF = -1e30

SWA_HEAD_DIM = 64
SWA_KV_HEADS = 2
SWA_GROUP = 8
SWA_WIDTH = SWA_KV_HEADS * SWA_GROUP * SWA_HEAD_DIM
KV_WIDTH = SWA_KV_HEADS * SWA_HEAD_DIM
WINDOW = 128
CHUNK = 64

GLA_HEADS = 4
GLA_DK = 128
GLA_DV = 256
GLA_QK_WIDTH = GLA_HEADS * GLA_DK
GLA_V_WIDTH = GLA_HEADS * GLA_DV
GLA_GATE_RANK = 16
GLA_GATE_NORM = 16.0

N_GROUPS = 4
EXPERTS_PER_GROUP = 8
N_EXPERTS = N_GROUPS * EXPERTS_PER_GROUP
TOP_K = 2
EXPERT_HIDDEN = D_MODEL // 4

_C_Q = 0
_C_K = _C_Q + SWA_WIDTH
_C_V = _C_K + KV_WIDTH
_C_GQ = _C_V + KV_WIDTH
_C_GK = _C_GQ + GLA_QK_WIDTH
_C_GV = _C_GK + GLA_QK_WIDTH
_C_R = _C_GV + GLA_V_WIDTH
_C_AB = _C_R + GLA_V_WIDTH

LANES = 128
SUBLANES = 8
PACKED_WIDTH = D_MODEL // 2
assert PACKED_WIDTH == SUBLANES * LANES
VMEM_LIMIT = 56 * 1024 * 1024

ROUTE_E0, ROUTE_E1, ROUTE_G0, ROUTE_G1 = 0, 1, 2, 3


def _dot(a, b):
    return jnp.dot(a, b, preferred_element_type=f32)


def _dot_nt(a, b):
    return lax.dot_general(a, b, (((1,), (1,)), ((), ())), preferred_element_type=f32)


def _dot_tn(a, b):
    return lax.dot_general(a, b, (((0,), (0,)), ((), ())), preferred_element_type=f32)


def _silu(x):
    return x / (1.0 + jnp.exp(-x))


def _rows(ref):
    v = ref[...]
    return v.reshape(v.shape[-2:])


def _rms(x):
    return x * lax.rsqrt(jnp.mean(x * x, axis=-1, keepdims=True) + EPS)


def _resident(shape):
    return pl.BlockSpec(shape, lambda *_: (0,) * len(shape), pipeline_mode=pl.Buffered(1))


def _mod_spec(comp, tiles_per_seq, n_tiles, lag=0):
    def index(i):
        tile = jnp.minimum(jnp.maximum(i - lag, 0), n_tiles - 1)
        return ((tile // tiles_per_seq) * N_MOD + comp, 0, 0)
    return pl.BlockSpec((1, 1, D_MODEL), index)


def _adaln_kernel(c_ref, w_ref, b_ref, o_ref):
    a = _silu(c_ref[...]).astype(bf16)
    o_ref[...] = _dot(a, w_ref[...].astype(bf16)) + b_ref[...]


def _adaln(c_all, w_ada, b_ada, tn=1024):
    r = c_all.shape[0]
    n = w_ada.shape[1]
    return pl.pallas_call(
        _adaln_kernel,
        out_shape=jax.ShapeDtypeStruct((r, n), f32),
        grid=(n // tn,),
        in_specs=[pl.BlockSpec((r, D_MODEL), lambda j: (0, 0)),
                  pl.BlockSpec((D_MODEL, tn), lambda j: (0, j)),
                  pl.BlockSpec((1, tn), lambda j: (0, j))],
        out_specs=pl.BlockSpec((r, tn), lambda j: (0, j)),
        compiler_params=pltpu.CompilerParams(dimension_semantics=("arbitrary",),
                                             vmem_limit_bytes=VMEM_LIMIT),
        name="adaln",
    )(c_all, w_ada, b_ada.reshape(1, n))


def _proj_kernel(x_ref, sh_ref, sc_ref, g_ref, wq_ref, w_ref, wab_ref, wg_ref, bg_ref,
                 q_ref, k_ref, v_ref, gq_ref, gk_ref, gv_ref, r_ref, lg_ref, *, q_transposed):
    h = (_rms(x_ref[...]) * g_ref[...]) * (1.0 + _rows(sc_ref)) + _rows(sh_ref)
    hb = h.astype(bf16)
    if q_transposed:
        q_ref[...] = (_dot_nt(wq_ref[...], hb) * (SWA_HEAD_DIM ** -0.5)).astype(bf16)
    else:
        q_ref[...] = _dot(hb, wq_ref[...]).astype(bf16)
    col = lambda c: c - _C_K
    kv = _dot(hb, w_ref[:, col(_C_K):col(_C_GQ)])
    k_ref[...] = kv[:, :KV_WIDTH]
    v_ref[...] = kv[:, KV_WIDTH:]
    gq_ref[...] = _dot(hb, w_ref[:, col(_C_GQ):col(_C_GK)]).astype(bf16)
    gk_ref[...] = _dot(hb, w_ref[:, col(_C_GK):col(_C_GV)]).astype(bf16)
    gv_ref[...] = _dot(hb, w_ref[:, col(_C_GV):col(_C_R)]).astype(bf16)
    r_ref[...] = _dot(hb, w_ref[:, col(_C_R):col(_C_AB)]).astype(bf16)
    ab = _dot(hb, wab_ref[...])
    z = _dot(ab.astype(bf16), wg_ref[...]) + bg_ref[...]
    log_sig = jnp.minimum(z, 0.0) - jnp.log1p(jnp.exp(-jnp.abs(z)))
    lg_ref[...] = log_sig / GLA_GATE_NORM


def _proj(x2d, sh, sc, g_mix, w_q, w_rest, w_ab, w_gate, b_gate, *, tm, mod_specs, q_transposed):
    n = x2d.shape[0]
    row = lambda w: pl.BlockSpec((tm, w), lambda i: (i, 0))
    outs = [(KV_WIDTH, f32), (KV_WIDTH, f32), (GLA_QK_WIDTH, bf16),
            (GLA_QK_WIDTH, bf16), (GLA_V_WIDTH, bf16), (GLA_V_WIDTH, bf16), (GLA_QK_WIDTH, f32)]
    if q_transposed:
        q_shape, q_spec = (SWA_WIDTH, n), pl.BlockSpec((SWA_WIDTH, tm), lambda i: (0, i))
    else:
        q_shape, q_spec = (n, SWA_WIDTH), row(SWA_WIDTH)
    return pl.pallas_call(
        functools.partial(_proj_kernel, q_transposed=q_transposed),
        out_shape=[jax.ShapeDtypeStruct(q_shape, bf16)]
                  + [jax.ShapeDtypeStruct((n, w), dt) for w, dt in outs],
        grid=(n // tm,),
        in_specs=[row(D_MODEL), mod_specs[0], mod_specs[1],
                  _resident((1, D_MODEL)), _resident(w_q.shape), _resident(w_rest.shape),
                  _resident(w_ab.shape), _resident(w_gate.shape), _resident((1, GLA_QK_WIDTH))],
        out_specs=[q_spec] + [row(w) for w, _ in outs],
        compiler_params=pltpu.CompilerParams(dimension_semantics=("parallel",),
                                             vmem_limit_bytes=VMEM_LIMIT),
        name="proj",
    )(x2d, sh, sc, g_mix.reshape(1, D_MODEL), w_q, w_rest, w_ab, w_gate, b_gate.reshape(1, -1))


def _swa_kernel(q_ref, kp_ref, vp_ref, kc_ref, vc_ref, sink_ref, o_ref, *, tq):
    pad = jnp.zeros((WINDOW - tq, KV_WIDTH), f32)
    k_all = jnp.concatenate([kp_ref[...], kc_ref[...], pad], axis=0).astype(bf16)
    v_all = jnp.concatenate([vp_ref[...], vc_ref[...], pad], axis=0).astype(bf16)
    rows, cols = SWA_GROUP * tq, 2 * WINDOW
    valid = lax.broadcasted_iota(jnp.int32, (rows, cols), 1) < WINDOW + tq
    q = q_ref[...]
    outs = []
    for j in range(SWA_KV_HEADS):
        heads = [q[:, (j * SWA_GROUP + g) * SWA_HEAD_DIM:(j * SWA_GROUP + g + 1) * SWA_HEAD_DIM]
                 for g in range(SWA_GROUP)]
        qs = jnp.concatenate(heads, axis=0)
        kj = k_all[:, j * SWA_HEAD_DIM:(j + 1) * SWA_HEAD_DIM]
        vj = v_all[:, j * SWA_HEAD_DIM:(j + 1) * SWA_HEAD_DIM]
        s = _dot_nt(qs, kj) * (SWA_HEAD_DIM ** -0.5)
        s = jnp.where(valid, s, NEG_INF)
        sink = sink_ref[j]
        m = jnp.maximum(jnp.max(s, axis=1, keepdims=True), sink)
        p = jnp.exp(s - m)
        den = jnp.sum(p, axis=1, keepdims=True) + jnp.exp(sink - m)
        o = _dot(p.astype(bf16), vj) / den
        outs.append(jnp.concatenate([o[g * tq:(g + 1) * tq] for g in range(SWA_GROUP)], axis=1))
    o_ref[...] = jnp.concatenate(outs, axis=1).astype(bf16)


def _swa(q, k_past, v_past, k_new, v_new, sinks, *, n_seq, tq):
    sink_rows = jnp.repeat(sinks.astype(f32).reshape(SWA_KV_HEADS, SWA_GROUP), tq, axis=1)
    sink_rows = sink_rows.reshape(SWA_KV_HEADS, SWA_GROUP * tq, 1)
    new = lambda w: pl.BlockSpec((tq, w), lambda b: (b, 0))
    past = pl.BlockSpec((WINDOW, KV_WIDTH), lambda b: (b, 0))
    return pl.pallas_call(
        functools.partial(_swa_kernel, tq=tq),
        out_shape=jax.ShapeDtypeStruct(q.shape, bf16),
        grid=(n_seq,),
        in_specs=[new(SWA_WIDTH), past, past, new(KV_WIDTH), new(KV_WIDTH),
                  pl.BlockSpec(sink_rows.shape, lambda b: (0, 0, 0))],
        out_specs=new(SWA_WIDTH),
        compiler_params=pltpu.CompilerParams(dimension_semantics=("parallel",),
                                             vmem_limit_bytes=VMEM_LIMIT),
        name="swa",
    )(q, k_past, v_past, k_new, v_new, sink_rows)


def _swa_t_kernel(q_ref, kp_ref, vp_ref, kc_ref, vc_ref, sink_ref, bias0_ref, bias_ref, o_ref, *, n_sub):
    tq = WINDOW
    k_all = jnp.concatenate([kp_ref[...], kc_ref[...]], axis=0).astype(bf16)
    v_all = jnp.concatenate([vp_ref[...], vc_ref[...]], axis=0).astype(bf16)
    for sub in range(n_sub):
        keys = slice(sub * tq, sub * tq + 2 * WINDOW)
        toks = slice(sub * tq, (sub + 1) * tq)
        bias = (bias0_ref if sub == 0 else bias_ref)[0]
        for j in range(SWA_KV_HEADS):
            head = lambda g: slice((j * SWA_GROUP + g) * SWA_HEAD_DIM, (j * SWA_GROUP + g + 1) * SWA_HEAD_DIM)
            qs = jnp.concatenate([q_ref[head(g), toks] for g in range(SWA_GROUP)], axis=1)
            kj = k_all[keys, j * SWA_HEAD_DIM:(j + 1) * SWA_HEAD_DIM]
            vj = v_all[keys, j * SWA_HEAD_DIM:(j + 1) * SWA_HEAD_DIM]
            s = _dot(kj, qs) + bias
            sink = sink_ref[j]
            m = jnp.maximum(jnp.max(s, axis=0, keepdims=True), sink)
            p = jnp.exp(s - m)
            den = jnp.sum(p, axis=0, keepdims=True) + jnp.exp(sink - m)
            o = _dot_tn(vj, p.astype(bf16)) / den
            for g in range(SWA_GROUP):
                o_ref[head(g), toks] = o[:, g * tq:(g + 1) * tq].astype(bf16)


def _swa_t(q_t, k, v, sinks, *, n_seq, tiles, n_sub):
    tq = WINDOW
    steps = tiles // n_sub
    sink_cols = jnp.repeat(sinks.astype(f32).reshape(SWA_KV_HEADS, SWA_GROUP), tq, axis=1)
    sink_cols = sink_cols.reshape(SWA_KV_HEADS, 1, SWA_GROUP * tq)
    shape = (2 * WINDOW, SWA_GROUP * tq)
    key = lax.broadcasted_iota(jnp.int32, shape, 0)
    q_chunk = (lax.broadcasted_iota(jnp.int32, shape, 1) % tq) // CHUNK
    band = (key // CHUNK >= q_chunk) & (key // CHUNK <= q_chunk + WINDOW // CHUNK)
    bias = jnp.where(jnp.stack([band & (key >= WINDOW), band]), 0.0, NEG_INF).astype(f32)
    bias0_spec = pl.BlockSpec((1,) + shape, lambda b, u: (jnp.minimum(u, 1), 0, 0))
    bias_spec = pl.BlockSpec((1,) + shape, lambda b, u: (1, 0, 0))
    qspec = pl.BlockSpec((SWA_WIDTH, n_sub * tq), lambda b, u: (0, b * steps + u))
    cur = pl.BlockSpec((n_sub * tq, KV_WIDTH), lambda b, u: (b * steps + u, 0))
    prev = pl.BlockSpec((WINDOW, KV_WIDTH), lambda b, u: (b * tiles + jnp.maximum(n_sub * u - 1, 0), 0))
    return pl.pallas_call(
        functools.partial(_swa_t_kernel, n_sub=n_sub),
        out_shape=jax.ShapeDtypeStruct(q_t.shape, bf16),
        grid=(n_seq, steps),
        in_specs=[qspec, prev, prev, cur, cur, pl.BlockSpec(sink_cols.shape, lambda b, u: (0, 0, 0)),
                  bias0_spec, bias_spec],
        out_specs=qspec,
        compiler_params=pltpu.CompilerParams(dimension_semantics=("parallel", "arbitrary"),
                                             vmem_limit_bytes=VMEM_LIMIT),
        name="swa_t",
    )(q_t, k, v, k, v, sink_cols, bias, bias)


def _gla_kernel(q_ref, k_ref, v_ref, lg_ref, r_ref, gh_ref, s0_ref, o_ref, s_ref, *, c, n_sub):
    @pl.when(pl.program_id(1) == 0)
    def _():
        s_ref[...] = s0_ref[...]

    row = lax.broadcasted_iota(jnp.int32, (c, GLA_DK), 0)
    causal = (lax.broadcasted_iota(jnp.int32, (c, c), 0) >= lax.broadcasted_iota(jnp.int32, (c, c), 1))
    for sub in range(n_sub):
        ts = slice(sub * c, (sub + 1) * c)
        for h in range(GLA_HEADS):
            ks = slice(h * GLA_DK, (h + 1) * GLA_DK)
            vs = slice(h * GLA_DV, (h + 1) * GLA_DV)
            b = lg_ref[ts, ks]
            step = 1
            while step < c:
                b = b + jnp.where(row >= step, pltpu.roll(b, step, 0), 0.0)
                step *= 2
            b_last = b[c - 1:c, :]
            q = q_ref[ts, ks].astype(f32) * (GLA_DK ** -0.5)
            k = k_ref[ts, ks].astype(f32)
            v = v_ref[ts, vs]
            qd = (q * jnp.exp(b)).astype(bf16)
            kd = (k * jnp.exp(-b)).astype(bf16)
            kl = (k * jnp.exp(b_last - b)).astype(bf16)
            a = jnp.where(causal, _dot_nt(qd, kd), 0.0)
            s = s_ref[0, h]
            o = _dot(qd, s.astype(bf16)) + _dot(a.astype(bf16), v)
            decay = jnp.broadcast_to(jnp.exp(b_last), (GLA_DK, GLA_DK)).T
            s_ref[0, h] = s * jnp.concatenate([decay, decay], axis=1) + _dot_tn(kl, v)
            on = _rms(o) * gh_ref[...]
            o_ref[ts, vs] = (on * _silu(r_ref[ts, vs].astype(f32))).astype(bf16)


def _gla(gq, gk, gv, lg, r, g_head, s0, *, n_seq, c, n_sub, steps):
    rows = c * n_sub
    blk = lambda w: pl.BlockSpec((rows, w), lambda b, t: (b * steps + t, 0))
    state = pl.BlockSpec((1, GLA_HEADS, GLA_DK, GLA_DV), lambda b, t: (b, 0, 0, 0))
    return pl.pallas_call(
        functools.partial(_gla_kernel, c=c, n_sub=n_sub),
        out_shape=[jax.ShapeDtypeStruct(gv.shape, bf16),
                   jax.ShapeDtypeStruct((n_seq, GLA_HEADS, GLA_DK, GLA_DV), f32)],
        grid=(n_seq, steps),
        in_specs=[blk(GLA_QK_WIDTH), blk(GLA_QK_WIDTH), blk(GLA_V_WIDTH), blk(GLA_QK_WIDTH),
                  blk(GLA_V_WIDTH), pl.BlockSpec((1, GLA_DV), lambda b, t: (0, 0)), state],
        out_specs=[blk(GLA_V_WIDTH), state],
        compiler_params=pltpu.CompilerParams(dimension_semantics=("parallel", "arbitrary"),
                                             vmem_limit_bytes=VMEM_LIMIT),
        name="gla",
    )(gq, gk, gv, lg, r, g_head.reshape(1, GLA_DV), s0)


def _pack_pair(hb):
    w = hb.shape[1] // 2
    a = lax.bitcast_convert_type(hb[:, :w].astype(f32), u32)
    b = lax.bitcast_convert_type(hb[:, w:].astype(f32), u32)
    return a | (b >> 16)


def _unpack_pair(p):
    a = lax.bitcast_convert_type(p & jnp.uint32(0xFFFF0000), f32)
    b = lax.bitcast_convert_type(p << 16, f32)
    return a, b


def _load_row_tiles(ref, n):
    return jnp.concatenate([ref[pl.ds(s, n, stride=SUBLANES), :] for s in range(SUBLANES)], axis=1)


def _store_row_tiles(ref, val):
    n = val.shape[0]
    for s in range(SUBLANES):
        ref[pl.ds(s, n, stride=SUBLANES), :] = val[:, s * LANES:(s + 1) * LANES]


def _row_tile(ref, r):
    return ref.at[pl.ds(pl.multiple_of(r * SUBLANES, SUBLANES), SUBLANES), :]


def _outproj_kernel(*refs, n_alias, oa_transposed):
    (oa_ref, ob_ref, x_ref, gt_ref, sh_ref, sc_ref, g_ref, w_ref, wr_ref, br_ref) = refs[:10]
    x1_ref, h2_ref, route_ref, x1_s = refs[10 + n_alias:]

    @pl.when(pl.program_id(0) == 0)
    def _():
        x1_s[...] = jnp.zeros(x1_s.shape, x1_s.dtype)

    h2 = (_rms(x1_s[...]) * g_ref[...]) * (1.0 + _rows(sc_ref)) + _rows(sh_ref)
    hi = h2.astype(bf16)
    _store_row_tiles(h2_ref, _pack_pair(hi))
    lo = (h2 - hi.astype(f32)).astype(bf16)
    r1 = _dot(hi, wr_ref[...])
    logits = r1[:, :LANES] + r1[:, LANES:] + _dot(lo, wr_ref[:, :LANES]) + br_ref[...]
    lane = lax.broadcasted_iota(jnp.int32, logits.shape, 1)
    lane_f = lane.astype(f32)
    neg = float("-inf")
    first = lambda hit: jnp.min(jnp.where(hit, lane_f, float(LANES)), axis=1, keepdims=True)
    lg_g = jnp.where(lane < N_GROUPS, logits, neg)
    g_max = jnp.max(lg_g, axis=1, keepdims=True)
    g_sel = first(lg_g == g_max)
    p_sel = 1.0 / jnp.sum(jnp.exp(lg_g - g_max), axis=1, keepdims=True)
    e_lane = lane - N_GROUPS
    in_group = (e_lane >= 0) & (e_lane < N_EXPERTS) & ((e_lane // EXPERTS_PER_GROUP).astype(f32) == g_sel)
    lg_e = jnp.where(in_group, logits, neg)
    v1 = jnp.max(lg_e, axis=1, keepdims=True)
    i1 = first(lg_e == v1)
    lg_e2 = jnp.where(lane_f == i1, neg, lg_e)
    v2 = jnp.max(lg_e2, axis=1, keepdims=True)
    i2 = first(lg_e2 == v2)
    e = jnp.exp(v2 - v1)
    g1 = p_sel / (1.0 + e)
    g2 = p_sel * e / (1.0 + e)
    rec = jnp.where(lane == ROUTE_E0, i1 - N_GROUPS, 0.0)
    rec = jnp.where(lane == ROUTE_E1, i2 - N_GROUPS, rec)
    rec = jnp.where(lane == ROUTE_G0, g1, rec)
    rec = jnp.where(lane == ROUTE_G1, g2, rec)
    route_ref[...] = rec

    oa_dot = _dot_tn if oa_transposed else _dot
    mix = oa_dot(oa_ref[...], w_ref[:SWA_WIDTH, :]) + _dot(ob_ref[...], w_ref[SWA_WIDTH:, :])
    x1 = x_ref[...] + _rows(gt_ref) * mix
    x1_ref[...] = x1
    x1_s[...] = x1


def _outproj(oa, ob, x2d, gt, sh, sc, g_ffn, w_out, w_route, b_route, *, tm, mod_specs,
             n_total, row0, bufs=None, fill_steps=0, oa_transposed=False):
    n = x2d.shape[0]
    tiles = n // tm
    blocks = tiles + fill_steps
    row = lambda w: pl.BlockSpec((tm, w), lambda i: (jnp.minimum(i, tiles - 1), 0))
    oa_spec = row(SWA_WIDTH)
    if oa_transposed:
        oa_spec = pl.BlockSpec((SWA_WIDTH, tm), lambda i: (0, jnp.minimum(i, tiles - 1)))
    head_blk = lambda i: row0 // tm + jnp.minimum(i, blocks - 1)
    tail_blk = lambda i: row0 // tm + jnp.maximum(i - 1, 0)
    out_x1 = pl.BlockSpec((tm, D_MODEL), lambda i: (head_blk(i), 0))
    out_tiles = pl.BlockSpec((tm * SUBLANES, LANES), lambda i: (tail_blk(i), 0))
    out_route = pl.BlockSpec((tm, LANES), lambda i: (tail_blk(i), 0))
    alias_in = list(bufs) if bufs is not None else []
    n_in = 10
    return pl.pallas_call(
        functools.partial(_outproj_kernel, n_alias=len(alias_in), oa_transposed=oa_transposed),
        out_shape=[jax.ShapeDtypeStruct((n_total, D_MODEL), f32),
                   jax.ShapeDtypeStruct((n_total * SUBLANES, LANES), u32),
                   jax.ShapeDtypeStruct((n_total, LANES), f32)],
        grid=(blocks + 1,),
        in_specs=[oa_spec, row(GLA_V_WIDTH), row(D_MODEL), mod_specs[0], mod_specs[1],
                  mod_specs[2], _resident((1, D_MODEL)), _resident(w_out.shape),
                  _resident(w_route.shape), _resident((1, LANES))]
                 + [pl.BlockSpec(memory_space=pl.ANY)] * len(alias_in),
        out_specs=[out_x1, out_tiles, out_route],
        scratch_shapes=[pltpu.VMEM((tm, D_MODEL), f32)],
        input_output_aliases={n_in + a: a for a in range(len(alias_in))},
        compiler_params=pltpu.CompilerParams(dimension_semantics=("arbitrary",),
                                             vmem_limit_bytes=VMEM_LIMIT),
        name="outproj",
    )(oa, ob, x2d, gt, sh, sc, g_ffn.reshape(1, D_MODEL), w_out, w_route, b_route, *alias_in)


def _rank_kernel(route_ref, pos_ref, pend_ref, rec_s, *, n_chunks, tb):
    n_e = N_EXPERTS
    expert = lax.broadcasted_iota(jnp.int32, (n_e, LANES), 0)
    expert_f = expert.astype(f32)
    earlier = (lax.broadcasted_iota(jnp.int32, (LANES, LANES), 0)
               < lax.broadcasted_iota(jnp.int32, (LANES, LANES), 1)).astype(bf16)

    def onehots(c):
        rec = rec_s[c]
        return expert_f == rec[ROUTE_E0:ROUTE_E0 + 1, :], expert_f == rec[ROUTE_E1:ROUTE_E1 + 1, :]

    def count(c, cnt):
        rec_s[c] = route_ref[pl.ds(pl.multiple_of(c * LANES, LANES), LANES), :].T[:SUBLANES, :]
        h0, h1 = onehots(c)
        return cnt + jnp.sum((h0 | h1).astype(f32), axis=1, keepdims=True)

    cnt = lax.fori_loop(0, n_chunks, count, jnp.zeros((n_e, 1), f32))
    padded = jnp.floor((cnt + (tb - 1.0)) / tb) * tb
    end = jnp.broadcast_to(padded, (n_e, LANES))
    step = 1
    while step < n_e:
        end = end + jnp.where(expert >= step, pltpu.roll(end, step, 0), 0.0)
        step *= 2
    pend_ref[...] = end.astype(jnp.int32)

    def rank(c, base):
        h0, h1 = onehots(c)
        both = h0 | h1
        off = _dot(both.astype(bf16), earlier) + base
        pos_ref[0, pl.ds(c, 1), :] = jnp.sum(jnp.where(h0, off, 0.0), axis=0, keepdims=True).astype(jnp.int32)
        pos_ref[1, pl.ds(c, 1), :] = jnp.sum(jnp.where(h1, off, 0.0), axis=0, keepdims=True).astype(jnp.int32)
        return base + jnp.sum(both.astype(f32), axis=1, keepdims=True)

    lax.fori_loop(0, n_chunks, rank, (end - padded)[:, 0:1])


def _rank(route, *, tb):
    n = route.shape[0]
    n_chunks = n // LANES
    return pl.pallas_call(
        functools.partial(_rank_kernel, n_chunks=n_chunks, tb=tb),
        out_shape=[jax.ShapeDtypeStruct((TOP_K, n_chunks, LANES), jnp.int32),
                   jax.ShapeDtypeStruct((N_EXPERTS, LANES), jnp.int32)],
        in_specs=[_resident(route.shape)],
        scratch_shapes=[pltpu.VMEM((n_chunks, SUBLANES, LANES), f32)],
        compiler_params=pltpu.CompilerParams(vmem_limit_bytes=VMEM_LIMIT),
        name="rank",
    )(route)


def _dispatch_kernel(pos_ref, pend_ref, h2_ref, xs_hbm, stage, zbuf, sem, zsem, *, tm, tb, n_tot, steps):
    i = pl.program_id(0)
    slot = i % 2

    blk = tb * SUBLANES

    def wait_slot(s):
        for _ in range(TOP_K):
            pltpu.make_async_copy(stage.at[s], xs_hbm.at[pl.ds(0, tm * SUBLANES), :], sem.at[s]).wait()

    @pl.when(i == 0)
    def _():
        zbuf[...] = jnp.zeros(zbuf.shape, zbuf.dtype)

        def fill_block(b, start):
            cp = pltpu.make_async_copy(zbuf, xs_hbm.at[pl.ds(pl.multiple_of(b * blk, blk), blk), :], zsem)
            cp.start() if start else cp.wait()

        def fill(e, start):
            end = pend_ref[e]
            prev = jnp.where(e > 0, pend_ref[jnp.maximum(e - 1, 0)], 0)

            @pl.when(end > prev)
            def _():
                fill_block(end // tb - 1, start)

        fill_unused = fill_block

        first_unused = pend_ref[N_EXPERTS - 1] // tb
        n_blocks = xs_hbm.shape[0] // blk
        lax.fori_loop(0, N_EXPERTS, lambda e, c: (fill(e, True), c)[1], 0)
        lax.fori_loop(first_unused, n_blocks, lambda b, c: (fill_unused(b, True), c)[1], 0)
        lax.fori_loop(0, N_EXPERTS, lambda e, c: (fill(e, False), c)[1], 0)
        lax.fori_loop(first_unused, n_blocks, lambda b, c: (fill_unused(b, False), c)[1], 0)

    @pl.when(i >= 2)
    def _():
        wait_slot(slot)

    stage[slot] = h2_ref[...]

    def scatter(r, carry):
        for k in range(TOP_K):
            d = pos_ref[k * n_tot + i * tm + r]
            pltpu.make_async_copy(_row_tile(stage.at[slot], r), _row_tile(xs_hbm, d),
                                  sem.at[slot]).start(priority=k)
        return carry

    lax.fori_loop(0, tm, scatter, 0, unroll=8)

    @pl.when(i == steps - 1)
    def _():
        wait_slot(slot)
        if steps > 1:
            wait_slot(1 - slot)


def _dispatch(pos_flat, pend, h2p, *, tm, tb, n_blocks):
    n_tot = h2p.shape[0] // SUBLANES
    steps = n_tot // tm
    return pl.pallas_call(
        functools.partial(_dispatch_kernel, tm=tm, tb=tb, n_tot=n_tot, steps=steps),
        out_shape=jax.ShapeDtypeStruct((n_blocks * tb * SUBLANES, LANES), u32),
        grid_spec=pltpu.PrefetchScalarGridSpec(
            num_scalar_prefetch=2,
            grid=(steps,),
            in_specs=[pl.BlockSpec((tm * SUBLANES, LANES), lambda i, pos, pend: (i, 0))],
            out_specs=pl.BlockSpec(memory_space=pl.ANY),
            scratch_shapes=[pltpu.VMEM((2, tm * SUBLANES, LANES), u32),
                            pltpu.VMEM((tb * SUBLANES, LANES), u32),
                            pltpu.SemaphoreType.DMA((2,)), pltpu.SemaphoreType.DMA(())]),
        compiler_params=pltpu.CompilerParams(dimension_semantics=("arbitrary",),
                                             vmem_limit_bytes=VMEM_LIMIT),
        name="dispatch",
    )(pos_flat, pend, h2p)


def _expert_kernel(be_ref, nu_ref, xs_ref, wg_ref, wu_ref, wd_ref, y_ref, wg_b, wu_b, wd_b, *, tb):
    i = pl.program_id(0)
    used = i < nu_ref[0]

    @pl.when(used & ((i == 0) | (be_ref[i] != be_ref[jnp.maximum(i - 1, 0)])))
    def _():
        wg_b[...] = wg_ref[0].astype(bf16)
        wu_b[...] = wu_ref[0].astype(bf16)
        wd_b[...] = wd_ref[0].astype(bf16)

    @pl.when(used)
    def _():
        a, b = _unpack_pair(_load_row_tiles(xs_ref, tb))
        x = jnp.concatenate([a.astype(bf16), b.astype(bf16)], axis=1)
        g = _dot(x, wg_b[...])
        u = _dot(x, wu_b[...])
        y = _dot((_silu(g) * u).astype(bf16), wd_b[...])
        _store_row_tiles(y_ref, _pack_pair(y.astype(bf16)))

    @pl.when(jnp.logical_not(used))
    def _():
        y_ref[...] = jnp.zeros(y_ref.shape, y_ref.dtype)


def _experts(block_e, n_used, xs, w_eg, w_eu, w_ed, *, tb):
    n_blocks = block_e.shape[0]
    blk = (tb * SUBLANES, LANES)
    wspec = lambda shape: pl.BlockSpec((1,) + shape, lambda i, be, nu: (be[i], 0, 0))
    return pl.pallas_call(
        functools.partial(_expert_kernel, tb=tb),
        out_shape=jax.ShapeDtypeStruct(xs.shape, u32),
        grid_spec=pltpu.PrefetchScalarGridSpec(
            num_scalar_prefetch=2,
            grid=(n_blocks,),
            in_specs=[pl.BlockSpec(blk, lambda i, be, nu: (jnp.minimum(i, nu[0] - 1), 0)),
                      wspec((D_MODEL, EXPERT_HIDDEN)), wspec((D_MODEL, EXPERT_HIDDEN)),
                      wspec((EXPERT_HIDDEN, D_MODEL))],
            out_specs=pl.BlockSpec(blk, lambda i, be, nu: (i, 0)),
            scratch_shapes=[pltpu.VMEM((D_MODEL, EXPERT_HIDDEN), bf16),
                            pltpu.VMEM((D_MODEL, EXPERT_HIDDEN), bf16),
                            pltpu.VMEM((EXPERT_HIDDEN, D_MODEL), bf16)]),
        compiler_params=pltpu.CompilerParams(dimension_semantics=("arbitrary",),
                                             vmem_limit_bytes=VMEM_LIMIT),
        name="experts",
    )(block_e, n_used, xs, w_eg, w_eu, w_ed)


def _combine_kernel(pos_ref, yb_hbm, x1_ref, route_ref, gt_ref, gf_ref, y_ref, ybuf, sem, *, tm):
    i = pl.program_id(0)
    slot = i % 2

    def start(blk, s):
        def body(r, carry):
            for k in range(TOP_K):
                t = pos_ref[(k * pl.num_programs(0) + blk) * tm + r]
                pltpu.make_async_copy(_row_tile(yb_hbm, t), _row_tile(ybuf.at[s, k], r),
                                      sem.at[s]).start(priority=k)
            return carry
        lax.fori_loop(0, tm, body, 0, unroll=8)

    @pl.when(i == 0)
    def _():
        start(0, 0)

    @pl.when(i + 1 < pl.num_programs(0))
    def _():
        start(i + 1, 1 - slot)

    for k in range(TOP_K):
        pltpu.make_async_copy(yb_hbm.at[pl.ds(0, tm * SUBLANES), :], ybuf.at[slot, k], sem.at[slot]).wait()
    route = route_ref[...]
    a0, b0 = _unpack_pair(_load_row_tiles(ybuf.at[slot, 0], tm))
    a1, b1 = _unpack_pair(_load_row_tiles(ybuf.at[slot, 1], tm))
    g0 = route[:, ROUTE_G0:ROUTE_G0 + 1]
    g1 = route[:, ROUTE_G1:ROUTE_G1 + 1]
    moe = jnp.concatenate([a0 * g0 + a1 * g1, b0 * g0 + b1 * g1], axis=1)
    x2 = x1_ref[...] + _rows(gt_ref) * moe
    y_ref[...] = _rms(x2) * gf_ref[...]


def _combine(pos_km, yb, x1, route, gt, g_final, *, tm, n, row0, gt_spec):
    blk0 = row0 // tm
    return pl.pallas_call(
        functools.partial(_combine_kernel, tm=tm),
        out_shape=jax.ShapeDtypeStruct((n, D_MODEL), f32),
        grid_spec=pltpu.PrefetchScalarGridSpec(
            num_scalar_prefetch=1,
            grid=(n // tm,),
            in_specs=[pl.BlockSpec(memory_space=pl.ANY),
                      pl.BlockSpec((tm, D_MODEL), lambda i, pos: (blk0 + i, 0)),
                      pl.BlockSpec((tm, LANES), lambda i, pos: (blk0 + i, 0)),
                      gt_spec,
                      pl.BlockSpec((1, D_MODEL), lambda i, pos: (0, 0))],
            out_specs=pl.BlockSpec((tm, D_MODEL), lambda i, pos: (i, 0)),
            scratch_shapes=[pltpu.VMEM((2, TOP_K, tm * SUBLANES, LANES), u32),
                            pltpu.SemaphoreType.DMA((2,))]),
        compiler_params=pltpu.CompilerParams(dimension_semantics=("arbitrary",),
                                             vmem_limit_bytes=VMEM_LIMIT),
        name="combine",
    )(pos_km, yb, x1, route, gt, g_final.reshape(1, D_MODEL))


def kernel(x_prompt, x_sample, cache_swa_k, cache_swa_v, state_gla, c_prompt, c_sample, g_mix_norm, g_ffn_norm, w_ada, b_ada, w_in, attn_sinks, w_gla_gate, b_gla_gate, g_gla_norm, w_out, w_router_group, b_router_group, w_router_expert, b_router_expert, w_expert_gate, w_expert_up, w_expert_down, g_final):
    depth = w_in.shape[0]
    assert depth == 1
    bp, t, d = x_prompt.shape
    bs, ts, _ = x_sample.shape
    n_p, n_s = bp * t, bs * ts
    n_tot = n_p + n_s
    assert n_tot % LANES == 0
    tm = 512
    to = 256
    tb = 512
    tc = 256
    td = LANES
    gla_c = 128
    gla_sub = 4
    swa_sub = 2

    w_in0 = w_in[0]
    w_q = w_in0[:, :_C_K].astype(bf16)
    w_q_t = w_q.T
    w_rest = w_in0[:, _C_K:_C_AB].astype(bf16)
    w_ab = jnp.pad(w_in0[:, _C_AB:], ((0, 0), (0, LANES - GLA_GATE_RANK))).astype(bf16)
    w_gate = jnp.pad(w_gla_gate[0], ((0, LANES - GLA_GATE_RANK), (0, 0))).astype(bf16)
    w_out_b = w_out[0].astype(bf16)
    n_r = N_GROUPS + N_EXPERTS
    w_r = jnp.pad(jnp.concatenate([w_router_group[0], w_router_expert[0]], axis=1),
                  ((0, 0), (0, LANES - n_r)))
    w_r_hi = w_r.astype(bf16)
    w_r_lo = (w_r - w_r_hi.astype(f32)).astype(bf16)
    w_route = jnp.concatenate([w_r_hi, w_r_lo], axis=1)
    b_route = jnp.pad(jnp.concatenate([b_router_group[0], b_router_expert[0]]),
                      (0, LANES - n_r)).reshape(1, LANES)

    c_all = jnp.concatenate([c_prompt, c_sample], axis=0)
    mod = _adaln(c_all, w_ada[0], b_ada[0])
    mod_tab = mod.reshape((bp + bs) * N_MOD, 1, d)
    mod_s = jnp.repeat(mod[bp:].reshape(bs, N_MOD, d), ts, axis=0)
    mod_s = [mod_s[:, m] for m in range(N_MOD)]
    pmod = lambda comp, tile, lag=0: _mod_spec(comp, t // tile, n_p // tile, lag)
    smod = pl.BlockSpec((n_s, d), lambda i: (0, 0))

    xp = x_prompt.reshape(n_p, d)
    xs = x_sample.reshape(n_s, d)
    proj_w = (w_rest, w_ab, w_gate, b_gla_gate[0])
    qp, kp, vp, gqp, gkp, gvp, rp, lgp = _proj(xp, mod_tab, mod_tab, g_mix_norm[0], w_q_t, *proj_w, tm=tm,
                                               mod_specs=(pmod(0, tm), pmod(1, tm)), q_transposed=True)
    qs, ks, vs, gqs, gks, gvs, rs, lgs = _proj(xs, mod_s[0], mod_s[1], g_mix_norm[0], w_q, *proj_w, tm=n_s,
                                               mod_specs=(smod, smod), q_transposed=False)

    sinks = attn_sinks[0]
    oap = _swa_t(qp, kp, vp, sinks, n_seq=bp, tiles=t // WINDOW, n_sub=swa_sub)
    ck = cache_swa_k[0].reshape(bs * WINDOW, KV_WIDTH)
    cv = cache_swa_v[0].reshape(bs * WINDOW, KV_WIDTH)
    oas = _swa(qs, ck, cv, ks, vs, sinks, n_seq=bs, tq=ts)
    s_zero = jnp.zeros((bp, GLA_HEADS, GLA_DK, GLA_DV), f32)
    obp, sp = _gla(gqp, gkp, gvp, lgp, rp, g_gla_norm[0], s_zero, n_seq=bp, c=gla_c, n_sub=gla_sub,
                   steps=t // (gla_c * gla_sub))
    obs, ss = _gla(gqs, gks, gvs, lgs, rs, g_gla_norm[0], state_gla[0], n_seq=bs, c=ts, n_sub=1, steps=1)

    out_w = (g_ffn_norm[0], w_out_b, w_route, b_route)
    bufs = _outproj(oap, obp, xp, mod_tab, mod_tab, mod_tab, *out_w, tm=to,
                    mod_specs=(pmod(2, to), pmod(3, to, 1), pmod(4, to, 1)), n_total=n_tot, row0=0,
                    fill_steps=-(-n_s // to), oa_transposed=True)
    x1, h2p, route = _outproj(oas, obs, xs, mod_s[2], mod_s[3], mod_s[4], *out_w, tm=n_s,
                              mod_specs=(smod, smod, smod), n_total=n_tot, row0=n_p, bufs=bufs)

    n_blocks = -(-(n_tot * TOP_K + N_EXPERTS * (tb - 1)) // tb)
    pos, pend_tab = _rank(route, tb=tb)
    pend = pend_tab[:, 0]
    block_e = jnp.minimum(jnp.sum(pend[None, :] <= (jnp.arange(n_blocks, dtype=jnp.int32) * tb)[:, None],
                                  axis=1), N_EXPERTS - 1).astype(jnp.int32)
    n_used = pend[N_EXPERTS - 1:] // tb
    xsort = _dispatch(pos.reshape(-1), pend, h2p, tm=td, tb=tb, n_blocks=n_blocks)
    yb = _experts(block_e, n_used, xsort, w_expert_gate[0], w_expert_up[0], w_expert_down[0], tb=tb)

    pos = pos.reshape(TOP_K, n_tot)
    pos_p = pos[:, :n_p].reshape(-1)
    pos_s = pos[:, n_p:].reshape(-1)
    gt_p = pl.BlockSpec((1, 1, d), lambda i, p: ((i // (t // tc)) * N_MOD + 5, 0, 0))
    gt_s = pl.BlockSpec((n_s, d), lambda i, p: (0, 0))
    y_p = _combine(pos_p, yb, x1, route, mod_tab, g_final, tm=tc, n=n_p, row0=0, gt_spec=gt_p)
    y_s = _combine(pos_s, yb, x1, route, mod_s[5], g_final, tm=n_s, n=n_s, row0=n_p, gt_spec=gt_s)

    kv_shape = (SWA_KV_HEADS, SWA_HEAD_DIM)
    k_state_p = kp.reshape(bp, t, *kv_shape)[:, -WINDOW:][None]
    v_state_p = vp.reshape(bp, t, *kv_shape)[:, -WINDOW:][None]
    return (y_p.reshape(bp, t, d), y_s.reshape(bs, ts, d), k_state_p, v_state_p, sp[None],
            ks.reshape(bs, ts, *kv_shape)[None], vs.reshape(bs, ts, *kv_shape)[None], ss[None])
```

```python
import functools

import jax
import jax.numpy as jnp
from jax import lax
from jax.experimental import pallas as pl
from jax.experimental.pallas import tpu as pltpu

f32 = jnp.float32
bf16 = jnp.bfloat16
u32 = jnp.uint32

D_MODEL = 2048
N_MOD = 6
EPS = 1e-6
NEG_INF = -1e30

SWA_HEAD_DIM = 64
SWA_KV_HEADS = 2
SWA_GROUP = 8
SWA_WIDTH = SWA_KV_HEADS * SWA_GROUP * SWA_HEAD_DIM
KV_WIDTH = SWA_KV_HEADS * SWA_HEAD_DIM
WINDOW = 128
CHUNK = 64

GLA_HEADS = 4
GLA_DK = 128
GLA_DV = 256
GLA_QK_WIDTH = GLA_HEADS * GLA_DK
GLA_V_WIDTH = GLA_HEADS * GLA_DV
GLA_GATE_RANK = 16
GLA_GATE_NORM = 16.0

N_GROUPS = 4
EXPERTS_PER_GROUP = 8
N_EXPERTS = N_GROUPS * EXPERTS_PER_GROUP
TOP_K = 2
EXPERT_HIDDEN = D_MODEL // 4

_C_Q = 0
_C_K = _C_Q + SWA_WIDTH
_C_V = _C_K + KV_WIDTH
_C_GQ = _C_V + KV_WIDTH
_C_GK = _C_GQ + GLA_QK_WIDTH
_C_GV = _C_GK + GLA_QK_WIDTH
_C_R = _C_GV + GLA_V_WIDTH
_C_AB = _C_R + GLA_V_WIDTH

LANES = 128
SUBLANES = 8
PACKED_WIDTH = D_MODEL // 2
assert PACKED_WIDTH == SUBLANES * LANES
VMEM_LIMIT = 56 * 1024 * 1024

ROUTE_E0, ROUTE_E1, ROUTE_G0, ROUTE_G1 = 0, 1, 2, 3


def _dot(a, b):
    return jnp.dot(a, b, preferred_element_type=f32)


def _dot_nt(a, b):
    return lax.dot_general(a, b, (((1,), (1,)), ((), ())), preferred_element_type=f32)


def _dot_tn(a, b):
    return lax.dot_general(a, b, (((0,), (0,)), ((), ())), preferred_element_type=f32)


def _silu(x):
    return x / (1.0 + jnp.exp(-x))


def _rows(ref):
    v = ref[...]
    return v.reshape(v.shape[-2:])


def _rms(x):
    return x * lax.rsqrt(jnp.mean(x * x, axis=-1, keepdims=True) + EPS)


def _resident(shape):
    return pl.BlockSpec(shape, lambda *_: (0,) * len(shape), pipeline_mode=pl.Buffered(1))


def _mod_spec(comp, tiles_per_seq, n_tiles, lag=0):
    def index(i):
        tile = jnp.minimum(jnp.maximum(i - lag, 0), n_tiles - 1)
        return ((tile // tiles_per_seq) * N_MOD + comp, 0, 0)
    return pl.BlockSpec((1, 1, D_MODEL), index)


def _adaln_kernel(c_ref, w_ref, b_ref, o_ref):
    a = _silu(c_ref[...]).astype(bf16)
    o_ref[...] = _dot(a, w_ref[...].astype(bf16)) + b_ref[...]


def _adaln(c_all, w_ada, b_ada, tn=1024):
    r = c_all.shape[0]
    n = w_ada.shape[1]
    return pl.pallas_call(
        _adaln_kernel,
        out_shape=jax.ShapeDtypeStruct((r, n), f32),
        grid=(n // tn,),
        in_specs=[pl.BlockSpec((r, D_MODEL), lambda j: (0, 0)),
                  pl.BlockSpec((D_MODEL, tn), lambda j: (0, j)),
                  pl.BlockSpec((1, tn), lambda j: (0, j))],
        out_specs=pl.BlockSpec((r, tn), lambda j: (0, j)),
        compiler_params=pltpu.CompilerParams(dimension_semantics=("arbitrary",),
                                             vmem_limit_bytes=VMEM_LIMIT),
        name="adaln",
    )(c_all, w_ada, b_ada.reshape(1, n))


def _proj_kernel(x_ref, sh_ref, sc_ref, g_ref, wq_ref, w_ref, wab_ref, wg_ref, bg_ref,
                 q_ref, k_ref, v_ref, gq_ref, gk_ref, gv_ref, r_ref, lg_ref, *, q_transposed):
    h = (_rms(x_ref[...]) * g_ref[...]) * (1.0 + _rows(sc_ref)) + _rows(sh_ref)
    hb = h.astype(bf16)
    if q_transposed:
        q_ref[...] = (_dot_nt(wq_ref[...], hb) * (SWA_HEAD_DIM ** -0.5)).astype(bf16)
    else:
        q_ref[...] = _dot(hb, wq_ref[...]).astype(bf16)
    col = lambda c: c - _C_K
    kv = _dot(hb, w_ref[:, col(_C_K):col(_C_GQ)])
    k_ref[...] = kv[:, :KV_WIDTH]
    v_ref[...] = kv[:, KV_WIDTH:]
    gq_ref[...] = _dot(hb, w_ref[:, col(_C_GQ):col(_C_GK)]).astype(bf16)
    gk_ref[...] = _dot(hb, w_ref[:, col(_C_GK):col(_C_GV)]).astype(bf16)
    gv_ref[...] = _dot(hb, w_ref[:, col(_C_GV):col(_C_R)]).astype(bf16)
    r_ref[...] = _dot(hb, w_ref[:, col(_C_R):col(_C_AB)]).astype(bf16)
    ab = _dot(hb, wab_ref[...])
    z = _dot(ab.astype(bf16), wg_ref[...]) + bg_ref[...]
    log_sig = jnp.minimum(z, 0.0) - jnp.log1p(jnp.exp(-jnp.abs(z)))
    lg_ref[...] = log_sig / GLA_GATE_NORM


def _proj(x2d, sh, sc, g_mix, w_q, w_rest, w_ab, w_gate, b_gate, *, tm, mod_specs, q_transposed):
    n = x2d.shape[0]
    row = lambda w: pl.BlockSpec((tm, w), lambda i: (i, 0))
    outs = [(KV_WIDTH, f32), (KV_WIDTH, f32), (GLA_QK_WIDTH, bf16),
            (GLA_QK_WIDTH, bf16), (GLA_V_WIDTH, bf16), (GLA_V_WIDTH, bf16), (GLA_QK_WIDTH, f32)]
    if q_transposed:
        q_shape, q_spec = (SWA_WIDTH, n), pl.BlockSpec((SWA_WIDTH, tm), lambda i: (0, i))
    else:
        q_shape, q_spec = (n, SWA_WIDTH), row(SWA_WIDTH)
    return pl.pallas_call(
        functools.partial(_proj_kernel, q_transposed=q_transposed),
        out_shape=[jax.ShapeDtypeStruct(q_shape, bf16)]
                  + [jax.ShapeDtypeStruct((n, w), dt) for w, dt in outs],
        grid=(n // tm,),
        in_specs=[row(D_MODEL), mod_specs[0], mod_specs[1],
                  _resident((1, D_MODEL)), _resident(w_q.shape), _resident(w_rest.shape),
                  _resident(w_ab.shape), _resident(w_gate.shape), _resident((1, GLA_QK_WIDTH))],
        out_specs=[q_spec] + [row(w) for w, _ in outs],
        compiler_params=pltpu.CompilerParams(dimension_semantics=("parallel",),
                                             vmem_limit_bytes=VMEM_LIMIT),
        name="proj",
    )(x2d, sh, sc, g_mix.reshape(1, D_MODEL), w_q, w_rest, w_ab, w_gate, b_gate.reshape(1, -1))


def _swa_kernel(q_ref, kp_ref, vp_ref, kc_ref, vc_ref, sink_ref, o_ref, *, tq):
    pad = jnp.zeros((WINDOW - tq, KV_WIDTH), f32)
    k_all = jnp.concatenate([kp_ref[...], kc_ref[...], pad], axis=0).astype(bf16)
    v_all = jnp.concatenate([vp_ref[...], vc_ref[...], pad], axis=0).astype(bf16)
    rows, cols = SWA_GROUP * tq, 2 * WINDOW
    valid = lax.broadcasted_iota(jnp.int32, (rows, cols), 1) < WINDOW + tq
    q = q_ref[...]
    outs = []
    for j in range(SWA_KV_HEADS):
        heads = [q[:, (j * SWA_GROUP + g) * SWA_HEAD_DIM:(j * SWA_GROUP + g + 1) * SWA_HEAD_DIM]
                 for g in range(SWA_GROUP)]
        qs = jnp.concatenate(heads, axis=0)
        kj = k_all[:, j * SWA_HEAD_DIM:(j + 1) * SWA_HEAD_DIM]
        vj = v_all[:, j * SWA_HEAD_DIM:(j + 1) * SWA_HEAD_DIM]
        s = _dot_nt(qs, kj) * (SWA_HEAD_DIM ** -0.5)
        s = jnp.where(valid, s, NEG_INF)
        sink = sink_ref[j]
        m = jnp.maximum(jnp.max(s, axis=1, keepdims=True), sink)
        p = jnp.exp(s - m)
        den = jnp.sum(p, axis=1, keepdims=True) + jnp.exp(sink - m)
        o = _dot(p.astype(bf16), vj) / den
        outs.append(jnp.concatenate([o[g * tq:(g + 1) * tq] for g in range(SWA_GROUP)], axis=1))
    o_ref[...] = jnp.concatenate(outs, axis=1).astype(bf16)


def _swa(q, k_past, v_past, k_new, v_new, sinks, *, n_seq, tq):
    sink_rows = jnp.repeat(sinks.astype(f32).reshape(SWA_KV_HEADS, SWA_GROUP), tq, axis=1)
    sink_rows = sink_rows.reshape(SWA_KV_HEADS, SWA_GROUP * tq, 1)
    new = lambda w: pl.BlockSpec((tq, w), lambda b: (b, 0))
    past = pl.BlockSpec((WINDOW, KV_WIDTH), lambda b: (b, 0))
    return pl.pallas_call(
        functools.partial(_swa_kernel, tq=tq),
        out_shape=jax.ShapeDtypeStruct(q.shape, bf16),
        grid=(n_seq,),
        in_specs=[new(SWA_WIDTH), past, past, new(KV_WIDTH), new(KV_WIDTH),
                  pl.BlockSpec(sink_rows.shape, lambda b: (0, 0, 0))],
        out_specs=new(SWA_WIDTH),
        compiler_params=pltpu.CompilerParams(dimension_semantics=("parallel",),
                                             vmem_limit_bytes=VMEM_LIMIT),
        name="swa",
    )(q, k_past, v_past, k_new, v_new, sink_rows)


def _swa_t_kernel(q_ref, kp_ref, vp_ref, kc_ref, vc_ref, sink_ref, bias0_ref, bias_ref, o_ref, *, n_sub):
    tq = WINDOW
    k_all = jnp.concatenate([kp_ref[...], kc_ref[...]], axis=0).astype(bf16)
    v_all = jnp.concatenate([vp_ref[...], vc_ref[...]], axis=0).astype(bf16)
    for sub in range(n_sub):
        keys = slice(sub * tq, sub * tq + 2 * WINDOW)
        toks = slice(sub * tq, (sub + 1) * tq)
        bias = (bias0_ref if sub == 0 else bias_ref)[0]
        for j in range(SWA_KV_HEADS):
            head = lambda g: slice((j * SWA_GROUP + g) * SWA_HEAD_DIM, (j * SWA_GROUP + g + 1) * SWA_HEAD_DIM)
            qs = jnp.concatenate([q_ref[head(g), toks] for g in range(SWA_GROUP)], axis=1)
            kj = k_all[keys, j * SWA_HEAD_DIM:(j + 1) * SWA_HEAD_DIM]
            vj = v_all[keys, j * SWA_HEAD_DIM:(j + 1) * SWA_HEAD_DIM]
            s = _dot(kj, qs) + bias
            sink = sink_ref[j]
            m = jnp.maximum(jnp.max(s, axis=0, keepdims=True), sink)
            p = jnp.exp(s - m)
            den = jnp.sum(p, axis=0, keepdims=True) + jnp.exp(sink - m)
            o = _dot_tn(vj, p.astype(bf16)) / den
            for g in range(SWA_GROUP):
                o_ref[head(g), toks] = o[:, g * tq:(g + 1) * tq].astype(bf16)


def _swa_t(q_t, k, v, sinks, *, n_seq, tiles, n_sub):
    tq = WINDOW
    steps = tiles // n_sub
    sink_cols = jnp.repeat(sinks.astype(f32).reshape(SWA_KV_HEADS, SWA_GROUP), tq, axis=1)
    sink_cols = sink_cols.reshape(SWA_KV_HEADS, 1, SWA_GROUP * tq)
    shape = (2 * WINDOW, SWA_GROUP * tq)
    key = lax.broadcasted_iota(jnp.int32, shape, 0)
    q_chunk = (lax.broadcasted_iota(jnp.int32, shape, 1) % tq) // CHUNK
    band = (key // CHUNK >= q_chunk) & (key // CHUNK <= q_chunk + WINDOW // CHUNK)
    bias = jnp.where(jnp.stack([band & (key >= WINDOW), band]), 0.0, NEG_INF).astype(f32)
    bias0_spec = pl.BlockSpec((1,) + shape, lambda b, u: (jnp.minimum(u, 1), 0, 0))
    bias_spec = pl.BlockSpec((1,) + shape, lambda b, u: (1, 0, 0))
    qspec = pl.BlockSpec((SWA_WIDTH, n_sub * tq), lambda b, u: (0, b * steps + u))
    cur = pl.BlockSpec((n_sub * tq, KV_WIDTH), lambda b, u: (b * steps + u, 0))
    prev = pl.BlockSpec((WINDOW, KV_WIDTH), lambda b, u: (b * tiles + jnp.maximum(n_sub * u - 1, 0), 0))
    return pl.pallas_call(
        functools.partial(_swa_t_kernel, n_sub=n_sub),
        out_shape=jax.ShapeDtypeStruct(q_t.shape, bf16),
        grid=(n_seq, steps),
        in_specs=[qspec, prev, prev, cur, cur, pl.BlockSpec(sink_cols.shape, lambda b, u: (0, 0, 0)),
                  bias0_spec, bias_spec],
        out_specs=qspec,
        compiler_params=pltpu.CompilerParams(dimension_semantics=("parallel", "arbitrary"),
                                             vmem_limit_bytes=VMEM_LIMIT),
        name="swa_t",
    )(q_t, k, v, k, v, sink_cols, bias, bias)


def _gla_kernel(q_ref, k_ref, v_ref, lg_ref, r_ref, gh_ref, s0_ref, o_ref, s_ref, *, c, n_sub):
    @pl.when(pl.program_id(1) == 0)
    def _():
        s_ref[...] = s0_ref[...]

    row = lax.broadcasted_iota(jnp.int32, (c, GLA_DK), 0)
    causal = (lax.broadcasted_iota(jnp.int32, (c, c), 0) >= lax.broadcasted_iota(jnp.int32, (c, c), 1))
    for sub in range(n_sub):
        ts = slice(sub * c, (sub + 1) * c)
        for h in range(GLA_HEADS):
            ks = slice(h * GLA_DK, (h + 1) * GLA_DK)
            vs = slice(h * GLA_DV, (h + 1) * GLA_DV)
            b = lg_ref[ts, ks]
            step = 1
            while step < c:
                b = b + jnp.where(row >= step, pltpu.roll(b, step, 0), 0.0)
                step *= 2
            b_last = b[c - 1:c, :]
            q = q_ref[ts, ks].astype(f32) * (GLA_DK ** -0.5)
            k = k_ref[ts, ks].astype(f32)
            v = v_ref[ts, vs]
            qd = (q * jnp.exp(b)).astype(bf16)
            kd = (k * jnp.exp(-b)).astype(bf16)
            kl = (k * jnp.exp(b_last - b)).astype(bf16)
            a = jnp.where(causal, _dot_nt(qd, kd), 0.0)
            s = s_ref[0, h]
            o = _dot(qd, s.astype(bf16)) + _dot(a.astype(bf16), v)
            decay = jnp.broadcast_to(jnp.exp(b_last), (GLA_DK, GLA_DK)).T
            s_ref[0, h] = s * jnp.concatenate([decay, decay], axis=1) + _dot_tn(kl, v)
            on = _rms(o) * gh_ref[...]
            o_ref[ts, vs] = (on * _silu(r_ref[ts, vs].astype(f32))).astype(bf16)


def _gla(gq, gk, gv, lg, r, g_head, s0, *, n_seq, c, n_sub, steps):
    rows = c * n_sub
    blk = lambda w: pl.BlockSpec((rows, w), lambda b, t: (b * steps + t, 0))
    state = pl.BlockSpec((1, GLA_HEADS, GLA_DK, GLA_DV), lambda b, t: (b, 0, 0, 0))
    return pl.pallas_call(
        functools.partial(_gla_kernel, c=c, n_sub=n_sub),
        out_shape=[jax.ShapeDtypeStruct(gv.shape, bf16),
                   jax.ShapeDtypeStruct((n_seq, GLA_HEADS, GLA_DK, GLA_DV), f32)],
        grid=(n_seq, steps),
        in_specs=[blk(GLA_QK_WIDTH), blk(GLA_QK_WIDTH), blk(GLA_V_WIDTH), blk(GLA_QK_WIDTH),
                  blk(GLA_V_WIDTH), pl.BlockSpec((1, GLA_DV), lambda b, t: (0, 0)), state],
        out_specs=[blk(GLA_V_WIDTH), state],
        compiler_params=pltpu.CompilerParams(dimension_semantics=("parallel", "arbitrary"),
                                             vmem_limit_bytes=VMEM_LIMIT),
        name="gla",
    )(gq, gk, gv, lg, r, g_head.reshape(1, GLA_DV), s0)


def _pack_pair(hb):
    w = hb.shape[1] // 2
    a = lax.bitcast_convert_type(hb[:, :w].astype(f32), u32)
    b = lax.bitcast_convert_type(hb[:, w:].astype(f32), u32)
    return a | (b >> 16)


def _unpack_pair(p):
    a = lax.bitcast_convert_type(p & jnp.uint32(0xFFFF0000), f32)
    b = lax.bitcast_convert_type(p << 16, f32)
    return a, b


def _load_row_tiles(ref, n):
    return jnp.concatenate([ref[pl.ds(s, n, stride=SUBLANES), :] for s in range(SUBLANES)], axis=1)


def _store_row_tiles(ref, val):
    n = val.shape[0]
    for s in range(SUBLANES):
        ref[pl.ds(s, n, stride=SUBLANES), :] = val[:, s * LANES:(s + 1) * LANES]


def _row_tile(ref, r):
    return ref.at[pl.ds(pl.multiple_of(r * SUBLANES, SUBLANES), SUBLANES), :]


def _outproj_kernel(*refs, n_alias, oa_transposed):
    (oa_ref, ob_ref, x_ref, gt_ref, sh_ref, sc_ref, g_ref, w_ref, wr_ref, br_ref) = refs[:10]
    x1_ref, h2_ref, route_ref, x1_s = refs[10 + n_alias:]

    @pl.when(pl.program_id(0) == 0)
    def _():
        x1_s[...] = jnp.zeros(x1_s.shape, x1_s.dtype)

    h2 = (_rms(x1_s[...]) * g_ref[...]) * (1.0 + _rows(sc_ref)) + _rows(sh_ref)
    hi = h2.astype(bf16)
    _store_row_tiles(h2_ref, _pack_pair(hi))
    lo = (h2 - hi.astype(f32)).astype(bf16)
    r1 = _dot(hi, wr_ref[...])
    logits = r1[:, :LANES] + r1[:, LANES:] + _dot(lo, wr_ref[:, :LANES]) + br_ref[...]
    lane = lax.broadcasted_iota(jnp.int32, logits.shape, 1)
    lane_f = lane.astype(f32)
    neg = float("-inf")
    first = lambda hit: jnp.min(jnp.where(hit, lane_f, float(LANES)), axis=1, keepdims=True)
    lg_g = jnp.where(lane < N_GROUPS, logits, neg)
    g_max = jnp.max(lg_g, axis=1, keepdims=True)
    g_sel = first(lg_g == g_max)
    p_sel = 1.0 / jnp.sum(jnp.exp(lg_g - g_max), axis=1, keepdims=True)
    e_lane = lane - N_GROUPS
    in_group = (e_lane >= 0) & (e_lane < N_EXPERTS) & ((e_lane // EXPERTS_PER_GROUP).astype(f32) == g_sel)
    lg_e = jnp.where(in_group, logits, neg)
    v1 = jnp.max(lg_e, axis=1, keepdims=True)
    i1 = first(lg_e == v1)
    lg_e2 = jnp.where(lane_f == i1, neg, lg_e)
    v2 = jnp.max(lg_e2, axis=1, keepdims=True)
    i2 = first(lg_e2 == v2)
    e = jnp.exp(v2 - v1)
    g1 = p_sel / (1.0 + e)
    g2 = p_sel * e / (1.0 + e)
    rec = jnp.where(lane == ROUTE_E0, i1 - N_GROUPS, 0.0)
    rec = jnp.where(lane == ROUTE_E1, i2 - N_GROUPS, rec)
    rec = jnp.where(lane == ROUTE_G0, g1, rec)
    rec = jnp.where(lane == ROUTE_G1, g2, rec)
    route_ref[...] = rec

    oa_dot = _dot_tn if oa_transposed else _dot
    mix = oa_dot(oa_ref[...], w_ref[:SWA_WIDTH, :]) + _dot(ob_ref[...], w_ref[SWA_WIDTH:, :])
    x1 = x_ref[...] + _rows(gt_ref) * mix
    x1_ref[...] = x1
    x1_s[...] = x1


def _outproj(oa, ob, x2d, gt, sh, sc, g_ffn, w_out, w_route, b_route, *, tm, mod_specs,
             n_total, row0, bufs=None, fill_steps=0, oa_transposed=False):
    n = x2d.shape[0]
    tiles = n // tm
    blocks = tiles + fill_steps
    row = lambda w: pl.BlockSpec((tm, w), lambda i: (jnp.minimum(i, tiles - 1), 0))
    oa_spec = row(SWA_WIDTH)
    if oa_transposed:
        oa_spec = pl.BlockSpec((SWA_WIDTH, tm), lambda i: (0, jnp.minimum(i, tiles - 1)))
    head_blk = lambda i: row0 // tm + jnp.minimum(i, blocks - 1)
    tail_blk = lambda i: row0 // tm + jnp.maximum(i - 1, 0)
    out_x1 = pl.BlockSpec((tm, D_MODEL), lambda i: (head_blk(i), 0))
    out_tiles = pl.BlockSpec((tm * SUBLANES, LANES), lambda i: (tail_blk(i), 0))
    out_route = pl.BlockSpec((tm, LANES), lambda i: (tail_blk(i), 0))
    alias_in = list(bufs) if bufs is not None else []
    n_in = 10
    return pl.pallas_call(
        functools.partial(_outproj_kernel, n_alias=len(alias_in), oa_transposed=oa_transposed),
        out_shape=[jax.ShapeDtypeStruct((n_total, D_MODEL), f32),
                   jax.ShapeDtypeStruct((n_total * SUBLANES, LANES), u32),
                   jax.ShapeDtypeStruct((n_total, LANES), f32)],
        grid=(blocks + 1,),
        in_specs=[oa_spec, row(GLA_V_WIDTH), row(D_MODEL), mod_specs[0], mod_specs[1],
                  mod_specs[2], _resident((1, D_MODEL)), _resident(w_out.shape),
                  _resident(w_route.shape), _resident((1, LANES))]
                 + [pl.BlockSpec(memory_space=pl.ANY)] * len(alias_in),
        out_specs=[out_x1, out_tiles, out_route],
        scratch_shapes=[pltpu.VMEM((tm, D_MODEL), f32)],
        input_output_aliases={n_in + a: a for a in range(len(alias_in))},
        compiler_params=pltpu.CompilerParams(dimension_semantics=("arbitrary",),
                                             vmem_limit_bytes=VMEM_LIMIT),
        name="outproj",
    )(oa, ob, x2d, gt, sh, sc, g_ffn.reshape(1, D_MODEL), w_out, w_route, b_route, *alias_in)


def _rank_kernel(route_ref, pos_ref, pend_ref, rec_s, *, n_chunks, tb):
    n_e = N_EXPERTS
    expert = lax.broadcasted_iota(jnp.int32, (n_e, LANES), 0)
    expert_f = expert.astype(f32)
    earlier = (lax.broadcasted_iota(jnp.int32, (LANES, LANES), 0)
               < lax.broadcasted_iota(jnp.int32, (LANES, LANES), 1)).astype(bf16)

    def onehots(c):
        rec = rec_s[c]
        return expert_f == rec[ROUTE_E0:ROUTE_E0 + 1, :], expert_f == rec[ROUTE_E1:ROUTE_E1 + 1, :]

    def count(c, cnt):
        rec_s[c] = route_ref[pl.ds(pl.multiple_of(c * LANES, LANES), LANES), :].T[:SUBLANES, :]
        h0, h1 = onehots(c)
        return cnt + jnp.sum((h0 | h1).astype(f32), axis=1, keepdims=True)

    cnt = lax.fori_loop(0, n_chunks, count, jnp.zeros((n_e, 1), f32))
    padded = jnp.floor((cnt + (tb - 1.0)) / tb) * tb
    end = jnp.broadcast_to(padded, (n_e, LANES))
    step = 1
    while step < n_e:
        end = end + jnp.where(expert >= step, pltpu.roll(end, step, 0), 0.0)
        step *= 2
    pend_ref[...] = end.astype(jnp.int32)

    def rank(c, base):
        h0, h1 = onehots(c)
        both = h0 | h1
        off = _dot(both.astype(bf16), earlier) + base
        pos_ref[0, pl.ds(c, 1), :] = jnp.sum(jnp.where(h0, off, 0.0), axis=0, keepdims=True).astype(jnp.int32)
        pos_ref[1, pl.ds(c, 1), :] = jnp.sum(jnp.where(h1, off, 0.0), axis=0, keepdims=True).astype(jnp.int32)
        return base + jnp.sum(both.astype(f32), axis=1, keepdims=True)

    lax.fori_loop(0, n_chunks, rank, (end - padded)[:, 0:1])


def _rank(route, *, tb):
    n = route.shape[0]
    n_chunks = n // LANES
    return pl.pallas_call(
        functools.partial(_rank_kernel, n_chunks=n_chunks, tb=tb),
        out_shape=[jax.ShapeDtypeStruct((TOP_K, n_chunks, LANES), jnp.int32),
                   jax.ShapeDtypeStruct((N_EXPERTS, LANES), jnp.int32)],
        in_specs=[_resident(route.shape)],
        scratch_shapes=[pltpu.VMEM((n_chunks, SUBLANES, LANES), f32)],
        compiler_params=pltpu.CompilerParams(vmem_limit_bytes=VMEM_LIMIT),
        name="rank",
    )(route)


def _dispatch_kernel(pos_ref, pend_ref, h2_ref, xs_hbm, stage, zbuf, sem, zsem, *, tm, tb, n_tot, steps):
    i = pl.program_id(0)
    slot = i % 2

    blk = tb * SUBLANES

    def wait_slot(s):
        for _ in range(TOP_K):
            pltpu.make_async_copy(stage.at[s], xs_hbm.at[pl.ds(0, tm * SUBLANES), :], sem.at[s]).wait()

    @pl.when(i == 0)
    def _():
        zbuf[...] = jnp.zeros(zbuf.shape, zbuf.dtype)

        def fill_block(b, start):
            cp = pltpu.make_async_copy(zbuf, xs_hbm.at[pl.ds(pl.multiple_of(b * blk, blk), blk), :], zsem)
            cp.start() if start else cp.wait()

        def fill(e, start):
            end = pend_ref[e]
            prev = jnp.where(e > 0, pend_ref[jnp.maximum(e - 1, 0)], 0)

            @pl.when(end > prev)
            def _():
                fill_block(end // tb - 1, start)

        fill_unused = fill_block

        first_unused = pend_ref[N_EXPERTS - 1] // tb
        n_blocks = xs_hbm.shape[0] // blk
        lax.fori_loop(0, N_EXPERTS, lambda e, c: (fill(e, True), c)[1], 0)
        lax.fori_loop(first_unused, n_blocks, lambda b, c: (fill_unused(b, True), c)[1], 0)
        lax.fori_loop(0, N_EXPERTS, lambda e, c: (fill(e, False), c)[1], 0)
        lax.fori_loop(first_unused, n_blocks, lambda b, c: (fill_unused(b, False), c)[1], 0)

    @pl.when(i >= 2)
    def _():
        wait_slot(slot)

    stage[slot] = h2_ref[...]

    def scatter(r, carry):
        for k in range(TOP_K):
            d = pos_ref[k * n_tot + i * tm + r]
            pltpu.make_async_copy(_row_tile(stage.at[slot], r), _row_tile(xs_hbm, d),
                                  sem.at[slot]).start(priority=k)
        return carry

    lax.fori_loop(0, tm, scatter, 0, unroll=8)

    @pl.when(i == steps - 1)
    def _():
        wait_slot(slot)
        if steps > 1:
            wait_slot(1 - slot)


def _dispatch(pos_flat, pend, h2p, *, tm, tb, n_blocks):
    n_tot = h2p.shape[0] // SUBLANES
    steps = n_tot // tm
    return pl.pallas_call(
        functools.partial(_dispatch_kernel, tm=tm, tb=tb, n_tot=n_tot, steps=steps),
        out_shape=jax.ShapeDtypeStruct((n_blocks * tb * SUBLANES, LANES), u32),
        grid_spec=pltpu.PrefetchScalarGridSpec(
            num_scalar_prefetch=2,
            grid=(steps,),
            in_specs=[pl.BlockSpec((tm * SUBLANES, LANES), lambda i, pos, pend: (i, 0))],
            out_specs=pl.BlockSpec(memory_space=pl.ANY),
            scratch_shapes=[pltpu.VMEM((2, tm * SUBLANES, LANES), u32),
                            pltpu.VMEM((tb * SUBLANES, LANES), u32),
                            pltpu.SemaphoreType.DMA((2,)), pltpu.SemaphoreType.DMA(())]),
        compiler_params=pltpu.CompilerParams(dimension_semantics=("arbitrary",),
                                             vmem_limit_bytes=VMEM_LIMIT),
        name="dispatch",
    )(pos_flat, pend, h2p)


def _expert_kernel(be_ref, ne_ref, nu_ref, xs_ref, wg_hbm, wu_hbm, wd_hbm, y_ref,
                   wg_s, wu_s, wd_s, wg_b, wu_b, wd_b, sem, *, tb):
    i = pl.program_id(0)
    used = i < nu_ref[0]
    e = be_ref[i]

    def fetch(expert):
        pairs = ((wg_hbm, wg_s), (wu_hbm, wu_s), (wd_hbm, wd_s))
        return [pltpu.make_async_copy(src.at[expert], dst, sem.at[n]) for n, (src, dst) in enumerate(pairs)]

    @pl.when(i == 0)
    def _():
        for cp in fetch(e):
            cp.start()

    @pl.when(used & ((i == 0) | (e != be_ref[jnp.maximum(i - 1, 0)])))
    def _():
        for cp in fetch(e):
            cp.wait()
        wg_b[...] = wg_s[...].astype(bf16)
        wu_b[...] = wu_s[...].astype(bf16)
        wd_b[...] = wd_s[...].astype(bf16)
        nxt = ne_ref[i]

        @pl.when(nxt != e)
        def _():
            for cp in fetch(nxt):
                cp.start()

    @pl.when(used)
    def _():
        a, b = _unpack_pair(_load_row_tiles(xs_ref, tb))
        x = jnp.concatenate([a.astype(bf16), b.astype(bf16)], axis=1)
        g = _dot(x, wg_b[...])
        u = _dot(x, wu_b[...])
        y = _dot((_silu(g) * u).astype(bf16), wd_b[...])
        _store_row_tiles(y_ref, _pack_pair(y.astype(bf16)))

    @pl.when(jnp.logical_not(used))
    def _():
        y_ref[...] = jnp.zeros(y_ref.shape, y_ref.dtype)


def _experts(block_e, next_e, n_used, xs, w_eg, w_eu, w_ed, *, tb):
    n_blocks = block_e.shape[0]
    blk = (tb * SUBLANES, LANES)
    up, down = (D_MODEL, EXPERT_HIDDEN), (EXPERT_HIDDEN, D_MODEL)
    hbm = pl.BlockSpec(memory_space=pl.ANY)
    return pl.pallas_call(
        functools.partial(_expert_kernel, tb=tb),
        out_shape=jax.ShapeDtypeStruct(xs.shape, u32),
        grid_spec=pltpu.PrefetchScalarGridSpec(
            num_scalar_prefetch=3,
            grid=(n_blocks,),
            in_specs=[pl.BlockSpec(blk, lambda i, be, ne, nu: (jnp.minimum(i, nu[0] - 1), 0)),
                      hbm, hbm, hbm],
            out_specs=pl.BlockSpec(blk, lambda i, be, ne, nu: (i, 0)),
            scratch_shapes=[pltpu.VMEM(up, f32), pltpu.VMEM(up, f32), pltpu.VMEM(down, f32),
                            pltpu.VMEM(up, bf16), pltpu.VMEM(up, bf16), pltpu.VMEM(down, bf16),
                            pltpu.SemaphoreType.DMA((3,))]),
        compiler_params=pltpu.CompilerParams(dimension_semantics=("arbitrary",),
                                             vmem_limit_bytes=VMEM_LIMIT),
        name="experts",
    )(block_e, next_e, n_used, xs, w_eg, w_eu, w_ed)


def _combine_kernel(pos_ref, yb_hbm, x1_ref, route_ref, gt_ref, gf_ref, y_ref, ybuf, sem, *, tm):
    i = pl.program_id(0)
    slot = i % 2

    def start(blk, s):
        def body(r, carry):
            for k in range(TOP_K):
                t = pos_ref[(k * pl.num_programs(0) + blk) * tm + r]
                pltpu.make_async_copy(_row_tile(yb_hbm, t), _row_tile(ybuf.at[s, k], r),
                                      sem.at[s]).start(priority=k)
            return carry
        lax.fori_loop(0, tm, body, 0, unroll=8)

    @pl.when(i == 0)
    def _():
        start(0, 0)

    @pl.when(i + 1 < pl.num_programs(0))
    def _():
        start(i + 1, 1 - slot)

    for k in range(TOP_K):
        pltpu.make_async_copy(yb_hbm.at[pl.ds(0, tm * SUBLANES), :], ybuf.at[slot, k], sem.at[slot]).wait()
    route = route_ref[...]
    a0, b0 = _unpack_pair(_load_row_tiles(ybuf.at[slot, 0], tm))
    a1, b1 = _unpack_pair(_load_row_tiles(ybuf.at[slot, 1], tm))
    g0 = route[:, ROUTE_G0:ROUTE_G0 + 1]
    g1 = route[:, ROUTE_G1:ROUTE_G1 + 1]
    moe = jnp.concatenate([a0 * g0 + a1 * g1, b0 * g0 + b1 * g1], axis=1)
    x2 = x1_ref[...] + _rows(gt_ref) * moe
    y_ref[...] = _rms(x2) * gf_ref[...]


def _combine(pos_km, yb, x1, route, gt, g_final, *, tm, n, row0, gt_spec):
    blk0 = row0 // tm
    return pl.pallas_call(
        functools.partial(_combine_kernel, tm=tm),
        out_shape=jax.ShapeDtypeStruct((n, D_MODEL), f32),
        grid_spec=pltpu.PrefetchScalarGridSpec(
            num_scalar_prefetch=1,
            grid=(n // tm,),
            in_specs=[pl.BlockSpec(memory_space=pl.ANY),
                      pl.BlockSpec((tm, D_MODEL), lambda i, pos: (blk0 + i, 0)),
                      pl.BlockSpec((tm, LANES), lambda i, pos: (blk0 + i, 0)),
                      gt_spec,
                      pl.BlockSpec((1, D_MODEL), lambda i, pos: (0, 0))],
            out_specs=pl.BlockSpec((tm, D_MODEL), lambda i, pos: (i, 0)),
            scratch_shapes=[pltpu.VMEM((2, TOP_K, tm * SUBLANES, LANES), u32),
                            pltpu.SemaphoreType.DMA((2,))]),
        compiler_params=pltpu.CompilerParams(dimension_semantics=("arbitrary",),
                                             vmem_limit_bytes=VMEM_LIMIT),
        name="combine",
    )(pos_km, yb, x1, route, gt, g_final.reshape(1, D_MODEL))


def kernel(x_prompt, x_sample, cache_swa_k, cache_swa_v, state_gla, c_prompt, c_sample, g_mix_norm, g_ffn_norm, w_ada, b_ada, w_in, attn_sinks, w_gla_gate, b_gla_gate, g_gla_norm, w_out, w_router_group, b_router_group, w_router_expert, b_router_expert, w_expert_gate, w_expert_up, w_expert_down, g_final):
    depth = w_in.shape[0]
    assert depth == 1
    bp, t, d = x_prompt.shape
    bs, ts, _ = x_sample.shape
    n_p, n_s = bp * t, bs * ts
    n_tot = n_p + n_s
    assert n_tot % LANES == 0
    tm = 512
    to = 256
    tb = 512
    tc = 256
    td = LANES
    gla_c = 128
    gla_sub = 4
    swa_sub = 2

    w_in0 = w_in[0]
    w_q = w_in0[:, :_C_K].astype(bf16)
    w_q_t = w_q.T
    w_rest = w_in0[:, _C_K:_C_AB].astype(bf16)
    w_ab = jnp.pad(w_in0[:, _C_AB:], ((0, 0), (0, LANES - GLA_GATE_RANK))).astype(bf16)
    w_gate = jnp.pad(w_gla_gate[0], ((0, LANES - GLA_GATE_RANK), (0, 0))).astype(bf16)
    w_out_b = w_out[0].astype(bf16)
    n_r = N_GROUPS + N_EXPERTS
    w_r = jnp.pad(jnp.concatenate([w_router_group[0], w_router_expert[0]], axis=1),
                  ((0, 0), (0, LANES - n_r)))
    w_r_hi = w_r.astype(bf16)
    w_r_lo = (w_r - w_r_hi.astype(f32)).astype(bf16)
    w_route = jnp.concatenate([w_r_hi, w_r_lo], axis=1)
    b_route = jnp.pad(jnp.concatenate([b_router_group[0], b_router_expert[0]]),
                      (0, LANES - n_r)).reshape(1, LANES)

    c_all = jnp.concatenate([c_prompt, c_sample], axis=0)
    mod = _adaln(c_all, w_ada[0], b_ada[0])
    mod_tab = mod.reshape((bp + bs) * N_MOD, 1, d)
    mod_s = jnp.repeat(mod[bp:].reshape(bs, N_MOD, d), ts, axis=0)
    mod_s = [mod_s[:, m] for m in range(N_MOD)]
    pmod = lambda comp, tile, lag=0: _mod_spec(comp, t // tile, n_p // tile, lag)
    smod = pl.BlockSpec((n_s, d), lambda i: (0, 0))

    xp = x_prompt.reshape(n_p, d)
    xs = x_sample.reshape(n_s, d)
    proj_w = (w_rest, w_ab, w_gate, b_gla_gate[0])
    qp, kp, vp, gqp, gkp, gvp, rp, lgp = _proj(xp, mod_tab, mod_tab, g_mix_norm[0], w_q_t, *proj_w, tm=tm,
                                               mod_specs=(pmod(0, tm), pmod(1, tm)), q_transposed=True)
    qs, ks, vs, gqs, gks, gvs, rs, lgs = _proj(xs, mod_s[0], mod_s[1], g_mix_norm[0], w_q, *proj_w, tm=n_s,
                                               mod_specs=(smod, smod), q_transposed=False)

    sinks = attn_sinks[0]
    oap = _swa_t(qp, kp, vp, sinks, n_seq=bp, tiles=t // WINDOW, n_sub=swa_sub)
    ck = cache_swa_k[0].reshape(bs * WINDOW, KV_WIDTH)
    cv = cache_swa_v[0].reshape(bs * WINDOW, KV_WIDTH)
    oas = _swa(qs, ck, cv, ks, vs, sinks, n_seq=bs, tq=ts)
    s_zero = jnp.zeros((bp, GLA_HEADS, GLA_DK, GLA_DV), f32)
    obp, sp = _gla(gqp, gkp, gvp, lgp, rp, g_gla_norm[0], s_zero, n_seq=bp, c=gla_c, n_sub=gla_sub,
                   steps=t // (gla_c * gla_sub))
    obs, ss = _gla(gqs, gks, gvs, lgs, rs, g_gla_norm[0], state_gla[0], n_seq=bs, c=ts, n_sub=1, steps=1)

    out_w = (g_ffn_norm[0], w_out_b, w_route, b_route)
    bufs = _outproj(oap, obp, xp, mod_tab, mod_tab, mod_tab, *out_w, tm=to,
                    mod_specs=(pmod(2, to), pmod(3, to, 1), pmod(4, to, 1)), n_total=n_tot, row0=0,
                    fill_steps=-(-n_s // to), oa_transposed=True)
    x1, h2p, route = _outproj(oas, obs, xs, mod_s[2], mod_s[3], mod_s[4], *out_w, tm=n_s,
                              mod_specs=(smod, smod, smod), n_total=n_tot, row0=n_p, bufs=bufs)

    n_blocks = -(-(n_tot * TOP_K + N_EXPERTS * (tb - 1)) // tb)
    pos, pend_tab = _rank(route, tb=tb)
    pend = pend_tab[:, 0]
    block_e = jnp.minimum(jnp.sum(pend[None, :] <= (jnp.arange(n_blocks, dtype=jnp.int32) * tb)[:, None],
                                  axis=1), N_EXPERTS - 1).astype(jnp.int32)
    n_used = pend[N_EXPERTS - 1:] // tb
    experts = jnp.arange(N_EXPERTS, dtype=jnp.int32)
    has_rows = jnp.diff(pend, prepend=0) > 0
    later = jnp.where((experts[None, :] > experts[:, None]) & has_rows[None, :], experts[None, :], N_EXPERTS)
    next_with_rows = jnp.min(later, axis=1)
    next_with_rows = jnp.where(next_with_rows == N_EXPERTS, experts, next_with_rows)
    next_e = jnp.sum(jnp.where(block_e[:, None] == experts[None, :], next_with_rows[None, :], 0),
                     axis=1).astype(jnp.int32)
    xsort = _dispatch(pos.reshape(-1), pend, h2p, tm=td, tb=tb, n_blocks=n_blocks)
    yb = _experts(block_e, next_e, n_used, xsort, w_expert_gate[0], w_expert_up[0], w_expert_down[0], tb=tb)

    pos = pos.reshape(TOP_K, n_tot)
    pos_p = pos[:, :n_p].reshape(-1)
    pos_s = pos[:, n_p:].reshape(-1)
    gt_p = pl.BlockSpec((1, 1, d), lambda i, p: ((i // (t // tc)) * N_MOD + 5, 0, 0))
    gt_s = pl.BlockSpec((n_s, d), lambda i, p: (0, 0))
    y_p = _combine(pos_p, yb, x1, route, mod_tab, g_final, tm=tc, n=n_p, row0=0, gt_spec=gt_p)
    y_s = _combine(pos_s, yb, x1, route, mod_s[5], g_final, tm=n_s, n=n_s, row0=n_p, gt_spec=gt_s)

    kv_shape = (SWA_KV_HEADS, SWA_HEAD_DIM)
    k_state_p = kp.reshape(bp, t, *kv_shape)[:, -WINDOW:][None]
    v_state_p = vp.reshape(bp, t, *kv_shape)[:, -WINDOW:][None]
    return (y_p.reshape(bp, t, d), y_s.reshape(bs, ts, d), k_state_p, v_state_p, sp[None],
            ks.reshape(bs, ts, *kv_shape)[None], vs.reshape(bs, ts, *kv_shape)[None], ss[None])
```

```python
import functools
import math

import jax
import jax.numpy as jnp
from jax import lax
from jax.experimental import pallas as pl
from jax.experimental.pallas import tpu as pltpu

f32 = jnp.float32
bf16 = jnp.bfloat16
u32 = jnp.uint32

D_MODEL = 2048
N_MOD = 6
EPS = 1e-6
NEG_INF = -1e30
LOG2_E = math.log2(math.e)

SWA_HEAD_DIM = 64
SWA_KV_HEADS = 2
SWA_GROUP = 8
SWA_WIDTH = SWA_KV_HEADS * SWA_GROUP * SWA_HEAD_DIM
KV_WIDTH = SWA_KV_HEADS * SWA_HEAD_DIM
WINDOW = 128
CHUNK = 64

GLA_HEADS = 4
GLA_DK = 128
GLA_DV = 256
GLA_QK_WIDTH = GLA_HEADS * GLA_DK
GLA_V_WIDTH = GLA_HEADS * GLA_DV
GLA_GATE_RANK = 16
GLA_GATE_NORM = 16.0

N_GROUPS = 4
EXPERTS_PER_GROUP = 8
N_EXPERTS = N_GROUPS * EXPERTS_PER_GROUP
TOP_K = 2
EXPERT_HIDDEN = D_MODEL // 4

_C_Q = 0
_C_K = _C_Q + SWA_WIDTH
_C_V = _C_K + KV_WIDTH
_C_GQ = _C_V + KV_WIDTH
_C_GK = _C_GQ + GLA_QK_WIDTH
_C_GV = _C_GK + GLA_QK_WIDTH
_C_R = _C_GV + GLA_V_WIDTH
_C_AB = _C_R + GLA_V_WIDTH

LANES = 128
SUBLANES = 8
PACKED_WIDTH = D_MODEL // 2
assert PACKED_WIDTH == SUBLANES * LANES
VMEM_LIMIT = 56 * 1024 * 1024

ROUTE_E0, ROUTE_E1, ROUTE_G0, ROUTE_G1 = 0, 1, 2, 3


def _dot(a, b):
    return jnp.dot(a, b, preferred_element_type=f32)


def _dot_nt(a, b):
    return lax.dot_general(a, b, (((1,), (1,)), ((), ())), preferred_element_type=f32)


def _dot_tn(a, b):
    return lax.dot_general(a, b, (((0,), (0,)), ((), ())), preferred_element_type=f32)


def _silu(x):
    return x / (1.0 + jnp.exp(-x))


def _rows(ref):
    v = ref[...]
    return v.reshape(v.shape[-2:])


def _rms(x):
    return x * lax.rsqrt(jnp.mean(x * x, axis=-1, keepdims=True) + EPS)


def _resident(shape):
    return pl.BlockSpec(shape, lambda *_: (0,) * len(shape), pipeline_mode=pl.Buffered(1))


def _mod_spec(comp, tiles_per_seq, n_tiles, lag=0):
    def index(i):
        tile = jnp.minimum(jnp.maximum(i - lag, 0), n_tiles - 1)
        return ((tile // tiles_per_seq) * N_MOD + comp, 0, 0)
    return pl.BlockSpec((1, 1, D_MODEL), index)


def _adaln_kernel(c_ref, w_ref, b_ref, o_ref):
    a = _silu(c_ref[...]).astype(bf16)
    o_ref[...] = _dot(a, w_ref[...].astype(bf16)) + b_ref[...]


def _adaln(c_all, w_ada, b_ada, tn=1024):
    r = c_all.shape[0]
    n = w_ada.shape[1]
    return pl.pallas_call(
        _adaln_kernel,
        out_shape=jax.ShapeDtypeStruct((r, n), f32),
        grid=(n // tn,),
        in_specs=[pl.BlockSpec((r, D_MODEL), lambda j: (0, 0)),
                  pl.BlockSpec((D_MODEL, tn), lambda j: (0, j)),
                  pl.BlockSpec((1, tn), lambda j: (0, j))],
        out_specs=pl.BlockSpec((r, tn), lambda j: (0, j)),
        compiler_params=pltpu.CompilerParams(dimension_semantics=("arbitrary",),
                                             vmem_limit_bytes=VMEM_LIMIT),
        name="adaln",
    )(c_all, w_ada, b_ada.reshape(1, n))


def _proj_kernel(x_ref, sh_ref, sc_ref, g_ref, wq_ref, w_ref, wab_ref, wg_ref, bg_ref,
                 q_ref, k_ref, v_ref, gq_ref, gk_ref, gv_ref, r_ref, lg_ref, *, q_transposed):
    h = (_rms(x_ref[...]) * g_ref[...]) * (1.0 + _rows(sc_ref)) + _rows(sh_ref)
    hb = h.astype(bf16)
    if q_transposed:
        q_ref[...] = (_dot_nt(wq_ref[...], hb) * (SWA_HEAD_DIM ** -0.5)).astype(bf16)
    else:
        q_ref[...] = _dot(hb, wq_ref[...]).astype(bf16)
    col = lambda c: c - _C_K
    kv = _dot(hb, w_ref[:, col(_C_K):col(_C_GQ)])
    k_ref[...] = kv[:, :KV_WIDTH]
    v_ref[...] = kv[:, KV_WIDTH:]
    gq_ref[...] = (_dot(hb, w_ref[:, col(_C_GQ):col(_C_GK)]) * (GLA_DK ** -0.5)).astype(bf16)
    gk_ref[...] = _dot(hb, w_ref[:, col(_C_GK):col(_C_GV)]).astype(bf16)
    gv_ref[...] = _dot(hb, w_ref[:, col(_C_GV):col(_C_R)]).astype(bf16)
    r_ref[...] = _silu(_dot(hb, w_ref[:, col(_C_R):col(_C_AB)])).astype(bf16)
    ab = _dot(hb, wab_ref[...])
    z = _dot(ab.astype(bf16), wg_ref[...]) + bg_ref[...]
    log_sig = jnp.minimum(z, 0.0) - jnp.log1p(jnp.exp(-jnp.abs(z)))
    lg_ref[...] = log_sig * (LOG2_E / GLA_GATE_NORM)


def _proj(x2d, sh, sc, g_mix, w_q, w_rest, w_ab, w_gate, b_gate, *, tm, mod_specs, q_transposed):
    n = x2d.shape[0]
    row = lambda w: pl.BlockSpec((tm, w), lambda i: (i, 0))
    outs = [(KV_WIDTH, f32), (KV_WIDTH, f32), (GLA_QK_WIDTH, bf16),
            (GLA_QK_WIDTH, bf16), (GLA_V_WIDTH, bf16), (GLA_V_WIDTH, bf16), (GLA_QK_WIDTH, f32)]
    if q_transposed:
        q_shape, q_spec = (SWA_WIDTH, n), pl.BlockSpec((SWA_WIDTH, tm), lambda i: (0, i))
    else:
        q_shape, q_spec = (n, SWA_WIDTH), row(SWA_WIDTH)
    return pl.pallas_call(
        functools.partial(_proj_kernel, q_transposed=q_transposed),
        out_shape=[jax.ShapeDtypeStruct(q_shape, bf16)]
                  + [jax.ShapeDtypeStruct((n, w), dt) for w, dt in outs],
        grid=(n // tm,),
        in_specs=[row(D_MODEL), mod_specs[0], mod_specs[1],
                  _resident((1, D_MODEL)), _resident(w_q.shape), _resident(w_rest.shape),
                  _resident(w_ab.shape), _resident(w_gate.shape), _resident((1, GLA_QK_WIDTH))],
        out_specs=[q_spec] + [row(w) for w, _ in outs],
        compiler_params=pltpu.CompilerParams(dimension_semantics=("parallel",),
                                             vmem_limit_bytes=VMEM_LIMIT),
        name="proj",
    )(x2d, sh, sc, g_mix.reshape(1, D_MODEL), w_q, w_rest, w_ab, w_gate, b_gate.reshape(1, -1))


def _swa_kernel(q_ref, kp_ref, vp_ref, kc_ref, vc_ref, sink_ref, o_ref, *, tq):
    pad = jnp.zeros((WINDOW - tq, KV_WIDTH), f32)
    k_all = jnp.concatenate([kp_ref[...], kc_ref[...], pad], axis=0).astype(bf16)
    v_all = jnp.concatenate([vp_ref[...], vc_ref[...], pad], axis=0).astype(bf16)
    rows, cols = SWA_GROUP * tq, 2 * WINDOW
    valid = lax.broadcasted_iota(jnp.int32, (rows, cols), 1) < WINDOW + tq
    q = q_ref[...]
    outs = []
    for j in range(SWA_KV_HEADS):
        heads = [q[:, (j * SWA_GROUP + g) * SWA_HEAD_DIM:(j * SWA_GROUP + g + 1) * SWA_HEAD_DIM]
                 for g in range(SWA_GROUP)]
        qs = jnp.concatenate(heads, axis=0)
        kj = k_all[:, j * SWA_HEAD_DIM:(j + 1) * SWA_HEAD_DIM]
        vj = v_all[:, j * SWA_HEAD_DIM:(j + 1) * SWA_HEAD_DIM]
        s = _dot_nt(qs, kj) * (SWA_HEAD_DIM ** -0.5)
        s = jnp.where(valid, s, NEG_INF)
        sink = sink_ref[j]
        m = jnp.maximum(jnp.max(s, axis=1, keepdims=True), sink)
        p = jnp.exp(s - m)
        den = jnp.sum(p, axis=1, keepdims=True) + jnp.exp(sink - m)
        o = _dot(p.astype(bf16), vj) / den
        outs.append(jnp.concatenate([o[g * tq:(g + 1) * tq] for g in range(SWA_GROUP)], axis=1))
    o_ref[...] = jnp.concatenate(outs, axis=1).astype(bf16)


def _swa(q, k_past, v_past, k_new, v_new, sinks, *, n_seq, tq):
    sink_rows = jnp.repeat(sinks.astype(f32).reshape(SWA_KV_HEADS, SWA_GROUP), tq, axis=1)
    sink_rows = sink_rows.reshape(SWA_KV_HEADS, SWA_GROUP * tq, 1)
    new = lambda w: pl.BlockSpec((tq, w), lambda b: (b, 0))
    past = pl.BlockSpec((WINDOW, KV_WIDTH), lambda b: (b, 0))
    return pl.pallas_call(
        functools.partial(_swa_kernel, tq=tq),
        out_shape=jax.ShapeDtypeStruct(q.shape, bf16),
        grid=(n_seq,),
        in_specs=[new(SWA_WIDTH), past, past, new(KV_WIDTH), new(KV_WIDTH),
                  pl.BlockSpec(sink_rows.shape, lambda b: (0, 0, 0))],
        out_specs=new(SWA_WIDTH),
        compiler_params=pltpu.CompilerParams(dimension_semantics=("parallel",),
                                             vmem_limit_bytes=VMEM_LIMIT),
        name="swa",
    )(q, k_past, v_past, k_new, v_new, sink_rows)


def _swa_t_kernel(q_ref, kp_ref, vp_ref, kc_ref, vc_ref, sink_ref, bias0_ref, bias_ref, o_ref, *, n_sub):
    tq = WINDOW
    k_all = jnp.concatenate([kp_ref[...], kc_ref[...]], axis=0).astype(bf16)
    v_all = jnp.concatenate([vp_ref[...], vc_ref[...]], axis=0).astype(bf16)
    for sub in range(n_sub):
        keys = slice(sub * tq, sub * tq + 2 * WINDOW)
        toks = slice(sub * tq, (sub + 1) * tq)
        bias = (bias0_ref if sub == 0 else bias_ref)[0]
        for j in range(SWA_KV_HEADS):
            head = lambda g: slice((j * SWA_GROUP + g) * SWA_HEAD_DIM, (j * SWA_GROUP + g + 1) * SWA_HEAD_DIM)
            qs = jnp.concatenate([q_ref[head(g), toks] for g in range(SWA_GROUP)], axis=1)
            kj = k_all[keys, j * SWA_HEAD_DIM:(j + 1) * SWA_HEAD_DIM]
            vj = v_all[keys, j * SWA_HEAD_DIM:(j + 1) * SWA_HEAD_DIM]
            s = _dot(kj, qs) + bias
            sink = sink_ref[j]
            m = jnp.maximum(jnp.max(s, axis=0, keepdims=True), sink)
            p = jnp.exp(s - m)
            den = jnp.sum(p, axis=0, keepdims=True) + jnp.exp(sink - m)
            o = _dot_tn(vj, p.astype(bf16)) / den
            for g in range(SWA_GROUP):
                o_ref[head(g), toks] = o[:, g * tq:(g + 1) * tq].astype(bf16)


def _swa_t(q_t, k, v, sinks, *, n_seq, tiles, n_sub):
    tq = WINDOW
    steps = tiles // n_sub
    sink_cols = jnp.repeat(sinks.astype(f32).reshape(SWA_KV_HEADS, SWA_GROUP), tq, axis=1)
    sink_cols = sink_cols.reshape(SWA_KV_HEADS, 1, SWA_GROUP * tq)
    shape = (2 * WINDOW, SWA_GROUP * tq)
    key = lax.broadcasted_iota(jnp.int32, shape, 0)
    q_chunk = (lax.broadcasted_iota(jnp.int32, shape, 1) % tq) // CHUNK
    band = (key // CHUNK >= q_chunk) & (key // CHUNK <= q_chunk + WINDOW // CHUNK)
    bias = jnp.where(jnp.stack([band & (key >= WINDOW), band]), 0.0, NEG_INF).astype(f32)
    bias0_spec = pl.BlockSpec((1,) + shape, lambda b, u: (jnp.minimum(u, 1), 0, 0))
    bias_spec = pl.BlockSpec((1,) + shape, lambda b, u: (1, 0, 0))
    qspec = pl.BlockSpec((SWA_WIDTH, n_sub * tq), lambda b, u: (0, b * steps + u))
    cur = pl.BlockSpec((n_sub * tq, KV_WIDTH), lambda b, u: (b * steps + u, 0))
    prev = pl.BlockSpec((WINDOW, KV_WIDTH), lambda b, u: (b * tiles + jnp.maximum(n_sub * u - 1, 0), 0))
    return pl.pallas_call(
        functools.partial(_swa_t_kernel, n_sub=n_sub),
        out_shape=jax.ShapeDtypeStruct(q_t.shape, bf16),
        grid=(n_seq, steps),
        in_specs=[qspec, prev, prev, cur, cur, pl.BlockSpec(sink_cols.shape, lambda b, u: (0, 0, 0)),
                  bias0_spec, bias_spec],
        out_specs=qspec,
        compiler_params=pltpu.CompilerParams(dimension_semantics=("parallel", "arbitrary"),
                                             vmem_limit_bytes=VMEM_LIMIT),
        name="swa_t",
    )(q_t, k, v, k, v, sink_cols, bias, bias)


def _gla_kernel(q_ref, k_ref, v_ref, lg_ref, r_ref, gh_ref, s0_ref, o_ref, s_ref, *, c, n_sub):
    @pl.when(pl.program_id(1) == 0)
    def _():
        s_ref[...] = s0_ref[...]

    row = lax.broadcasted_iota(jnp.int32, (c, GLA_DK), 0)
    causal = (lax.broadcasted_iota(jnp.int32, (c, c), 0) >= lax.broadcasted_iota(jnp.int32, (c, c), 1))
    for sub in range(n_sub):
        ts = slice(sub * c, (sub + 1) * c)
        for h in range(GLA_HEADS):
            ks = slice(h * GLA_DK, (h + 1) * GLA_DK)
            vs = slice(h * GLA_DV, (h + 1) * GLA_DV)
            b = lg_ref[ts, ks]
            step = 1
            while step < c:
                b = b + jnp.where(row >= step, pltpu.roll(b, step, 0), 0.0)
                step *= 2
            b_last = b[c - 1:c, :]
            q = q_ref[ts, ks].astype(f32)
            k = k_ref[ts, ks].astype(f32)
            v = v_ref[ts, vs]
            qd = (q * jnp.exp2(b)).astype(bf16)
            kd = (k * jnp.exp2(-b)).astype(bf16)
            kl = (k * jnp.exp2(b_last - b)).astype(bf16)
            a = jnp.where(causal, _dot_nt(qd, kd), 0.0)
            s = s_ref[0, h]
            o = _dot(qd, s.astype(bf16)) + _dot(a.astype(bf16), v)
            decay = jnp.broadcast_to(jnp.exp2(b_last), (GLA_DK, GLA_DK)).T
            s_ref[0, h] = s * jnp.concatenate([decay, decay], axis=1) + _dot_tn(kl, v)
            on = _rms(o) * gh_ref[...]
            o_ref[ts, vs] = (on * r_ref[ts, vs].astype(f32)).astype(bf16)


def _gla(gq, gk, gv, lg, r, g_head, s0, *, n_seq, c, n_sub, steps):
    rows = c * n_sub
    blk = lambda w: pl.BlockSpec((rows, w), lambda b, t: (b * steps + t, 0))
    state = pl.BlockSpec((1, GLA_HEADS, GLA_DK, GLA_DV), lambda b, t: (b, 0, 0, 0))
    return pl.pallas_call(
        functools.partial(_gla_kernel, c=c, n_sub=n_sub),
        out_shape=[jax.ShapeDtypeStruct(gv.shape, bf16),
                   jax.ShapeDtypeStruct((n_seq, GLA_HEADS, GLA_DK, GLA_DV), f32)],
        grid=(n_seq, steps),
        in_specs=[blk(GLA_QK_WIDTH), blk(GLA_QK_WIDTH), blk(GLA_V_WIDTH), blk(GLA_QK_WIDTH),
                  blk(GLA_V_WIDTH), pl.BlockSpec((1, GLA_DV), lambda b, t: (0, 0)), state],
        out_specs=[blk(GLA_V_WIDTH), state],
        compiler_params=pltpu.CompilerParams(dimension_semantics=("parallel", "arbitrary"),
                                             vmem_limit_bytes=VMEM_LIMIT),
        name="gla",
    )(gq, gk, gv, lg, r, g_head.reshape(1, GLA_DV), s0)


def _pack_pair(hb):
    w = hb.shape[1] // 2
    a = lax.bitcast_convert_type(hb[:, :w].astype(f32), u32)
    b = lax.bitcast_convert_type(hb[:, w:].astype(f32), u32)
    return a | (b >> 16)


def _unpack_pair(p):
    a = lax.bitcast_convert_type(p & jnp.uint32(0xFFFF0000), f32)
    b = lax.bitcast_convert_type(p << 16, f32)
    return a, b


def _load_row_tiles(ref, n):
    return jnp.concatenate([ref[pl.ds(s, n, stride=SUBLANES), :] for s in range(SUBLANES)], axis=1)


def _store_row_tiles(ref, val):
    n = val.shape[0]
    for s in range(SUBLANES):
        ref[pl.ds(s, n, stride=SUBLANES), :] = val[:, s * LANES:(s + 1) * LANES]


def _row_tile(ref, r):
    return ref.at[pl.ds(pl.multiple_of(r * SUBLANES, SUBLANES), SUBLANES), :]


def _outproj_kernel(*refs, n_alias, oa_transposed):
    (oa_ref, ob_ref, x_ref, gt_ref, sh_ref, sc_ref, g_ref, w_ref, wr_ref, br_ref) = refs[:10]
    x1_ref, h2_ref, route_ref, x1_s = refs[10 + n_alias:]

    @pl.when(pl.program_id(0) == 0)
    def _():
        x1_s[...] = jnp.zeros(x1_s.shape, x1_s.dtype)

    h2 = (_rms(x1_s[...]) * g_ref[...]) * (1.0 + _rows(sc_ref)) + _rows(sh_ref)
    hi = h2.astype(bf16)
    _store_row_tiles(h2_ref, _pack_pair(hi))
    lo = (h2 - hi.astype(f32)).astype(bf16)
    r1 = _dot(hi, wr_ref[...])
    logits = r1[:, :LANES] + r1[:, LANES:] + _dot(lo, wr_ref[:, :LANES]) + br_ref[...]
    lane = lax.broadcasted_iota(jnp.int32, logits.shape, 1)
    lane_f = lane.astype(f32)
    neg = float("-inf")
    first = lambda hit: jnp.min(jnp.where(hit, lane_f, float(LANES)), axis=1, keepdims=True)
    lg_g = jnp.where(lane < N_GROUPS, logits, neg)
    g_max = jnp.max(lg_g, axis=1, keepdims=True)
    g_sel = first(lg_g == g_max)
    p_sel = 1.0 / jnp.sum(jnp.exp(lg_g - g_max), axis=1, keepdims=True)
    e_lane = lane - N_GROUPS
    in_group = (e_lane >= 0) & (e_lane < N_EXPERTS) & ((e_lane // EXPERTS_PER_GROUP).astype(f32) == g_sel)
    lg_e = jnp.where(in_group, logits, neg)
    v1 = jnp.max(lg_e, axis=1, keepdims=True)
    i1 = first(lg_e == v1)
    lg_e2 = jnp.where(lane_f == i1, neg, lg_e)
    v2 = jnp.max(lg_e2, axis=1, keepdims=True)
    i2 = first(lg_e2 == v2)
    e = jnp.exp(v2 - v1)
    g1 = p_sel / (1.0 + e)
    g2 = p_sel * e / (1.0 + e)
    rec = jnp.where(lane == ROUTE_E0, i1 - N_GROUPS, 0.0)
    rec = jnp.where(lane == ROUTE_E1, i2 - N_GROUPS, rec)
    rec = jnp.where(lane == ROUTE_G0, g1, rec)
    rec = jnp.where(lane == ROUTE_G1, g2, rec)
    route_ref[...] = rec

    oa_dot = _dot_tn if oa_transposed else _dot
    mix = oa_dot(oa_ref[...], w_ref[:SWA_WIDTH, :]) + _dot(ob_ref[...], w_ref[SWA_WIDTH:, :])
    x1 = x_ref[...] + _rows(gt_ref) * mix
    x1_ref[...] = x1
    x1_s[...] = x1


def _outproj(oa, ob, x2d, gt, sh, sc, g_ffn, w_out, w_route, b_route, *, tm, mod_specs,
             n_total, row0, bufs=None, fill_steps=0, oa_transposed=False):
    n = x2d.shape[0]
    tiles = n // tm
    blocks = tiles + fill_steps
    row = lambda w: pl.BlockSpec((tm, w), lambda i: (jnp.minimum(i, tiles - 1), 0))
    oa_spec = row(SWA_WIDTH)
    if oa_transposed:
        oa_spec = pl.BlockSpec((SWA_WIDTH, tm), lambda i: (0, jnp.minimum(i, tiles - 1)))
    head_blk = lambda i: row0 // tm + jnp.minimum(i, blocks - 1)
    tail_blk = lambda i: row0 // tm + jnp.maximum(i - 1, 0)
    out_x1 = pl.BlockSpec((tm, D_MODEL), lambda i: (head_blk(i), 0))
    out_tiles = pl.BlockSpec((tm * SUBLANES, LANES), lambda i: (tail_blk(i), 0))
    out_route = pl.BlockSpec((tm, LANES), lambda i: (tail_blk(i), 0))
    alias_in = list(bufs) if bufs is not None else []
    n_in = 10
    return pl.pallas_call(
        functools.partial(_outproj_kernel, n_alias=len(alias_in), oa_transposed=oa_transposed),
        out_shape=[jax.ShapeDtypeStruct((n_total, D_MODEL), f32),
                   jax.ShapeDtypeStruct((n_total * SUBLANES, LANES), u32),
                   jax.ShapeDtypeStruct((n_total, LANES), f32)],
        grid=(blocks + 1,),
        in_specs=[oa_spec, row(GLA_V_WIDTH), row(D_MODEL), mod_specs[0], mod_specs[1],
                  mod_specs[2], _resident((1, D_MODEL)), _resident(w_out.shape),
                  _resident(w_route.shape), _resident((1, LANES))]
                 + [pl.BlockSpec(memory_space=pl.ANY)] * len(alias_in),
        out_specs=[out_x1, out_tiles, out_route],
        scratch_shapes=[pltpu.VMEM((tm, D_MODEL), f32)],
        input_output_aliases={n_in + a: a for a in range(len(alias_in))},
        compiler_params=pltpu.CompilerParams(dimension_semantics=("arbitrary",),
                                             vmem_limit_bytes=VMEM_LIMIT),
        name="outproj",
    )(oa, ob, x2d, gt, sh, sc, g_ffn.reshape(1, D_MODEL), w_out, w_route, b_route, *alias_in)


def _rank_kernel(route_ref, pos_ref, pend_ref, rec_s, *, n_chunks, tb):
    n_e = N_EXPERTS
    expert = lax.broadcasted_iota(jnp.int32, (n_e, LANES), 0)
    expert_f = expert.astype(f32)
    earlier = (lax.broadcasted_iota(jnp.int32, (LANES, LANES), 0)
               < lax.broadcasted_iota(jnp.int32, (LANES, LANES), 1)).astype(bf16)

    def onehots(c):
        rec = rec_s[c]
        return expert_f == rec[ROUTE_E0:ROUTE_E0 + 1, :], expert_f == rec[ROUTE_E1:ROUTE_E1 + 1, :]

    def count(c, cnt):
        rec_s[c] = route_ref[pl.ds(pl.multiple_of(c * LANES, LANES), LANES), :].T[:SUBLANES, :]
        h0, h1 = onehots(c)
        return cnt + jnp.sum((h0 | h1).astype(f32), axis=1, keepdims=True)

    cnt = lax.fori_loop(0, n_chunks, count, jnp.zeros((n_e, 1), f32))
    padded = jnp.floor((cnt + (tb - 1.0)) / tb) * tb
    end = jnp.broadcast_to(padded, (n_e, LANES))
    step = 1
    while step < n_e:
        end = end + jnp.where(expert >= step, pltpu.roll(end, step, 0), 0.0)
        step *= 2
    pend_ref[...] = end.astype(jnp.int32)

    def rank(c, base):
        h0, h1 = onehots(c)
        both = h0 | h1
        off = _dot(both.astype(bf16), earlier) + base
        pos_ref[0, pl.ds(c, 1), :] = jnp.sum(jnp.where(h0, off, 0.0), axis=0, keepdims=True).astype(jnp.int32)
        pos_ref[1, pl.ds(c, 1), :] = jnp.sum(jnp.where(h1, off, 0.0), axis=0, keepdims=True).astype(jnp.int32)
        return base + jnp.sum(both.astype(f32), axis=1, keepdims=True)

    lax.fori_loop(0, n_chunks, rank, (end - padded)[:, 0:1])


def _rank(route, *, tb):
    n = route.shape[0]
    n_chunks = n // LANES
    return pl.pallas_call(
        functools.partial(_rank_kernel, n_chunks=n_chunks, tb=tb),
        out_shape=[jax.ShapeDtypeStruct((TOP_K, n_chunks, LANES), jnp.int32),
                   jax.ShapeDtypeStruct((N_EXPERTS, LANES), jnp.int32)],
        in_specs=[_resident(route.shape)],
        scratch_shapes=[pltpu.VMEM((n_chunks, SUBLANES, LANES), f32)],
        compiler_params=pltpu.CompilerParams(vmem_limit_bytes=VMEM_LIMIT),
        name="rank",
    )(route)


def _dispatch_kernel(pos_ref, pend_ref, h2_ref, xs_hbm, stage, zbuf, sem, zsem, *, tm, tb, n_tot, steps):
    i = pl.program_id(0)
    slot = i % 2

    blk = tb * SUBLANES

    def wait_slot(s):
        for _ in range(TOP_K):
            pltpu.make_async_copy(stage.at[s], xs_hbm.at[pl.ds(0, tm * SUBLANES), :], sem.at[s]).wait()

    @pl.when(i == 0)
    def _():
        zbuf[...] = jnp.zeros(zbuf.shape, zbuf.dtype)

        def fill_block(b, start):
            cp = pltpu.make_async_copy(zbuf, xs_hbm.at[pl.ds(pl.multiple_of(b * blk, blk), blk), :], zsem)
            cp.start() if start else cp.wait()

        def fill(e, start):
            end = pend_ref[e]
            prev = jnp.where(e > 0, pend_ref[jnp.maximum(e - 1, 0)], 0)

            @pl.when(end > prev)
            def _():
                fill_block(end // tb - 1, start)

        fill_unused = fill_block

        first_unused = pend_ref[N_EXPERTS - 1] // tb
        n_blocks = xs_hbm.shape[0] // blk
        lax.fori_loop(0, N_EXPERTS, lambda e, c: (fill(e, True), c)[1], 0)
        lax.fori_loop(first_unused, n_blocks, lambda b, c: (fill_unused(b, True), c)[1], 0)
        lax.fori_loop(0, N_EXPERTS, lambda e, c: (fill(e, False), c)[1], 0)
        lax.fori_loop(first_unused, n_blocks, lambda b, c: (fill_unused(b, False), c)[1], 0)

    @pl.when(i >= 2)
    def _():
        wait_slot(slot)

    stage[slot] = h2_ref[...]

    def scatter(r, carry):
        for k in range(TOP_K):
            d = pos_ref[k * n_tot + i * tm + r]
            pltpu.make_async_copy(_row_tile(stage.at[slot], r), _row_tile(xs_hbm, d),
                                  sem.at[slot]).start(priority=k)
        return carry

    lax.fori_loop(0, tm, scatter, 0, unroll=8)

    @pl.when(i == steps - 1)
    def _():
        wait_slot(slot)
        if steps > 1:
            wait_slot(1 - slot)


def _dispatch(pos_flat, pend, h2p, *, tm, tb, n_blocks):
    n_tot = h2p.shape[0] // SUBLANES
    steps = n_tot // tm
    return pl.pallas_call(
        functools.partial(_dispatch_kernel, tm=tm, tb=tb, n_tot=n_tot, steps=steps),
        out_shape=jax.ShapeDtypeStruct((n_blocks * tb * SUBLANES, LANES), u32),
        grid_spec=pltpu.PrefetchScalarGridSpec(
            num_scalar_prefetch=2,
            grid=(steps,),
            in_specs=[pl.BlockSpec((tm * SUBLANES, LANES), lambda i, pos, pend: (i, 0))],
            out_specs=pl.BlockSpec(memory_space=pl.ANY),
            scratch_shapes=[pltpu.VMEM((2, tm * SUBLANES, LANES), u32),
                            pltpu.VMEM((tb * SUBLANES, LANES), u32),
                            pltpu.SemaphoreType.DMA((2,)), pltpu.SemaphoreType.DMA(())]),
        compiler_params=pltpu.CompilerParams(dimension_semantics=("arbitrary",),
                                             vmem_limit_bytes=VMEM_LIMIT),
        name="dispatch",
    )(pos_flat, pend, h2p)


def _expert_kernel(be_ref, ne_ref, nu_ref, xs_ref, wg_hbm, wu_hbm, wd_hbm, y_ref,
                   wg_s, wu_s, wd_s, wg_b, wu_b, wd_b, sem, *, tb):
    i = pl.program_id(0)
    used = i < nu_ref[0]
    e = be_ref[i]

    def fetch(expert):
        pairs = ((wg_hbm, wg_s), (wu_hbm, wu_s), (wd_hbm, wd_s))
        return [pltpu.make_async_copy(src.at[expert], dst, sem.at[n]) for n, (src, dst) in enumerate(pairs)]

    @pl.when(i == 0)
    def _():
        for cp in fetch(e):
            cp.start()

    @pl.when(used & ((i == 0) | (e != be_ref[jnp.maximum(i - 1, 0)])))
    def _():
        for cp in fetch(e):
            cp.wait()
        wg_b[...] = wg_s[...].astype(bf16)
        wu_b[...] = wu_s[...].astype(bf16)
        wd_b[...] = wd_s[...].astype(bf16)
        nxt = ne_ref[i]

        @pl.when(nxt != e)
        def _():
            for cp in fetch(nxt):
                cp.start()

    @pl.when(used)
    def _():
        a, b = _unpack_pair(_load_row_tiles(xs_ref, tb))
        x = jnp.concatenate([a.astype(bf16), b.astype(bf16)], axis=1)
        g = _dot(x, wg_b[...])
        u = _dot(x, wu_b[...])
        y = _dot((_silu(g) * u).astype(bf16), wd_b[...])
        _store_row_tiles(y_ref, _pack_pair(y.astype(bf16)))

    @pl.when(jnp.logical_not(used))
    def _():
        y_ref[...] = jnp.zeros(y_ref.shape, y_ref.dtype)


def _experts(block_e, next_e, n_used, xs, w_eg, w_eu, w_ed, *, tb):
    n_blocks = block_e.shape[0]
    blk = (tb * SUBLANES, LANES)
    up, down = (D_MODEL, EXPERT_HIDDEN), (EXPERT_HIDDEN, D_MODEL)
    hbm = pl.BlockSpec(memory_space=pl.ANY)
    return pl.pallas_call(
        functools.partial(_expert_kernel, tb=tb),
        out_shape=jax.ShapeDtypeStruct(xs.shape, u32),
        grid_spec=pltpu.PrefetchScalarGridSpec(
            num_scalar_prefetch=3,
            grid=(n_blocks,),
            in_specs=[pl.BlockSpec(blk, lambda i, be, ne, nu: (jnp.minimum(i, nu[0] - 1), 0)),
                      hbm, hbm, hbm],
            out_specs=pl.BlockSpec(blk, lambda i, be, ne, nu: (i, 0)),
            scratch_shapes=[pltpu.VMEM(up, f32), pltpu.VMEM(up, f32), pltpu.VMEM(down, f32),
                            pltpu.VMEM(up, bf16), pltpu.VMEM(up, bf16), pltpu.VMEM(down, bf16),
                            pltpu.SemaphoreType.DMA((3,))]),
        compiler_params=pltpu.CompilerParams(dimension_semantics=("arbitrary",),
                                             vmem_limit_bytes=VMEM_LIMIT),
        name="experts",
    )(block_e, next_e, n_used, xs, w_eg, w_eu, w_ed)


def _combine_kernel(pos_ref, yb_hbm, x1_ref, route_ref, gt_ref, gf_ref, y_ref, ybuf, sem, *, tm):
    i = pl.program_id(0)
    slot = i % 2

    def start(blk, s):
        def body(r, carry):
            for k in range(TOP_K):
                t = pos_ref[(k * pl.num_programs(0) + blk) * tm + r]
                pltpu.make_async_copy(_row_tile(yb_hbm, t), _row_tile(ybuf.at[s, k], r),
                                      sem.at[s]).start(priority=k)
            return carry
        lax.fori_loop(0, tm, body, 0, unroll=8)

    @pl.when(i == 0)
    def _():
        start(0, 0)

    @pl.when(i + 1 < pl.num_programs(0))
    def _():
        start(i + 1, 1 - slot)

    for k in range(TOP_K):
        pltpu.make_async_copy(yb_hbm.at[pl.ds(0, tm * SUBLANES), :], ybuf.at[slot, k], sem.at[slot]).wait()
    route = route_ref[...]
    a0, b0 = _unpack_pair(_load_row_tiles(ybuf.at[slot, 0], tm))
    a1, b1 = _unpack_pair(_load_row_tiles(ybuf.at[slot, 1], tm))
    g0 = route[:, ROUTE_G0:ROUTE_G0 + 1]
    g1 = route[:, ROUTE_G1:ROUTE_G1 + 1]
    moe = jnp.concatenate([a0 * g0 + a1 * g1, b0 * g0 + b1 * g1], axis=1)
    x2 = x1_ref[...] + _rows(gt_ref) * moe
    y_ref[...] = _rms(x2) * gf_ref[...]


def _combine(pos_km, yb, x1, route, gt, g_final, *, tm, n, row0, gt_spec):
    blk0 = row0 // tm
    return pl.pallas_call(
        functools.partial(_combine_kernel, tm=tm),
        out_shape=jax.ShapeDtypeStruct((n, D_MODEL), f32),
        grid_spec=pltpu.PrefetchScalarGridSpec(
            num_scalar_prefetch=1,
            grid=(n // tm,),
            in_specs=[pl.BlockSpec(memory_space=pl.ANY),
                      pl.BlockSpec((tm, D_MODEL), lambda i, pos: (blk0 + i, 0)),
                      pl.BlockSpec((tm, LANES), lambda i, pos: (blk0 + i, 0)),
                      gt_spec,
                      pl.BlockSpec((1, D_MODEL), lambda i, pos: (0, 0))],
            out_specs=pl.BlockSpec((tm, D_MODEL), lambda i, pos: (i, 0)),
            scratch_shapes=[pltpu.VMEM((2, TOP_K, tm * SUBLANES, LANES), u32),
                            pltpu.SemaphoreType.DMA((2,))]),
        compiler_params=pltpu.CompilerParams(dimension_semantics=("arbitrary",),
                                             vmem_limit_bytes=VMEM_LIMIT),
        name="combine",
    )(pos_km, yb, x1, route, gt, g_final.reshape(1, D_MODEL))


def kernel(x_prompt, x_sample, cache_swa_k, cache_swa_v, state_gla, c_prompt, c_sample, g_mix_norm, g_ffn_norm, w_ada, b_ada, w_in, attn_sinks, w_gla_gate, b_gla_gate, g_gla_norm, w_out, w_router_group, b_router_group, w_router_expert, b_router_expert, w_expert_gate, w_expert_up, w_expert_down, g_final):
    depth = w_in.shape[0]
    assert depth == 1
    bp, t, d = x_prompt.shape
    bs, ts, _ = x_sample.shape
    n_p, n_s = bp * t, bs * ts
    n_tot = n_p + n_s
    assert n_tot % LANES == 0
    tm = 512
    to = 256
    tb = 512
    tc = 256
    td = LANES
    gla_c = 128
    gla_sub = 4
    swa_sub = 2

    w_in0 = w_in[0]
    w_q = w_in0[:, :_C_K].astype(bf16)
    w_q_t = w_q.T
    w_rest = w_in0[:, _C_K:_C_AB].astype(bf16)
    w_ab = jnp.pad(w_in0[:, _C_AB:], ((0, 0), (0, LANES - GLA_GATE_RANK))).astype(bf16)
    w_gate = jnp.pad(w_gla_gate[0], ((0, LANES - GLA_GATE_RANK), (0, 0))).astype(bf16)
    w_out_b = w_out[0].astype(bf16)
    n_r = N_GROUPS + N_EXPERTS
    w_r = jnp.pad(jnp.concatenate([w_router_group[0], w_router_expert[0]], axis=1),
                  ((0, 0), (0, LANES - n_r)))
    w_r_hi = w_r.astype(bf16)
    w_r_lo = (w_r - w_r_hi.astype(f32)).astype(bf16)
    w_route = jnp.concatenate([w_r_hi, w_r_lo], axis=1)
    b_route = jnp.pad(jnp.concatenate([b_router_group[0], b_router_expert[0]]),
                      (0, LANES - n_r)).reshape(1, LANES)

    c_all = jnp.concatenate([c_prompt, c_sample], axis=0)
    mod = _adaln(c_all, w_ada[0], b_ada[0])
    mod_tab = mod.reshape((bp + bs) * N_MOD, 1, d)
    mod_s = jnp.repeat(mod[bp:].reshape(bs, N_MOD, d), ts, axis=0)
    mod_s = [mod_s[:, m] for m in range(N_MOD)]
    pmod = lambda comp, tile, lag=0: _mod_spec(comp, t // tile, n_p // tile, lag)
    smod = pl.BlockSpec((n_s, d), lambda i: (0, 0))

    xp = x_prompt.reshape(n_p, d)
    xs = x_sample.reshape(n_s, d)
    proj_w = (w_rest, w_ab, w_gate, b_gla_gate[0])
    qp, kp, vp, gqp, gkp, gvp, rp, lgp = _proj(xp, mod_tab, mod_tab, g_mix_norm[0], w_q_t, *proj_w, tm=tm,
                                               mod_specs=(pmod(0, tm), pmod(1, tm)), q_transposed=True)
    qs, ks, vs, gqs, gks, gvs, rs, lgs = _proj(xs, mod_s[0], mod_s[1], g_mix_norm[0], w_q, *proj_w, tm=n_s,
                                               mod_specs=(smod, smod), q_transposed=False)

    sinks = attn_sinks[0]
    oap = _swa_t(qp, kp, vp, sinks, n_seq=bp, tiles=t // WINDOW, n_sub=swa_sub)
    ck = cache_swa_k[0].reshape(bs * WINDOW, KV_WIDTH)
    cv = cache_swa_v[0].reshape(bs * WINDOW, KV_WIDTH)
    oas = _swa(qs, ck, cv, ks, vs, sinks, n_seq=bs, tq=ts)
    s_zero = jnp.zeros((bp, GLA_HEADS, GLA_DK, GLA_DV), f32)
    obp, sp = _gla(gqp, gkp, gvp, lgp, rp, g_gla_norm[0], s_zero, n_seq=bp, c=gla_c, n_sub=gla_sub,
                   steps=t // (gla_c * gla_sub))
    obs, ss = _gla(gqs, gks, gvs, lgs, rs, g_gla_norm[0], state_gla[0], n_seq=bs, c=ts, n_sub=1, steps=1)

    out_w = (g_ffn_norm[0], w_out_b, w_route, b_route)
    bufs = _outproj(oap, obp, xp, mod_tab, mod_tab, mod_tab, *out_w, tm=to,
                    mod_specs=(pmod(2, to), pmod(3, to, 1), pmod(4, to, 1)), n_total=n_tot, row0=0,
                    fill_steps=-(-n_s // to), oa_transposed=True)
    x1, h2p, route = _outproj(oas, obs, xs, mod_s[2], mod_s[3], mod_s[4], *out_w, tm=n_s,
                              mod_specs=(smod, smod, smod), n_total=n_tot, row0=n_p, bufs=bufs)

    n_blocks = -(-(n_tot * TOP_K + N_EXPERTS * (tb - 1)) // tb)
    pos, pend_tab = _rank(route, tb=tb)
    pend = pend_tab[:, 0]
    block_e = jnp.minimum(jnp.sum(pend[None, :] <= (jnp.arange(n_blocks, dtype=jnp.int32) * tb)[:, None],
                                  axis=1), N_EXPERTS - 1).astype(jnp.int32)
    n_used = pend[N_EXPERTS - 1:] // tb
    experts = jnp.arange(N_EXPERTS, dtype=jnp.int32)
    has_rows = jnp.diff(pend, prepend=0) > 0
    later = jnp.where((experts[None, :] > experts[:, None]) & has_rows[None, :], experts[None, :], N_EXPERTS)
    next_with_rows = jnp.min(later, axis=1)
    next_with_rows = jnp.where(next_with_rows == N_EXPERTS, experts, next_with_rows)
    next_e = jnp.sum(jnp.where(block_e[:, None] == experts[None, :], next_with_rows[None, :], 0),
                     axis=1).astype(jnp.int32)
    xsort = _dispatch(pos.reshape(-1), pend, h2p, tm=td, tb=tb, n_blocks=n_blocks)
    yb = _experts(block_e, next_e, n_used, xsort, w_expert_gate[0], w_expert_up[0], w_expert_down[0], tb=tb)

    pos = pos.reshape(TOP_K, n_tot)
    pos_p = pos[:, :n_p].reshape(-1)
    pos_s = pos[:, n_p:].reshape(-1)
    gt_p = pl.BlockSpec((1, 1, d), lambda i, p: ((i // (t // tc)) * N_MOD + 5, 0, 0))
    gt_s = pl.BlockSpec((n_s, d), lambda i, p: (0, 0))
    y_p = _combine(pos_p, yb, x1, route, mod_tab, g_final, tm=tc, n=n_p, row0=0, gt_spec=gt_p)
    y_s = _combine(pos_s, yb, x1, route, mod_s[5], g_final, tm=n_s, n=n_s, row0=n_p, gt_spec=gt_s)

    kv_shape = (SWA_KV_HEADS, SWA_HEAD_DIM)
    last = lambda a: a.reshape(bp, t, KV_WIDTH)[:, t - WINDOW:, :].reshape(1, bp, WINDOW, *kv_shape)
    k_state_p, v_state_p = last(kp), last(vp)
    return (y_p.reshape(bp, t, d), y_s.reshape(bs, ts, d), k_state_p, v_state_p, sp[None],
            ks.reshape(bs, ts, *kv_shape)[None], vs.reshape(bs, ts, *kv_shape)[None], ss[None])
```

```python
import functools
import math

import jax
import jax.numpy as jnp
from jax import lax
from jax.experimental import pallas as pl
from jax.experimental.pallas import tpu as pltpu

f32 = jnp.float32
bf16 = jnp.bfloat16
u32 = jnp.uint32

D_MODEL = 2048
N_MOD = 6
EPS = 1e-6
NEG_INF = -1e30
LOG2_E = math.log2(math.e)

SWA_HEAD_DIM = 64
SWA_KV_HEADS = 2
SWA_GROUP = 8
SWA_WIDTH = SWA_KV_HEADS * SWA_GROUP * SWA_HEAD_DIM
KV_WIDTH = SWA_KV_HEADS * SWA_HEAD_DIM
WINDOW = 128
CHUNK = 64

GLA_HEADS = 4
GLA_DK = 128
GLA_DV = 256
GLA_QK_WIDTH = GLA_HEADS * GLA_DK
GLA_V_WIDTH = GLA_HEADS * GLA_DV
GLA_GATE_RANK = 16
GLA_GATE_NORM = 16.0

N_GROUPS = 4
EXPERTS_PER_GROUP = 8
N_EXPERTS = N_GROUPS * EXPERTS_PER_GROUP
TOP_K = 2
EXPERT_HIDDEN = D_MODEL // 4

_C_Q = 0
_C_K = _C_Q + SWA_WIDTH
_C_V = _C_K + KV_WIDTH
_C_GQ = _C_V + KV_WIDTH
_C_GK = _C_GQ + GLA_QK_WIDTH
_C_GV = _C_GK + GLA_QK_WIDTH
_C_R = _C_GV + GLA_V_WIDTH
_C_AB = _C_R + GLA_V_WIDTH

LANES = 128
SUBLANES = 8
PACKED_WIDTH = D_MODEL // 2
assert PACKED_WIDTH == SUBLANES * LANES
VMEM_LIMIT = 56 * 1024 * 1024

ROUTE_E0, ROUTE_E1, ROUTE_G0, ROUTE_G1 = 0, 1, 2, 3


def _dot(a, b):
    return jnp.dot(a, b, preferred_element_type=f32)


def _dot_nt(a, b):
    return lax.dot_general(a, b, (((1,), (1,)), ((), ())), preferred_element_type=f32)


def _dot_tn(a, b):
    return lax.dot_general(a, b, (((0,), (0,)), ((), ())), preferred_element_type=f32)


def _silu(x):
    return x / (1.0 + jnp.exp(-x))


def _rows(ref):
    v = ref[...]
    return v.reshape(v.shape[-2:])


def _rms(x):
    return x * lax.rsqrt(jnp.mean(x * x, axis=-1, keepdims=True) + EPS)


def _resident(shape):
    return pl.BlockSpec(shape, lambda *_: (0,) * len(shape), pipeline_mode=pl.Buffered(1))


def _mod_spec(comp, tiles_per_seq, n_tiles, lag=0):
    def index(i):
        tile = jnp.minimum(jnp.maximum(i - lag, 0), n_tiles - 1)
        return ((tile // tiles_per_seq) * N_MOD + comp, 0, 0)
    return pl.BlockSpec((1, 1, D_MODEL), index)


def _adaln_kernel(c_ref, w_ref, b_ref, o_ref):
    a = _silu(c_ref[...]).astype(bf16)
    o_ref[...] = _dot(a, w_ref[...].astype(bf16)) + b_ref[...]


def _adaln(c_all, w_ada, b_ada, tn=1024):
    r = c_all.shape[0]
    n = w_ada.shape[1]
    return pl.pallas_call(
        _adaln_kernel,
        out_shape=jax.ShapeDtypeStruct((r, n), f32),
        grid=(n // tn,),
        in_specs=[pl.BlockSpec((r, D_MODEL), lambda j: (0, 0)),
                  pl.BlockSpec((D_MODEL, tn), lambda j: (0, j)),
                  pl.BlockSpec((1, tn), lambda j: (0, j))],
        out_specs=pl.BlockSpec((r, tn), lambda j: (0, j)),
        compiler_params=pltpu.CompilerParams(dimension_semantics=("arbitrary",),
                                             vmem_limit_bytes=VMEM_LIMIT),
        name="adaln",
    )(c_all, w_ada, b_ada.reshape(1, n))


def _proj_kernel(x_ref, sh_ref, sc_ref, g_ref, wq_ref, w_ref, wab_ref, wg_ref, bg_ref,
                 q_ref, k_ref, v_ref, gq_ref, gk_ref, gv_ref, r_ref, lg_ref, *, q_transposed):
    h = (_rms(x_ref[...]) * g_ref[...]) * (1.0 + _rows(sc_ref)) + _rows(sh_ref)
    hb = h.astype(bf16)
    if q_transposed:
        q_ref[...] = (_dot_nt(wq_ref[...], hb) * (LOG2_E * SWA_HEAD_DIM ** -0.5)).astype(bf16)
    else:
        q_ref[...] = _dot(hb, wq_ref[...]).astype(bf16)
    col = lambda c: c - _C_K
    kv = _dot(hb, w_ref[:, col(_C_K):col(_C_GQ)])
    k_ref[...] = kv[:, :KV_WIDTH]
    v_ref[...] = kv[:, KV_WIDTH:]
    gq_ref[...] = (_dot(hb, w_ref[:, col(_C_GQ):col(_C_GK)]) * (GLA_DK ** -0.5)).astype(bf16)
    gk_ref[...] = _dot(hb, w_ref[:, col(_C_GK):col(_C_GV)]).astype(bf16)
    gv_ref[...] = _dot(hb, w_ref[:, col(_C_GV):col(_C_R)]).astype(bf16)
    r_ref[...] = _silu(_dot(hb, w_ref[:, col(_C_R):col(_C_AB)])).astype(bf16)
    ab = _dot(hb, wab_ref[...])
    z = _dot(ab.astype(bf16), wg_ref[...]) + bg_ref[...]
    log_sig = jnp.minimum(z, 0.0) - jnp.log1p(jnp.exp(-jnp.abs(z)))
    lg_ref[...] = log_sig * (LOG2_E / GLA_GATE_NORM)


def _proj(x2d, sh, sc, g_mix, w_q, w_rest, w_ab, w_gate, b_gate, *, tm, mod_specs, q_transposed):
    n = x2d.shape[0]
    row = lambda w: pl.BlockSpec((tm, w), lambda i: (i, 0))
    outs = [(KV_WIDTH, f32), (KV_WIDTH, f32), (GLA_QK_WIDTH, bf16),
            (GLA_QK_WIDTH, bf16), (GLA_V_WIDTH, bf16), (GLA_V_WIDTH, bf16), (GLA_QK_WIDTH, f32)]
    if q_transposed:
        q_shape, q_spec = (SWA_WIDTH, n), pl.BlockSpec((SWA_WIDTH, tm), lambda i: (0, i))
    else:
        q_shape, q_spec = (n, SWA_WIDTH), row(SWA_WIDTH)
    return pl.pallas_call(
        functools.partial(_proj_kernel, q_transposed=q_transposed),
        out_shape=[jax.ShapeDtypeStruct(q_shape, bf16)]
                  + [jax.ShapeDtypeStruct((n, w), dt) for w, dt in outs],
        grid=(n // tm,),
        in_specs=[row(D_MODEL), mod_specs[0], mod_specs[1],
                  _resident((1, D_MODEL)), _resident(w_q.shape), _resident(w_rest.shape),
                  _resident(w_ab.shape), _resident(w_gate.shape), _resident((1, GLA_QK_WIDTH))],
        out_specs=[q_spec] + [row(w) for w, _ in outs],
        compiler_params=pltpu.CompilerParams(dimension_semantics=("parallel",),
                                             vmem_limit_bytes=VMEM_LIMIT),
        name="proj",
    )(x2d, sh, sc, g_mix.reshape(1, D_MODEL), w_q, w_rest, w_ab, w_gate, b_gate.reshape(1, -1))


def _swa_kernel(q_ref, kp_ref, vp_ref, kc_ref, vc_ref, sink_ref, o_ref, *, tq):
    pad = jnp.zeros((WINDOW - tq, KV_WIDTH), f32)
    k_all = jnp.concatenate([kp_ref[...], kc_ref[...], pad], axis=0).astype(bf16)
    v_all = jnp.concatenate([vp_ref[...], vc_ref[...], pad], axis=0).astype(bf16)
    rows, cols = SWA_GROUP * tq, 2 * WINDOW
    valid = lax.broadcasted_iota(jnp.int32, (rows, cols), 1) < WINDOW + tq
    q = q_ref[...]
    outs = []
    for j in range(SWA_KV_HEADS):
        heads = [q[:, (j * SWA_GROUP + g) * SWA_HEAD_DIM:(j * SWA_GROUP + g + 1) * SWA_HEAD_DIM]
                 for g in range(SWA_GROUP)]
        qs = jnp.concatenate(heads, axis=0)
        kj = k_all[:, j * SWA_HEAD_DIM:(j + 1) * SWA_HEAD_DIM]
        vj = v_all[:, j * SWA_HEAD_DIM:(j + 1) * SWA_HEAD_DIM]
        s = _dot_nt(qs, kj) * (SWA_HEAD_DIM ** -0.5)
        s = jnp.where(valid, s, NEG_INF)
        sink = sink_ref[j]
        m = jnp.maximum(jnp.max(s, axis=1, keepdims=True), sink)
        p = jnp.exp(s - m)
        den = jnp.sum(p, axis=1, keepdims=True) + jnp.exp(sink - m)
        o = _dot(p.astype(bf16), vj) / den
        outs.append(jnp.concatenate([o[g * tq:(g + 1) * tq] for g in range(SWA_GROUP)], axis=1))
    o_ref[...] = jnp.concatenate(outs, axis=1).astype(bf16)


def _swa(q, k_past, v_past, k_new, v_new, sinks, *, n_seq, tq):
    sink_rows = jnp.repeat(sinks.astype(f32).reshape(SWA_KV_HEADS, SWA_GROUP), tq, axis=1)
    sink_rows = sink_rows.reshape(SWA_KV_HEADS, SWA_GROUP * tq, 1)
    new = lambda w: pl.BlockSpec((tq, w), lambda b: (b, 0))
    past = pl.BlockSpec((WINDOW, KV_WIDTH), lambda b: (b, 0))
    return pl.pallas_call(
        functools.partial(_swa_kernel, tq=tq),
        out_shape=jax.ShapeDtypeStruct(q.shape, bf16),
        grid=(n_seq,),
        in_specs=[new(SWA_WIDTH), past, past, new(KV_WIDTH), new(KV_WIDTH),
                  pl.BlockSpec(sink_rows.shape, lambda b: (0, 0, 0))],
        out_specs=new(SWA_WIDTH),
        compiler_params=pltpu.CompilerParams(dimension_semantics=("parallel",),
                                             vmem_limit_bytes=VMEM_LIMIT),
        name="swa",
    )(q, k_past, v_past, k_new, v_new, sink_rows)


def _swa_t_kernel(q_ref, kp_ref, vp_ref, kc_ref, vc_ref, sink_ref, mk_ref, mq0_ref, mq_ref, o_ref, *, n_sub):
    tq = WINDOW
    hd = SWA_HEAD_DIM
    k_all = jnp.concatenate([kp_ref[...], kc_ref[...]], axis=0).astype(bf16)
    v_all = jnp.concatenate([vp_ref[...], vc_ref[...]], axis=0).astype(bf16)
    low_lanes = lax.broadcasted_iota(jnp.int32, (2 * WINDOW, KV_WIDTH), 1) < hd
    ones = jnp.ones((2 * WINDOW, KV_WIDTH), bf16)
    for sub in range(n_sub):
        keys = slice(sub * tq, sub * tq + 2 * WINDOW)
        toks = slice(sub * tq, (sub + 1) * tq)
        mq = (mq0_ref if sub == 0 else mq_ref)[0]
        for j in range(SWA_KV_HEADS):
            head = lambda g: slice((j * SWA_GROUP + g) * hd, (j * SWA_GROUP + g + 1) * hd)
            qs = jnp.concatenate([q_ref[head(g), toks] for g in range(SWA_GROUP)], axis=1)
            own = low_lanes if j == 0 else jnp.logical_not(low_lanes)
            k_aug = jnp.where(own, k_all[keys], mk_ref[j])
            v_aug = jnp.where(own, v_all[keys], ones)
            q_aug = jnp.concatenate([qs, mq] if j == 0 else [mq, qs], axis=0)
            s = _dot(k_aug, q_aug)
            sink = sink_ref[j]
            m = jnp.maximum(jnp.max(s, axis=0, keepdims=True), sink)
            p = jnp.exp2(s - m).astype(bf16)
            o_aug = _dot_tn(v_aug, p)
            pv, p_sum = (o_aug[:hd], o_aug[hd:hd + 1]) if j == 0 else (o_aug[hd:], o_aug[0:1])
            o = pv / (p_sum + jnp.exp2(sink - m))
            for g in range(SWA_GROUP):
                o_ref[head(g), toks] = o[:, g * tq:(g + 1) * tq].astype(bf16)


def _swa_t(q_t, k, v, sinks, *, n_seq, tiles, n_sub):
    tq = WINDOW
    steps = tiles // n_sub
    hd = SWA_HEAD_DIM
    sink_cols = jnp.repeat(sinks.astype(f32).reshape(SWA_KV_HEADS, SWA_GROUP) * LOG2_E, tq, axis=1)
    sink_cols = sink_cols.reshape(SWA_KV_HEADS, 1, SWA_GROUP * tq)
    n_kc = 2 * WINDOW // CHUNK
    key_chunk = jnp.arange(2 * WINDOW) // CHUNK
    lane = jnp.arange(KV_WIDTH)
    mk = jnp.stack([lane[None, :] == hd + key_chunk[:, None], lane[None, :] == key_chunk[:, None]]).astype(bf16)
    q_chunk = (jnp.arange(SWA_GROUP * tq) % tq) // CHUNK
    kc = jnp.arange(hd)[:, None]
    band = (kc >= q_chunk[None, :]) & (kc <= q_chunk[None, :] + WINDOW // CHUNK)
    visible = jnp.stack([band & (kc >= WINDOW // CHUNK), band]) | (kc >= n_kc)
    mq = jnp.where(visible, 0.0, NEG_INF).astype(bf16)
    mq0_spec = pl.BlockSpec((1, hd, SWA_GROUP * tq), lambda b, u: (jnp.minimum(u, 1), 0, 0))
    mq_spec = pl.BlockSpec((1, hd, SWA_GROUP * tq), lambda b, u: (1, 0, 0))
    qspec = pl.BlockSpec((SWA_WIDTH, n_sub * tq), lambda b, u: (0, b * steps + u))
    cur = pl.BlockSpec((n_sub * tq, KV_WIDTH), lambda b, u: (b * steps + u, 0))
    prev = pl.BlockSpec((WINDOW, KV_WIDTH), lambda b, u: (b * tiles + jnp.maximum(n_sub * u - 1, 0), 0))
    return pl.pallas_call(
        functools.partial(_swa_t_kernel, n_sub=n_sub),
        out_shape=jax.ShapeDtypeStruct(q_t.shape, bf16),
        grid=(n_seq, steps),
        in_specs=[qspec, prev, prev, cur, cur, pl.BlockSpec(sink_cols.shape, lambda b, u: (0, 0, 0)),
                  pl.BlockSpec(mk.shape, lambda b, u: (0, 0, 0)), mq0_spec, mq_spec],
        out_specs=qspec,
        compiler_params=pltpu.CompilerParams(dimension_semantics=("parallel", "arbitrary"),
                                             vmem_limit_bytes=VMEM_LIMIT),
        name="swa_t",
    )(q_t, k, v, k, v, sink_cols, mk, mq, mq)


def _gla_kernel(q_ref, k_ref, v_ref, lg_ref, r_ref, gh_ref, s0_ref, o_ref, s_ref, *, c, n_sub):
    @pl.when(pl.program_id(1) == 0)
    def _():
        s_ref[...] = s0_ref[...]

    row = lax.broadcasted_iota(jnp.int32, (c, GLA_DK), 0)
    causal = (lax.broadcasted_iota(jnp.int32, (c, c), 0) >= lax.broadcasted_iota(jnp.int32, (c, c), 1))
    for sub in range(n_sub):
        ts = slice(sub * c, (sub + 1) * c)
        for h in range(GLA_HEADS):
            ks = slice(h * GLA_DK, (h + 1) * GLA_DK)
            vs = slice(h * GLA_DV, (h + 1) * GLA_DV)
            b = lg_ref[ts, ks]
            step = 1
            while step < c:
                b = b + jnp.where(row >= step, pltpu.roll(b, step, 0), 0.0)
                step *= 2
            b_last = b[c - 1:c, :]
            q = q_ref[ts, ks].astype(f32)
            k = k_ref[ts, ks].astype(f32)
            v = v_ref[ts, vs]
            qd = (q * jnp.exp2(b)).astype(bf16)
            kd = (k * jnp.exp2(-b)).astype(bf16)
            kl = (k * jnp.exp2(b_last - b)).astype(bf16)
            a = jnp.where(causal, _dot_nt(qd, kd), 0.0)
            s = s_ref[0, h]
            o = _dot(qd, s.astype(bf16)) + _dot(a.astype(bf16), v)
            decay = jnp.broadcast_to(jnp.exp2(b_last), (GLA_DK, GLA_DK)).T
            s_ref[0, h] = s * jnp.concatenate([decay, decay], axis=1) + _dot_tn(kl, v)
            on = _rms(o) * gh_ref[...]
            o_ref[ts, vs] = (on * r_ref[ts, vs].astype(f32)).astype(bf16)


def _gla(gq, gk, gv, lg, r, g_head, s0, *, n_seq, c, n_sub, steps):
    rows = c * n_sub
    blk = lambda w: pl.BlockSpec((rows, w), lambda b, t: (b * steps + t, 0))
    state = pl.BlockSpec((1, GLA_HEADS, GLA_DK, GLA_DV), lambda b, t: (b, 0, 0, 0))
    return pl.pallas_call(
        functools.partial(_gla_kernel, c=c, n_sub=n_sub),
        out_shape=[jax.ShapeDtypeStruct(gv.shape, bf16),
                   jax.ShapeDtypeStruct((n_seq, GLA_HEADS, GLA_DK, GLA_DV), f32)],
        grid=(n_seq, steps),
        in_specs=[blk(GLA_QK_WIDTH), blk(GLA_QK_WIDTH), blk(GLA_V_WIDTH), blk(GLA_QK_WIDTH),
                  blk(GLA_V_WIDTH), pl.BlockSpec((1, GLA_DV), lambda b, t: (0, 0)), state],
        out_specs=[blk(GLA_V_WIDTH), state],
        compiler_params=pltpu.CompilerParams(dimension_semantics=("parallel", "arbitrary"),
                                             vmem_limit_bytes=VMEM_LIMIT),
        name="gla",
    )(gq, gk, gv, lg, r, g_head.reshape(1, GLA_DV), s0)


def _pack_pair(hb):
    w = hb.shape[1] // 2
    a = lax.bitcast_convert_type(hb[:, :w].astype(f32), u32)
    b = lax.bitcast_convert_type(hb[:, w:].astype(f32), u32)
    return a | (b >> 16)


def _unpack_pair(p):
    a = lax.bitcast_convert_type(p & jnp.uint32(0xFFFF0000), f32)
    b = lax.bitcast_convert_type(p << 16, f32)
    return a, b


def _load_row_tiles(ref, n):
    return jnp.concatenate([ref[pl.ds(s, n, stride=SUBLANES), :] for s in range(SUBLANES)], axis=1)


def _store_row_tiles(ref, val):
    n = val.shape[0]
    for s in range(SUBLANES):
        ref[pl.ds(s, n, stride=SUBLANES), :] = val[:, s * LANES:(s + 1) * LANES]


def _row_tile(ref, r):
    return ref.at[pl.ds(pl.multiple_of(r * SUBLANES, SUBLANES), SUBLANES), :]


def _outproj_kernel(*refs, n_alias, oa_transposed):
    (oa_ref, ob_ref, x_ref, gt_ref, sh_ref, sc_ref, g_ref, w_ref, wr_ref, br_ref) = refs[:10]
    x1_ref, h2_ref, route_ref, x1_s = refs[10 + n_alias:]

    @pl.when(pl.program_id(0) == 0)
    def _():
        x1_s[...] = jnp.zeros(x1_s.shape, x1_s.dtype)

    h2 = (_rms(x1_s[...]) * g_ref[...]) * (1.0 + _rows(sc_ref)) + _rows(sh_ref)
    hi = h2.astype(bf16)
    _store_row_tiles(h2_ref, _pack_pair(hi))
    lo = (h2 - hi.astype(f32)).astype(bf16)
    r1 = _dot(hi, wr_ref[...])
    logits = r1[:, :LANES] + r1[:, LANES:] + _dot(lo, wr_ref[:, :LANES]) + br_ref[...]
    lane = lax.broadcasted_iota(jnp.int32, logits.shape, 1)
    lane_f = lane.astype(f32)
    neg = float("-inf")
    first = lambda hit: jnp.min(jnp.where(hit, lane_f, float(LANES)), axis=1, keepdims=True)
    lg_g = jnp.where(lane < N_GROUPS, logits, neg)
    g_max = jnp.max(lg_g, axis=1, keepdims=True)
    g_sel = first(lg_g == g_max)
    p_sel = 1.0 / jnp.sum(jnp.exp(lg_g - g_max), axis=1, keepdims=True)
    e_lane = lane - N_GROUPS
    in_group = (e_lane >= 0) & (e_lane < N_EXPERTS) & ((e_lane // EXPERTS_PER_GROUP).astype(f32) == g_sel)
    lg_e = jnp.where(in_group, logits, neg)
    v1 = jnp.max(lg_e, axis=1, keepdims=True)
    i1 = first(lg_e == v1)
    lg_e2 = jnp.where(lane_f == i1, neg, lg_e)
    v2 = jnp.max(lg_e2, axis=1, keepdims=True)
    i2 = first(lg_e2 == v2)
    e = jnp.exp(v2 - v1)
    g1 = p_sel / (1.0 + e)
    g2 = p_sel * e / (1.0 + e)
    rec = jnp.where(lane == ROUTE_E0, i1 - N_GROUPS, 0.0)
    rec = jnp.where(lane == ROUTE_E1, i2 - N_GROUPS, rec)
    rec = jnp.where(lane == ROUTE_G0, g1, rec)
    rec = jnp.where(lane == ROUTE_G1, g2, rec)
    route_ref[...] = rec

    oa_dot = _dot_tn if oa_transposed else _dot
    mix = oa_dot(oa_ref[...], w_ref[:SWA_WIDTH, :]) + _dot(ob_ref[...], w_ref[SWA_WIDTH:, :])
    x1 = x_ref[...] + _rows(gt_ref) * mix
    x1_ref[...] = x1
    x1_s[...] = x1


def _outproj(oa, ob, x2d, gt, sh, sc, g_ffn, w_out, w_route, b_route, *, tm, mod_specs,
             n_total, row0, bufs=None, fill_steps=0, oa_transposed=False):
    n = x2d.shape[0]
    tiles = n // tm
    blocks = tiles + fill_steps
    row = lambda w: pl.BlockSpec((tm, w), lambda i: (jnp.minimum(i, tiles - 1), 0))
    oa_spec = row(SWA_WIDTH)
    if oa_transposed:
        oa_spec = pl.BlockSpec((SWA_WIDTH, tm), lambda i: (0, jnp.minimum(i, tiles - 1)))
    head_blk = lambda i: row0 // tm + jnp.minimum(i, blocks - 1)
    tail_blk = lambda i: row0 // tm + jnp.maximum(i - 1, 0)
    out_x1 = pl.BlockSpec((tm, D_MODEL), lambda i: (head_blk(i), 0))
    out_tiles = pl.BlockSpec((tm * SUBLANES, LANES), lambda i: (tail_blk(i), 0))
    out_route = pl.BlockSpec((tm, LANES), lambda i: (tail_blk(i), 0))
    alias_in = list(bufs) if bufs is not None else []
    n_in = 10
    return pl.pallas_call(
        functools.partial(_outproj_kernel, n_alias=len(alias_in), oa_transposed=oa_transposed),
        out_shape=[jax.ShapeDtypeStruct((n_total, D_MODEL), f32),
                   jax.ShapeDtypeStruct((n_total * SUBLANES, LANES), u32),
                   jax.ShapeDtypeStruct((n_total, LANES), f32)],
        grid=(blocks + 1,),
        in_specs=[oa_spec, row(GLA_V_WIDTH), row(D_MODEL), mod_specs[0], mod_specs[1],
                  mod_specs[2], _resident((1, D_MODEL)), _resident(w_out.shape),
                  _resident(w_route.shape), _resident((1, LANES))]
                 + [pl.BlockSpec(memory_space=pl.ANY)] * len(alias_in),
        out_specs=[out_x1, out_tiles, out_route],
        scratch_shapes=[pltpu.VMEM((tm, D_MODEL), f32)],
        input_output_aliases={n_in + a: a for a in range(len(alias_in))},
        compiler_params=pltpu.CompilerParams(dimension_semantics=("arbitrary",),
                                             vmem_limit_bytes=VMEM_LIMIT),
        name="outproj",
    )(oa, ob, x2d, gt, sh, sc, g_ffn.reshape(1, D_MODEL), w_out, w_route, b_route, *alias_in)


def _rank_kernel(route_ref, pos_ref, pend_ref, rec_s, *, n_chunks, tb):
    n_e = N_EXPERTS
    expert = lax.broadcasted_iota(jnp.int32, (n_e, LANES), 0)
    expert_f = expert.astype(f32)
    earlier = (lax.broadcasted_iota(jnp.int32, (LANES, LANES), 0)
               < lax.broadcasted_iota(jnp.int32, (LANES, LANES), 1)).astype(bf16)

    def onehots(c):
        rec = rec_s[c]
        return expert_f == rec[ROUTE_E0:ROUTE_E0 + 1, :], expert_f == rec[ROUTE_E1:ROUTE_E1 + 1, :]

    def count(c, cnt):
        rec_s[c] = route_ref[pl.ds(pl.multiple_of(c * LANES, LANES), LANES), :].T[:SUBLANES, :]
        h0, h1 = onehots(c)
        return cnt + jnp.sum((h0 | h1).astype(f32), axis=1, keepdims=True)

    cnt = lax.fori_loop(0, n_chunks, count, jnp.zeros((n_e, 1), f32), unroll=4)
    padded = jnp.floor((cnt + (tb - 1.0)) / tb) * tb
    end = jnp.broadcast_to(padded, (n_e, LANES))
    step = 1
    while step < n_e:
        end = end + jnp.where(expert >= step, pltpu.roll(end, step, 0), 0.0)
        step *= 2
    pend_ref[...] = end.astype(jnp.int32)

    def rank(c, base):
        h0, h1 = onehots(c)
        both = h0 | h1
        off = _dot(both.astype(bf16), earlier) + base
        pos_ref[0, pl.ds(c, 1), :] = jnp.sum(jnp.where(h0, off, 0.0), axis=0, keepdims=True).astype(jnp.int32)
        pos_ref[1, pl.ds(c, 1), :] = jnp.sum(jnp.where(h1, off, 0.0), axis=0, keepdims=True).astype(jnp.int32)
        return base + jnp.sum(both.astype(f32), axis=1, keepdims=True)

    lax.fori_loop(0, n_chunks, rank, (end - padded)[:, 0:1], unroll=4)


def _rank(route, *, tb):
    n = route.shape[0]
    n_chunks = n // LANES
    return pl.pallas_call(
        functools.partial(_rank_kernel, n_chunks=n_chunks, tb=tb),
        out_shape=[jax.ShapeDtypeStruct((TOP_K, n_chunks, LANES), jnp.int32),
                   jax.ShapeDtypeStruct((N_EXPERTS, LANES), jnp.int32)],
        in_specs=[_resident(route.shape)],
        scratch_shapes=[pltpu.VMEM((n_chunks, SUBLANES, LANES), f32)],
        compiler_params=pltpu.CompilerParams(vmem_limit_bytes=VMEM_LIMIT),
        name="rank",
    )(route)


def _dispatch_kernel(pos_ref, pend_ref, h2_ref, xs_hbm, stage, zbuf, sem, zsem, *, tm, tb, n_tot, steps):
    i = pl.program_id(0)
    slot = i % 2

    blk = tb * SUBLANES

    def wait_slot(s):
        for _ in range(TOP_K):
            pltpu.make_async_copy(stage.at[s], xs_hbm.at[pl.ds(0, tm * SUBLANES), :], sem.at[s]).wait()

    @pl.when(i == 0)
    def _():
        zbuf[...] = jnp.zeros(zbuf.shape, zbuf.dtype)

        def fill_block(b, start):
            cp = pltpu.make_async_copy(zbuf, xs_hbm.at[pl.ds(pl.multiple_of(b * blk, blk), blk), :], zsem)
            cp.start() if start else cp.wait()

        def fill(e, start):
            end = pend_ref[e]
            prev = jnp.where(e > 0, pend_ref[jnp.maximum(e - 1, 0)], 0)

            @pl.when(end > prev)
            def _():
                fill_block(end // tb - 1, start)

        fill_unused = fill_block

        first_unused = pend_ref[N_EXPERTS - 1] // tb
        n_blocks = xs_hbm.shape[0] // blk
        lax.fori_loop(0, N_EXPERTS, lambda e, c: (fill(e, True), c)[1], 0)
        lax.fori_loop(first_unused, n_blocks, lambda b, c: (fill_unused(b, True), c)[1], 0)
        lax.fori_loop(0, N_EXPERTS, lambda e, c: (fill(e, False), c)[1], 0)
        lax.fori_loop(first_unused, n_blocks, lambda b, c: (fill_unused(b, False), c)[1], 0)

    @pl.when(i >= 2)
    def _():
        wait_slot(slot)

    stage[slot] = h2_ref[...]

    def scatter(r, carry):
        for k in range(TOP_K):
            d = pos_ref[k * n_tot + i * tm + r]
            pltpu.make_async_copy(_row_tile(stage.at[slot], r), _row_tile(xs_hbm, d),
                                  sem.at[slot]).start(priority=k)
        return carry

    lax.fori_loop(0, tm, scatter, 0, unroll=8)

    @pl.when(i == steps - 1)
    def _():
        wait_slot(slot)
        if steps > 1:
            wait_slot(1 - slot)


def _dispatch(pos_flat, pend, h2p, *, tm, tb, n_blocks):
    n_tot = h2p.shape[0] // SUBLANES
    steps = n_tot // tm
    return pl.pallas_call(
        functools.partial(_dispatch_kernel, tm=tm, tb=tb, n_tot=n_tot, steps=steps),
        out_shape=jax.ShapeDtypeStruct((n_blocks * tb * SUBLANES, LANES), u32),
        grid_spec=pltpu.PrefetchScalarGridSpec(
            num_scalar_prefetch=2,
            grid=(steps,),
            in_specs=[pl.BlockSpec((tm * SUBLANES, LANES), lambda i, pos, pend: (i, 0))],
            out_specs=pl.BlockSpec(memory_space=pl.ANY),
            scratch_shapes=[pltpu.VMEM((2, tm * SUBLANES, LANES), u32),
                            pltpu.VMEM((tb * SUBLANES, LANES), u32),
                            pltpu.SemaphoreType.DMA((2,)), pltpu.SemaphoreType.DMA(())]),
        compiler_params=pltpu.CompilerParams(dimension_semantics=("arbitrary",),
                                             vmem_limit_bytes=VMEM_LIMIT),
        name="dispatch",
    )(pos_flat, pend, h2p)


def _expert_kernel(be_ref, ne_ref, nu_ref, xs_ref, wg_hbm, wu_hbm, wd_hbm, y_ref,
                   wg_s, wu_s, wd_s, wg_b, wu_b, wd_b, sem, *, tb):
    i = pl.program_id(0)
    used = i < nu_ref[0]
    e = be_ref[i]

    def fetch(expert):
        pairs = ((wg_hbm, wg_s), (wu_hbm, wu_s), (wd_hbm, wd_s))
        return [pltpu.make_async_copy(src.at[expert], dst, sem.at[n]) for n, (src, dst) in enumerate(pairs)]

    @pl.when(i == 0)
    def _():
        for cp in fetch(e):
            cp.start()

    @pl.when(used & ((i == 0) | (e != be_ref[jnp.maximum(i - 1, 0)])))
    def _():
        for cp in fetch(e):
            cp.wait()
        wg_b[...] = wg_s[...].astype(bf16)
        wu_b[...] = wu_s[...].astype(bf16)
        wd_b[...] = wd_s[...].astype(bf16)
        nxt = ne_ref[i]

        @pl.when(nxt != e)
        def _():
            for cp in fetch(nxt):
                cp.start()

    @pl.when(used)
    def _():
        a, b = _unpack_pair(_load_row_tiles(xs_ref, tb))
        x = jnp.concatenate([a.astype(bf16), b.astype(bf16)], axis=1)
        g = _dot(x, wg_b[...])
        u = _dot(x, wu_b[...])
        y = _dot((_silu(g) * u).astype(bf16), wd_b[...])
        _store_row_tiles(y_ref, _pack_pair(y.astype(bf16)))

    @pl.when(jnp.logical_not(used))
    def _():
        y_ref[...] = jnp.zeros(y_ref.shape, y_ref.dtype)


def _experts(block_e, next_e, n_used, xs, w_eg, w_eu, w_ed, *, tb):
    n_blocks = block_e.shape[0]
    blk = (tb * SUBLANES, LANES)
    up, down = (D_MODEL, EXPERT_HIDDEN), (EXPERT_HIDDEN, D_MODEL)
    hbm = pl.BlockSpec(memory_space=pl.ANY)
    return pl.pallas_call(
        functools.partial(_expert_kernel, tb=tb),
        out_shape=jax.ShapeDtypeStruct(xs.shape, u32),
        grid_spec=pltpu.PrefetchScalarGridSpec(
            num_scalar_prefetch=3,
            grid=(n_blocks,),
            in_specs=[pl.BlockSpec(blk, lambda i, be, ne, nu: (jnp.minimum(i, nu[0] - 1), 0)),
                      hbm, hbm, hbm],
            out_specs=pl.BlockSpec(blk, lambda i, be, ne, nu: (i, 0)),
            scratch_shapes=[pltpu.VMEM(up, f32), pltpu.VMEM(up, f32), pltpu.VMEM(down, f32),
                            pltpu.VMEM(up, bf16), pltpu.VMEM(up, bf16), pltpu.VMEM(down, bf16),
                            pltpu.SemaphoreType.DMA((3,))]),
        compiler_params=pltpu.CompilerParams(dimension_semantics=("arbitrary",),
                                             vmem_limit_bytes=VMEM_LIMIT),
        name="experts",
    )(block_e, next_e, n_used, xs, w_eg, w_eu, w_ed)


def _combine_kernel(pos_ref, yb_hbm, x1_ref, route_ref, gt_ref, gf_ref, y_ref, ybuf, sem, *, tm):
    i = pl.program_id(0)
    slot = i % 2

    def start(blk, s):
        def body(r, carry):
            for k in range(TOP_K):
                t = pos_ref[(k * pl.num_programs(0) + blk) * tm + r]
                pltpu.make_async_copy(_row_tile(yb_hbm, t), _row_tile(ybuf.at[s, k], r),
                                      sem.at[s]).start(priority=k)
            return carry
        lax.fori_loop(0, tm, body, 0, unroll=8)

    @pl.when(i == 0)
    def _():
        start(0, 0)

    @pl.when(i + 1 < pl.num_programs(0))
    def _():
        start(i + 1, 1 - slot)

    for k in range(TOP_K):
        pltpu.make_async_copy(yb_hbm.at[pl.ds(0, tm * SUBLANES), :], ybuf.at[slot, k], sem.at[slot]).wait()
    route = route_ref[...]
    a0, b0 = _unpack_pair(_load_row_tiles(ybuf.at[slot, 0], tm))
    a1, b1 = _unpack_pair(_load_row_tiles(ybuf.at[slot, 1], tm))
    g0 = route[:, ROUTE_G0:ROUTE_G0 + 1]
    g1 = route[:, ROUTE_G1:ROUTE_G1 + 1]
    moe = jnp.concatenate([a0 * g0 + a1 * g1, b0 * g0 + b1 * g1], axis=1)
    x2 = x1_ref[...] + _rows(gt_ref) * moe
    y_ref[...] = _rms(x2) * gf_ref[...]


def _combine(pos_km, yb, x1, route, gt, g_final, *, tm, n, row0, gt_spec):
    blk0 = row0 // tm
    return pl.pallas_call(
        functools.partial(_combine_kernel, tm=tm),
        out_shape=jax.ShapeDtypeStruct((n, D_MODEL), f32),
        grid_spec=pltpu.PrefetchScalarGridSpec(
            num_scalar_prefetch=1,
            grid=(n // tm,),
            in_specs=[pl.BlockSpec(memory_space=pl.ANY),
                      pl.BlockSpec((tm, D_MODEL), lambda i, pos: (blk0 + i, 0)),
                      pl.BlockSpec((tm, LANES), lambda i, pos: (blk0 + i, 0)),
                      gt_spec,
                      pl.BlockSpec((1, D_MODEL), lambda i, pos: (0, 0))],
            out_specs=pl.BlockSpec((tm, D_MODEL), lambda i, pos: (i, 0)),
            scratch_shapes=[pltpu.VMEM((2, TOP_K, tm * SUBLANES, LANES), u32),
                            pltpu.SemaphoreType.DMA((2,))]),
        compiler_params=pltpu.CompilerParams(dimension_semantics=("arbitrary",),
                                             vmem_limit_bytes=VMEM_LIMIT),
        name="combine",
    )(pos_km, yb, x1, route, gt, g_final.reshape(1, D_MODEL))


def kernel(x_prompt, x_sample, cache_swa_k, cache_swa_v, state_gla, c_prompt, c_sample, g_mix_norm, g_ffn_norm, w_ada, b_ada, w_in, attn_sinks, w_gla_gate, b_gla_gate, g_gla_norm, w_out, w_router_group, b_router_group, w_router_expert, b_router_expert, w_expert_gate, w_expert_up, w_expert_down, g_final):
    depth = w_in.shape[0]
    assert depth == 1
    bp, t, d = x_prompt.shape
    bs, ts, _ = x_sample.shape
    n_p, n_s = bp * t, bs * ts
    n_tot = n_p + n_s
    assert n_tot % LANES == 0
    tm = 512
    to = 256
    tb = 512
    tc = 256
    td = LANES
    gla_c = 128
    gla_sub = 4
    swa_sub = 2

    w_in0 = w_in[0]
    w_q = w_in0[:, :_C_K].astype(bf16)
    w_q_t = w_q.T
    w_rest = w_in0[:, _C_K:_C_AB].astype(bf16)
    w_ab = jnp.pad(w_in0[:, _C_AB:], ((0, 0), (0, LANES - GLA_GATE_RANK))).astype(bf16)
    w_gate = jnp.pad(w_gla_gate[0], ((0, LANES - GLA_GATE_RANK), (0, 0))).astype(bf16)
    w_out_b = w_out[0].astype(bf16)
    n_r = N_GROUPS + N_EXPERTS
    w_r = jnp.pad(jnp.concatenate([w_router_group[0], w_router_expert[0]], axis=1),
                  ((0, 0), (0, LANES - n_r)))
    w_r_hi = w_r.astype(bf16)
    w_r_lo = (w_r - w_r_hi.astype(f32)).astype(bf16)
    w_route = jnp.concatenate([w_r_hi, w_r_lo], axis=1)
    b_route = jnp.pad(jnp.concatenate([b_router_group[0], b_router_expert[0]]),
                      (0, LANES - n_r)).reshape(1, LANES)

    c_all = jnp.concatenate([c_prompt, c_sample], axis=0)
    mod = _adaln(c_all, w_ada[0], b_ada[0])
    mod_tab = mod.reshape((bp + bs) * N_MOD, 1, d)
    mod_s = jnp.repeat(mod[bp:].reshape(bs, N_MOD, d), ts, axis=0)
    mod_s = [mod_s[:, m] for m in range(N_MOD)]
    pmod = lambda comp, tile, lag=0: _mod_spec(comp, t // tile, n_p // tile, lag)
    smod = pl.BlockSpec((n_s, d), lambda i: (0, 0))

    xp = x_prompt.reshape(n_p, d)
    xs = x_sample.reshape(n_s, d)
    proj_w = (w_rest, w_ab, w_gate, b_gla_gate[0])
    qp, kp, vp, gqp, gkp, gvp, rp, lgp = _proj(xp, mod_tab, mod_tab, g_mix_norm[0], w_q_t, *proj_w, tm=tm,
                                               mod_specs=(pmod(0, tm), pmod(1, tm)), q_transposed=True)
    qs, ks, vs, gqs, gks, gvs, rs, lgs = _proj(xs, mod_s[0], mod_s[1], g_mix_norm[0], w_q, *proj_w, tm=n_s,
                                               mod_specs=(smod, smod), q_transposed=False)

    sinks = attn_sinks[0]
    oap = _swa_t(qp, kp, vp, sinks, n_seq=bp, tiles=t // WINDOW, n_sub=swa_sub)
    ck = cache_swa_k[0].reshape(bs * WINDOW, KV_WIDTH)
    cv = cache_swa_v[0].reshape(bs * WINDOW, KV_WIDTH)
    oas = _swa(qs, ck, cv, ks, vs, sinks, n_seq=bs, tq=ts)
    s_zero = jnp.zeros((bp, GLA_HEADS, GLA_DK, GLA_DV), f32)
    obp, sp = _gla(gqp, gkp, gvp, lgp, rp, g_gla_norm[0], s_zero, n_seq=bp, c=gla_c, n_sub=gla_sub,
                   steps=t // (gla_c * gla_sub))
    obs, ss = _gla(gqs, gks, gvs, lgs, rs, g_gla_norm[0], state_gla[0], n_seq=bs, c=ts, n_sub=1, steps=1)

    out_w = (g_ffn_norm[0], w_out_b, w_route, b_route)
    bufs = _outproj(oap, obp, xp, mod_tab, mod_tab, mod_tab, *out_w, tm=to,
                    mod_specs=(pmod(2, to), pmod(3, to, 1), pmod(4, to, 1)), n_total=n_tot, row0=0,
                    fill_steps=-(-n_s // to), oa_transposed=True)
    x1, h2p, route = _outproj(oas, obs, xs, mod_s[2], mod_s[3], mod_s[4], *out_w, tm=n_s,
                              mod_specs=(smod, smod, smod), n_total=n_tot, row0=n_p, bufs=bufs)

    n_blocks = -(-(n_tot * TOP_K + N_EXPERTS * (tb - 1)) // tb)
    pos, pend_tab = _rank(route, tb=tb)
    pend = pend_tab[:, 0]
    block_e = jnp.minimum(jnp.sum(pend[None, :] <= (jnp.arange(n_blocks, dtype=jnp.int32) * tb)[:, None],
                                  axis=1), N_EXPERTS - 1).astype(jnp.int32)
    n_used = pend[N_EXPERTS - 1:] // tb
    experts = jnp.arange(N_EXPERTS, dtype=jnp.int32)
    has_rows = jnp.diff(pend, prepend=0) > 0
    later = jnp.where((experts[None, :] > experts[:, None]) & has_rows[None, :], experts[None, :], N_EXPERTS)
    next_with_rows = jnp.min(later, axis=1)
    next_with_rows = jnp.where(next_with_rows == N_EXPERTS, experts, next_with_rows)
    next_e = jnp.sum(jnp.where(block_e[:, None] == experts[None, :], next_with_rows[None, :], 0),
                     axis=1).astype(jnp.int32)
    xsort = _dispatch(pos.reshape(-1), pend, h2p, tm=td, tb=tb, n_blocks=n_blocks)
    yb = _experts(block_e, next_e, n_used, xsort, w_expert_gate[0], w_expert_up[0], w_expert_down[0], tb=tb)

    pos = pos.reshape(TOP_K, n_tot)
    pos_p = pos[:, :n_p].reshape(-1)
    pos_s = pos[:, n_p:].reshape(-1)
    gt_p = pl.BlockSpec((1, 1, d), lambda i, p: ((i // (t // tc)) * N_MOD + 5, 0, 0))
    gt_s = pl.BlockSpec((n_s, d), lambda i, p: (0, 0))
    y_p = _combine(pos_p, yb, x1, route, mod_tab, g_final, tm=tc, n=n_p, row0=0, gt_spec=gt_p)
    y_s = _combine(pos_s, yb, x1, route, mod_s[5], g_final, tm=n_s, n=n_s, row0=n_p, gt_spec=gt_s)

    kv_shape = (SWA_KV_HEADS, SWA_HEAD_DIM)
    last = lambda a: a.reshape(bp, t, KV_WIDTH)[:, t - WINDOW:, :].reshape(1, bp, WINDOW, *kv_shape)
    k_state_p, v_state_p = last(kp), last(vp)
    return (y_p.reshape(bp, t, d), y_s.reshape(bs, ts, d), k_state_p, v_state_p, sp[None],
            ks.reshape(bs, ts, *kv_shape)[None], vs.reshape(bs, ts, *kv_shape)[None], ss[None])
```

```python
import functools
import math

import jax
import jax.numpy as jnp
from jax import lax
from jax.experimental import pallas as pl
from jax.experimental.pallas import tpu as pltpu

f32 = jnp.float32
bf16 = jnp.bfloat16
u32 = jnp.uint32

D_MODEL = 2048
N_MOD = 6
EPS = 1e-6
NEG_INF = -1e30
LOG2_E = math.log2(math.e)

SWA_HEAD_DIM = 64
SWA_KV_HEADS = 2
SWA_GROUP = 8
SWA_WIDTH = SWA_KV_HEADS * SWA_GROUP * SWA_HEAD_DIM
KV_WIDTH = SWA_KV_HEADS * SWA_HEAD_DIM
WINDOW = 128
CHUNK = 64

GLA_HEADS = 4
GLA_DK = 128
GLA_DV = 256
GLA_QK_WIDTH = GLA_HEADS * GLA_DK
GLA_V_WIDTH = GLA_HEADS * GLA_DV
GLA_GATE_RANK = 16
GLA_GATE_NORM = 16.0

N_GROUPS = 4
EXPERTS_PER_GROUP = 8
N_EXPERTS = N_GROUPS * EXPERTS_PER_GROUP
TOP_K = 2
EXPERT_HIDDEN = D_MODEL // 4

_C_Q = 0
_C_K = _C_Q + SWA_WIDTH
_C_V = _C_K + KV_WIDTH
_C_GQ = _C_V + KV_WIDTH
_C_GK = _C_GQ + GLA_QK_WIDTH
_C_GV = _C_GK + GLA_QK_WIDTH
_C_R = _C_GV + GLA_V_WIDTH
_C_AB = _C_R + GLA_V_WIDTH

LANES = 128
SUBLANES = 8
PACKED_WIDTH = D_MODEL // 2
assert PACKED_WIDTH == SUBLANES * LANES
VMEM_LIMIT = 56 * 1024 * 1024

ROUTE_E0, ROUTE_E1, ROUTE_G0, ROUTE_G1 = 0, 1, 2, 3


def _dot(a, b):
    return jnp.dot(a, b, preferred_element_type=f32)


def _dot_nt(a, b):
    return lax.dot_general(a, b, (((1,), (1,)), ((), ())), preferred_element_type=f32)


def _dot_tn(a, b):
    return lax.dot_general(a, b, (((0,), (0,)), ((), ())), preferred_element_type=f32)


def _silu(x):
    return x / (1.0 + jnp.exp(-x))


def _rows(ref):
    v = ref[...]
    return v.reshape(v.shape[-2:])


def _rms(x):
    return x * lax.rsqrt(jnp.mean(x * x, axis=-1, keepdims=True) + EPS)


def _resident(shape):
    return pl.BlockSpec(shape, lambda *_: (0,) * len(shape), pipeline_mode=pl.Buffered(1))


def _mod_spec(comp, tiles_per_seq, n_tiles, lag=0):
    def index(i):
        tile = jnp.minimum(jnp.maximum(i - lag, 0), n_tiles - 1)
        return ((tile // tiles_per_seq) * N_MOD + comp, 0, 0)
    return pl.BlockSpec((1, 1, D_MODEL), index)


def _adaln_kernel(c_ref, w_ref, b_ref, o_ref):
    a = _silu(c_ref[...]).astype(bf16)
    o_ref[...] = _dot(a, w_ref[...].astype(bf16)) + b_ref[...]


def _adaln(c_all, w_ada, b_ada, tn=1024):
    r = c_all.shape[0]
    n = w_ada.shape[1]
    return pl.pallas_call(
        _adaln_kernel,
        out_shape=jax.ShapeDtypeStruct((r, n), f32),
        grid=(n // tn,),
        in_specs=[pl.BlockSpec((r, D_MODEL), lambda j: (0, 0)),
                  pl.BlockSpec((D_MODEL, tn), lambda j: (0, j)),
                  pl.BlockSpec((1, tn), lambda j: (0, j))],
        out_specs=pl.BlockSpec((r, tn), lambda j: (0, j)),
        compiler_params=pltpu.CompilerParams(dimension_semantics=("arbitrary",),
                                             vmem_limit_bytes=VMEM_LIMIT),
        name="adaln",
    )(c_all, w_ada, b_ada.reshape(1, n))


def _proj_kernel(x_ref, sh_ref, sc_ref, g_ref, wq_ref, w_ref, wab_ref, wg_ref, bg_ref,
                 q_ref, k_ref, v_ref, gq_ref, gk_ref, gv_ref, r_ref, lg_ref, *, q_transposed):
    h = (_rms(x_ref[...]) * g_ref[...]) * (1.0 + _rows(sc_ref)) + _rows(sh_ref)
    hb = h.astype(bf16)
    col = lambda c: c - _C_K
    ab = _dot(hb, wab_ref[...])
    r_ref[...] = _silu(_dot(hb, w_ref[:, col(_C_R):col(_C_AB)])).astype(bf16)
    gq_ref[...] = (_dot(hb, w_ref[:, col(_C_GQ):col(_C_GK)]) * (GLA_DK ** -0.5)).astype(bf16)
    z = _dot(ab.astype(bf16), wg_ref[...]) + bg_ref[...]
    log_sig = jnp.minimum(z, 0.0) - jnp.log1p(jnp.exp(-jnp.abs(z)))
    lg_ref[...] = log_sig * (LOG2_E / GLA_GATE_NORM)
    gk_ref[...] = _dot(hb, w_ref[:, col(_C_GK):col(_C_GV)]).astype(bf16)
    gv_ref[...] = _dot(hb, w_ref[:, col(_C_GV):col(_C_R)]).astype(bf16)
    kv = _dot(hb, w_ref[:, col(_C_K):col(_C_GQ)])
    k_ref[...] = kv[:, :KV_WIDTH]
    v_ref[...] = kv[:, KV_WIDTH:]
    if q_transposed:
        q_ref[...] = (_dot_nt(wq_ref[...], hb) * (LOG2_E * SWA_HEAD_DIM ** -0.5)).astype(bf16)
    else:
        q_ref[...] = _dot(hb, wq_ref[...]).astype(bf16)


def _proj(x2d, sh, sc, g_mix, w_q, w_rest, w_ab, w_gate, b_gate, *, tm, mod_specs, q_transposed):
    n = x2d.shape[0]
    row = lambda w: pl.BlockSpec((tm, w), lambda i: (i, 0))
    outs = [(KV_WIDTH, f32), (KV_WIDTH, f32), (GLA_QK_WIDTH, bf16),
            (GLA_QK_WIDTH, bf16), (GLA_V_WIDTH, bf16), (GLA_V_WIDTH, bf16), (GLA_QK_WIDTH, f32)]
    if q_transposed:
        q_shape, q_spec = (SWA_WIDTH, n), pl.BlockSpec((SWA_WIDTH, tm), lambda i: (0, i))
    else:
        q_shape, q_spec = (n, SWA_WIDTH), row(SWA_WIDTH)
    return pl.pallas_call(
        functools.partial(_proj_kernel, q_transposed=q_transposed),
        out_shape=[jax.ShapeDtypeStruct(q_shape, bf16)]
                  + [jax.ShapeDtypeStruct((n, w), dt) for w, dt in outs],
        grid=(n // tm,),
        in_specs=[row(D_MODEL), mod_specs[0], mod_specs[1],
                  _resident((1, D_MODEL)), _resident(w_q.shape), _resident(w_rest.shape),
                  _resident(w_ab.shape), _resident(w_gate.shape), _resident((1, GLA_QK_WIDTH))],
        out_specs=[q_spec] + [row(w) for w, _ in outs],
        compiler_params=pltpu.CompilerParams(dimension_semantics=("parallel",),
                                             vmem_limit_bytes=VMEM_LIMIT),
        name="proj",
    )(x2d, sh, sc, g_mix.reshape(1, D_MODEL), w_q, w_rest, w_ab, w_gate, b_gate.reshape(1, -1))


def _swa_kernel(q_ref, kp_ref, vp_ref, kc_ref, vc_ref, sink_ref, o_ref, *, tq):
    pad = jnp.zeros((WINDOW - tq, KV_WIDTH), f32)
    k_all = jnp.concatenate([kp_ref[...], kc_ref[...], pad], axis=0).astype(bf16)
    v_all = jnp.concatenate([vp_ref[...], vc_ref[...], pad], axis=0).astype(bf16)
    rows, cols = SWA_GROUP * tq, 2 * WINDOW
    valid = lax.broadcasted_iota(jnp.int32, (rows, cols), 1) < WINDOW + tq
    q = q_ref[...]
    outs = []
    for j in range(SWA_KV_HEADS):
        heads = [q[:, (j * SWA_GROUP + g) * SWA_HEAD_DIM:(j * SWA_GROUP + g + 1) * SWA_HEAD_DIM]
                 for g in range(SWA_GROUP)]
        qs = jnp.concatenate(heads, axis=0)
        kj = k_all[:, j * SWA_HEAD_DIM:(j + 1) * SWA_HEAD_DIM]
        vj = v_all[:, j * SWA_HEAD_DIM:(j + 1) * SWA_HEAD_DIM]
        s = _dot_nt(qs, kj) * (SWA_HEAD_DIM ** -0.5)
        s = jnp.where(valid, s, NEG_INF)
        sink = sink_ref[j]
        m = jnp.maximum(jnp.max(s, axis=1, keepdims=True), sink)
        p = jnp.exp(s - m)
        den = jnp.sum(p, axis=1, keepdims=True) + jnp.exp(sink - m)
        o = _dot(p.astype(bf16), vj) / den
        outs.append(jnp.concatenate([o[g * tq:(g + 1) * tq] for g in range(SWA_GROUP)], axis=1))
    o_ref[...] = jnp.concatenate(outs, axis=1).astype(bf16)


def _swa(q, k_past, v_past, k_new, v_new, sinks, *, n_seq, tq):
    sink_rows = jnp.repeat(sinks.astype(f32).reshape(SWA_KV_HEADS, SWA_GROUP), tq, axis=1)
    sink_rows = sink_rows.reshape(SWA_KV_HEADS, SWA_GROUP * tq, 1)
    new = lambda w: pl.BlockSpec((tq, w), lambda b: (b, 0))
    past = pl.BlockSpec((WINDOW, KV_WIDTH), lambda b: (b, 0))
    return pl.pallas_call(
        functools.partial(_swa_kernel, tq=tq),
        out_shape=jax.ShapeDtypeStruct(q.shape, bf16),
        grid=(n_seq,),
        in_specs=[new(SWA_WIDTH), past, past, new(KV_WIDTH), new(KV_WIDTH),
                  pl.BlockSpec(sink_rows.shape, lambda b: (0, 0, 0))],
        out_specs=new(SWA_WIDTH),
        compiler_params=pltpu.CompilerParams(dimension_semantics=("parallel",),
                                             vmem_limit_bytes=VMEM_LIMIT),
        name="swa",
    )(q, k_past, v_past, k_new, v_new, sink_rows)


def _swa_t_kernel(q_ref, kp_ref, vp_ref, kc_ref, vc_ref, sink_ref, mk_ref, mq0_ref, mq_ref, o_ref, *, n_sub):
    tq = WINDOW
    hd = SWA_HEAD_DIM
    k_all = jnp.concatenate([kp_ref[...], kc_ref[...]], axis=0).astype(bf16)
    v_all = jnp.concatenate([vp_ref[...], vc_ref[...]], axis=0).astype(bf16)
    low_lanes = lax.broadcasted_iota(jnp.int32, (2 * WINDOW, KV_WIDTH), 1) < hd
    ones = jnp.ones((2 * WINDOW, KV_WIDTH), bf16)
    for sub in range(n_sub):
        keys = slice(sub * tq, sub * tq + 2 * WINDOW)
        toks = slice(sub * tq, (sub + 1) * tq)
        mq = (mq0_ref if sub == 0 else mq_ref)[0]
        for j in range(SWA_KV_HEADS):
            head = lambda g: slice((j * SWA_GROUP + g) * hd, (j * SWA_GROUP + g + 1) * hd)
            qs = jnp.concatenate([q_ref[head(g), toks] for g in range(SWA_GROUP)], axis=1)
            own = low_lanes if j == 0 else jnp.logical_not(low_lanes)
            k_aug = jnp.where(own, k_all[keys], mk_ref[j])
            v_aug = jnp.where(own, v_all[keys], ones)
            q_aug = jnp.concatenate([qs, mq] if j == 0 else [mq, qs], axis=0)
            s = _dot(k_aug, q_aug)
            sink = sink_ref[j]
            m = jnp.maximum(jnp.max(s, axis=0, keepdims=True), sink)
            p = jnp.exp2(s - m).astype(bf16)
            o_aug = _dot_tn(v_aug, p)
            pv, p_sum = (o_aug[:hd], o_aug[hd:hd + 1]) if j == 0 else (o_aug[hd:], o_aug[0:1])
            o = pv / (p_sum + jnp.exp2(sink - m))
            for g in range(SWA_GROUP):
                o_ref[head(g), toks] = o[:, g * tq:(g + 1) * tq].astype(bf16)


def _swa_t(q_t, k, v, sinks, *, n_seq, tiles, n_sub):
    tq = WINDOW
    steps = tiles // n_sub
    hd = SWA_HEAD_DIM
    sink_cols = jnp.repeat(sinks.astype(f32).reshape(SWA_KV_HEADS, SWA_GROUP) * LOG2_E, tq, axis=1)
    sink_cols = sink_cols.reshape(SWA_KV_HEADS, 1, SWA_GROUP * tq)
    n_kc = 2 * WINDOW // CHUNK
    key_chunk = jnp.arange(2 * WINDOW) // CHUNK
    lane = jnp.arange(KV_WIDTH)
    mk = jnp.stack([lane[None, :] == hd + key_chunk[:, None], lane[None, :] == key_chunk[:, None]]).astype(bf16)
    q_chunk = (jnp.arange(SWA_GROUP * tq) % tq) // CHUNK
    kc = jnp.arange(hd)[:, None]
    band = (kc >= q_chunk[None, :]) & (kc <= q_chunk[None, :] + WINDOW // CHUNK)
    visible = jnp.stack([band & (kc >= WINDOW // CHUNK), band]) | (kc >= n_kc)
    mq = jnp.where(visible, 0.0, NEG_INF).astype(bf16)
    mq0_spec = pl.BlockSpec((1, hd, SWA_GROUP * tq), lambda b, u: (jnp.minimum(u, 1), 0, 0))
    mq_spec = pl.BlockSpec((1, hd, SWA_GROUP * tq), lambda b, u: (1, 0, 0))
    qspec = pl.BlockSpec((SWA_WIDTH, n_sub * tq), lambda b, u: (0, b * steps + u))
    cur = pl.BlockSpec((n_sub * tq, KV_WIDTH), lambda b, u: (b * steps + u, 0))
    prev = pl.BlockSpec((WINDOW, KV_WIDTH), lambda b, u: (b * tiles + jnp.maximum(n_sub * u - 1, 0), 0))
    return pl.pallas_call(
        functools.partial(_swa_t_kernel, n_sub=n_sub),
        out_shape=jax.ShapeDtypeStruct(q_t.shape, bf16),
        grid=(n_seq, steps),
        in_specs=[qspec, prev, prev, cur, cur, pl.BlockSpec(sink_cols.shape, lambda b, u: (0, 0, 0)),
                  pl.BlockSpec(mk.shape, lambda b, u: (0, 0, 0)), mq0_spec, mq_spec],
        out_specs=qspec,
        compiler_params=pltpu.CompilerParams(dimension_semantics=("parallel", "arbitrary"),
                                             vmem_limit_bytes=VMEM_LIMIT),
        name="swa_t",
    )(q_t, k, v, k, v, sink_cols, mk, mq, mq)


def _gla_kernel(q_ref, k_ref, v_ref, lg_ref, r_ref, gh_ref, s0_ref, o_ref, s_ref, *, c, n_sub):
    @pl.when(pl.program_id(1) == 0)
    def _():
        s_ref[...] = s0_ref[...]

    row = lax.broadcasted_iota(jnp.int32, (c, GLA_DK), 0)
    causal = (lax.broadcasted_iota(jnp.int32, (c, c), 0) >= lax.broadcasted_iota(jnp.int32, (c, c), 1))
    for sub in range(n_sub):
        ts = slice(sub * c, (sub + 1) * c)
        for h in range(GLA_HEADS):
            ks = slice(h * GLA_DK, (h + 1) * GLA_DK)
            vs = slice(h * GLA_DV, (h + 1) * GLA_DV)
            b = lg_ref[ts, ks]
            step = 1
            while step < c:
                b = b + jnp.where(row >= step, pltpu.roll(b, step, 0), 0.0)
                step *= 2
            b_last = b[c - 1:c, :]
            q = q_ref[ts, ks].astype(f32)
            k = k_ref[ts, ks].astype(f32)
            v = v_ref[ts, vs]
            qd = (q * jnp.exp2(b)).astype(bf16)
            kd = (k * jnp.exp2(-b)).astype(bf16)
            kl = (k * jnp.exp2(b_last - b)).astype(bf16)
            a = jnp.where(causal, _dot_nt(qd, kd), 0.0)
            s = s_ref[0, h]
            o = _dot(qd, s.astype(bf16)) + _dot(a.astype(bf16), v)
            decay = jnp.broadcast_to(jnp.exp2(b_last), (GLA_DK, GLA_DK)).T
            s_ref[0, h] = s * jnp.concatenate([decay, decay], axis=1) + _dot_tn(kl, v)
            on = _rms(o) * gh_ref[...]
            o_ref[ts, vs] = (on * r_ref[ts, vs].astype(f32)).astype(bf16)


def _gla(gq, gk, gv, lg, r, g_head, s0, *, n_seq, c, n_sub, steps):
    rows = c * n_sub
    blk = lambda w: pl.BlockSpec((rows, w), lambda b, t: (b * steps + t, 0))
    state = pl.BlockSpec((1, GLA_HEADS, GLA_DK, GLA_DV), lambda b, t: (b, 0, 0, 0))
    return pl.pallas_call(
        functools.partial(_gla_kernel, c=c, n_sub=n_sub),
        out_shape=[jax.ShapeDtypeStruct(gv.shape, bf16),
                   jax.ShapeDtypeStruct((n_seq, GLA_HEADS, GLA_DK, GLA_DV), f32)],
        grid=(n_seq, steps),
        in_specs=[blk(GLA_QK_WIDTH), blk(GLA_QK_WIDTH), blk(GLA_V_WIDTH), blk(GLA_QK_WIDTH),
                  blk(GLA_V_WIDTH), pl.BlockSpec((1, GLA_DV), lambda b, t: (0, 0)), state],
        out_specs=[blk(GLA_V_WIDTH), state],
        compiler_params=pltpu.CompilerParams(dimension_semantics=("parallel", "arbitrary"),
                                             vmem_limit_bytes=VMEM_LIMIT),
        name="gla",
    )(gq, gk, gv, lg, r, g_head.reshape(1, GLA_DV), s0)


def _pack_pair(hb):
    w = hb.shape[1] // 2
    a = lax.bitcast_convert_type(hb[:, :w].astype(f32), u32)
    b = lax.bitcast_convert_type(hb[:, w:].astype(f32), u32)
    return a | (b >> 16)


def _unpack_pair(p):
    a = lax.bitcast_convert_type(p & jnp.uint32(0xFFFF0000), f32)
    b = lax.bitcast_convert_type(p << 16, f32)
    return a, b


def _load_row_tiles(ref, n):
    return jnp.concatenate([ref[pl.ds(s, n, stride=SUBLANES), :] for s in range(SUBLANES)], axis=1)


def _store_row_tiles(ref, val):
    n = val.shape[0]
    for s in range(SUBLANES):
        ref[pl.ds(s, n, stride=SUBLANES), :] = val[:, s * LANES:(s + 1) * LANES]


def _row_tile(ref, r):
    return ref.at[pl.ds(pl.multiple_of(r * SUBLANES, SUBLANES), SUBLANES), :]


def _outproj_kernel(*refs, n_alias, oa_transposed):
    (oa_ref, ob_ref, x_ref, gt_ref, sh_ref, sc_ref, g_ref, w_ref, wr_ref, br_ref) = refs[:10]
    x1_ref, h2_ref, route_ref, x1_s = refs[10 + n_alias:]

    @pl.when(pl.program_id(0) == 0)
    def _():
        x1_s[...] = jnp.zeros(x1_s.shape, x1_s.dtype)

    h2 = (_rms(x1_s[...]) * g_ref[...]) * (1.0 + _rows(sc_ref)) + _rows(sh_ref)
    hi = h2.astype(bf16)
    _store_row_tiles(h2_ref, _pack_pair(hi))
    lo = (h2 - hi.astype(f32)).astype(bf16)
    r1 = _dot(hi, wr_ref[...])
    logits = r1[:, :LANES] + r1[:, LANES:] + _dot(lo, wr_ref[:, :LANES]) + br_ref[...]
    lane = lax.broadcasted_iota(jnp.int32, logits.shape, 1)
    lane_f = lane.astype(f32)
    neg = float("-inf")
    first = lambda hit: jnp.min(jnp.where(hit, lane_f, float(LANES)), axis=1, keepdims=True)
    lg_g = jnp.where(lane < N_GROUPS, logits, neg)
    g_max = jnp.max(lg_g, axis=1, keepdims=True)
    g_sel = first(lg_g == g_max)
    p_sel = 1.0 / jnp.sum(jnp.exp(lg_g - g_max), axis=1, keepdims=True)
    e_lane = lane - N_GROUPS
    in_group = (e_lane >= 0) & (e_lane < N_EXPERTS) & ((e_lane // EXPERTS_PER_GROUP).astype(f32) == g_sel)
    lg_e = jnp.where(in_group, logits, neg)
    v1 = jnp.max(lg_e, axis=1, keepdims=True)
    i1 = first(lg_e == v1)
    lg_e2 = jnp.where(lane_f == i1, neg, lg_e)
    v2 = jnp.max(lg_e2, axis=1, keepdims=True)
    i2 = first(lg_e2 == v2)
    e = jnp.exp(v2 - v1)
    g1 = p_sel / (1.0 + e)
    g2 = p_sel * e / (1.0 + e)
    rec = jnp.where(lane == ROUTE_E0, i1 - N_GROUPS, 0.0)
    rec = jnp.where(lane == ROUTE_E1, i2 - N_GROUPS, rec)
    rec = jnp.where(lane == ROUTE_G0, g1, rec)
    rec = jnp.where(lane == ROUTE_G1, g2, rec)
    route_ref[...] = rec

    oa_dot = _dot_tn if oa_transposed else _dot
    mix = oa_dot(oa_ref[...], w_ref[:SWA_WIDTH, :]) + _dot(ob_ref[...], w_ref[SWA_WIDTH:, :])
    x1 = x_ref[...] + _rows(gt_ref) * mix
    x1_ref[...] = x1
    x1_s[...] = x1


def _outproj(oa, ob, x2d, gt, sh, sc, g_ffn, w_out, w_route, b_route, *, tm, mod_specs,
             n_total, row0, bufs=None, fill_steps=0, oa_transposed=False):
    n = x2d.shape[0]
    tiles = n // tm
    blocks = tiles + fill_steps
    row = lambda w: pl.BlockSpec((tm, w), lambda i: (jnp.minimum(i, tiles - 1), 0))
    oa_spec = row(SWA_WIDTH)
    if oa_transposed:
        oa_spec = pl.BlockSpec((SWA_WIDTH, tm), lambda i: (0, jnp.minimum(i, tiles - 1)))
    head_blk = lambda i: row0 // tm + jnp.minimum(i, blocks - 1)
    tail_blk = lambda i: row0 // tm + jnp.maximum(i - 1, 0)
    out_x1 = pl.BlockSpec((tm, D_MODEL), lambda i: (head_blk(i), 0))
    out_tiles = pl.BlockSpec((tm * SUBLANES, LANES), lambda i: (tail_blk(i), 0))
    out_route = pl.BlockSpec((tm, LANES), lambda i: (tail_blk(i), 0))
    alias_in = list(bufs) if bufs is not None else []
    n_in = 10
    return pl.pallas_call(
        functools.partial(_outproj_kernel, n_alias=len(alias_in), oa_transposed=oa_transposed),
        out_shape=[jax.ShapeDtypeStruct((n_total, D_MODEL), f32),
                   jax.ShapeDtypeStruct((n_total * SUBLANES, LANES), u32),
                   jax.ShapeDtypeStruct((n_total, LANES), f32)],
        grid=(blocks + 1,),
        in_specs=[oa_spec, row(GLA_V_WIDTH), row(D_MODEL), mod_specs[0], mod_specs[1],
                  mod_specs[2], _resident((1, D_MODEL)), _resident(w_out.shape),
                  _resident(w_route.shape), _resident((1, LANES))]
                 + [pl.BlockSpec(memory_space=pl.ANY)] * len(alias_in),
        out_specs=[out_x1, out_tiles, out_route],
        scratch_shapes=[pltpu.VMEM((tm, D_MODEL), f32)],
        input_output_aliases={n_in + a: a for a in range(len(alias_in))},
        compiler_params=pltpu.CompilerParams(dimension_semantics=("arbitrary",),
                                             vmem_limit_bytes=VMEM_LIMIT),
        name="outproj",
    )(oa, ob, x2d, gt, sh, sc, g_ffn.reshape(1, D_MODEL), w_out, w_route, b_route, *alias_in)


def _rank_kernel(route_ref, pos_ref, pend_ref, rec_s, *, n_chunks, tb):
    n_e = N_EXPERTS
    expert = lax.broadcasted_iota(jnp.int32, (n_e, LANES), 0)
    expert_f = expert.astype(f32)
    earlier = (lax.broadcasted_iota(jnp.int32, (LANES, LANES), 0)
               < lax.broadcasted_iota(jnp.int32, (LANES, LANES), 1)).astype(bf16)

    def onehots(c):
        rec = rec_s[c]
        return expert_f == rec[ROUTE_E0:ROUTE_E0 + 1, :], expert_f == rec[ROUTE_E1:ROUTE_E1 + 1, :]

    def count(c, cnt):
        rec_s[c] = route_ref[pl.ds(pl.multiple_of(c * LANES, LANES), LANES), :].T[:SUBLANES, :]
        h0, h1 = onehots(c)
        return cnt + jnp.sum((h0 | h1).astype(f32), axis=1, keepdims=True)

    cnt = lax.fori_loop(0, n_chunks, count, jnp.zeros((n_e, 1), f32), unroll=4)
    padded = jnp.floor((cnt + (tb - 1.0)) / tb) * tb
    end = jnp.broadcast_to(padded, (n_e, LANES))
    step = 1
    while step < n_e:
        end = end + jnp.where(expert >= step, pltpu.roll(end, step, 0), 0.0)
        step *= 2
    pend_ref[...] = end.astype(jnp.int32)

    def rank(c, base):
        h0, h1 = onehots(c)
        both = h0 | h1
        off = _dot(both.astype(bf16), earlier) + base
        pos_ref[0, pl.ds(c, 1), :] = jnp.sum(jnp.where(h0, off, 0.0), axis=0, keepdims=True).astype(jnp.int32)
        pos_ref[1, pl.ds(c, 1), :] = jnp.sum(jnp.where(h1, off, 0.0), axis=0, keepdims=True).astype(jnp.int32)
        return base + jnp.sum(both.astype(f32), axis=1, keepdims=True)

    lax.fori_loop(0, n_chunks, rank, (end - padded)[:, 0:1], unroll=4)


def _rank(route, *, tb):
    n = route.shape[0]
    n_chunks = n // LANES
    return pl.pallas_call(
        functools.partial(_rank_kernel, n_chunks=n_chunks, tb=tb),
        out_shape=[jax.ShapeDtypeStruct((TOP_K, n_chunks, LANES), jnp.int32),
                   jax.ShapeDtypeStruct((N_EXPERTS, LANES), jnp.int32)],
        in_specs=[_resident(route.shape)],
        scratch_shapes=[pltpu.VMEM((n_chunks, SUBLANES, LANES), f32)],
        compiler_params=pltpu.CompilerParams(vmem_limit_bytes=VMEM_LIMIT),
        name="rank",
    )(route)


def _dispatch_kernel(pos_ref, pend_ref, h2_ref, xs_hbm, stage, zbuf, sem, zsem, *, tm, tb, n_tot, steps):
    i = pl.program_id(0)
    slot = i % 2

    blk = tb * SUBLANES

    def wait_slot(s):
        for _ in range(TOP_K):
            pltpu.make_async_copy(stage.at[s], xs_hbm.at[pl.ds(0, tm * SUBLANES), :], sem.at[s]).wait()

    @pl.when(i == 0)
    def _():
        zbuf[...] = jnp.zeros(zbuf.shape, zbuf.dtype)

        def fill_block(b, start):
            cp = pltpu.make_async_copy(zbuf, xs_hbm.at[pl.ds(pl.multiple_of(b * blk, blk), blk), :], zsem)
            cp.start() if start else cp.wait()

        def fill(e, start):
            end = pend_ref[e]
            prev = jnp.where(e > 0, pend_ref[jnp.maximum(e - 1, 0)], 0)

            @pl.when(end > prev)
            def _():
                fill_block(end // tb - 1, start)

        fill_unused = fill_block

        first_unused = pend_ref[N_EXPERTS - 1] // tb
        n_blocks = xs_hbm.shape[0] // blk
        lax.fori_loop(0, N_EXPERTS, lambda e, c: (fill(e, True), c)[1], 0)
        lax.fori_loop(first_unused, n_blocks, lambda b, c: (fill_unused(b, True), c)[1], 0)
        lax.fori_loop(0, N_EXPERTS, lambda e, c: (fill(e, False), c)[1], 0)
        lax.fori_loop(first_unused, n_blocks, lambda b, c: (fill_unused(b, False), c)[1], 0)

    @pl.when(i >= 2)
    def _():
        wait_slot(slot)

    stage[slot] = h2_ref[...]

    def scatter(r, carry):
        for k in range(TOP_K):
            d = pos_ref[k * n_tot + i * tm + r]
            pltpu.make_async_copy(_row_tile(stage.at[slot], r), _row_tile(xs_hbm, d),
                                  sem.at[slot]).start(priority=k)
        return carry

    lax.fori_loop(0, tm, scatter, 0, unroll=8)

    @pl.when(i == steps - 1)
    def _():
        wait_slot(slot)
        if steps > 1:
            wait_slot(1 - slot)


def _dispatch(pos_flat, pend, h2p, *, tm, tb, n_blocks):
    n_tot = h2p.shape[0] // SUBLANES
    steps = n_tot // tm
    return pl.pallas_call(
        functools.partial(_dispatch_kernel, tm=tm, tb=tb, n_tot=n_tot, steps=steps),
        out_shape=jax.ShapeDtypeStruct((n_blocks * tb * SUBLANES, LANES), u32),
        grid_spec=pltpu.PrefetchScalarGridSpec(
            num_scalar_prefetch=2,
            grid=(steps,),
            in_specs=[pl.BlockSpec((tm * SUBLANES, LANES), lambda i, pos, pend: (i, 0))],
            out_specs=pl.BlockSpec(memory_space=pl.ANY),
            scratch_shapes=[pltpu.VMEM((2, tm * SUBLANES, LANES), u32),
                            pltpu.VMEM((tb * SUBLANES, LANES), u32),
                            pltpu.SemaphoreType.DMA((2,)), pltpu.SemaphoreType.DMA(())]),
        compiler_params=pltpu.CompilerParams(dimension_semantics=("arbitrary",),
                                             vmem_limit_bytes=VMEM_LIMIT),
        name="dispatch",
    )(pos_flat, pend, h2p)


def _expert_kernel(be_ref, ne_ref, nu_ref, xs_ref, wg_hbm, wu_hbm, wd_hbm, y_ref,
                   wg_s, wu_s, wd_s, wg_b, wu_b, wd_b, sem, *, tb):
    i = pl.program_id(0)
    used = i < nu_ref[0]
    e = be_ref[i]

    def fetch(expert):
        pairs = ((wg_hbm, wg_s), (wu_hbm, wu_s), (wd_hbm, wd_s))
        return [pltpu.make_async_copy(src.at[expert], dst, sem.at[n]) for n, (src, dst) in enumerate(pairs)]

    @pl.when(i == 0)
    def _():
        for cp in fetch(e):
            cp.start()

    @pl.when(used & ((i == 0) | (e != be_ref[jnp.maximum(i - 1, 0)])))
    def _():
        for cp in fetch(e):
            cp.wait()
        wg_b[...] = wg_s[...].astype(bf16)
        wu_b[...] = wu_s[...].astype(bf16)
        wd_b[...] = wd_s[...].astype(bf16)
        nxt = ne_ref[i]

        @pl.when(nxt != e)
        def _():
            for cp in fetch(nxt):
                cp.start()

    @pl.when(used)
    def _():
        a, b = _unpack_pair(_load_row_tiles(xs_ref, tb))
        x = jnp.concatenate([a.astype(bf16), b.astype(bf16)], axis=1)
        g = _dot(x, wg_b[...])
        u = _dot(x, wu_b[...])
        y = _dot((_silu(g) * u).astype(bf16), wd_b[...])
        _store_row_tiles(y_ref, _pack_pair(y.astype(bf16)))

    @pl.when(jnp.logical_not(used))
    def _():
        y_ref[...] = jnp.zeros(y_ref.shape, y_ref.dtype)


def _experts(block_e, next_e, n_used, xs, w_eg, w_eu, w_ed, *, tb):
    n_blocks = block_e.shape[0]
    blk = (tb * SUBLANES, LANES)
    up, down = (D_MODEL, EXPERT_HIDDEN), (EXPERT_HIDDEN, D_MODEL)
    hbm = pl.BlockSpec(memory_space=pl.ANY)
    return pl.pallas_call(
        functools.partial(_expert_kernel, tb=tb),
        out_shape=jax.ShapeDtypeStruct(xs.shape, u32),
        grid_spec=pltpu.PrefetchScalarGridSpec(
            num_scalar_prefetch=3,
            grid=(n_blocks,),
            in_specs=[pl.BlockSpec(blk, lambda i, be, ne, nu: (jnp.minimum(i, nu[0] - 1), 0)),
                      hbm, hbm, hbm],
            out_specs=pl.BlockSpec(blk, lambda i, be, ne, nu: (i, 0)),
            scratch_shapes=[pltpu.VMEM(up, f32), pltpu.VMEM(up, f32), pltpu.VMEM(down, f32),
                            pltpu.VMEM(up, bf16), pltpu.VMEM(up, bf16), pltpu.VMEM(down, bf16),
                            pltpu.SemaphoreType.DMA((3,))]),
        compiler_params=pltpu.CompilerParams(dimension_semantics=("arbitrary",),
                                             vmem_limit_bytes=VMEM_LIMIT),
        name="experts",
    )(block_e, next_e, n_used, xs, w_eg, w_eu, w_ed)


def _combine_kernel(pos_ref, yb_hbm, x1_ref, route_ref, gt_ref, gf_ref, y_ref, ybuf, sem, *, tm):
    i = pl.program_id(0)
    slot = i % 2

    def start(blk, s):
        def body(r, carry):
            for k in range(TOP_K):
                t = pos_ref[(k * pl.num_programs(0) + blk) * tm + r]
                pltpu.make_async_copy(_row_tile(yb_hbm, t), _row_tile(ybuf.at[s, k], r),
                                      sem.at[s]).start(priority=k)
            return carry
        lax.fori_loop(0, tm, body, 0, unroll=8)

    @pl.when(i == 0)
    def _():
        start(0, 0)

    @pl.when(i + 1 < pl.num_programs(0))
    def _():
        start(i + 1, 1 - slot)

    for k in range(TOP_K):
        pltpu.make_async_copy(yb_hbm.at[pl.ds(0, tm * SUBLANES), :], ybuf.at[slot, k], sem.at[slot]).wait()
    route = route_ref[...]
    a0, b0 = _unpack_pair(_load_row_tiles(ybuf.at[slot, 0], tm))
    a1, b1 = _unpack_pair(_load_row_tiles(ybuf.at[slot, 1], tm))
    g0 = route[:, ROUTE_G0:ROUTE_G0 + 1]
    g1 = route[:, ROUTE_G1:ROUTE_G1 + 1]
    moe = jnp.concatenate([a0 * g0 + a1 * g1, b0 * g0 + b1 * g1], axis=1)
    x2 = x1_ref[...] + _rows(gt_ref) * moe
    y_ref[...] = _rms(x2) * gf_ref[...]


def _combine(pos_km, yb, x1, route, gt, g_final, *, tm, n, row0, gt_spec):
    blk0 = row0 // tm
    return pl.pallas_call(
        functools.partial(_combine_kernel, tm=tm),
        out_shape=jax.ShapeDtypeStruct((n, D_MODEL), f32),
        grid_spec=pltpu.PrefetchScalarGridSpec(
            num_scalar_prefetch=1,
            grid=(n // tm,),
            in_specs=[pl.BlockSpec(memory_space=pl.ANY),
                      pl.BlockSpec((tm, D_MODEL), lambda i, pos: (blk0 + i, 0)),
                      pl.BlockSpec((tm, LANES), lambda i, pos: (blk0 + i, 0)),
                      gt_spec,
                      pl.BlockSpec((1, D_MODEL), lambda i, pos: (0, 0))],
            out_specs=pl.BlockSpec((tm, D_MODEL), lambda i, pos: (i, 0)),
            scratch_shapes=[pltpu.VMEM((2, TOP_K, tm * SUBLANES, LANES), u32),
                            pltpu.SemaphoreType.DMA((2,))]),
        compiler_params=pltpu.CompilerParams(dimension_semantics=("arbitrary",),
                                             vmem_limit_bytes=VMEM_LIMIT),
        name="combine",
    )(pos_km, yb, x1, route, gt, g_final.reshape(1, D_MODEL))


def kernel(x_prompt, x_sample, cache_swa_k, cache_swa_v, state_gla, c_prompt, c_sample, g_mix_norm, g_ffn_norm, w_ada, b_ada, w_in, attn_sinks, w_gla_gate, b_gla_gate, g_gla_norm, w_out, w_router_group, b_router_group, w_router_expert, b_router_expert, w_expert_gate, w_expert_up, w_expert_down, g_final):
    depth = w_in.shape[0]
    assert depth == 1
    bp, t, d = x_prompt.shape
    bs, ts, _ = x_sample.shape
    n_p, n_s = bp * t, bs * ts
    n_tot = n_p + n_s
    assert n_tot % LANES == 0
    tm = 512
    to = 256
    tb = 512
    tc = 256
    td = LANES
    gla_c = 128
    gla_sub = 8
    swa_sub = 4

    w_in0 = w_in[0]
    w_q = w_in0[:, :_C_K].astype(bf16)
    w_q_t = w_q.T
    w_rest = w_in0[:, _C_K:_C_AB].astype(bf16)
    w_ab = jnp.pad(w_in0[:, _C_AB:], ((0, 0), (0, LANES - GLA_GATE_RANK))).astype(bf16)
    w_gate = jnp.pad(w_gla_gate[0], ((0, LANES - GLA_GATE_RANK), (0, 0))).astype(bf16)
    w_out_b = w_out[0].astype(bf16)
    n_r = N_GROUPS + N_EXPERTS
    w_r = jnp.pad(jnp.concatenate([w_router_group[0], w_router_expert[0]], axis=1),
                  ((0, 0), (0, LANES - n_r)))
    w_r_hi = w_r.astype(bf16)
    w_r_lo = (w_r - w_r_hi.astype(f32)).astype(bf16)
    w_route = jnp.concatenate([w_r_hi, w_r_lo], axis=1)
    b_route = jnp.pad(jnp.concatenate([b_router_group[0], b_router_expert[0]]),
                      (0, LANES - n_r)).reshape(1, LANES)

    c_all = jnp.concatenate([c_prompt, c_sample], axis=0)
    mod = _adaln(c_all, w_ada[0], b_ada[0])
    mod_tab = mod.reshape((bp + bs) * N_MOD, 1, d)
    mod_s = jnp.repeat(mod[bp:].reshape(bs, N_MOD, d), ts, axis=0)
    mod_s = [mod_s[:, m] for m in range(N_MOD)]
    pmod = lambda comp, tile, lag=0: _mod_spec(comp, t // tile, n_p // tile, lag)
    smod = pl.BlockSpec((n_s, d), lambda i: (0, 0))

    xp = x_prompt.reshape(n_p, d)
    xs = x_sample.reshape(n_s, d)
    proj_w = (w_rest, w_ab, w_gate, b_gla_gate[0])
    qp, kp, vp, gqp, gkp, gvp, rp, lgp = _proj(xp, mod_tab, mod_tab, g_mix_norm[0], w_q_t, *proj_w, tm=tm,
                                               mod_specs=(pmod(0, tm), pmod(1, tm)), q_transposed=True)
    qs, ks, vs, gqs, gks, gvs, rs, lgs = _proj(xs, mod_s[0], mod_s[1], g_mix_norm[0], w_q, *proj_w, tm=n_s,
                                               mod_specs=(smod, smod), q_transposed=False)

    sinks = attn_sinks[0]
    oap = _swa_t(qp, kp, vp, sinks, n_seq=bp, tiles=t // WINDOW, n_sub=swa_sub)
    ck = cache_swa_k[0].reshape(bs * WINDOW, KV_WIDTH)
    cv = cache_swa_v[0].reshape(bs * WINDOW, KV_WIDTH)
    oas = _swa(qs, ck, cv, ks, vs, sinks, n_seq=bs, tq=ts)
    s_zero = jnp.zeros((bp, GLA_HEADS, GLA_DK, GLA_DV), f32)
    obp, sp = _gla(gqp, gkp, gvp, lgp, rp, g_gla_norm[0], s_zero, n_seq=bp, c=gla_c, n_sub=gla_sub,
                   steps=t // (gla_c * gla_sub))
    obs, ss = _gla(gqs, gks, gvs, lgs, rs, g_gla_norm[0], state_gla[0], n_seq=bs, c=ts, n_sub=1, steps=1)

    out_w = (g_ffn_norm[0], w_out_b, w_route, b_route)
    bufs = _outproj(oap, obp, xp, mod_tab, mod_tab, mod_tab, *out_w, tm=to,
                    mod_specs=(pmod(2, to), pmod(3, to, 1), pmod(4, to, 1)), n_total=n_tot, row0=0,
                    fill_steps=-(-n_s // to), oa_transposed=True)
    x1, h2p, route = _outproj(oas, obs, xs, mod_s[2], mod_s[3], mod_s[4], *out_w, tm=n_s,
                              mod_specs=(smod, smod, smod), n_total=n_tot, row0=n_p, bufs=bufs)

    n_blocks = -(-(n_tot * TOP_K + N_EXPERTS * (tb - 1)) // tb)
    pos, pend_tab = _rank(route, tb=tb)
    pend = pend_tab[:, 0]
    block_e = jnp.minimum(jnp.sum(pend[None, :] <= (jnp.arange(n_blocks, dtype=jnp.int32) * tb)[:, None],
                                  axis=1), N_EXPERTS - 1).astype(jnp.int32)
    n_used = pend[N_EXPERTS - 1:] // tb
    experts = jnp.arange(N_EXPERTS, dtype=jnp.int32)
    has_rows = jnp.diff(pend, prepend=0) > 0
    later = jnp.where((experts[None, :] > experts[:, None]) & has_rows[None, :], experts[None, :], N_EXPERTS)
    next_with_rows = jnp.min(later, axis=1)
    next_with_rows = jnp.where(next_with_rows == N_EXPERTS, experts, next_with_rows)
    next_e = jnp.sum(jnp.where(block_e[:, None] == experts[None, :], next_with_rows[None, :], 0),
                     axis=1).astype(jnp.int32)
    xsort = _dispatch(pos.reshape(-1), pend, h2p, tm=td, tb=tb, n_blocks=n_blocks)
    yb = _experts(block_e, next_e, n_used, xsort, w_expert_gate[0], w_expert_up[0], w_expert_down[0], tb=tb)

    pos = pos.reshape(TOP_K, n_tot)
    pos_p = pos[:, :n_p].reshape(-1)
    pos_s = pos[:, n_p:].reshape(-1)
    gt_p = pl.BlockSpec((1, 1, d), lambda i, p: ((i // (t // tc)) * N_MOD + 5, 0, 0))
    gt_s = pl.BlockSpec((n_s, d), lambda i, p: (0, 0))
    y_p = _combine(pos_p, yb, x1, route, mod_tab, g_final, tm=tc, n=n_p, row0=0, gt_spec=gt_p)
    y_s = _combine(pos_s, yb, x1, route, mod_s[5], g_final, tm=n_s, n=n_s, row0=n_p, gt_spec=gt_s)

    kv_shape = (SWA_KV_HEADS, SWA_HEAD_DIM)
    last = lambda a: a.reshape(bp, t, KV_WIDTH)[:, t - WINDOW:, :].reshape(1, bp, WINDOW, *kv_shape)
    k_state_p, v_state_p = last(kp), last(vp)
    return (y_p.reshape(bp, t, d), y_s.reshape(bs, ts, d), k_state_p, v_state_p, sp[None],
            ks.reshape(bs, ts, *kv_shape)[None], vs.reshape(bs, ts, *kv_shape)[None], ss[None])
```

```python
import functools
import math

import jax
import jax.numpy as jnp
from jax import lax
from jax.experimental import pallas as pl
from jax.experimental.pallas import tpu as pltpu

f32 = jnp.float32
bf16 = jnp.bfloat16
u32 = jnp.uint32

D_MODEL = 2048
N_MOD = 6
EPS = 1e-6
NEG_INF = -1e30
LOG2_E = math.log2(math.e)

SWA_HEAD_DIM = 64
SWA_KV_HEADS = 2
SWA_GROUP = 8
SWA_WIDTH = SWA_KV_HEADS * SWA_GROUP * SWA_HEAD_DIM
KV_WIDTH = SWA_KV_HEADS * SWA_HEAD_DIM
WINDOW = 128
CHUNK = 64

GLA_HEADS = 4
GLA_DK = 128
GLA_DV = 256
GLA_QK_WIDTH = GLA_HEADS * GLA_DK
GLA_V_WIDTH = GLA_HEADS * GLA_DV
GLA_GATE_RANK = 16
GLA_GATE_NORM = 16.0

N_GROUPS = 4
EXPERTS_PER_GROUP = 8
N_EXPERTS = N_GROUPS * EXPERTS_PER_GROUP
TOP_K = 2
EXPERT_HIDDEN = D_MODEL // 4

_C_Q = 0
_C_K = _C_Q + SWA_WIDTH
_C_V = _C_K + KV_WIDTH
_C_GQ = _C_V + KV_WIDTH
_C_GK = _C_GQ + GLA_QK_WIDTH
_C_GV = _C_GK + GLA_QK_WIDTH
_C_R = _C_GV + GLA_V_WIDTH
_C_AB = _C_R + GLA_V_WIDTH

LANES = 128
SUBLANES = 8
PACKED_WIDTH = D_MODEL // 2
assert PACKED_WIDTH == SUBLANES * LANES
VMEM_LIMIT = 56 * 1024 * 1024

ROUTE_E0, ROUTE_E1, ROUTE_G0, ROUTE_G1 = 0, 1, 2, 3


def _dot(a, b):
    return jnp.dot(a, b, preferred_element_type=f32)


def _dot_nt(a, b):
    return lax.dot_general(a, b, (((1,), (1,)), ((), ())), preferred_element_type=f32)


def _dot_tn(a, b):
    return lax.dot_general(a, b, (((0,), (0,)), ((), ())), preferred_element_type=f32)


def _silu(x):
    return x / (1.0 + jnp.exp(-x))


def _rows(ref):
    v = ref[...]
    return v.reshape(v.shape[-2:])


def _rms(x):
    return x * lax.rsqrt(jnp.mean(x * x, axis=-1, keepdims=True) + EPS)


def _resident(shape):
    return pl.BlockSpec(shape, lambda *_: (0,) * len(shape), pipeline_mode=pl.Buffered(1))


def _mod_spec(comp, tiles_per_seq, n_tiles, lag=0):
    def index(i):
        tile = jnp.minimum(jnp.maximum(i - lag, 0), n_tiles - 1)
        return ((tile // tiles_per_seq) * N_MOD + comp, 0, 0)
    return pl.BlockSpec((1, 1, D_MODEL), index)


def _adaln_kernel(c_ref, w_ref, b_ref, o_ref):
    a = _silu(c_ref[...]).astype(bf16)
    o_ref[...] = _dot(a, w_ref[...].astype(bf16)) + b_ref[...]


def _adaln(c_all, w_ada, b_ada, tn=1024):
    r = c_all.shape[0]
    n = w_ada.shape[1]
    return pl.pallas_call(
        _adaln_kernel,
        out_shape=jax.ShapeDtypeStruct((r, n), f32),
        grid=(n // tn,),
        in_specs=[pl.BlockSpec((r, D_MODEL), lambda j: (0, 0)),
                  pl.BlockSpec((D_MODEL, tn), lambda j: (0, j)),
                  pl.BlockSpec((1, tn), lambda j: (0, j))],
        out_specs=pl.BlockSpec((r, tn), lambda j: (0, j)),
        compiler_params=pltpu.CompilerParams(dimension_semantics=("arbitrary",),
                                             vmem_limit_bytes=VMEM_LIMIT),
        name="adaln",
    )(c_all, w_ada, b_ada.reshape(1, n))


def _proj_kernel(x_ref, sh_ref, sc_ref, g_ref, wq_ref, w_ref, wab_ref, wg_ref, bg_ref,
                 q_ref, k_ref, v_ref, gq_ref, gk_ref, gv_ref, r_ref, lg_ref, *, q_transposed):
    h = (_rms(x_ref[...]) * g_ref[...]) * (1.0 + _rows(sc_ref)) + _rows(sh_ref)
    hb = h.astype(bf16)
    col = lambda c: c - _C_K
    ab = _dot(hb, wab_ref[...])
    r_ref[...] = _silu(_dot(hb, w_ref[:, col(_C_R):col(_C_AB)])).astype(bf16)
    gq_ref[...] = (_dot(hb, w_ref[:, col(_C_GQ):col(_C_GK)]) * (GLA_DK ** -0.5)).astype(bf16)
    z = _dot(ab.astype(bf16), wg_ref[...]) + bg_ref[...]
    log_sig = jnp.minimum(z, 0.0) - jnp.log1p(jnp.exp(-jnp.abs(z)))
    lg_ref[...] = log_sig * (LOG2_E / GLA_GATE_NORM)
    gk_ref[...] = _dot(hb, w_ref[:, col(_C_GK):col(_C_GV)]).astype(bf16)
    gv_ref[...] = _dot(hb, w_ref[:, col(_C_GV):col(_C_R)]).astype(bf16)
    kv = _dot(hb, w_ref[:, col(_C_K):col(_C_GQ)])
    k_ref[...] = kv[:, :KV_WIDTH]
    v_ref[...] = kv[:, KV_WIDTH:]
    if q_transposed:
        q_ref[...] = (_dot_nt(wq_ref[...], hb) * (LOG2_E * SWA_HEAD_DIM ** -0.5)).astype(bf16)
    else:
        q_ref[...] = _dot(hb, wq_ref[...]).astype(bf16)


def _proj(x2d, sh, sc, g_mix, w_q, w_rest, w_ab, w_gate, b_gate, *, tm, mod_specs, q_transposed):
    n = x2d.shape[0]
    row = lambda w: pl.BlockSpec((tm, w), lambda i: (i, 0))
    outs = [(KV_WIDTH, f32), (KV_WIDTH, f32), (GLA_QK_WIDTH, bf16),
            (GLA_QK_WIDTH, bf16), (GLA_V_WIDTH, bf16), (GLA_V_WIDTH, bf16), (GLA_QK_WIDTH, f32)]
    if q_transposed:
        q_shape, q_spec = (SWA_WIDTH, n), pl.BlockSpec((SWA_WIDTH, tm), lambda i: (0, i))
    else:
        q_shape, q_spec = (n, SWA_WIDTH), row(SWA_WIDTH)
    return pl.pallas_call(
        functools.partial(_proj_kernel, q_transposed=q_transposed),
        out_shape=[jax.ShapeDtypeStruct(q_shape, bf16)]
                  + [jax.ShapeDtypeStruct((n, w), dt) for w, dt in outs],
        grid=(n // tm,),
        in_specs=[row(D_MODEL), mod_specs[0], mod_specs[1],
                  _resident((1, D_MODEL)), _resident(w_q.shape), _resident(w_rest.shape),
                  _resident(w_ab.shape), _resident(w_gate.shape), _resident((1, GLA_QK_WIDTH))],
        out_specs=[q_spec] + [row(w) for w, _ in outs],
        compiler_params=pltpu.CompilerParams(dimension_semantics=("parallel",),
                                             vmem_limit_bytes=VMEM_LIMIT),
        name="proj",
    )(x2d, sh, sc, g_mix.reshape(1, D_MODEL), w_q, w_rest, w_ab, w_gate, b_gate.reshape(1, -1))


def _swa_kernel(q_ref, kp_ref, vp_ref, kc_ref, vc_ref, sink_ref, o_ref, *, tq):
    pad = jnp.zeros((WINDOW - tq, KV_WIDTH), f32)
    k_all = jnp.concatenate([kp_ref[...], kc_ref[...], pad], axis=0).astype(bf16)
    v_all = jnp.concatenate([vp_ref[...], vc_ref[...], pad], axis=0).astype(bf16)
    rows, cols = SWA_GROUP * tq, 2 * WINDOW
    valid = lax.broadcasted_iota(jnp.int32, (rows, cols), 1) < WINDOW + tq
    q = q_ref[...]
    outs = []
    for j in range(SWA_KV_HEADS):
        heads = [q[:, (j * SWA_GROUP + g) * SWA_HEAD_DIM:(j * SWA_GROUP + g + 1) * SWA_HEAD_DIM]
                 for g in range(SWA_GROUP)]
        qs = jnp.concatenate(heads, axis=0)
        kj = k_all[:, j * SWA_HEAD_DIM:(j + 1) * SWA_HEAD_DIM]
        vj = v_all[:, j * SWA_HEAD_DIM:(j + 1) * SWA_HEAD_DIM]
        s = _dot_nt(qs, kj) * (SWA_HEAD_DIM ** -0.5)
        s = jnp.where(valid, s, NEG_INF)
        sink = sink_ref[j]
        m = jnp.maximum(jnp.max(s, axis=1, keepdims=True), sink)
        p = jnp.exp(s - m)
        den = jnp.sum(p, axis=1, keepdims=True) + jnp.exp(sink - m)
        o = _dot(p.astype(bf16), vj) / den
        outs.append(jnp.concatenate([o[g * tq:(g + 1) * tq] for g in range(SWA_GROUP)], axis=1))
    o_ref[...] = jnp.concatenate(outs, axis=1).astype(bf16)


def _swa(q, k_past, v_past, k_new, v_new, sinks, *, n_seq, tq):
    sink_rows = jnp.repeat(sinks.astype(f32).reshape(SWA_KV_HEADS, SWA_GROUP), tq, axis=1)
    sink_rows = sink_rows.reshape(SWA_KV_HEADS, SWA_GROUP * tq, 1)
    new = lambda w: pl.BlockSpec((tq, w), lambda b: (b, 0))
    past = pl.BlockSpec((WINDOW, KV_WIDTH), lambda b: (b, 0))
    return pl.pallas_call(
        functools.partial(_swa_kernel, tq=tq),
        out_shape=jax.ShapeDtypeStruct(q.shape, bf16),
        grid=(n_seq,),
        in_specs=[new(SWA_WIDTH), past, past, new(KV_WIDTH), new(KV_WIDTH),
                  pl.BlockSpec(sink_rows.shape, lambda b: (0, 0, 0))],
        out_specs=new(SWA_WIDTH),
        compiler_params=pltpu.CompilerParams(dimension_semantics=("parallel",),
                                             vmem_limit_bytes=VMEM_LIMIT),
        name="swa",
    )(q, k_past, v_past, k_new, v_new, sink_rows)


def _swa_t_kernel(q_ref, kp_ref, vp_ref, kc_ref, vc_ref, sink_ref, mk_ref, mq0_ref, mq_ref, o_ref, *, n_sub):
    tq = WINDOW
    hd = SWA_HEAD_DIM
    k_all = jnp.concatenate([kp_ref[...], kc_ref[...]], axis=0).astype(bf16)
    v_all = jnp.concatenate([vp_ref[...], vc_ref[...]], axis=0).astype(bf16)
    low_lanes = lax.broadcasted_iota(jnp.int32, (2 * WINDOW, KV_WIDTH), 1) < hd
    ones = jnp.ones((2 * WINDOW, KV_WIDTH), bf16)
    for sub in range(n_sub):
        keys = slice(sub * tq, sub * tq + 2 * WINDOW)
        toks = slice(sub * tq, (sub + 1) * tq)
        mq = (mq0_ref if sub == 0 else mq_ref)[0]
        for j in range(SWA_KV_HEADS):
            head = lambda g: slice((j * SWA_GROUP + g) * hd, (j * SWA_GROUP + g + 1) * hd)
            qs = jnp.concatenate([q_ref[head(g), toks] for g in range(SWA_GROUP)], axis=1)
            own = low_lanes if j == 0 else jnp.logical_not(low_lanes)
            k_aug = jnp.where(own, k_all[keys], mk_ref[j])
            v_aug = jnp.where(own, v_all[keys], ones)
            q_aug = jnp.concatenate([qs, mq] if j == 0 else [mq, qs], axis=0)
            s = _dot(k_aug, q_aug)
            sink = sink_ref[j]
            m = jnp.maximum(jnp.max(s, axis=0, keepdims=True), sink)
            p = jnp.exp2(s - m).astype(bf16)
            o_aug = _dot_tn(v_aug, p)
            pv, p_sum = (o_aug[:hd], o_aug[hd:hd + 1]) if j == 0 else (o_aug[hd:], o_aug[0:1])
            o = pv / (p_sum + jnp.exp2(sink - m))
            for g in range(SWA_GROUP):
                o_ref[head(g), toks] = o[:, g * tq:(g + 1) * tq].astype(bf16)


def _swa_t(q_t, k, v, sinks, *, n_seq, tiles, n_sub):
    tq = WINDOW
    steps = tiles // n_sub
    hd = SWA_HEAD_DIM
    sink_cols = jnp.repeat(sinks.astype(f32).reshape(SWA_KV_HEADS, SWA_GROUP) * LOG2_E, tq, axis=1)
    sink_cols = sink_cols.reshape(SWA_KV_HEADS, 1, SWA_GROUP * tq)
    n_kc = 2 * WINDOW // CHUNK
    key_chunk = jnp.arange(2 * WINDOW) // CHUNK
    lane = jnp.arange(KV_WIDTH)
    mk = jnp.stack([lane[None, :] == hd + key_chunk[:, None], lane[None, :] == key_chunk[:, None]]).astype(bf16)
    q_chunk = (jnp.arange(SWA_GROUP * tq) % tq) // CHUNK
    kc = jnp.arange(hd)[:, None]
    band = (kc >= q_chunk[None, :]) & (kc <= q_chunk[None, :] + WINDOW // CHUNK)
    visible = jnp.stack([band & (kc >= WINDOW // CHUNK), band]) | (kc >= n_kc)
    mq = jnp.where(visible, 0.0, NEG_INF).astype(bf16)
    mq0_spec = pl.BlockSpec((1, hd, SWA_GROUP * tq), lambda b, u: (jnp.minimum(u, 1), 0, 0))
    mq_spec = pl.BlockSpec((1, hd, SWA_GROUP * tq), lambda b, u: (1, 0, 0))
    qspec = pl.BlockSpec((SWA_WIDTH, n_sub * tq), lambda b, u: (0, b * steps + u))
    cur = pl.BlockSpec((n_sub * tq, KV_WIDTH), lambda b, u: (b * steps + u, 0))
    prev = pl.BlockSpec((WINDOW, KV_WIDTH), lambda b, u: (b * tiles + jnp.maximum(n_sub * u - 1, 0), 0))
    return pl.pallas_call(
        functools.partial(_swa_t_kernel, n_sub=n_sub),
        out_shape=jax.ShapeDtypeStruct(q_t.shape, bf16),
        grid=(n_seq, steps),
        in_specs=[qspec, prev, prev, cur, cur, pl.BlockSpec(sink_cols.shape, lambda b, u: (0, 0, 0)),
                  pl.BlockSpec(mk.shape, lambda b, u: (0, 0, 0)), mq0_spec, mq_spec],
        out_specs=qspec,
        compiler_params=pltpu.CompilerParams(dimension_semantics=("parallel", "arbitrary"),
                                             vmem_limit_bytes=VMEM_LIMIT),
        name="swa_t",
    )(q_t, k, v, k, v, sink_cols, mk, mq, mq)


def _gla_kernel(q_ref, k_ref, v_ref, lg_ref, r_ref, gh_ref, s0_ref, o_ref, s_ref, *, c, n_sub):
    @pl.when(pl.program_id(1) == 0)
    def _():
        s_ref[...] = s0_ref[...]

    row = lax.broadcasted_iota(jnp.int32, (c, GLA_DK), 0)
    causal = (lax.broadcasted_iota(jnp.int32, (c, c), 0) >= lax.broadcasted_iota(jnp.int32, (c, c), 1))
    for sub in range(n_sub):
        ts = slice(sub * c, (sub + 1) * c)
        for h in range(GLA_HEADS):
            ks = slice(h * GLA_DK, (h + 1) * GLA_DK)
            vs = slice(h * GLA_DV, (h + 1) * GLA_DV)
            b = lg_ref[ts, ks]
            step = 1
            while step < c:
                b = b + jnp.where(row >= step, pltpu.roll(b, step, 0), 0.0)
                step *= 2
            b_last = b[c - 1:c, :]
            q = q_ref[ts, ks].astype(f32)
            k = k_ref[ts, ks].astype(f32)
            v = v_ref[ts, vs]
            qd = (q * jnp.exp2(b)).astype(bf16)
            kd = (k * jnp.exp2(-b)).astype(bf16)
            kl = (k * jnp.exp2(b_last - b)).astype(bf16)
            a = jnp.where(causal, _dot_nt(qd, kd), 0.0)
            s = s_ref[0, h]
            o = _dot(qd, s.astype(bf16)) + _dot(a.astype(bf16), v)
            decay = jnp.broadcast_to(jnp.exp2(b_last), (GLA_DK, GLA_DK)).T
            s_ref[0, h] = s * jnp.concatenate([decay, decay], axis=1) + _dot_tn(kl, v)
            on = _rms(o) * gh_ref[...]
            o_ref[ts, vs] = (on * r_ref[ts, vs].astype(f32)).astype(bf16)


def _gla(gq, gk, gv, lg, r, g_head, s0, *, n_seq, c, n_sub, steps):
    rows = c * n_sub
    blk = lambda w: pl.BlockSpec((rows, w), lambda b, t: (b * steps + t, 0))
    state = pl.BlockSpec((1, GLA_HEADS, GLA_DK, GLA_DV), lambda b, t: (b, 0, 0, 0))
    return pl.pallas_call(
        functools.partial(_gla_kernel, c=c, n_sub=n_sub),
        out_shape=[jax.ShapeDtypeStruct(gv.shape, bf16),
                   jax.ShapeDtypeStruct((n_seq, GLA_HEADS, GLA_DK, GLA_DV), f32)],
        grid=(n_seq, steps),
        in_specs=[blk(GLA_QK_WIDTH), blk(GLA_QK_WIDTH), blk(GLA_V_WIDTH), blk(GLA_QK_WIDTH),
                  blk(GLA_V_WIDTH), pl.BlockSpec((1, GLA_DV), lambda b, t: (0, 0)), state],
        out_specs=[blk(GLA_V_WIDTH), state],
        compiler_params=pltpu.CompilerParams(dimension_semantics=("parallel", "arbitrary"),
                                             vmem_limit_bytes=VMEM_LIMIT),
        name="gla",
    )(gq, gk, gv, lg, r, g_head.reshape(1, GLA_DV), s0)


def _pack_pair(hb):
    w = hb.shape[1] // 2
    a = lax.bitcast_convert_type(hb[:, :w].astype(f32), u32)
    b = lax.bitcast_convert_type(hb[:, w:].astype(f32), u32)
    return a | (b >> 16)


def _unpack_pair(p):
    a = lax.bitcast_convert_type(p & jnp.uint32(0xFFFF0000), f32)
    b = lax.bitcast_convert_type(p << 16, f32)
    return a, b


def _load_row_tiles(ref, n):
    return jnp.concatenate([ref[pl.ds(s, n, stride=SUBLANES), :] for s in range(SUBLANES)], axis=1)


def _store_row_tiles(ref, val):
    n = val.shape[0]
    for s in range(SUBLANES):
        ref[pl.ds(s, n, stride=SUBLANES), :] = val[:, s * LANES:(s + 1) * LANES]


def _row_tile(ref, r):
    return ref.at[pl.ds(pl.multiple_of(r * SUBLANES, SUBLANES), SUBLANES), :]


def _outproj_kernel(*refs, n_alias, oa_transposed):
    (oa_ref, ob_ref, x_ref, gt_ref, sh_ref, sc_ref, g_ref, w_ref, wr_ref, br_ref) = refs[:10]
    x1_ref, h2_ref, route_ref, x1_s = refs[10 + n_alias:]

    @pl.when(pl.program_id(0) == 0)
    def _():
        x1_s[...] = jnp.zeros(x1_s.shape, x1_s.dtype)

    h2 = (_rms(x1_s[...]) * g_ref[...]) * (1.0 + _rows(sc_ref)) + _rows(sh_ref)
    hi = h2.astype(bf16)
    _store_row_tiles(h2_ref, _pack_pair(hi))
    lo = (h2 - hi.astype(f32)).astype(bf16)
    r1 = _dot(hi, wr_ref[...])
    logits = r1[:, :LANES] + r1[:, LANES:] + _dot(lo, wr_ref[:, :LANES]) + br_ref[...]
    lane = lax.broadcasted_iota(jnp.int32, logits.shape, 1)
    lane_f = lane.astype(f32)
    neg = float("-inf")
    first = lambda hit: jnp.min(jnp.where(hit, lane_f, float(LANES)), axis=1, keepdims=True)
    lg_g = jnp.where(lane < N_GROUPS, logits, neg)
    g_max = jnp.max(lg_g, axis=1, keepdims=True)
    g_sel = first(lg_g == g_max)
    p_sel = 1.0 / jnp.sum(jnp.exp(lg_g - g_max), axis=1, keepdims=True)
    e_lane = lane - N_GROUPS
    in_group = (e_lane >= 0) & (e_lane < N_EXPERTS) & ((e_lane // EXPERTS_PER_GROUP).astype(f32) == g_sel)
    lg_e = jnp.where(in_group, logits, neg)
    v1 = jnp.max(lg_e, axis=1, keepdims=True)
    i1 = first(lg_e == v1)
    lg_e2 = jnp.where(lane_f == i1, neg, lg_e)
    v2 = jnp.max(lg_e2, axis=1, keepdims=True)
    i2 = first(lg_e2 == v2)
    e = jnp.exp(v2 - v1)
    g1 = p_sel / (1.0 + e)
    g2 = p_sel * e / (1.0 + e)
    rec = jnp.where(lane == ROUTE_E0, i1 - N_GROUPS, 0.0)
    rec = jnp.where(lane == ROUTE_E1, i2 - N_GROUPS, rec)
    rec = jnp.where(lane == ROUTE_G0, g1, rec)
    rec = jnp.where(lane == ROUTE_G1, g2, rec)
    route_ref[...] = rec

    oa_dot = _dot_tn if oa_transposed else _dot
    mix = oa_dot(oa_ref[...], w_ref[:SWA_WIDTH, :]) + _dot(ob_ref[...], w_ref[SWA_WIDTH:, :])
    x1 = x_ref[...] + _rows(gt_ref) * mix
    x1_ref[...] = x1
    x1_s[...] = x1


def _outproj(oa, ob, x2d, gt, sh, sc, g_ffn, w_out, w_route, b_route, *, tm, mod_specs,
             n_total, row0, bufs=None, fill_steps=0, oa_transposed=False):
    n = x2d.shape[0]
    tiles = n // tm
    blocks = tiles + fill_steps
    row = lambda w: pl.BlockSpec((tm, w), lambda i: (jnp.minimum(i, tiles - 1), 0))
    oa_spec = row(SWA_WIDTH)
    if oa_transposed:
        oa_spec = pl.BlockSpec((SWA_WIDTH, tm), lambda i: (0, jnp.minimum(i, tiles - 1)))
    head_blk = lambda i: row0 // tm + jnp.minimum(i, blocks - 1)
    tail_blk = lambda i: row0 // tm + jnp.maximum(i - 1, 0)
    out_x1 = pl.BlockSpec((tm, D_MODEL), lambda i: (head_blk(i), 0))
    out_tiles = pl.BlockSpec((tm * SUBLANES, LANES), lambda i: (tail_blk(i), 0))
    out_route = pl.BlockSpec((tm, LANES), lambda i: (tail_blk(i), 0))
    alias_in = list(bufs) if bufs is not None else []
    n_in = 10
    return pl.pallas_call(
        functools.partial(_outproj_kernel, n_alias=len(alias_in), oa_transposed=oa_transposed),
        out_shape=[jax.ShapeDtypeStruct((n_total, D_MODEL), f32),
                   jax.ShapeDtypeStruct((n_total * SUBLANES, LANES), u32),
                   jax.ShapeDtypeStruct((n_total, LANES), f32)],
        grid=(blocks + 1,),
        in_specs=[oa_spec, row(GLA_V_WIDTH), row(D_MODEL), mod_specs[0], mod_specs[1],
                  mod_specs[2], _resident((1, D_MODEL)), _resident(w_out.shape),
                  _resident(w_route.shape), _resident((1, LANES))]
                 + [pl.BlockSpec(memory_space=pl.ANY)] * len(alias_in),
        out_specs=[out_x1, out_tiles, out_route],
        scratch_shapes=[pltpu.VMEM((tm, D_MODEL), f32)],
        input_output_aliases={n_in + a: a for a in range(len(alias_in))},
        compiler_params=pltpu.CompilerParams(dimension_semantics=("arbitrary",),
                                             vmem_limit_bytes=VMEM_LIMIT),
        name="outproj",
    )(oa, ob, x2d, gt, sh, sc, g_ffn.reshape(1, D_MODEL), w_out, w_route, b_route, *alias_in)


def _rank_kernel(route_ref, pos_ref, pend_ref, rec_s, *, n_chunks, tb):
    n_e = N_EXPERTS
    expert = lax.broadcasted_iota(jnp.int32, (n_e, LANES), 0)
    expert_f = expert.astype(f32)
    earlier = (lax.broadcasted_iota(jnp.int32, (LANES, LANES), 0)
               < lax.broadcasted_iota(jnp.int32, (LANES, LANES), 1)).astype(bf16)

    def onehots(c):
        rec = rec_s[c]
        return expert_f == rec[ROUTE_E0:ROUTE_E0 + 1, :], expert_f == rec[ROUTE_E1:ROUTE_E1 + 1, :]

    def count(c, cnt):
        rec_s[c] = route_ref[pl.ds(pl.multiple_of(c * LANES, LANES), LANES), :].T[:SUBLANES, :]
        h0, h1 = onehots(c)
        return cnt + jnp.sum((h0 | h1).astype(f32), axis=1, keepdims=True)

    cnt = lax.fori_loop(0, n_chunks, count, jnp.zeros((n_e, 1), f32), unroll=4)
    padded = jnp.floor((cnt + (tb - 1.0)) / tb) * tb
    end = jnp.broadcast_to(padded, (n_e, LANES))
    step = 1
    while step < n_e:
        end = end + jnp.where(expert >= step, pltpu.roll(end, step, 0), 0.0)
        step *= 2
    pend_ref[0] = end.astype(jnp.int32)
    pend_ref[1] = jnp.broadcast_to(jnp.where(cnt > 0, cnt - (padded - tb), 0.0), (n_e, LANES)).astype(jnp.int32)

    def rank(c, base):
        h0, h1 = onehots(c)
        both = h0 | h1
        off = _dot(both.astype(bf16), earlier) + base
        pos_ref[0, pl.ds(c, 1), :] = jnp.sum(jnp.where(h0, off, 0.0), axis=0, keepdims=True).astype(jnp.int32)
        pos_ref[1, pl.ds(c, 1), :] = jnp.sum(jnp.where(h1, off, 0.0), axis=0, keepdims=True).astype(jnp.int32)
        return base + jnp.sum(both.astype(f32), axis=1, keepdims=True)

    lax.fori_loop(0, n_chunks, rank, (end - padded)[:, 0:1], unroll=4)


def _rank(route, *, tb):
    n = route.shape[0]
    n_chunks = n // LANES
    return pl.pallas_call(
        functools.partial(_rank_kernel, n_chunks=n_chunks, tb=tb),
        out_shape=[jax.ShapeDtypeStruct((TOP_K, n_chunks, LANES), jnp.int32),
                   jax.ShapeDtypeStruct((2, N_EXPERTS, LANES), jnp.int32)],
        in_specs=[_resident(route.shape)],
        scratch_shapes=[pltpu.VMEM((n_chunks, SUBLANES, LANES), f32)],
        compiler_params=pltpu.CompilerParams(vmem_limit_bytes=VMEM_LIMIT),
        name="rank",
    )(route)


def _dispatch_kernel(pos_ref, pend_ref, h2_ref, xs_hbm, stage, zbuf, sem, zsem, *, tm, tb, n_tot, steps):
    i = pl.program_id(0)
    slot = i % 2

    blk = tb * SUBLANES

    def wait_slot(s):
        for _ in range(TOP_K):
            pltpu.make_async_copy(stage.at[s], xs_hbm.at[pl.ds(0, tm * SUBLANES), :], sem.at[s]).wait()

    @pl.when(i == 0)
    def _():
        zbuf[...] = jnp.zeros(zbuf.shape, zbuf.dtype)

        def fill_block(b, start):
            cp = pltpu.make_async_copy(zbuf, xs_hbm.at[pl.ds(pl.multiple_of(b * blk, blk), blk), :], zsem)
            cp.start() if start else cp.wait()

        def fill(e, start):
            end = pend_ref[e]
            prev = jnp.where(e > 0, pend_ref[jnp.maximum(e - 1, 0)], 0)

            @pl.when(end > prev)
            def _():
                fill_block(end // tb - 1, start)

        fill_unused = fill_block

        first_unused = pend_ref[N_EXPERTS - 1] // tb
        n_blocks = xs_hbm.shape[0] // blk
        lax.fori_loop(0, N_EXPERTS, lambda e, c: (fill(e, True), c)[1], 0)
        lax.fori_loop(first_unused, n_blocks, lambda b, c: (fill_unused(b, True), c)[1], 0)
        lax.fori_loop(0, N_EXPERTS, lambda e, c: (fill(e, False), c)[1], 0)
        lax.fori_loop(first_unused, n_blocks, lambda b, c: (fill_unused(b, False), c)[1], 0)

    @pl.when(i >= 2)
    def _():
        wait_slot(slot)

    stage[slot] = h2_ref[...]

    def scatter(r, carry):
        for k in range(TOP_K):
            d = pos_ref[k * n_tot + i * tm + r]
            pltpu.make_async_copy(_row_tile(stage.at[slot], r), _row_tile(xs_hbm, d),
                                  sem.at[slot]).start(priority=k)
        return carry

    lax.fori_loop(0, tm, scatter, 0, unroll=8)

    @pl.when(i == steps - 1)
    def _():
        wait_slot(slot)
        if steps > 1:
            wait_slot(1 - slot)


def _dispatch(pos_flat, pend, h2p, *, tm, tb, n_blocks):
    n_tot = h2p.shape[0] // SUBLANES
    steps = n_tot // tm
    return pl.pallas_call(
        functools.partial(_dispatch_kernel, tm=tm, tb=tb, n_tot=n_tot, steps=steps),
        out_shape=jax.ShapeDtypeStruct((n_blocks * tb * SUBLANES, LANES), u32),
        grid_spec=pltpu.PrefetchScalarGridSpec(
            num_scalar_prefetch=2,
            grid=(steps,),
            in_specs=[pl.BlockSpec((tm * SUBLANES, LANES), lambda i, pos, pend: (i, 0))],
            out_specs=pl.BlockSpec(memory_space=pl.ANY),
            scratch_shapes=[pltpu.VMEM((2, tm * SUBLANES, LANES), u32),
                            pltpu.VMEM((tb * SUBLANES, LANES), u32),
                            pltpu.SemaphoreType.DMA((2,)), pltpu.SemaphoreType.DMA(())]),
        compiler_params=pltpu.CompilerParams(dimension_semantics=("arbitrary",),
                                             vmem_limit_bytes=VMEM_LIMIT),
        name="dispatch",
    )(pos_flat, pend, h2p)


def _expert_kernel(be_ref, ne_ref, hf_ref, nu_ref, xs_ref, wg_hbm, wu_hbm, wd_hbm, y_ref,
                   wg_s, wu_s, wd_s, wg_b, wu_b, wd_b, sem, *, tb):
    i = pl.program_id(0)
    used = i < nu_ref[0]
    e = be_ref[i]

    def fetch(expert):
        pairs = ((wg_hbm, wg_s), (wu_hbm, wu_s), (wd_hbm, wd_s))
        return [pltpu.make_async_copy(src.at[expert], dst, sem.at[n]) for n, (src, dst) in enumerate(pairs)]

    @pl.when(i == 0)
    def _():
        for cp in fetch(e):
            cp.start()

    @pl.when(used & ((i == 0) | (e != be_ref[jnp.maximum(i - 1, 0)])))
    def _():
        for cp in fetch(e):
            cp.wait()
        wg_b[...] = wg_s[...].astype(bf16)
        wu_b[...] = wu_s[...].astype(bf16)
        wd_b[...] = wd_s[...].astype(bf16)
        nxt = ne_ref[i]

        @pl.when(nxt != e)
        def _():
            for cp in fetch(nxt):
                cp.start()

    def ffn(rows):
        tiles = pl.ds(0, rows * SUBLANES)
        a, b = _unpack_pair(_load_row_tiles(xs_ref.at[tiles], rows))
        x = jnp.concatenate([a.astype(bf16), b.astype(bf16)], axis=1)
        g = _dot(x, wg_b[...])
        u = _dot(x, wu_b[...])
        y = _dot((_silu(g) * u).astype(bf16), wd_b[...])
        _store_row_tiles(y_ref.at[tiles], _pack_pair(y.astype(bf16)))

    half_full = hf_ref[i] == 1

    @pl.when(used & jnp.logical_not(half_full))
    def _():
        ffn(tb)

    @pl.when(used & half_full)
    def _():
        ffn(tb // 2)
        rest = pl.ds(tb // 2 * SUBLANES, tb // 2 * SUBLANES)
        y_ref[rest, :] = jnp.zeros((tb // 2 * SUBLANES, LANES), y_ref.dtype)

    @pl.when(jnp.logical_not(used))
    def _():
        y_ref[...] = jnp.zeros(y_ref.shape, y_ref.dtype)


def _experts(block_e, next_e, half_full, n_used, xs, w_eg, w_eu, w_ed, *, tb):
    n_blocks = block_e.shape[0]
    blk = (tb * SUBLANES, LANES)
    up, down = (D_MODEL, EXPERT_HIDDEN), (EXPERT_HIDDEN, D_MODEL)
    hbm = pl.BlockSpec(memory_space=pl.ANY)
    return pl.pallas_call(
        functools.partial(_expert_kernel, tb=tb),
        out_shape=jax.ShapeDtypeStruct(xs.shape, u32),
        grid_spec=pltpu.PrefetchScalarGridSpec(
            num_scalar_prefetch=4,
            grid=(n_blocks,),
            in_specs=[pl.BlockSpec(blk, lambda i, be, ne, hf, nu: (jnp.minimum(i, nu[0] - 1), 0)),
                      hbm, hbm, hbm],
            out_specs=pl.BlockSpec(blk, lambda i, be, ne, hf, nu: (i, 0)),
            scratch_shapes=[pltpu.VMEM(up, f32), pltpu.VMEM(up, f32), pltpu.VMEM(down, f32),
                            pltpu.VMEM(up, bf16), pltpu.VMEM(up, bf16), pltpu.VMEM(down, bf16),
                            pltpu.SemaphoreType.DMA((3,))]),
        compiler_params=pltpu.CompilerParams(dimension_semantics=("arbitrary",),
                                             vmem_limit_bytes=VMEM_LIMIT),
        name="experts",
    )(block_e, next_e, half_full, n_used, xs, w_eg, w_eu, w_ed)


def _combine_kernel(pos_ref, yb_hbm, x1_ref, route_ref, gt_ref, gf_ref, y_ref, ybuf, sem, *, tm):
    i = pl.program_id(0)
    slot = i % 2

    def start(blk, s):
        def body(r, carry):
            for k in range(TOP_K):
                t = pos_ref[(k * pl.num_programs(0) + blk) * tm + r]
                pltpu.make_async_copy(_row_tile(yb_hbm, t), _row_tile(ybuf.at[s, k], r),
                                      sem.at[s]).start(priority=k)
            return carry
        lax.fori_loop(0, tm, body, 0, unroll=8)

    @pl.when(i == 0)
    def _():
        start(0, 0)

    @pl.when(i + 1 < pl.num_programs(0))
    def _():
        start(i + 1, 1 - slot)

    for k in range(TOP_K):
        pltpu.make_async_copy(yb_hbm.at[pl.ds(0, tm * SUBLANES), :], ybuf.at[slot, k], sem.at[slot]).wait()
    route = route_ref[...]
    a0, b0 = _unpack_pair(_load_row_tiles(ybuf.at[slot, 0], tm))
    a1, b1 = _unpack_pair(_load_row_tiles(ybuf.at[slot, 1], tm))
    g0 = route[:, ROUTE_G0:ROUTE_G0 + 1]
    g1 = route[:, ROUTE_G1:ROUTE_G1 + 1]
    moe = jnp.concatenate([a0 * g0 + a1 * g1, b0 * g0 + b1 * g1], axis=1)
    x2 = x1_ref[...] + _rows(gt_ref) * moe
    y_ref[...] = _rms(x2) * gf_ref[...]


def _combine(pos_km, yb, x1, route, gt, g_final, *, tm, n, row0, gt_spec):
    blk0 = row0 // tm
    return pl.pallas_call(
        functools.partial(_combine_kernel, tm=tm),
        out_shape=jax.ShapeDtypeStruct((n, D_MODEL), f32),
        grid_spec=pltpu.PrefetchScalarGridSpec(
            num_scalar_prefetch=1,
            grid=(n // tm,),
            in_specs=[pl.BlockSpec(memory_space=pl.ANY),
                      pl.BlockSpec((tm, D_MODEL), lambda i, pos: (blk0 + i, 0)),
                      pl.BlockSpec((tm, LANES), lambda i, pos: (blk0 + i, 0)),
                      gt_spec,
                      pl.BlockSpec((1, D_MODEL), lambda i, pos: (0, 0))],
            out_specs=pl.BlockSpec((tm, D_MODEL), lambda i, pos: (i, 0)),
            scratch_shapes=[pltpu.VMEM((2, TOP_K, tm * SUBLANES, LANES), u32),
                            pltpu.SemaphoreType.DMA((2,))]),
        compiler_params=pltpu.CompilerParams(dimension_semantics=("arbitrary",),
                                             vmem_limit_bytes=VMEM_LIMIT),
        name="combine",
    )(pos_km, yb, x1, route, gt, g_final.reshape(1, D_MODEL))


def kernel(x_prompt, x_sample, cache_swa_k, cache_swa_v, state_gla, c_prompt, c_sample, g_mix_norm, g_ffn_norm, w_ada, b_ada, w_in, attn_sinks, w_gla_gate, b_gla_gate, g_gla_norm, w_out, w_router_group, b_router_group, w_router_expert, b_router_expert, w_expert_gate, w_expert_up, w_expert_down, g_final):
    depth = w_in.shape[0]
    assert depth == 1
    bp, t, d = x_prompt.shape
    bs, ts, _ = x_sample.shape
    n_p, n_s = bp * t, bs * ts
    n_tot = n_p + n_s
    assert n_tot % LANES == 0
    tm = 512
    to = 256
    tb = 512
    tc = 256
    td = LANES
    gla_c = 128
    gla_sub = 8
    swa_sub = 4

    w_in0 = w_in[0]
    w_q = w_in0[:, :_C_K].astype(bf16)
    w_q_t = w_q.T
    w_rest = w_in0[:, _C_K:_C_AB].astype(bf16)
    w_ab = jnp.pad(w_in0[:, _C_AB:], ((0, 0), (0, LANES - GLA_GATE_RANK))).astype(bf16)
    w_gate = jnp.pad(w_gla_gate[0], ((0, LANES - GLA_GATE_RANK), (0, 0))).astype(bf16)
    w_out_b = w_out[0].astype(bf16)
    n_r = N_GROUPS + N_EXPERTS
    w_r = jnp.pad(jnp.concatenate([w_router_group[0], w_router_expert[0]], axis=1),
                  ((0, 0), (0, LANES - n_r)))
    w_r_hi = w_r.astype(bf16)
    w_r_lo = (w_r - w_r_hi.astype(f32)).astype(bf16)
    w_route = jnp.concatenate([w_r_hi, w_r_lo], axis=1)
    b_route = jnp.pad(jnp.concatenate([b_router_group[0], b_router_expert[0]]),
                      (0, LANES - n_r)).reshape(1, LANES)

    c_all = jnp.concatenate([c_prompt, c_sample], axis=0)
    mod = _adaln(c_all, w_ada[0], b_ada[0])
    mod_tab = mod.reshape((bp + bs) * N_MOD, 1, d)
    mod_s = jnp.repeat(mod[bp:].reshape(bs, N_MOD, d), ts, axis=0)
    mod_s = [mod_s[:, m] for m in range(N_MOD)]
    pmod = lambda comp, tile, lag=0: _mod_spec(comp, t // tile, n_p // tile, lag)
    smod = pl.BlockSpec((n_s, d), lambda i: (0, 0))

    xp = x_prompt.reshape(n_p, d)
    xs = x_sample.reshape(n_s, d)
    proj_w = (w_rest, w_ab, w_gate, b_gla_gate[0])
    qp, kp, vp, gqp, gkp, gvp, rp, lgp = _proj(xp, mod_tab, mod_tab, g_mix_norm[0], w_q_t, *proj_w, tm=tm,
                                               mod_specs=(pmod(0, tm), pmod(1, tm)), q_transposed=True)
    qs, ks, vs, gqs, gks, gvs, rs, lgs = _proj(xs, mod_s[0], mod_s[1], g_mix_norm[0], w_q, *proj_w, tm=n_s,
                                               mod_specs=(smod, smod), q_transposed=False)

    sinks = attn_sinks[0]
    oap = _swa_t(qp, kp, vp, sinks, n_seq=bp, tiles=t // WINDOW, n_sub=swa_sub)
    ck = cache_swa_k[0].reshape(bs * WINDOW, KV_WIDTH)
    cv = cache_swa_v[0].reshape(bs * WINDOW, KV_WIDTH)
    oas = _swa(qs, ck, cv, ks, vs, sinks, n_seq=bs, tq=ts)
    s_zero = jnp.zeros((bp, GLA_HEADS, GLA_DK, GLA_DV), f32)
    obp, sp = _gla(gqp, gkp, gvp, lgp, rp, g_gla_norm[0], s_zero, n_seq=bp, c=gla_c, n_sub=gla_sub,
                   steps=t // (gla_c * gla_sub))
    obs, ss = _gla(gqs, gks, gvs, lgs, rs, g_gla_norm[0], state_gla[0], n_seq=bs, c=ts, n_sub=1, steps=1)

    out_w = (g_ffn_norm[0], w_out_b, w_route, b_route)
    bufs = _outproj(oap, obp, xp, mod_tab, mod_tab, mod_tab, *out_w, tm=to,
                    mod_specs=(pmod(2, to), pmod(3, to, 1), pmod(4, to, 1)), n_total=n_tot, row0=0,
                    fill_steps=-(-n_s // to), oa_transposed=True)
    x1, h2p, route = _outproj(oas, obs, xs, mod_s[2], mod_s[3], mod_s[4], *out_w, tm=n_s,
                              mod_specs=(smod, smod, smod), n_total=n_tot, row0=n_p, bufs=bufs)

    n_blocks = -(-(n_tot * TOP_K + N_EXPERTS * (tb - 1)) // tb)
    pos, pend_tab = _rank(route, tb=tb)
    pend, last_rows = pend_tab[0, :, 0], pend_tab[1, :, 0]
    block_row0 = jnp.arange(n_blocks, dtype=jnp.int32) * tb
    block_e = jnp.minimum(jnp.sum(pend[None, :] <= block_row0[:, None], axis=1), N_EXPERTS - 1).astype(jnp.int32)
    n_used = pend[N_EXPERTS - 1:] // tb
    experts = jnp.arange(N_EXPERTS, dtype=jnp.int32)
    has_rows = jnp.diff(pend, prepend=0) > 0
    later = jnp.where((experts[None, :] > experts[:, None]) & has_rows[None, :], experts[None, :], N_EXPERTS)
    next_with_rows = jnp.min(later, axis=1)
    next_with_rows = jnp.where(next_with_rows == N_EXPERTS, experts, next_with_rows)
    of_block = lambda per_expert: jnp.sum(jnp.where(block_e[:, None] == experts[None, :], per_expert[None, :], 0),
                                          axis=1).astype(jnp.int32)
    next_e = of_block(next_with_rows)
    half_full = ((block_row0 + tb == of_block(pend)) & (of_block(last_rows) <= tb // 2)).astype(jnp.int32)
    xsort = _dispatch(pos.reshape(-1), pend, h2p, tm=td, tb=tb, n_blocks=n_blocks)
    yb = _experts(block_e, next_e, half_full, n_used, xsort, w_expert_gate[0], w_expert_up[0],
                  w_expert_down[0], tb=tb)

    pos = pos.reshape(TOP_K, n_tot)
    pos_p = pos[:, :n_p].reshape(-1)
    pos_s = pos[:, n_p:].reshape(-1)
    gt_p = pl.BlockSpec((1, 1, d), lambda i, p: ((i // (t // tc)) * N_MOD + 5, 0, 0))
    gt_s = pl.BlockSpec((n_s, d), lambda i, p: (0, 0))
    y_p = _combine(pos_p, yb, x1, route, mod_tab, g_final, tm=tc, n=n_p, row0=0, gt_spec=gt_p)
    y_s = _combine(pos_s, yb, x1, route, mod_s[5], g_final, tm=n_s, n=n_s, row0=n_p, gt_spec=gt_s)

    kv_shape = (SWA_KV_HEADS, SWA_HEAD_DIM)
    last = lambda a: a.reshape(bp, t, KV_WIDTH)[:, t - WINDOW:, :].reshape(1, bp, WINDOW, *kv_shape)
    k_state_p, v_state_p = last(kp), last(vp)
    return (y_p.reshape(bp, t, d), y_s.reshape(bs, ts, d), k_state_p, v_state_p, sp[None],
            ks.reshape(bs, ts, *kv_shape)[None], vs.reshape(bs, ts, *kv_shape)[None], ss[None])
```

```python
import functools
import math

import jax
import jax.numpy as jnp
from jax import lax
from jax.experimental import pallas as pl
from jax.experimental.pallas import tpu as pltpu

f32 = jnp.float32
bf16 = jnp.bfloat16
u32 = jnp.uint32

D_MODEL = 2048
N_MOD = 6
EPS = 1e-6
NEG_INF = -1e30
LOG2_E = math.log2(math.e)

SWA_HEAD_DIM = 64
SWA_KV_HEADS = 2
SWA_GROUP = 8
SWA_WIDTH = SWA_KV_HEADS * SWA_GROUP * SWA_HEAD_DIM
KV_WIDTH = SWA_KV_HEADS * SWA_HEAD_DIM
WINDOW = 128
CHUNK = 64

GLA_HEADS = 4
GLA_DK = 128
GLA_DV = 256
GLA_QK_WIDTH = GLA_HEADS * GLA_DK
GLA_V_WIDTH = GLA_HEADS * GLA_DV
GLA_GATE_RANK = 16
GLA_GATE_NORM = 16.0

N_GROUPS = 4
EXPERTS_PER_GROUP = 8
N_EXPERTS = N_GROUPS * EXPERTS_PER_GROUP
TOP_K = 2
EXPERT_HIDDEN = D_MODEL // 4

_C_Q = 0
_C_K = _C_Q + SWA_WIDTH
_C_V = _C_K + KV_WIDTH
_C_GQ = _C_V + KV_WIDTH
_C_GK = _C_GQ + GLA_QK_WIDTH
_C_GV = _C_GK + GLA_QK_WIDTH
_C_R = _C_GV + GLA_V_WIDTH
_C_AB = _C_R + GLA_V_WIDTH

LANES = 128
SUBLANES = 8
PACKED_WIDTH = D_MODEL // 2
assert PACKED_WIDTH == SUBLANES * LANES
VMEM_LIMIT = 56 * 1024 * 1024

ROUTE_E0, ROUTE_E1, ROUTE_G0, ROUTE_G1 = 0, 1, 2, 3


def _dot(a, b):
    return jnp.dot(a, b, preferred_element_type=f32)


def _dot_nt(a, b):
    return lax.dot_general(a, b, (((1,), (1,)), ((), ())), preferred_element_type=f32)


def _dot_tn(a, b):
    return lax.dot_general(a, b, (((0,), (0,)), ((), ())), preferred_element_type=f32)


def _silu(x):
    return x / (1.0 + jnp.exp(-x))


def _rows(ref):
    v = ref[...]
    return v.reshape(v.shape[-2:])


def _rms(x):
    return x * lax.rsqrt(jnp.mean(x * x, axis=-1, keepdims=True) + EPS)


def _resident(shape):
    return pl.BlockSpec(shape, lambda *_: (0,) * len(shape), pipeline_mode=pl.Buffered(1))


def _mod_spec(comp, tiles_per_seq, n_tiles, lag=0):
    def index(i):
        tile = jnp.minimum(jnp.maximum(i - lag, 0), n_tiles - 1)
        return ((tile // tiles_per_seq) * N_MOD + comp, 0, 0)
    return pl.BlockSpec((1, 1, D_MODEL), index)


def _adaln_kernel(c_ref, w_ref, b_ref, o_ref):
    a = _silu(c_ref[...]).astype(bf16)
    o_ref[...] = _dot(a, w_ref[...].astype(bf16)) + b_ref[...]


def _adaln(c_all, w_ada, b_ada, tn=1024):
    r = c_all.shape[0]
    n = w_ada.shape[1]
    return pl.pallas_call(
        _adaln_kernel,
        out_shape=jax.ShapeDtypeStruct((r, n), f32),
        grid=(n // tn,),
        in_specs=[pl.BlockSpec((r, D_MODEL), lambda j: (0, 0)),
                  pl.BlockSpec((D_MODEL, tn), lambda j: (0, j)),
                  pl.BlockSpec((1, tn), lambda j: (0, j))],
        out_specs=pl.BlockSpec((r, tn), lambda j: (0, j)),
        compiler_params=pltpu.CompilerParams(dimension_semantics=("arbitrary",),
                                             vmem_limit_bytes=VMEM_LIMIT),
        name="adaln",
    )(c_all, w_ada, b_ada.reshape(1, n))


def _proj_kernel(x_ref, sh_ref, sc_ref, g_ref, wq_ref, w_ref, wab_ref, wg_ref, bg_ref,
                 q_ref, k_ref, v_ref, gq_ref, gk_ref, gv_ref, r_ref, lg_ref, *, q_transposed):
    h = (_rms(x_ref[...]) * g_ref[...]) * (1.0 + _rows(sc_ref)) + _rows(sh_ref)
    hb = h.astype(bf16)
    col = lambda c: c - _C_K
    ab = _dot(hb, wab_ref[...])
    r_ref[...] = _silu(_dot(hb, w_ref[:, col(_C_R):col(_C_AB)])).astype(bf16)
    gq_ref[...] = (_dot(hb, w_ref[:, col(_C_GQ):col(_C_GK)]) * (GLA_DK ** -0.5)).astype(bf16)
    z = _dot(ab.astype(bf16), wg_ref[...]) + bg_ref[...]
    log_sig = jnp.minimum(z, 0.0) - jnp.log1p(jnp.exp(-jnp.abs(z)))
    lg_ref[...] = log_sig * (LOG2_E / GLA_GATE_NORM)
    gk_ref[...] = _dot(hb, w_ref[:, col(_C_GK):col(_C_GV)]).astype(bf16)
    gv_ref[...] = _dot(hb, w_ref[:, col(_C_GV):col(_C_R)]).astype(bf16)
    kv = _dot(hb, w_ref[:, col(_C_K):col(_C_GQ)])
    k_ref[...] = kv[:, :KV_WIDTH]
    v_ref[...] = kv[:, KV_WIDTH:]
    if q_transposed:
        q_ref[...] = (_dot_nt(wq_ref[...], hb) * (LOG2_E * SWA_HEAD_DIM ** -0.5)).astype(bf16)
    else:
        q_ref[...] = _dot(hb, wq_ref[...]).astype(bf16)


def _proj(x2d, sh, sc, g_mix, w_q, w_rest, w_ab, w_gate, b_gate, *, tm, mod_specs, q_transposed):
    n = x2d.shape[0]
    row = lambda w: pl.BlockSpec((tm, w), lambda i: (i, 0))
    outs = [(KV_WIDTH, f32), (KV_WIDTH, f32), (GLA_QK_WIDTH, bf16),
            (GLA_QK_WIDTH, bf16), (GLA_V_WIDTH, bf16), (GLA_V_WIDTH, bf16), (GLA_QK_WIDTH, f32)]
    if q_transposed:
        q_shape, q_spec = (SWA_WIDTH, n), pl.BlockSpec((SWA_WIDTH, tm), lambda i: (0, i))
    else:
        q_shape, q_spec = (n, SWA_WIDTH), row(SWA_WIDTH)
    return pl.pallas_call(
        functools.partial(_proj_kernel, q_transposed=q_transposed),
        out_shape=[jax.ShapeDtypeStruct(q_shape, bf16)]
                  + [jax.ShapeDtypeStruct((n, w), dt) for w, dt in outs],
        grid=(n // tm,),
        in_specs=[row(D_MODEL), mod_specs[0], mod_specs[1],
                  _resident((1, D_MODEL)), _resident(w_q.shape), _resident(w_rest.shape),
                  _resident(w_ab.shape), _resident(w_gate.shape), _resident((1, GLA_QK_WIDTH))],
        out_specs=[q_spec] + [row(w) for w, _ in outs],
        compiler_params=pltpu.CompilerParams(dimension_semantics=("parallel",),
                                             vmem_limit_bytes=VMEM_LIMIT),
        name="proj",
    )(x2d, sh, sc, g_mix.reshape(1, D_MODEL), w_q, w_rest, w_ab, w_gate, b_gate.reshape(1, -1))


def _swa_kernel(q_ref, kp_ref, vp_ref, kc_ref, vc_ref, sink_ref, o_ref, *, tq):
    pad = jnp.zeros((WINDOW - tq, KV_WIDTH), f32)
    k_all = jnp.concatenate([kp_ref[...], kc_ref[...], pad], axis=0).astype(bf16)
    v_all = jnp.concatenate([vp_ref[...], vc_ref[...], pad], axis=0).astype(bf16)
    rows, cols = SWA_GROUP * tq, 2 * WINDOW
    valid = lax.broadcasted_iota(jnp.int32, (rows, cols), 1) < WINDOW + tq
    q = q_ref[...]
    outs = []
    for j in range(SWA_KV_HEADS):
        heads = [q[:, (j * SWA_GROUP + g) * SWA_HEAD_DIM:(j * SWA_GROUP + g + 1) * SWA_HEAD_DIM]
                 for g in range(SWA_GROUP)]
        qs = jnp.concatenate(heads, axis=0)
        kj = k_all[:, j * SWA_HEAD_DIM:(j + 1) * SWA_HEAD_DIM]
        vj = v_all[:, j * SWA_HEAD_DIM:(j + 1) * SWA_HEAD_DIM]
        s = _dot_nt(qs, kj) * (SWA_HEAD_DIM ** -0.5)
        s = jnp.where(valid, s, NEG_INF)
        sink = sink_ref[j]
        m = jnp.maximum(jnp.max(s, axis=1, keepdims=True), sink)
        p = jnp.exp(s - m)
        den = jnp.sum(p, axis=1, keepdims=True) + jnp.exp(sink - m)
        o = _dot(p.astype(bf16), vj) / den
        outs.append(jnp.concatenate([o[g * tq:(g + 1) * tq] for g in range(SWA_GROUP)], axis=1))
    o_ref[...] = jnp.concatenate(outs, axis=1).astype(bf16)


def _swa(q, k_past, v_past, k_new, v_new, sinks, *, n_seq, tq):
    sink_rows = jnp.repeat(sinks.astype(f32).reshape(SWA_KV_HEADS, SWA_GROUP), tq, axis=1)
    sink_rows = sink_rows.reshape(SWA_KV_HEADS, SWA_GROUP * tq, 1)
    new = lambda w: pl.BlockSpec((tq, w), lambda b: (b, 0))
    past = pl.BlockSpec((WINDOW, KV_WIDTH), lambda b: (b, 0))
    return pl.pallas_call(
        functools.partial(_swa_kernel, tq=tq),
        out_shape=jax.ShapeDtypeStruct(q.shape, bf16),
        grid=(n_seq,),
        in_specs=[new(SWA_WIDTH), past, past, new(KV_WIDTH), new(KV_WIDTH),
                  pl.BlockSpec(sink_rows.shape, lambda b: (0, 0, 0))],
        out_specs=new(SWA_WIDTH),
        compiler_params=pltpu.CompilerParams(dimension_semantics=("parallel",),
                                             vmem_limit_bytes=VMEM_LIMIT),
        name="swa",
    )(q, k_past, v_past, k_new, v_new, sink_rows)


def _swa_t_kernel(q_ref, kp_ref, vp_ref, kc_ref, vc_ref, sink_ref, mk_ref, mq0_ref, mq_ref, o_ref, *, n_sub):
    tq = WINDOW
    hd = SWA_HEAD_DIM
    k_all = jnp.concatenate([kp_ref[...], kc_ref[...]], axis=0).astype(bf16)
    v_all = jnp.concatenate([vp_ref[...], vc_ref[...]], axis=0).astype(bf16)
    low_lanes = lax.broadcasted_iota(jnp.int32, (2 * WINDOW, KV_WIDTH), 1) < hd
    ones = jnp.ones((2 * WINDOW, KV_WIDTH), bf16)
    for sub in range(n_sub):
        keys = slice(sub * tq, sub * tq + 2 * WINDOW)
        toks = slice(sub * tq, (sub + 1) * tq)
        mq = (mq0_ref if sub == 0 else mq_ref)[0]
        for j in range(SWA_KV_HEADS):
            head = lambda g: slice((j * SWA_GROUP + g) * hd, (j * SWA_GROUP + g + 1) * hd)
            qs = jnp.concatenate([q_ref[head(g), toks] for g in range(SWA_GROUP)], axis=1)
            own = low_lanes if j == 0 else jnp.logical_not(low_lanes)
            k_aug = jnp.where(own, k_all[keys], mk_ref[j])
            v_aug = jnp.where(own, v_all[keys], ones)
            q_aug = jnp.concatenate([qs, mq] if j == 0 else [mq, qs], axis=0)
            s = _dot(k_aug, q_aug)
            sink = sink_ref[j]
            m = jnp.maximum(jnp.max(s, axis=0, keepdims=True), sink)
            p = jnp.exp2(s - m).astype(bf16)
            o_aug = _dot_tn(v_aug, p)
            pv, p_sum = (o_aug[:hd], o_aug[hd:hd + 1]) if j == 0 else (o_aug[hd:], o_aug[0:1])
            o = pv / (p_sum + jnp.exp2(sink - m))
            for g in range(SWA_GROUP):
                o_ref[head(g), toks] = o[:, g * tq:(g + 1) * tq].astype(bf16)


def _swa_t(q_t, k, v, sinks, *, n_seq, tiles, n_sub):
    tq = WINDOW
    steps = tiles // n_sub
    hd = SWA_HEAD_DIM
    sink_cols = jnp.repeat(sinks.astype(f32).reshape(SWA_KV_HEADS, SWA_GROUP) * LOG2_E, tq, axis=1)
    sink_cols = sink_cols.reshape(SWA_KV_HEADS, 1, SWA_GROUP * tq)
    n_kc = 2 * WINDOW // CHUNK
    key_chunk = jnp.arange(2 * WINDOW) // CHUNK
    lane = jnp.arange(KV_WIDTH)
    mk = jnp.stack([lane[None, :] == hd + key_chunk[:, None], lane[None, :] == key_chunk[:, None]]).astype(bf16)
    q_chunk = (jnp.arange(SWA_GROUP * tq) % tq) // CHUNK
    kc = jnp.arange(hd)[:, None]
    band = (kc >= q_chunk[None, :]) & (kc <= q_chunk[None, :] + WINDOW // CHUNK)
    visible = jnp.stack([band & (kc >= WINDOW // CHUNK), band]) | (kc >= n_kc)
    mq = jnp.where(visible, 0.0, NEG_INF).astype(bf16)
    mq0_spec = pl.BlockSpec((1, hd, SWA_GROUP * tq), lambda b, u: (jnp.minimum(u, 1), 0, 0))
    mq_spec = pl.BlockSpec((1, hd, SWA_GROUP * tq), lambda b, u: (1, 0, 0))
    qspec = pl.BlockSpec((SWA_WIDTH, n_sub * tq), lambda b, u: (0, b * steps + u))
    cur = pl.BlockSpec((n_sub * tq, KV_WIDTH), lambda b, u: (b * steps + u, 0))
    prev = pl.BlockSpec((WINDOW, KV_WIDTH), lambda b, u: (b * tiles + jnp.maximum(n_sub * u - 1, 0), 0))
    return pl.pallas_call(
        functools.partial(_swa_t_kernel, n_sub=n_sub),
        out_shape=jax.ShapeDtypeStruct(q_t.shape, bf16),
        grid=(n_seq, steps),
        in_specs=[qspec, prev, prev, cur, cur, pl.BlockSpec(sink_cols.shape, lambda b, u: (0, 0, 0)),
                  pl.BlockSpec(mk.shape, lambda b, u: (0, 0, 0)), mq0_spec, mq_spec],
        out_specs=qspec,
        compiler_params=pltpu.CompilerParams(dimension_semantics=("parallel", "arbitrary"),
                                             vmem_limit_bytes=VMEM_LIMIT),
        name="swa_t",
    )(q_t, k, v, k, v, sink_cols, mk, mq, mq)


def _gla_kernel(q_ref, k_ref, v_ref, lg_ref, r_ref, gh_ref, s0_ref, o_ref, s_ref, *, c, n_sub):
    @pl.when(pl.program_id(1) == 0)
    def _():
        s_ref[...] = s0_ref[...]

    row = lax.broadcasted_iota(jnp.int32, (c, GLA_DK), 0)
    causal = (lax.broadcasted_iota(jnp.int32, (c, c), 0) >= lax.broadcasted_iota(jnp.int32, (c, c), 1))
    for sub in range(n_sub):
        ts = slice(sub * c, (sub + 1) * c)
        for h in range(GLA_HEADS):
            ks = slice(h * GLA_DK, (h + 1) * GLA_DK)
            vs = slice(h * GLA_DV, (h + 1) * GLA_DV)
            b = lg_ref[ts, ks]
            step = 1
            while step < c:
                b = b + jnp.where(row >= step, pltpu.roll(b, step, 0), 0.0)
                step *= 2
            b_last = b[c - 1:c, :]
            q = q_ref[ts, ks].astype(f32)
            k = k_ref[ts, ks].astype(f32)
            v = v_ref[ts, vs]
            qd = (q * jnp.exp2(b)).astype(bf16)
            kd = (k * jnp.exp2(-b)).astype(bf16)
            kl = (k * jnp.exp2(b_last - b)).astype(bf16)
            a = jnp.where(causal, _dot_nt(qd, kd), 0.0)
            s = s_ref[0, h]
            o = _dot(qd, s.astype(bf16)) + _dot(a.astype(bf16), v)
            decay = jnp.broadcast_to(jnp.exp2(b_last), (GLA_DK, GLA_DK)).T
            s_ref[0, h] = s * jnp.concatenate([decay, decay], axis=1) + _dot_tn(kl, v)
            on = _rms(o) * gh_ref[...]
            o_ref[ts, vs] = (on * r_ref[ts, vs].astype(f32)).astype(bf16)


def _gla(gq, gk, gv, lg, r, g_head, s0, *, n_seq, c, n_sub, steps):
    rows = c * n_sub
    blk = lambda w: pl.BlockSpec((rows, w), lambda b, t: (b * steps + t, 0))
    state = pl.BlockSpec((1, GLA_HEADS, GLA_DK, GLA_DV), lambda b, t: (b, 0, 0, 0))
    return pl.pallas_call(
        functools.partial(_gla_kernel, c=c, n_sub=n_sub),
        out_shape=[jax.ShapeDtypeStruct(gv.shape, bf16),
                   jax.ShapeDtypeStruct((n_seq, GLA_HEADS, GLA_DK, GLA_DV), f32)],
        grid=(n_seq, steps),
        in_specs=[blk(GLA_QK_WIDTH), blk(GLA_QK_WIDTH), blk(GLA_V_WIDTH), blk(GLA_QK_WIDTH),
                  blk(GLA_V_WIDTH), pl.BlockSpec((1, GLA_DV), lambda b, t: (0, 0)), state],
        out_specs=[blk(GLA_V_WIDTH), state],
        compiler_params=pltpu.CompilerParams(dimension_semantics=("parallel", "arbitrary"),
                                             vmem_limit_bytes=VMEM_LIMIT),
        name="gla",
    )(gq, gk, gv, lg, r, g_head.reshape(1, GLA_DV), s0)


def _pack_pair(hb):
    w = hb.shape[1] // 2
    a = lax.bitcast_convert_type(hb[:, :w].astype(f32), u32)
    b = lax.bitcast_convert_type(hb[:, w:].astype(f32), u32)
    return a | (b >> 16)


def _unpack_pair(p):
    a = lax.bitcast_convert_type(p & jnp.uint32(0xFFFF0000), f32)
    b = lax.bitcast_convert_type(p << 16, f32)
    return a, b


def _load_row_tiles(ref, n):
    return jnp.concatenate([ref[pl.ds(s, n, stride=SUBLANES), :] for s in range(SUBLANES)], axis=1)


def _store_row_tiles(ref, val):
    n = val.shape[0]
    for s in range(SUBLANES):
        ref[pl.ds(s, n, stride=SUBLANES), :] = val[:, s * LANES:(s + 1) * LANES]


def _row_tile(ref, r):
    return ref.at[pl.ds(pl.multiple_of(r * SUBLANES, SUBLANES), SUBLANES), :]


def _outproj_kernel(*refs, n_alias, oa_transposed):
    (oa_ref, ob_ref, x_ref, gt_ref, sh_ref, sc_ref, g_ref, w_ref, wr_ref, br_ref) = refs[:10]
    x1_ref, h2_ref, route_ref, x1_s = refs[10 + n_alias:]

    @pl.when(pl.program_id(0) == 0)
    def _():
        x1_s[...] = jnp.zeros(x1_s.shape, x1_s.dtype)

    h2 = (_rms(x1_s[...]) * g_ref[...]) * (1.0 + _rows(sc_ref)) + _rows(sh_ref)
    hi = h2.astype(bf16)
    _store_row_tiles(h2_ref, _pack_pair(hi))
    lo = (h2 - hi.astype(f32)).astype(bf16)
    r1 = _dot(hi, wr_ref[...])
    logits = r1[:, :LANES] + r1[:, LANES:] + _dot(lo, wr_ref[:, :LANES]) + br_ref[...]
    lane = lax.broadcasted_iota(jnp.int32, logits.shape, 1)
    lane_f = lane.astype(f32)
    neg = float("-inf")
    first = lambda hit: jnp.min(jnp.where(hit, lane_f, float(LANES)), axis=1, keepdims=True)
    lg_g = jnp.where(lane < N_GROUPS, logits, neg)
    g_max = jnp.max(lg_g, axis=1, keepdims=True)
    g_sel = first(lg_g == g_max)
    p_sel = 1.0 / jnp.sum(jnp.exp(lg_g - g_max), axis=1, keepdims=True)
    e_lane = lane - N_GROUPS
    in_group = (e_lane >= 0) & (e_lane < N_EXPERTS) & ((e_lane // EXPERTS_PER_GROUP).astype(f32) == g_sel)
    lg_e = jnp.where(in_group, logits, neg)
    v1 = jnp.max(lg_e, axis=1, keepdims=True)
    i1 = first(lg_e == v1)
    lg_e2 = jnp.where(lane_f == i1, neg, lg_e)
    v2 = jnp.max(lg_e2, axis=1, keepdims=True)
    i2 = first(lg_e2 == v2)
    e = jnp.exp(v2 - v1)
    g1 = p_sel / (1.0 + e)
    g2 = p_sel * e / (1.0 + e)
    rec = jnp.where(lane == ROUTE_E0, i1 - N_GROUPS, 0.0)
    rec = jnp.where(lane == ROUTE_E1, i2 - N_GROUPS, rec)
    rec = jnp.where(lane == ROUTE_G0, g1, rec)
    rec = jnp.where(lane == ROUTE_G1, g2, rec)
    route_ref[...] = rec

    oa_dot = _dot_tn if oa_transposed else _dot
    mix = oa_dot(oa_ref[...], w_ref[:SWA_WIDTH, :]) + _dot(ob_ref[...], w_ref[SWA_WIDTH:, :])
    x1 = x_ref[...] + _rows(gt_ref) * mix
    x1_ref[...] = x1
    x1_s[...] = x1


def _outproj(oa, ob, x2d, gt, sh, sc, g_ffn, w_out, w_route, b_route, *, tm, mod_specs,
             n_total, row0, bufs=None, fill_steps=0, oa_transposed=False):
    n = x2d.shape[0]
    tiles = n // tm
    blocks = tiles + fill_steps
    row = lambda w: pl.BlockSpec((tm, w), lambda i: (jnp.minimum(i, tiles - 1), 0))
    oa_spec = row(SWA_WIDTH)
    if oa_transposed:
        oa_spec = pl.BlockSpec((SWA_WIDTH, tm), lambda i: (0, jnp.minimum(i, tiles - 1)))
    head_blk = lambda i: row0 // tm + jnp.minimum(i, blocks - 1)
    tail_blk = lambda i: row0 // tm + jnp.maximum(i - 1, 0)
    out_x1 = pl.BlockSpec((tm, D_MODEL), lambda i: (head_blk(i), 0))
    out_tiles = pl.BlockSpec((tm * SUBLANES, LANES), lambda i: (tail_blk(i), 0))
    out_route = pl.BlockSpec((tm, LANES), lambda i: (tail_blk(i), 0))
    alias_in = list(bufs) if bufs is not None else []
    n_in = 10
    return pl.pallas_call(
        functools.partial(_outproj_kernel, n_alias=len(alias_in), oa_transposed=oa_transposed),
        out_shape=[jax.ShapeDtypeStruct((n_total, D_MODEL), f32),
                   jax.ShapeDtypeStruct((n_total * SUBLANES, LANES), u32),
                   jax.ShapeDtypeStruct((n_total, LANES), f32)],
        grid=(blocks + 1,),
        in_specs=[oa_spec, row(GLA_V_WIDTH), row(D_MODEL), mod_specs[0], mod_specs[1],
                  mod_specs[2], _resident((1, D_MODEL)), _resident(w_out.shape),
                  _resident(w_route.shape), _resident((1, LANES))]
                 + [pl.BlockSpec(memory_space=pl.ANY)] * len(alias_in),
        out_specs=[out_x1, out_tiles, out_route],
        scratch_shapes=[pltpu.VMEM((tm, D_MODEL), f32)],
        input_output_aliases={n_in + a: a for a in range(len(alias_in))},
        compiler_params=pltpu.CompilerParams(dimension_semantics=("arbitrary",),
                                             vmem_limit_bytes=VMEM_LIMIT),
        name="outproj",
    )(oa, ob, x2d, gt, sh, sc, g_ffn.reshape(1, D_MODEL), w_out, w_route, b_route, *alias_in)


def _rank_kernel(route_ref, pos_ref, pend_ref, rec_s, *, n_chunks, tb):
    n_e = N_EXPERTS
    expert = lax.broadcasted_iota(jnp.int32, (n_e, LANES), 0)
    expert_f = expert.astype(f32)
    earlier = (lax.broadcasted_iota(jnp.int32, (LANES, LANES), 0)
               < lax.broadcasted_iota(jnp.int32, (LANES, LANES), 1)).astype(bf16)

    def onehots(c):
        rec = rec_s[c]
        return expert_f == rec[ROUTE_E0:ROUTE_E0 + 1, :], expert_f == rec[ROUTE_E1:ROUTE_E1 + 1, :]

    def count(c, cnt):
        rec_s[c] = route_ref[pl.ds(pl.multiple_of(c * LANES, LANES), LANES), :].T[:SUBLANES, :]
        h0, h1 = onehots(c)
        return cnt + jnp.sum((h0 | h1).astype(f32), axis=1, keepdims=True)

    cnt = lax.fori_loop(0, n_chunks, count, jnp.zeros((n_e, 1), f32), unroll=4)
    padded = jnp.floor((cnt + (tb - 1.0)) / tb) * tb
    end = jnp.broadcast_to(padded, (n_e, LANES))
    step = 1
    while step < n_e:
        end = end + jnp.where(expert >= step, pltpu.roll(end, step, 0), 0.0)
        step *= 2
    pend_ref[0] = end.astype(jnp.int32)
    pend_ref[1] = jnp.broadcast_to(jnp.where(cnt > 0, cnt - (padded - tb), 0.0), (n_e, LANES)).astype(jnp.int32)

    def rank(c, base):
        h0, h1 = onehots(c)
        both = h0 | h1
        off = _dot(both.astype(bf16), earlier) + base
        pos_ref[0, pl.ds(c, 1), :] = jnp.sum(jnp.where(h0, off, 0.0), axis=0, keepdims=True).astype(jnp.int32)
        pos_ref[1, pl.ds(c, 1), :] = jnp.sum(jnp.where(h1, off, 0.0), axis=0, keepdims=True).astype(jnp.int32)
        return base + jnp.sum(both.astype(f32), axis=1, keepdims=True)

    lax.fori_loop(0, n_chunks, rank, (end - padded)[:, 0:1], unroll=4)


def _rank(route, *, tb):
    n = route.shape[0]
    n_chunks = n // LANES
    return pl.pallas_call(
        functools.partial(_rank_kernel, n_chunks=n_chunks, tb=tb),
        out_shape=[jax.ShapeDtypeStruct((TOP_K, n_chunks, LANES), jnp.int32),
                   jax.ShapeDtypeStruct((2, N_EXPERTS, LANES), jnp.int32)],
        in_specs=[_resident(route.shape)],
        scratch_shapes=[pltpu.VMEM((n_chunks, SUBLANES, LANES), f32)],
        compiler_params=pltpu.CompilerParams(vmem_limit_bytes=VMEM_LIMIT),
        name="rank",
    )(route)


def _dispatch_kernel(pos_ref, pend_ref, h2_ref, xs_hbm, stage, zbuf, sem, zsem, *, tm, tb, n_tot, steps):
    i = pl.program_id(0)
    slot = i % 2

    blk = tb * SUBLANES

    def wait_slot(s):
        for _ in range(TOP_K):
            pltpu.make_async_copy(stage.at[s], xs_hbm.at[pl.ds(0, tm * SUBLANES), :], sem.at[s]).wait()

    @pl.when(i == 0)
    def _():
        zbuf[...] = jnp.zeros(zbuf.shape, zbuf.dtype)

        def fill_block(b, start):
            cp = pltpu.make_async_copy(zbuf, xs_hbm.at[pl.ds(pl.multiple_of(b * blk, blk), blk), :], zsem)
            cp.start() if start else cp.wait()

        def fill(e, start):
            end = pend_ref[e]
            prev = jnp.where(e > 0, pend_ref[jnp.maximum(e - 1, 0)], 0)

            @pl.when(end > prev)
            def _():
                fill_block(end // tb - 1, start)

        fill_unused = fill_block

        first_unused = pend_ref[N_EXPERTS - 1] // tb
        n_blocks = xs_hbm.shape[0] // blk
        lax.fori_loop(0, N_EXPERTS, lambda e, c: (fill(e, True), c)[1], 0)
        lax.fori_loop(first_unused, n_blocks, lambda b, c: (fill_unused(b, True), c)[1], 0)
        lax.fori_loop(0, N_EXPERTS, lambda e, c: (fill(e, False), c)[1], 0)
        lax.fori_loop(first_unused, n_blocks, lambda b, c: (fill_unused(b, False), c)[1], 0)

    @pl.when(i >= 2)
    def _():
        wait_slot(slot)

    stage[slot] = h2_ref[...]

    def scatter(r, carry):
        for k in range(TOP_K):
            d = pos_ref[k * n_tot + i * tm + r]
            pltpu.make_async_copy(_row_tile(stage.at[slot], r), _row_tile(xs_hbm, d),
                                  sem.at[slot]).start(priority=k)
        return carry

    lax.fori_loop(0, tm, scatter, 0, unroll=8)

    @pl.when(i == steps - 1)
    def _():
        wait_slot(slot)
        if steps > 1:
            wait_slot(1 - slot)


def _dispatch(pos_flat, pend, h2p, *, tm, tb, n_blocks):
    n_tot = h2p.shape[0] // SUBLANES
    steps = n_tot // tm
    return pl.pallas_call(
        functools.partial(_dispatch_kernel, tm=tm, tb=tb, n_tot=n_tot, steps=steps),
        out_shape=jax.ShapeDtypeStruct((n_blocks * tb * SUBLANES, LANES), u32),
        grid_spec=pltpu.PrefetchScalarGridSpec(
            num_scalar_prefetch=2,
            grid=(steps,),
            in_specs=[pl.BlockSpec((tm * SUBLANES, LANES), lambda i, pos, pend: (i, 0))],
            out_specs=pl.BlockSpec(memory_space=pl.ANY),
            scratch_shapes=[pltpu.VMEM((2, tm * SUBLANES, LANES), u32),
                            pltpu.VMEM((tb * SUBLANES, LANES), u32),
                            pltpu.SemaphoreType.DMA((2,)), pltpu.SemaphoreType.DMA(())]),
        compiler_params=pltpu.CompilerParams(dimension_semantics=("arbitrary",),
                                             vmem_limit_bytes=VMEM_LIMIT),
        name="dispatch",
    )(pos_flat, pend, h2p)


def _expert_kernel(be_ref, ne_ref, hf_ref, nu_ref, xs_ref, wg_hbm, wu_hbm, wd_hbm, y_ref,
                   wg_s, wu_s, wd_s, wg_b, wu_b, wd_b, sem, *, tb):
    i = pl.program_id(0)
    used = i < nu_ref[0]
    e = be_ref[i]

    def fetch(expert):
        pairs = ((wg_hbm, wg_s), (wu_hbm, wu_s), (wd_hbm, wd_s))
        return [pltpu.make_async_copy(src.at[expert], dst, sem.at[n]) for n, (src, dst) in enumerate(pairs)]

    @pl.when(i == 0)
    def _():
        for cp in fetch(e):
            cp.start()

    @pl.when(used & ((i == 0) | (e != be_ref[jnp.maximum(i - 1, 0)])))
    def _():
        for cp in fetch(e):
            cp.wait()
        wg_b[...] = wg_s[...].astype(bf16)
        wu_b[...] = wu_s[...].astype(bf16)
        wd_b[...] = wd_s[...].astype(bf16)
        nxt = ne_ref[i]

        @pl.when(nxt != e)
        def _():
            for cp in fetch(nxt):
                cp.start()

    def ffn(rows):
        tiles = pl.ds(0, rows * SUBLANES)
        a, b = _unpack_pair(_load_row_tiles(xs_ref.at[tiles], rows))
        x = jnp.concatenate([a.astype(bf16), b.astype(bf16)], axis=1)
        g = _dot(x, wg_b[...])
        u = _dot(x, wu_b[...])
        y = _dot((_silu(g) * u).astype(bf16), wd_b[...])
        _store_row_tiles(y_ref.at[tiles], _pack_pair(y.astype(bf16)))

    half_full = hf_ref[i] == 1

    @pl.when(used & jnp.logical_not(half_full))
    def _():
        ffn(tb)

    @pl.when(used & half_full)
    def _():
        ffn(tb // 2)
        rest = pl.ds(tb // 2 * SUBLANES, tb // 2 * SUBLANES)
        y_ref[rest, :] = jnp.zeros((tb // 2 * SUBLANES, LANES), y_ref.dtype)

    @pl.when(jnp.logical_not(used))
    def _():
        y_ref[...] = jnp.zeros(y_ref.shape, y_ref.dtype)


def _experts(block_e, next_e, half_full, n_used, xs, w_eg, w_eu, w_ed, *, tb):
    n_blocks = block_e.shape[0]
    blk = (tb * SUBLANES, LANES)
    up, down = (D_MODEL, EXPERT_HIDDEN), (EXPERT_HIDDEN, D_MODEL)
    hbm = pl.BlockSpec(memory_space=pl.ANY)
    return pl.pallas_call(
        functools.partial(_expert_kernel, tb=tb),
        out_shape=jax.ShapeDtypeStruct(xs.shape, u32),
        grid_spec=pltpu.PrefetchScalarGridSpec(
            num_scalar_prefetch=4,
            grid=(n_blocks,),
            in_specs=[pl.BlockSpec(blk, lambda i, be, ne, hf, nu: (jnp.minimum(i, nu[0] - 1), 0)),
                      hbm, hbm, hbm],
            out_specs=pl.BlockSpec(blk, lambda i, be, ne, hf, nu: (i, 0)),
            scratch_shapes=[pltpu.VMEM(up, f32), pltpu.VMEM(up, f32), pltpu.VMEM(down, f32),
                            pltpu.VMEM(up, bf16), pltpu.VMEM(up, bf16), pltpu.VMEM(down, bf16),
                            pltpu.SemaphoreType.DMA((3,))]),
        compiler_params=pltpu.CompilerParams(dimension_semantics=("arbitrary",),
                                             vmem_limit_bytes=VMEM_LIMIT),
        name="experts",
    )(block_e, next_e, half_full, n_used, xs, w_eg, w_eu, w_ed)


def _combine_kernel(pos_ref, yb_hbm, x1_ref, route_ref, gt_ref, gf_ref, y_ref, ybuf, sem, *, tm):
    i = pl.program_id(0)
    slot = i % 2

    def start(blk, s):
        def body(r, carry):
            for k in range(TOP_K):
                t = pos_ref[(k * pl.num_programs(0) + blk) * tm + r]
                pltpu.make_async_copy(_row_tile(yb_hbm, t), _row_tile(ybuf.at[s, k], r),
                                      sem.at[s]).start(priority=k)
            return carry
        lax.fori_loop(0, tm, body, 0, unroll=8)

    @pl.when(i == 0)
    def _():
        start(0, 0)

    @pl.when(i + 1 < pl.num_programs(0))
    def _():
        start(i + 1, 1 - slot)

    for k in range(TOP_K):
        pltpu.make_async_copy(yb_hbm.at[pl.ds(0, tm * SUBLANES), :], ybuf.at[slot, k], sem.at[slot]).wait()
    route = route_ref[...]
    a0, b0 = _unpack_pair(_load_row_tiles(ybuf.at[slot, 0], tm))
    a1, b1 = _unpack_pair(_load_row_tiles(ybuf.at[slot, 1], tm))
    g0 = route[:, ROUTE_G0:ROUTE_G0 + 1]
    g1 = route[:, ROUTE_G1:ROUTE_G1 + 1]
    moe = jnp.concatenate([a0 * g0 + a1 * g1, b0 * g0 + b1 * g1], axis=1)
    x2 = x1_ref[...] + _rows(gt_ref) * moe
    y_ref[...] = _rms(x2) * gf_ref[...]


def _combine(pos_km, yb, x1, route, gt, g_final, *, tm, n, row0, gt_spec):
    blk0 = row0 // tm
    return pl.pallas_call(
        functools.partial(_combine_kernel, tm=tm),
        out_shape=jax.ShapeDtypeStruct((n, D_MODEL), f32),
        grid_spec=pltpu.PrefetchScalarGridSpec(
            num_scalar_prefetch=1,
            grid=(n // tm,),
            in_specs=[pl.BlockSpec(memory_space=pl.ANY),
                      pl.BlockSpec((tm, D_MODEL), lambda i, pos: (blk0 + i, 0)),
                      pl.BlockSpec((tm, LANES), lambda i, pos: (blk0 + i, 0)),
                      gt_spec,
                      pl.BlockSpec((1, D_MODEL), lambda i, pos: (0, 0))],
            out_specs=pl.BlockSpec((tm, D_MODEL), lambda i, pos: (i, 0)),
            scratch_shapes=[pltpu.VMEM((2, TOP_K, tm * SUBLANES, LANES), u32),
                            pltpu.SemaphoreType.DMA((2,))]),
        compiler_params=pltpu.CompilerParams(dimension_semantics=("arbitrary",),
                                             vmem_limit_bytes=VMEM_LIMIT),
        name="combine",
    )(pos_km, yb, x1, route, gt, g_final.reshape(1, D_MODEL))


def kernel(x_prompt, x_sample, cache_swa_k, cache_swa_v, state_gla, c_prompt, c_sample, g_mix_norm, g_ffn_norm, w_ada, b_ada, w_in, attn_sinks, w_gla_gate, b_gla_gate, g_gla_norm, w_out, w_router_group, b_router_group, w_router_expert, b_router_expert, w_expert_gate, w_expert_up, w_expert_down, g_final):
    depth = w_in.shape[0]
    assert depth == 1
    bp, t, d = x_prompt.shape
    bs, ts, _ = x_sample.shape
    n_p, n_s = bp * t, bs * ts
    n_tot = n_p + n_s
    assert n_tot % LANES == 0
    tm = 512
    to = 256
    tb = 512
    tc = 256
    td = LANES
    gla_c = 128
    gla_sub = 16
    swa_sub = 8

    w_in0 = w_in[0]
    w_q = w_in0[:, :_C_K].astype(bf16)
    w_q_t = w_q.T
    w_rest = w_in0[:, _C_K:_C_AB].astype(bf16)
    w_ab = jnp.pad(w_in0[:, _C_AB:], ((0, 0), (0, LANES - GLA_GATE_RANK))).astype(bf16)
    w_gate = jnp.pad(w_gla_gate[0], ((0, LANES - GLA_GATE_RANK), (0, 0))).astype(bf16)
    w_out_b = w_out[0].astype(bf16)
    n_r = N_GROUPS + N_EXPERTS
    w_r = jnp.pad(jnp.concatenate([w_router_group[0], w_router_expert[0]], axis=1),
                  ((0, 0), (0, LANES - n_r)))
    w_r_hi = w_r.astype(bf16)
    w_r_lo = (w_r - w_r_hi.astype(f32)).astype(bf16)
    w_route = jnp.concatenate([w_r_hi, w_r_lo], axis=1)
    b_route = jnp.pad(jnp.concatenate([b_router_group[0], b_router_expert[0]]),
                      (0, LANES - n_r)).reshape(1, LANES)

    c_all = jnp.concatenate([c_prompt, c_sample], axis=0)
    mod = _adaln(c_all, w_ada[0], b_ada[0])
    mod_tab = mod.reshape((bp + bs) * N_MOD, 1, d)
    mod_s = jnp.repeat(mod[bp:].reshape(bs, N_MOD, d), ts, axis=0)
    mod_s = [mod_s[:, m] for m in range(N_MOD)]
    pmod = lambda comp, tile, lag=0: _mod_spec(comp, t // tile, n_p // tile, lag)
    smod = pl.BlockSpec((n_s, d), lambda i: (0, 0))

    xp = x_prompt.reshape(n_p, d)
    xs = x_sample.reshape(n_s, d)
    proj_w = (w_rest, w_ab, w_gate, b_gla_gate[0])
    qp, kp, vp, gqp, gkp, gvp, rp, lgp = _proj(xp, mod_tab, mod_tab, g_mix_norm[0], w_q_t, *proj_w, tm=tm,
                                               mod_specs=(pmod(0, tm), pmod(1, tm)), q_transposed=True)
    qs, ks, vs, gqs, gks, gvs, rs, lgs = _proj(xs, mod_s[0], mod_s[1], g_mix_norm[0], w_q, *proj_w, tm=n_s,
                                               mod_specs=(smod, smod), q_transposed=False)

    sinks = attn_sinks[0]
    oap = _swa_t(qp, kp, vp, sinks, n_seq=bp, tiles=t // WINDOW, n_sub=swa_sub)
    ck = cache_swa_k[0].reshape(bs * WINDOW, KV_WIDTH)
    cv = cache_swa_v[0].reshape(bs * WINDOW, KV_WIDTH)
    oas = _swa(qs, ck, cv, ks, vs, sinks, n_seq=bs, tq=ts)
    s_zero = jnp.zeros((bp, GLA_HEADS, GLA_DK, GLA_DV), f32)
    obp, sp = _gla(gqp, gkp, gvp, lgp, rp, g_gla_norm[0], s_zero, n_seq=bp, c=gla_c, n_sub=gla_sub,
                   steps=t // (gla_c * gla_sub))
    obs, ss = _gla(gqs, gks, gvs, lgs, rs, g_gla_norm[0], state_gla[0], n_seq=bs, c=ts, n_sub=1, steps=1)

    out_w = (g_ffn_norm[0], w_out_b, w_route, b_route)
    bufs = _outproj(oap, obp, xp, mod_tab, mod_tab, mod_tab, *out_w, tm=to,
                    mod_specs=(pmod(2, to), pmod(3, to, 1), pmod(4, to, 1)), n_total=n_tot, row0=0,
                    fill_steps=-(-n_s // to), oa_transposed=True)
    x1, h2p, route = _outproj(oas, obs, xs, mod_s[2], mod_s[3], mod_s[4], *out_w, tm=n_s,
                              mod_specs=(smod, smod, smod), n_total=n_tot, row0=n_p, bufs=bufs)

    n_blocks = -(-(n_tot * TOP_K + N_EXPERTS * (tb - 1)) // tb)
    pos, pend_tab = _rank(route, tb=tb)
    pend, last_rows = pend_tab[0, :, 0], pend_tab[1, :, 0]
    block_row0 = jnp.arange(n_blocks, dtype=jnp.int32) * tb
    block_e = jnp.minimum(jnp.sum(pend[None, :] <= block_row0[:, None], axis=1), N_EXPERTS - 1).astype(jnp.int32)
    n_used = pend[N_EXPERTS - 1:] // tb
    experts = jnp.arange(N_EXPERTS, dtype=jnp.int32)
    has_rows = jnp.diff(pend, prepend=0) > 0
    later = jnp.where((experts[None, :] > experts[:, None]) & has_rows[None, :], experts[None, :], N_EXPERTS)
    next_with_rows = jnp.min(later, axis=1)
    next_with_rows = jnp.where(next_with_rows == N_EXPERTS, experts, next_with_rows)
    of_block = lambda per_expert: jnp.sum(jnp.where(block_e[:, None] == experts[None, :], per_expert[None, :], 0),
                                          axis=1).astype(jnp.int32)
    next_e = of_block(next_with_rows)
    half_full = ((block_row0 + tb == of_block(pend)) & (of_block(last_rows) <= tb // 2)).astype(jnp.int32)
    xsort = _dispatch(pos.reshape(-1), pend, h2p, tm=td, tb=tb, n_blocks=n_blocks)
    yb = _experts(block_e, next_e, half_full, n_used, xsort, w_expert_gate[0], w_expert_up[0],
                  w_expert_down[0], tb=tb)

    pos = pos.reshape(TOP_K, n_tot)
    pos_p = pos[:, :n_p].reshape(-1)
    pos_s = pos[:, n_p:].reshape(-1)
    gt_p = pl.BlockSpec((1, 1, d), lambda i, p: ((i // (t // tc)) * N_MOD + 5, 0, 0))
    gt_s = pl.BlockSpec((n_s, d), lambda i, p: (0, 0))
    y_p = _combine(pos_p, yb, x1, route, mod_tab, g_final, tm=tc, n=n_p, row0=0, gt_spec=gt_p)
    y_s = _combine(pos_s, yb, x1, route, mod_s[5], g_final, tm=n_s, n=n_s, row0=n_p, gt_spec=gt_s)

    kv_shape = (SWA_KV_HEADS, SWA_HEAD_DIM)
    last = lambda a: a.reshape(bp, t, KV_WIDTH)[:, t - WINDOW:, :].reshape(1, bp, WINDOW, *kv_shape)
    k_state_p, v_state_p = last(kp), last(vp)
    return (y_p.reshape(bp, t, d), y_s.reshape(bs, ts, d), k_state_p, v_state_p, sp[None],
            ks.reshape(bs, ts, *kv_shape)[None], vs.reshape(bs, ts, *kv_shape)[None], ss[None])
```

```python
import functools
import math

import jax
import jax.numpy as jnp
from jax import lax
from jax.experimental import pallas as pl
from jax.experimental.pallas import tpu as pltpu

f32 = jnp.float32
bf16 = jnp.bfloat16
u32 = jnp.uint32

D_MODEL = 2048
N_MOD = 6
EPS = 1e-6
NEG_INF = -1e30
LOG2_E = math.log2(math.e)

SWA_HEAD_DIM = 64
SWA_KV_HEADS = 2
SWA_GROUP = 8
SWA_WIDTH = SWA_KV_HEADS * SWA_GROUP * SWA_HEAD_DIM
KV_WIDTH = SWA_KV_HEADS * SWA_HEAD_DIM
WINDOW = 128
CHUNK = 64

GLA_HEADS = 4
GLA_DK = 128
GLA_DV = 256
GLA_QK_WIDTH = GLA_HEADS * GLA_DK
GLA_V_WIDTH = GLA_HEADS * GLA_DV
GLA_GATE_RANK = 16
GLA_GATE_NORM = 16.0

N_GROUPS = 4
EXPERTS_PER_GROUP = 8
N_EXPERTS = N_GROUPS * EXPERTS_PER_GROUP
TOP_K = 2
EXPERT_HIDDEN = D_MODEL // 4

_C_Q = 0
_C_K = _C_Q + SWA_WIDTH
_C_V = _C_K + KV_WIDTH
_C_GQ = _C_V + KV_WIDTH
_C_GK = _C_GQ + GLA_QK_WIDTH
_C_GV = _C_GK + GLA_QK_WIDTH
_C_R = _C_GV + GLA_V_WIDTH
_C_AB = _C_R + GLA_V_WIDTH

LANES = 128
SUBLANES = 8
PACKED_WIDTH = D_MODEL // 2
assert PACKED_WIDTH == SUBLANES * LANES
VMEM_LIMIT = 56 * 1024 * 1024

ROUTE_E0, ROUTE_E1, ROUTE_G0, ROUTE_G1 = 0, 1, 2, 3


def _dot(a, b):
    return jnp.dot(a, b, preferred_element_type=f32)


def _dot_nt(a, b):
    return lax.dot_general(a, b, (((1,), (1,)), ((), ())), preferred_element_type=f32)


def _dot_tn(a, b):
    return lax.dot_general(a, b, (((0,), (0,)), ((), ())), preferred_element_type=f32)


def _silu(x):
    return x / (1.0 + jnp.exp(-x))


def _rows(ref):
    v = ref[...]
    return v.reshape(v.shape[-2:])


def _rms(x):
    return x * lax.rsqrt(jnp.mean(x * x, axis=-1, keepdims=True) + EPS)


def _resident(shape):
    return pl.BlockSpec(shape, lambda *_: (0,) * len(shape), pipeline_mode=pl.Buffered(1))


def _mod_spec(comp, tiles_per_seq, n_tiles, lag=0):
    def index(i):
        tile = jnp.minimum(jnp.maximum(i - lag, 0), n_tiles - 1)
        return ((tile // tiles_per_seq) * N_MOD + comp, 0, 0)
    return pl.BlockSpec((1, 1, D_MODEL), index)


def _adaln_kernel(c_ref, w_ref, b_ref, o_ref):
    a = _silu(c_ref[...]).astype(bf16)
    o_ref[...] = _dot(a, w_ref[...].astype(bf16)) + b_ref[...]


def _adaln(c_all, w_ada, b_ada, tn=1024):
    r = c_all.shape[0]
    n = w_ada.shape[1]
    return pl.pallas_call(
        _adaln_kernel,
        out_shape=jax.ShapeDtypeStruct((r, n), f32),
        grid=(n // tn,),
        in_specs=[pl.BlockSpec((r, D_MODEL), lambda j: (0, 0)),
                  pl.BlockSpec((D_MODEL, tn), lambda j: (0, j)),
                  pl.BlockSpec((1, tn), lambda j: (0, j))],
        out_specs=pl.BlockSpec((r, tn), lambda j: (0, j)),
        compiler_params=pltpu.CompilerParams(dimension_semantics=("arbitrary",),
                                             vmem_limit_bytes=VMEM_LIMIT),
        name="adaln",
    )(c_all, w_ada, b_ada.reshape(1, n))


def _proj_kernel(x_ref, sh_ref, sc_ref, g_ref, wq_ref, w_ref, wab_ref, wg_ref, bg_ref,
                 q_ref, k_ref, v_ref, gq_ref, gk_ref, gv_ref, r_ref, lg_ref, *, q_transposed):
    h = (_rms(x_ref[...]) * g_ref[...]) * (1.0 + _rows(sc_ref)) + _rows(sh_ref)
    hb = h.astype(bf16)
    col = lambda c: c - _C_K
    ab = _dot(hb, wab_ref[...])
    r_ref[...] = _silu(_dot(hb, w_ref[:, col(_C_R):col(_C_AB)])).astype(bf16)
    gq_ref[...] = (_dot(hb, w_ref[:, col(_C_GQ):col(_C_GK)]) * (GLA_DK ** -0.5)).astype(bf16)
    z = _dot(ab.astype(bf16), wg_ref[...]) + bg_ref[...]
    log_sig = jnp.minimum(z, 0.0) - jnp.log1p(jnp.exp(-jnp.abs(z)))
    lg_ref[...] = log_sig * (LOG2_E / GLA_GATE_NORM)
    gk_ref[...] = _dot(hb, w_ref[:, col(_C_GK):col(_C_GV)]).astype(bf16)
    gv_ref[...] = _dot(hb, w_ref[:, col(_C_GV):col(_C_R)]).astype(bf16)
    kv = _dot(hb, w_ref[:, col(_C_K):col(_C_GQ)])
    k_ref[...] = kv[:, :KV_WIDTH]
    v_ref[...] = kv[:, KV_WIDTH:]
    if q_transposed:
        q_ref[...] = (_dot_nt(wq_ref[...], hb) * (LOG2_E * SWA_HEAD_DIM ** -0.5)).astype(bf16)
    else:
        q_ref[...] = _dot(hb, wq_ref[...]).astype(bf16)


def _proj(x2d, sh, sc, g_mix, w_q, w_rest, w_ab, w_gate, b_gate, *, tm, mod_specs, q_transposed):
    n = x2d.shape[0]
    row = lambda w: pl.BlockSpec((tm, w), lambda i: (i, 0))
    outs = [(KV_WIDTH, f32), (KV_WIDTH, f32), (GLA_QK_WIDTH, bf16),
            (GLA_QK_WIDTH, bf16), (GLA_V_WIDTH, bf16), (GLA_V_WIDTH, bf16), (GLA_QK_WIDTH, f32)]
    if q_transposed:
        q_shape, q_spec = (SWA_WIDTH, n), pl.BlockSpec((SWA_WIDTH, tm), lambda i: (0, i))
    else:
        q_shape, q_spec = (n, SWA_WIDTH), row(SWA_WIDTH)
    return pl.pallas_call(
        functools.partial(_proj_kernel, q_transposed=q_transposed),
        out_shape=[jax.ShapeDtypeStruct(q_shape, bf16)]
                  + [jax.ShapeDtypeStruct((n, w), dt) for w, dt in outs],
        grid=(n // tm,),
        in_specs=[row(D_MODEL), mod_specs[0], mod_specs[1],
                  _resident((1, D_MODEL)), _resident(w_q.shape), _resident(w_rest.shape),
                  _resident(w_ab.shape), _resident(w_gate.shape), _resident((1, GLA_QK_WIDTH))],
        out_specs=[q_spec] + [row(w) for w, _ in outs],
        compiler_params=pltpu.CompilerParams(dimension_semantics=("parallel",),
                                             vmem_limit_bytes=VMEM_LIMIT),
        name="proj",
    )(x2d, sh, sc, g_mix.reshape(1, D_MODEL), w_q, w_rest, w_ab, w_gate, b_gate.reshape(1, -1))


def _swa_kernel(q_ref, kp_ref, vp_ref, kc_ref, vc_ref, sink_ref, o_ref, *, tq):
    pad = jnp.zeros((WINDOW - tq, KV_WIDTH), f32)
    k_all = jnp.concatenate([kp_ref[...], kc_ref[...], pad], axis=0).astype(bf16)
    v_all = jnp.concatenate([vp_ref[...], vc_ref[...], pad], axis=0).astype(bf16)
    rows, cols = SWA_GROUP * tq, 2 * WINDOW
    valid = lax.broadcasted_iota(jnp.int32, (rows, cols), 1) < WINDOW + tq
    q = q_ref[...]
    outs = []
    for j in range(SWA_KV_HEADS):
        heads = [q[:, (j * SWA_GROUP + g) * SWA_HEAD_DIM:(j * SWA_GROUP + g + 1) * SWA_HEAD_DIM]
                 for g in range(SWA_GROUP)]
        qs = jnp.concatenate(heads, axis=0)
        kj = k_all[:, j * SWA_HEAD_DIM:(j + 1) * SWA_HEAD_DIM]
        vj = v_all[:, j * SWA_HEAD_DIM:(j + 1) * SWA_HEAD_DIM]
        s = _dot_nt(qs, kj) * (SWA_HEAD_DIM ** -0.5)
        s = jnp.where(valid, s, NEG_INF)
        sink = sink_ref[j]
        m = jnp.maximum(jnp.max(s, axis=1, keepdims=True), sink)
        p = jnp.exp(s - m)
        den = jnp.sum(p, axis=1, keepdims=True) + jnp.exp(sink - m)
        o = _dot(p.astype(bf16), vj) / den
        outs.append(jnp.concatenate([o[g * tq:(g + 1) * tq] for g in range(SWA_GROUP)], axis=1))
    o_ref[...] = jnp.concatenate(outs, axis=1).astype(bf16)


def _swa(q, k_past, v_past, k_new, v_new, sinks, *, n_seq, tq):
    sink_rows = jnp.repeat(sinks.astype(f32).reshape(SWA_KV_HEADS, SWA_GROUP), tq, axis=1)
    sink_rows = sink_rows.reshape(SWA_KV_HEADS, SWA_GROUP * tq, 1)
    new = lambda w: pl.BlockSpec((tq, w), lambda b: (b, 0))
    past = pl.BlockSpec((WINDOW, KV_WIDTH), lambda b: (b, 0))
    return pl.pallas_call(
        functools.partial(_swa_kernel, tq=tq),
        out_shape=jax.ShapeDtypeStruct(q.shape, bf16),
        grid=(n_seq,),
        in_specs=[new(SWA_WIDTH), past, past, new(KV_WIDTH), new(KV_WIDTH),
                  pl.BlockSpec(sink_rows.shape, lambda b: (0, 0, 0))],
        out_specs=new(SWA_WIDTH),
        compiler_params=pltpu.CompilerParams(dimension_semantics=("parallel",),
                                             vmem_limit_bytes=VMEM_LIMIT),
        name="swa",
    )(q, k_past, v_past, k_new, v_new, sink_rows)


def _swa_t_kernel(q_ref, kp_ref, vp_ref, kc_ref, vc_ref, sink_ref, mk_ref, mq0_ref, mq_ref, o_ref, *, n_sub):
    tq = WINDOW
    hd = SWA_HEAD_DIM
    k_all = jnp.concatenate([kp_ref[...], kc_ref[...]], axis=0).astype(bf16)
    v_all = jnp.concatenate([vp_ref[...], vc_ref[...]], axis=0).astype(bf16)
    low_lanes = lax.broadcasted_iota(jnp.int32, (2 * WINDOW, KV_WIDTH), 1) < hd
    ones = jnp.ones((2 * WINDOW, KV_WIDTH), bf16)
    for sub in range(n_sub):
        keys = slice(sub * tq, sub * tq + 2 * WINDOW)
        toks = slice(sub * tq, (sub + 1) * tq)
        mq = (mq0_ref if sub == 0 else mq_ref)[0]
        for j in range(SWA_KV_HEADS):
            head = lambda g: slice((j * SWA_GROUP + g) * hd, (j * SWA_GROUP + g + 1) * hd)
            qs = jnp.concatenate([q_ref[head(g), toks] for g in range(SWA_GROUP)], axis=1)
            own = low_lanes if j == 0 else jnp.logical_not(low_lanes)
            k_aug = jnp.where(own, k_all[keys], mk_ref[j])
            v_aug = jnp.where(own, v_all[keys], ones)
            q_aug = jnp.concatenate([qs, mq] if j == 0 else [mq, qs], axis=0)
            s = _dot(k_aug, q_aug)
            sink = sink_ref[j]
            m = jnp.maximum(jnp.max(s, axis=0, keepdims=True), sink)
            p = jnp.exp2(s - m).astype(bf16)
            o_aug = _dot_tn(v_aug, p)
            pv, p_sum = (o_aug[:hd], o_aug[hd:hd + 1]) if j == 0 else (o_aug[hd:], o_aug[0:1])
            o = pv / (p_sum + jnp.exp2(sink - m))
            for g in range(SWA_GROUP):
                o_ref[head(g), toks] = o[:, g * tq:(g + 1) * tq].astype(bf16)


def _swa_t(q_t, k, v, sinks, *, n_seq, tiles, n_sub):
    tq = WINDOW
    steps = tiles // n_sub
    hd = SWA_HEAD_DIM
    sink_cols = jnp.repeat(sinks.astype(f32).reshape(SWA_KV_HEADS, SWA_GROUP) * LOG2_E, tq, axis=1)
    sink_cols = sink_cols.reshape(SWA_KV_HEADS, 1, SWA_GROUP * tq)
    n_kc = 2 * WINDOW // CHUNK
    key_chunk = jnp.arange(2 * WINDOW) // CHUNK
    lane = jnp.arange(KV_WIDTH)
    mk = jnp.stack([lane[None, :] == hd + key_chunk[:, None], lane[None, :] == key_chunk[:, None]]).astype(bf16)
    q_chunk = (jnp.arange(SWA_GROUP * tq) % tq) // CHUNK
    kc = jnp.arange(hd)[:, None]
    band = (kc >= q_chunk[None, :]) & (kc <= q_chunk[None, :] + WINDOW // CHUNK)
    visible = jnp.stack([band & (kc >= WINDOW // CHUNK), band]) | (kc >= n_kc)
    mq = jnp.where(visible, 0.0, NEG_INF).astype(bf16)
    mq0_spec = pl.BlockSpec((1, hd, SWA_GROUP * tq), lambda b, u: (jnp.minimum(u, 1), 0, 0))
    mq_spec = pl.BlockSpec((1, hd, SWA_GROUP * tq), lambda b, u: (1, 0, 0))
    qspec = pl.BlockSpec((SWA_WIDTH, n_sub * tq), lambda b, u: (0, b * steps + u))
    cur = pl.BlockSpec((n_sub * tq, KV_WIDTH), lambda b, u: (b * steps + u, 0))
    prev = pl.BlockSpec((WINDOW, KV_WIDTH), lambda b, u: (b * tiles + jnp.maximum(n_sub * u - 1, 0), 0))
    return pl.pallas_call(
        functools.partial(_swa_t_kernel, n_sub=n_sub),
        out_shape=jax.ShapeDtypeStruct(q_t.shape, bf16),
        grid=(n_seq, steps),
        in_specs=[qspec, prev, prev, cur, cur, pl.BlockSpec(sink_cols.shape, lambda b, u: (0, 0, 0)),
                  pl.BlockSpec(mk.shape, lambda b, u: (0, 0, 0)), mq0_spec, mq_spec],
        out_specs=qspec,
        compiler_params=pltpu.CompilerParams(dimension_semantics=("parallel", "arbitrary"),
                                             vmem_limit_bytes=VMEM_LIMIT),
        name="swa_t",
    )(q_t, k, v, k, v, sink_cols, mk, mq, mq)


def _gla_kernel(q_ref, k_ref, v_ref, lg_ref, r_ref, gh_ref, s0_ref, o_ref, s_ref, *, c, n_sub):
    @pl.when(pl.program_id(1) == 0)
    def _():
        s_ref[...] = s0_ref[...]

    row = lax.broadcasted_iota(jnp.int32, (c, GLA_DK), 0)
    causal = (lax.broadcasted_iota(jnp.int32, (c, c), 0) >= lax.broadcasted_iota(jnp.int32, (c, c), 1))
    for sub in range(n_sub):
        ts = slice(sub * c, (sub + 1) * c)
        for h in range(GLA_HEADS):
            ks = slice(h * GLA_DK, (h + 1) * GLA_DK)
            vs = slice(h * GLA_DV, (h + 1) * GLA_DV)
            b = lg_ref[ts, ks]
            step = 1
            while step < c:
                b = b + jnp.where(row >= step, pltpu.roll(b, step, 0), 0.0)
                step *= 2
            b_last = b[c - 1:c, :]
            q = q_ref[ts, ks].astype(f32)
            k = k_ref[ts, ks].astype(f32)
            v = v_ref[ts, vs]
            qd = (q * jnp.exp2(b)).astype(bf16)
            kd = (k * jnp.exp2(-b)).astype(bf16)
            kl = (k * jnp.exp2(b_last - b)).astype(bf16)
            a = jnp.where(causal, _dot_nt(qd, kd), 0.0)
            s = s_ref[0, h]
            o = _dot(qd, s.astype(bf16)) + _dot(a.astype(bf16), v)
            decay = jnp.broadcast_to(jnp.exp2(b_last), (GLA_DK, GLA_DK)).T
            s_ref[0, h] = s * jnp.concatenate([decay, decay], axis=1) + _dot_tn(kl, v)
            on = _rms(o) * gh_ref[...]
            o_ref[ts, vs] = (on * r_ref[ts, vs].astype(f32)).astype(bf16)


def _gla(gq, gk, gv, lg, r, g_head, s0, *, n_seq, c, n_sub, steps):
    rows = c * n_sub
    blk = lambda w: pl.BlockSpec((rows, w), lambda b, t: (b * steps + t, 0))
    state = pl.BlockSpec((1, GLA_HEADS, GLA_DK, GLA_DV), lambda b, t: (b, 0, 0, 0))
    return pl.pallas_call(
        functools.partial(_gla_kernel, c=c, n_sub=n_sub),
        out_shape=[jax.ShapeDtypeStruct(gv.shape, bf16),
                   jax.ShapeDtypeStruct((n_seq, GLA_HEADS, GLA_DK, GLA_DV), f32)],
        grid=(n_seq, steps),
        in_specs=[blk(GLA_QK_WIDTH), blk(GLA_QK_WIDTH), blk(GLA_V_WIDTH), blk(GLA_QK_WIDTH),
                  blk(GLA_V_WIDTH), pl.BlockSpec((1, GLA_DV), lambda b, t: (0, 0)), state],
        out_specs=[blk(GLA_V_WIDTH), state],
        compiler_params=pltpu.CompilerParams(dimension_semantics=("parallel", "arbitrary"),
                                             vmem_limit_bytes=VMEM_LIMIT),
        name="gla",
    )(gq, gk, gv, lg, r, g_head.reshape(1, GLA_DV), s0)


def _pack_pair(hb):
    w = hb.shape[1] // 2
    a = lax.bitcast_convert_type(hb[:, :w].astype(f32), u32)
    b = lax.bitcast_convert_type(hb[:, w:].astype(f32), u32)
    return a | (b >> 16)


def _unpack_pair(p):
    a = lax.bitcast_convert_type(p & jnp.uint32(0xFFFF0000), f32)
    b = lax.bitcast_convert_type(p << 16, f32)
    return a, b


def _load_row_tiles(ref, n):
    return jnp.concatenate([ref[pl.ds(s, n, stride=SUBLANES), :] for s in range(SUBLANES)], axis=1)


def _store_row_tiles(ref, val):
    n = val.shape[0]
    for s in range(SUBLANES):
        ref[pl.ds(s, n, stride=SUBLANES), :] = val[:, s * LANES:(s + 1) * LANES]


def _row_tile(ref, r):
    return ref.at[pl.ds(pl.multiple_of(r * SUBLANES, SUBLANES), SUBLANES), :]


def _outproj_kernel(*refs, n_alias, oa_transposed):
    (oa_ref, ob_ref, x_ref, gt_ref, sh_ref, sc_ref, g_ref, w_ref, wr_ref, br_ref) = refs[:10]
    x1_ref, h2_ref, route_ref, x1_s = refs[10 + n_alias:]

    @pl.when(pl.program_id(0) == 0)
    def _():
        x1_s[...] = jnp.zeros(x1_s.shape, x1_s.dtype)

    h2 = (_rms(x1_s[...]) * g_ref[...]) * (1.0 + _rows(sc_ref)) + _rows(sh_ref)
    hi = h2.astype(bf16)
    _store_row_tiles(h2_ref, _pack_pair(hi))
    logits = _dot(hi, wr_ref[...]) + br_ref[...]
    lane = lax.broadcasted_iota(jnp.int32, logits.shape, 1)
    lane_f = lane.astype(f32)
    neg = float("-inf")
    first = lambda hit: jnp.min(jnp.where(hit, lane_f, float(LANES)), axis=1, keepdims=True)
    lg_g = jnp.where(lane < N_GROUPS, logits, neg)
    g_max = jnp.max(lg_g, axis=1, keepdims=True)
    g_sel = first(lg_g == g_max)
    p_sel = 1.0 / jnp.sum(jnp.exp(lg_g - g_max), axis=1, keepdims=True)
    e_lane = lane - N_GROUPS
    in_group = (e_lane >= 0) & (e_lane < N_EXPERTS) & ((e_lane // EXPERTS_PER_GROUP).astype(f32) == g_sel)
    lg_e = jnp.where(in_group, logits, neg)
    v1 = jnp.max(lg_e, axis=1, keepdims=True)
    i1 = first(lg_e == v1)
    lg_e2 = jnp.where(lane_f == i1, neg, lg_e)
    v2 = jnp.max(lg_e2, axis=1, keepdims=True)
    i2 = first(lg_e2 == v2)
    e = jnp.exp(v2 - v1)
    g1 = p_sel / (1.0 + e)
    g2 = p_sel * e / (1.0 + e)
    rec = jnp.where(lane == ROUTE_E0, i1 - N_GROUPS, 0.0)
    rec = jnp.where(lane == ROUTE_E1, i2 - N_GROUPS, rec)
    rec = jnp.where(lane == ROUTE_G0, g1, rec)
    rec = jnp.where(lane == ROUTE_G1, g2, rec)
    route_ref[...] = rec

    oa_dot = _dot_tn if oa_transposed else _dot
    mix = oa_dot(oa_ref[...], w_ref[:SWA_WIDTH, :]) + _dot(ob_ref[...], w_ref[SWA_WIDTH:, :])
    x1 = x_ref[...] + _rows(gt_ref) * mix
    x1_ref[...] = x1
    x1_s[...] = x1


def _outproj(oa, ob, x2d, gt, sh, sc, g_ffn, w_out, w_route, b_route, *, tm, mod_specs,
             n_total, row0, bufs=None, fill_steps=0, oa_transposed=False):
    n = x2d.shape[0]
    tiles = n // tm
    blocks = tiles + fill_steps
    row = lambda w: pl.BlockSpec((tm, w), lambda i: (jnp.minimum(i, tiles - 1), 0))
    oa_spec = row(SWA_WIDTH)
    if oa_transposed:
        oa_spec = pl.BlockSpec((SWA_WIDTH, tm), lambda i: (0, jnp.minimum(i, tiles - 1)))
    head_blk = lambda i: row0 // tm + jnp.minimum(i, blocks - 1)
    tail_blk = lambda i: row0 // tm + jnp.maximum(i - 1, 0)
    out_x1 = pl.BlockSpec((tm, D_MODEL), lambda i: (head_blk(i), 0))
    out_tiles = pl.BlockSpec((tm * SUBLANES, LANES), lambda i: (tail_blk(i), 0))
    out_route = pl.BlockSpec((tm, LANES), lambda i: (tail_blk(i), 0))
    alias_in = list(bufs) if bufs is not None else []
    n_in = 10
    return pl.pallas_call(
        functools.partial(_outproj_kernel, n_alias=len(alias_in), oa_transposed=oa_transposed),
        out_shape=[jax.ShapeDtypeStruct((n_total, D_MODEL), f32),
                   jax.ShapeDtypeStruct((n_total * SUBLANES, LANES), u32),
                   jax.ShapeDtypeStruct((n_total, LANES), f32)],
        grid=(blocks + 1,),
        in_specs=[oa_spec, row(GLA_V_WIDTH), row(D_MODEL), mod_specs[0], mod_specs[1],
                  mod_specs[2], _resident((1, D_MODEL)), _resident(w_out.shape),
                  _resident(w_route.shape), _resident((1, LANES))]
                 + [pl.BlockSpec(memory_space=pl.ANY)] * len(alias_in),
        out_specs=[out_x1, out_tiles, out_route],
        scratch_shapes=[pltpu.VMEM((tm, D_MODEL), f32)],
        input_output_aliases={n_in + a: a for a in range(len(alias_in))},
        compiler_params=pltpu.CompilerParams(dimension_semantics=("arbitrary",),
                                             vmem_limit_bytes=VMEM_LIMIT),
        name="outproj",
    )(oa, ob, x2d, gt, sh, sc, g_ffn.reshape(1, D_MODEL), w_out, w_route, b_route, *alias_in)


def _rank_kernel(route_ref, pos_ref, pend_ref, rec_s, *, n_chunks, tb):
    n_e = N_EXPERTS
    expert = lax.broadcasted_iota(jnp.int32, (n_e, LANES), 0)
    expert_f = expert.astype(f32)
    earlier = (lax.broadcasted_iota(jnp.int32, (LANES, LANES), 0)
               < lax.broadcasted_iota(jnp.int32, (LANES, LANES), 1)).astype(bf16)

    def onehots(c):
        rec = rec_s[c]
        return expert_f == rec[ROUTE_E0:ROUTE_E0 + 1, :], expert_f == rec[ROUTE_E1:ROUTE_E1 + 1, :]

    def count(c, cnt):
        rec_s[c] = route_ref[pl.ds(pl.multiple_of(c * LANES, LANES), LANES), :].T[:SUBLANES, :]
        h0, h1 = onehots(c)
        return cnt + jnp.sum((h0 | h1).astype(f32), axis=1, keepdims=True)

    cnt = lax.fori_loop(0, n_chunks, count, jnp.zeros((n_e, 1), f32), unroll=4)
    padded = jnp.floor((cnt + (tb - 1.0)) / tb) * tb
    end = jnp.broadcast_to(padded, (n_e, LANES))
    step = 1
    while step < n_e:
        end = end + jnp.where(expert >= step, pltpu.roll(end, step, 0), 0.0)
        step *= 2
    pend_ref[0] = end.astype(jnp.int32)
    pend_ref[1] = jnp.broadcast_to(jnp.where(cnt > 0, cnt - (padded - tb), 0.0), (n_e, LANES)).astype(jnp.int32)

    def rank(c, base):
        h0, h1 = onehots(c)
        both = h0 | h1
        off = _dot(both.astype(bf16), earlier) + base
        pos_ref[0, pl.ds(c, 1), :] = jnp.sum(jnp.where(h0, off, 0.0), axis=0, keepdims=True).astype(jnp.int32)
        pos_ref[1, pl.ds(c, 1), :] = jnp.sum(jnp.where(h1, off, 0.0), axis=0, keepdims=True).astype(jnp.int32)
        return base + jnp.sum(both.astype(f32), axis=1, keepdims=True)

    lax.fori_loop(0, n_chunks, rank, (end - padded)[:, 0:1], unroll=4)


def _rank(route, *, tb):
    n = route.shape[0]
    n_chunks = n // LANES
    return pl.pallas_call(
        functools.partial(_rank_kernel, n_chunks=n_chunks, tb=tb),
        out_shape=[jax.ShapeDtypeStruct((TOP_K, n_chunks, LANES), jnp.int32),
                   jax.ShapeDtypeStruct((2, N_EXPERTS, LANES), jnp.int32)],
        in_specs=[_resident(route.shape)],
        scratch_shapes=[pltpu.VMEM((n_chunks, SUBLANES, LANES), f32)],
        compiler_params=pltpu.CompilerParams(vmem_limit_bytes=VMEM_LIMIT),
        name="rank",
    )(route)


def _dispatch_kernel(pos_ref, pend_ref, h2_ref, xs_hbm, stage, zbuf, sem, zsem, *, tm, tb, n_tot, steps):
    i = pl.program_id(0)
    slot = i % 2

    blk = tb * SUBLANES

    def wait_slot(s):
        for _ in range(TOP_K):
            pltpu.make_async_copy(stage.at[s], xs_hbm.at[pl.ds(0, tm * SUBLANES), :], sem.at[s]).wait()

    @pl.when(i == 0)
    def _():
        zbuf[...] = jnp.zeros(zbuf.shape, zbuf.dtype)

        def fill_block(b, start):
            cp = pltpu.make_async_copy(zbuf, xs_hbm.at[pl.ds(pl.multiple_of(b * blk, blk), blk), :], zsem)
            cp.start() if start else cp.wait()

        def fill(e, start):
            end = pend_ref[e]
            prev = jnp.where(e > 0, pend_ref[jnp.maximum(e - 1, 0)], 0)

            @pl.when(end > prev)
            def _():
                fill_block(end // tb - 1, start)

        fill_unused = fill_block

        first_unused = pend_ref[N_EXPERTS - 1] // tb
        n_blocks = xs_hbm.shape[0] // blk
        lax.fori_loop(0, N_EXPERTS, lambda e, c: (fill(e, True), c)[1], 0)
        lax.fori_loop(first_unused, n_blocks, lambda b, c: (fill_unused(b, True), c)[1], 0)
        lax.fori_loop(0, N_EXPERTS, lambda e, c: (fill(e, False), c)[1], 0)
        lax.fori_loop(first_unused, n_blocks, lambda b, c: (fill_unused(b, False), c)[1], 0)

    @pl.when(i >= 2)
    def _():
        wait_slot(slot)

    stage[slot] = h2_ref[...]

    def scatter(r, carry):
        for k in range(TOP_K):
            d = pos_ref[k * n_tot + i * tm + r]
            pltpu.make_async_copy(_row_tile(stage.at[slot], r), _row_tile(xs_hbm, d),
                                  sem.at[slot]).start(priority=k)
        return carry

    lax.fori_loop(0, tm, scatter, 0, unroll=8)

    @pl.when(i == steps - 1)
    def _():
        wait_slot(slot)
        if steps > 1:
            wait_slot(1 - slot)


def _dispatch(pos_flat, pend, h2p, *, tm, tb, n_blocks):
    n_tot = h2p.shape[0] // SUBLANES
    steps = n_tot // tm
    return pl.pallas_call(
        functools.partial(_dispatch_kernel, tm=tm, tb=tb, n_tot=n_tot, steps=steps),
        out_shape=jax.ShapeDtypeStruct((n_blocks * tb * SUBLANES, LANES), u32),
        grid_spec=pltpu.PrefetchScalarGridSpec(
            num_scalar_prefetch=2,
            grid=(steps,),
            in_specs=[pl.BlockSpec((tm * SUBLANES, LANES), lambda i, pos, pend: (i, 0))],
            out_specs=pl.BlockSpec(memory_space=pl.ANY),
            scratch_shapes=[pltpu.VMEM((2, tm * SUBLANES, LANES), u32),
                            pltpu.VMEM((tb * SUBLANES, LANES), u32),
                            pltpu.SemaphoreType.DMA((2,)), pltpu.SemaphoreType.DMA(())]),
        compiler_params=pltpu.CompilerParams(dimension_semantics=("arbitrary",),
                                             vmem_limit_bytes=VMEM_LIMIT),
        name="dispatch",
    )(pos_flat, pend, h2p)


def _expert_kernel(be_ref, ne_ref, hf_ref, nu_ref, xs_ref, wg_hbm, wu_hbm, wd_hbm, y_ref,
                   wg_s, wu_s, wd_s, wg_b, wu_b, wd_b, sem, *, tb):
    i = pl.program_id(0)
    used = i < nu_ref[0]
    e = be_ref[i]

    def fetch(expert):
        pairs = ((wg_hbm, wg_s), (wu_hbm, wu_s), (wd_hbm, wd_s))
        return [pltpu.make_async_copy(src.at[expert], dst, sem.at[n]) for n, (src, dst) in enumerate(pairs)]

    @pl.when(i == 0)
    def _():
        for cp in fetch(e):
            cp.start()

    @pl.when(used & ((i == 0) | (e != be_ref[jnp.maximum(i - 1, 0)])))
    def _():
        for cp in fetch(e):
            cp.wait()
        wg_b[...] = wg_s[...].astype(bf16)
        wu_b[...] = wu_s[...].astype(bf16)
        wd_b[...] = wd_s[...].astype(bf16)
        nxt = ne_ref[i]

        @pl.when(nxt != e)
        def _():
            for cp in fetch(nxt):
                cp.start()

    def ffn(rows):
        tiles = pl.ds(0, rows * SUBLANES)
        a, b = _unpack_pair(_load_row_tiles(xs_ref.at[tiles], rows))
        x = jnp.concatenate([a.astype(bf16), b.astype(bf16)], axis=1)
        g = _dot(x, wg_b[...])
        u = _dot(x, wu_b[...])
        y = _dot((_silu(g) * u).astype(bf16), wd_b[...])
        _store_row_tiles(y_ref.at[tiles], _pack_pair(y.astype(bf16)))

    half_full = hf_ref[i] == 1

    @pl.when(used & jnp.logical_not(half_full))
    def _():
        ffn(tb)

    @pl.when(used & half_full)
    def _():
        ffn(tb // 2)
        rest = pl.ds(tb // 2 * SUBLANES, tb // 2 * SUBLANES)
        y_ref[rest, :] = jnp.zeros((tb // 2 * SUBLANES, LANES), y_ref.dtype)

    @pl.when(jnp.logical_not(used))
    def _():
        y_ref[...] = jnp.zeros(y_ref.shape, y_ref.dtype)


def _experts(block_e, next_e, half_full, n_used, xs, w_eg, w_eu, w_ed, *, tb):
    n_blocks = block_e.shape[0]
    blk = (tb * SUBLANES, LANES)
    up, down = (D_MODEL, EXPERT_HIDDEN), (EXPERT_HIDDEN, D_MODEL)
    hbm = pl.BlockSpec(memory_space=pl.ANY)
    return pl.pallas_call(
        functools.partial(_expert_kernel, tb=tb),
        out_shape=jax.ShapeDtypeStruct(xs.shape, u32),
        grid_spec=pltpu.PrefetchScalarGridSpec(
            num_scalar_prefetch=4,
            grid=(n_blocks,),
            in_specs=[pl.BlockSpec(blk, lambda i, be, ne, hf, nu: (jnp.minimum(i, nu[0] - 1), 0)),
                      hbm, hbm, hbm],
            out_specs=pl.BlockSpec(blk, lambda i, be, ne, hf, nu: (i, 0)),
            scratch_shapes=[pltpu.VMEM(up, f32), pltpu.VMEM(up, f32), pltpu.VMEM(down, f32),
                            pltpu.VMEM(up, bf16), pltpu.VMEM(up, bf16), pltpu.VMEM(down, bf16),
                            pltpu.SemaphoreType.DMA((3,))]),
        compiler_params=pltpu.CompilerParams(dimension_semantics=("arbitrary",),
                                             vmem_limit_bytes=VMEM_LIMIT),
        name="experts",
    )(block_e, next_e, half_full, n_used, xs, w_eg, w_eu, w_ed)


def _combine_kernel(pos_ref, yb_hbm, x1_ref, route_ref, gt_ref, gf_ref, y_ref, ybuf, sem, *, tm):
    i = pl.program_id(0)
    slot = i % 2

    def start(blk, s):
        def body(r, carry):
            for k in range(TOP_K):
                t = pos_ref[(k * pl.num_programs(0) + blk) * tm + r]
                pltpu.make_async_copy(_row_tile(yb_hbm, t), _row_tile(ybuf.at[s, k], r),
                                      sem.at[s]).start(priority=k)
            return carry
        lax.fori_loop(0, tm, body, 0, unroll=8)

    @pl.when(i == 0)
    def _():
        start(0, 0)

    @pl.when(i + 1 < pl.num_programs(0))
    def _():
        start(i + 1, 1 - slot)

    for k in range(TOP_K):
        pltpu.make_async_copy(yb_hbm.at[pl.ds(0, tm * SUBLANES), :], ybuf.at[slot, k], sem.at[slot]).wait()
    route = route_ref[...]
    a0, b0 = _unpack_pair(_load_row_tiles(ybuf.at[slot, 0], tm))
    a1, b1 = _unpack_pair(_load_row_tiles(ybuf.at[slot, 1], tm))
    g0 = route[:, ROUTE_G0:ROUTE_G0 + 1]
    g1 = route[:, ROUTE_G1:ROUTE_G1 + 1]
    moe = jnp.concatenate([a0 * g0 + a1 * g1, b0 * g0 + b1 * g1], axis=1)
    x2 = x1_ref[...] + _rows(gt_ref) * moe
    y_ref[...] = _rms(x2) * gf_ref[...]


def _combine(pos_km, yb, x1, route, gt, g_final, *, tm, n, row0, gt_spec):
    blk0 = row0 // tm
    return pl.pallas_call(
        functools.partial(_combine_kernel, tm=tm),
        out_shape=jax.ShapeDtypeStruct((n, D_MODEL), f32),
        grid_spec=pltpu.PrefetchScalarGridSpec(
            num_scalar_prefetch=1,
            grid=(n // tm,),
            in_specs=[pl.BlockSpec(memory_space=pl.ANY),
                      pl.BlockSpec((tm, D_MODEL), lambda i, pos: (blk0 + i, 0)),
                      pl.BlockSpec((tm, LANES), lambda i, pos: (blk0 + i, 0)),
                      gt_spec,
                      pl.BlockSpec((1, D_MODEL), lambda i, pos: (0, 0))],
            out_specs=pl.BlockSpec((tm, D_MODEL), lambda i, pos: (i, 0)),
            scratch_shapes=[pltpu.VMEM((2, TOP_K, tm * SUBLANES, LANES), u32),
                            pltpu.SemaphoreType.DMA((2,))]),
        compiler_params=pltpu.CompilerParams(dimension_semantics=("arbitrary",),
                                             vmem_limit_bytes=VMEM_LIMIT),
        name="combine",
    )(pos_km, yb, x1, route, gt, g_final.reshape(1, D_MODEL))


def kernel(x_prompt, x_sample, cache_swa_k, cache_swa_v, state_gla, c_prompt, c_sample, g_mix_norm, g_ffn_norm, w_ada, b_ada, w_in, attn_sinks, w_gla_gate, b_gla_gate, g_gla_norm, w_out, w_router_group, b_router_group, w_router_expert, b_router_expert, w_expert_gate, w_expert_up, w_expert_down, g_final):
    depth = w_in.shape[0]
    assert depth == 1
    bp, t, d = x_prompt.shape
    bs, ts, _ = x_sample.shape
    n_p, n_s = bp * t, bs * ts
    n_tot = n_p + n_s
    assert n_tot % LANES == 0
    tm = 512
    to = 256
    tb = 512
    tc = 256
    td = LANES
    gla_c = 128
    gla_sub = 16
    swa_sub = 8

    w_in0 = w_in[0]
    w_q = w_in0[:, :_C_K].astype(bf16)
    w_q_t = w_q.T
    w_rest = w_in0[:, _C_K:_C_AB].astype(bf16)
    w_ab = jnp.pad(w_in0[:, _C_AB:], ((0, 0), (0, LANES - GLA_GATE_RANK))).astype(bf16)
    w_gate = jnp.pad(w_gla_gate[0], ((0, LANES - GLA_GATE_RANK), (0, 0))).astype(bf16)
    w_out_b = w_out[0].astype(bf16)
    n_r = N_GROUPS + N_EXPERTS
    w_r = jnp.pad(jnp.concatenate([w_router_group[0], w_router_expert[0]], axis=1),
                  ((0, 0), (0, LANES - n_r)))
    w_route = w_r.astype(bf16)
    b_route = jnp.pad(jnp.concatenate([b_router_group[0], b_router_expert[0]]),
                      (0, LANES - n_r)).reshape(1, LANES)

    c_all = jnp.concatenate([c_prompt, c_sample], axis=0)
    mod = _adaln(c_all, w_ada[0], b_ada[0])
    mod_tab = mod.reshape((bp + bs) * N_MOD, 1, d)
    mod_s = jnp.repeat(mod[bp:].reshape(bs, N_MOD, d), ts, axis=0)
    mod_s = [mod_s[:, m] for m in range(N_MOD)]
    pmod = lambda comp, tile, lag=0: _mod_spec(comp, t // tile, n_p // tile, lag)
    smod = pl.BlockSpec((n_s, d), lambda i: (0, 0))

    xp = x_prompt.reshape(n_p, d)
    xs = x_sample.reshape(n_s, d)
    proj_w = (w_rest, w_ab, w_gate, b_gla_gate[0])
    qp, kp, vp, gqp, gkp, gvp, rp, lgp = _proj(xp, mod_tab, mod_tab, g_mix_norm[0], w_q_t, *proj_w, tm=tm,
                                               mod_specs=(pmod(0, tm), pmod(1, tm)), q_transposed=True)
    qs, ks, vs, gqs, gks, gvs, rs, lgs = _proj(xs, mod_s[0], mod_s[1], g_mix_norm[0], w_q, *proj_w, tm=n_s,
                                               mod_specs=(smod, smod), q_transposed=False)

    sinks = attn_sinks[0]
    oap = _swa_t(qp, kp, vp, sinks, n_seq=bp, tiles=t // WINDOW, n_sub=swa_sub)
    ck = cache_swa_k[0].reshape(bs * WINDOW, KV_WIDTH)
    cv = cache_swa_v[0].reshape(bs * WINDOW, KV_WIDTH)
    oas = _swa(qs, ck, cv, ks, vs, sinks, n_seq=bs, tq=ts)
    s_zero = jnp.zeros((bp, GLA_HEADS, GLA_DK, GLA_DV), f32)
    obp, sp = _gla(gqp, gkp, gvp, lgp, rp, g_gla_norm[0], s_zero, n_seq=bp, c=gla_c, n_sub=gla_sub,
                   steps=t // (gla_c * gla_sub))
    obs, ss = _gla(gqs, gks, gvs, lgs, rs, g_gla_norm[0], state_gla[0], n_seq=bs, c=ts, n_sub=1, steps=1)

    out_w = (g_ffn_norm[0], w_out_b, w_route, b_route)
    bufs = _outproj(oap, obp, xp, mod_tab, mod_tab, mod_tab, *out_w, tm=to,
                    mod_specs=(pmod(2, to), pmod(3, to, 1), pmod(4, to, 1)), n_total=n_tot, row0=0,
                    fill_steps=-(-n_s // to), oa_transposed=True)
    x1, h2p, route = _outproj(oas, obs, xs, mod_s[2], mod_s[3], mod_s[4], *out_w, tm=n_s,
                              mod_specs=(smod, smod, smod), n_total=n_tot, row0=n_p, bufs=bufs)

    n_blocks = -(-(n_tot * TOP_K + N_EXPERTS * (tb - 1)) // tb)
    pos, pend_tab = _rank(route, tb=tb)
    pend, last_rows = pend_tab[0, :, 0], pend_tab[1, :, 0]
    block_row0 = jnp.arange(n_blocks, dtype=jnp.int32) * tb
    block_e = jnp.minimum(jnp.sum(pend[None, :] <= block_row0[:, None], axis=1), N_EXPERTS - 1).astype(jnp.int32)
    n_used = pend[N_EXPERTS - 1:] // tb
    experts = jnp.arange(N_EXPERTS, dtype=jnp.int32)
    has_rows = jnp.diff(pend, prepend=0) > 0
    later = jnp.where((experts[None, :] > experts[:, None]) & has_rows[None, :], experts[None, :], N_EXPERTS)
    next_with_rows = jnp.min(later, axis=1)
    next_with_rows = jnp.where(next_with_rows == N_EXPERTS, experts, next_with_rows)
    of_block = lambda per_expert: jnp.sum(jnp.where(block_e[:, None] == experts[None, :], per_expert[None, :], 0),
                                          axis=1).astype(jnp.int32)
    next_e = of_block(next_with_rows)
    half_full = ((block_row0 + tb == of_block(pend)) & (of_block(last_rows) <= tb // 2)).astype(jnp.int32)
    xsort = _dispatch(pos.reshape(-1), pend, h2p, tm=td, tb=tb, n_blocks=n_blocks)
    yb = _experts(block_e, next_e, half_full, n_used, xsort, w_expert_gate[0], w_expert_up[0],
                  w_expert_down[0], tb=tb)

    pos = pos.reshape(TOP_K, n_tot)
    pos_p = pos[:, :n_p].reshape(-1)
    pos_s = pos[:, n_p:].reshape(-1)
    gt_p = pl.BlockSpec((1, 1, d), lambda i, p: ((i // (t // tc)) * N_MOD + 5, 0, 0))
    gt_s = pl.BlockSpec((n_s, d), lambda i, p: (0, 0))
    y_p = _combine(pos_p, yb, x1, route, mod_tab, g_final, tm=tc, n=n_p, row0=0, gt_spec=gt_p)
    y_s = _combine(pos_s, yb, x1, route, mod_s[5], g_final, tm=n_s, n=n_s, row0=n_p, gt_spec=gt_s)

    kv_shape = (SWA_KV_HEADS, SWA_HEAD_DIM)
    last = lambda a: a.reshape(bp, t, KV_WIDTH)[:, t - WINDOW:, :].reshape(1, bp, WINDOW, *kv_shape)
    k_state_p, v_state_p = last(kp), last(vp)
    return (y_p.reshape(bp, t, d), y_s.reshape(bs, ts, d), k_state_p, v_state_p, sp[None],
            ks.reshape(bs, ts, *kv_shape)[None], vs.reshape(bs, ts, *kv_shape)[None], ss[None])
```

```python
import functools
import math

import jax
import jax.numpy as jnp
from jax import lax
from jax.experimental import pallas as pl
from jax.experimental.pallas import tpu as pltpu

f32 = jnp.float32
bf16 = jnp.bfloat16
u32 = jnp.uint32

D_MODEL = 2048
N_MOD = 6
EPS = 1e-6
NEG_INF = -1e30
LOG2_E = math.log2(math.e)

SWA_HEAD_DIM = 64
SWA_KV_HEADS = 2
SWA_GROUP = 8
SWA_WIDTH = SWA_KV_HEADS * SWA_GROUP * SWA_HEAD_DIM
KV_WIDTH = SWA_KV_HEADS * SWA_HEAD_DIM
WINDOW = 128
CHUNK = 64

GLA_HEADS = 4
GLA_DK = 128
GLA_DV = 256
GLA_QK_WIDTH = GLA_HEADS * GLA_DK
GLA_V_WIDTH = GLA_HEADS * GLA_DV
GLA_GATE_RANK = 16
GLA_GATE_NORM = 16.0

N_GROUPS = 4
EXPERTS_PER_GROUP = 8
N_EXPERTS = N_GROUPS * EXPERTS_PER_GROUP
TOP_K = 2
EXPERT_HIDDEN = D_MODEL // 4

_C_Q = 0
_C_K = _C_Q + SWA_WIDTH
_C_V = _C_K + KV_WIDTH
_C_GQ = _C_V + KV_WIDTH
_C_GK = _C_GQ + GLA_QK_WIDTH
_C_GV = _C_GK + GLA_QK_WIDTH
_C_R = _C_GV + GLA_V_WIDTH
_C_AB = _C_R + GLA_V_WIDTH

LANES = 128
SUBLANES = 8
PACKED_WIDTH = D_MODEL // 2
assert PACKED_WIDTH == SUBLANES * LANES
VMEM_LIMIT = 56 * 1024 * 1024

ROUTE_E0, ROUTE_E1, ROUTE_G0, ROUTE_G1 = 0, 1, 2, 3


def _dot(a, b):
    return jnp.dot(a, b, preferred_element_type=f32)


def _dot_nt(a, b):
    return lax.dot_general(a, b, (((1,), (1,)), ((), ())), preferred_element_type=f32)


def _dot_tn(a, b):
    return lax.dot_general(a, b, (((0,), (0,)), ((), ())), preferred_element_type=f32)


def _silu(x):
    return x / (1.0 + jnp.exp(-x))


def _rows(ref):
    v = ref[...]
    return v.reshape(v.shape[-2:])


def _rms(x):
    return x * lax.rsqrt(jnp.mean(x * x, axis=-1, keepdims=True) + EPS)


def _resident(shape):
    return pl.BlockSpec(shape, lambda *_: (0,) * len(shape), pipeline_mode=pl.Buffered(1))


def _mod_spec(comp, tiles_per_seq, n_tiles, lag=0):
    def index(i):
        tile = jnp.minimum(jnp.maximum(i - lag, 0), n_tiles - 1)
        return ((tile // tiles_per_seq) * N_MOD + comp, 0, 0)
    return pl.BlockSpec((1, 1, D_MODEL), index)


def _adaln_kernel(c_ref, w_ref, b_ref, o_ref):
    a = _silu(c_ref[...]).astype(bf16)
    o_ref[...] = _dot(a, w_ref[...].astype(bf16)) + b_ref[...]


def _adaln(c_all, w_ada, b_ada, tn=1024):
    r = c_all.shape[0]
    n = w_ada.shape[1]
    return pl.pallas_call(
        _adaln_kernel,
        out_shape=jax.ShapeDtypeStruct((r, n), f32),
        grid=(n // tn,),
        in_specs=[pl.BlockSpec((r, D_MODEL), lambda j: (0, 0)),
                  pl.BlockSpec((D_MODEL, tn), lambda j: (0, j)),
                  pl.BlockSpec((1, tn), lambda j: (0, j))],
        out_specs=pl.BlockSpec((r, tn), lambda j: (0, j)),
        compiler_params=pltpu.CompilerParams(dimension_semantics=("arbitrary",),
                                             vmem_limit_bytes=VMEM_LIMIT),
        name="adaln",
    )(c_all, w_ada, b_ada.reshape(1, n))


def _proj_kernel(x_ref, sh_ref, sc_ref, g_ref, wq_ref, w_ref, wab_ref, wg_ref, bg_ref,
                 q_ref, k_ref, v_ref, gq_ref, gk_ref, gv_ref, r_ref, lg_ref, *, q_transposed):
    h = (_rms(x_ref[...]) * g_ref[...]) * (1.0 + _rows(sc_ref)) + _rows(sh_ref)
    hb = h.astype(bf16)
    col = lambda c: c
    ab = _dot(hb, wab_ref[...])
    r_ref[...] = _silu(_dot(hb, w_ref[:, col(_C_R):col(_C_AB)])).astype(bf16)
    gq_ref[...] = (_dot(hb, w_ref[:, col(_C_GQ):col(_C_GK)]) * (GLA_DK ** -0.5)).astype(bf16)
    z = _dot(ab.astype(bf16), wg_ref[...]) + bg_ref[...]
    log_sig = jnp.minimum(z, 0.0) - jnp.log1p(jnp.exp(-jnp.abs(z)))
    lg_ref[...] = log_sig * (LOG2_E / GLA_GATE_NORM)
    gk_ref[...] = _dot(hb, w_ref[:, col(_C_GK):col(_C_GV)]).astype(bf16)
    gv_ref[...] = _dot(hb, w_ref[:, col(_C_GV):col(_C_R)]).astype(bf16)
    kv = _dot(hb, w_ref[:, col(_C_K):col(_C_GQ)])
    k_ref[...] = kv[:, :KV_WIDTH]
    v_ref[...] = kv[:, KV_WIDTH:]
    if q_transposed:
        q_ref[...] = (_dot_nt(wq_ref[...], hb) * (LOG2_E * SWA_HEAD_DIM ** -0.5)).astype(bf16)
    else:
        q_ref[...] = _dot(hb, wq_ref[...]).astype(bf16)


def _proj(x2d, sh, sc, g_mix, w_q, w_rest, w_ab, w_gate, b_gate, *, tm, mod_specs, q_transposed):
    n = x2d.shape[0]
    row = lambda w: pl.BlockSpec((tm, w), lambda i: (i, 0))
    outs = [(KV_WIDTH, f32), (KV_WIDTH, f32), (GLA_QK_WIDTH, bf16),
            (GLA_QK_WIDTH, bf16), (GLA_V_WIDTH, bf16), (GLA_V_WIDTH, bf16), (GLA_QK_WIDTH, f32)]
    if q_transposed:
        q_shape, q_spec = (SWA_WIDTH, n), pl.BlockSpec((SWA_WIDTH, tm), lambda i: (0, i))
    else:
        q_shape, q_spec = (n, SWA_WIDTH), row(SWA_WIDTH)
    return pl.pallas_call(
        functools.partial(_proj_kernel, q_transposed=q_transposed),
        out_shape=[jax.ShapeDtypeStruct(q_shape, bf16)]
                  + [jax.ShapeDtypeStruct((n, w), dt) for w, dt in outs],
        grid=(n // tm,),
        in_specs=[row(D_MODEL), mod_specs[0], mod_specs[1],
                  _resident((1, D_MODEL)), _resident(w_q.shape), _resident(w_rest.shape),
                  _resident(w_ab.shape), _resident(w_gate.shape), _resident((1, GLA_QK_WIDTH))],
        out_specs=[q_spec] + [row(w) for w, _ in outs],
        compiler_params=pltpu.CompilerParams(dimension_semantics=("parallel",),
                                             vmem_limit_bytes=VMEM_LIMIT),
        name="proj",
    )(x2d, sh, sc, g_mix.reshape(1, D_MODEL), w_q, w_rest, w_ab, w_gate, b_gate.reshape(1, -1))


def _swa_kernel(q_ref, kp_ref, vp_ref, kc_ref, vc_ref, sink_ref, o_ref, *, tq):
    pad = jnp.zeros((WINDOW - tq, KV_WIDTH), f32)
    k_all = jnp.concatenate([kp_ref[...], kc_ref[...], pad], axis=0).astype(bf16)
    v_all = jnp.concatenate([vp_ref[...], vc_ref[...], pad], axis=0).astype(bf16)
    rows, cols = SWA_GROUP * tq, 2 * WINDOW
    valid = lax.broadcasted_iota(jnp.int32, (rows, cols), 1) < WINDOW + tq
    q = q_ref[...]
    outs = []
    for j in range(SWA_KV_HEADS):
        heads = [q[:, (j * SWA_GROUP + g) * SWA_HEAD_DIM:(j * SWA_GROUP + g + 1) * SWA_HEAD_DIM]
                 for g in range(SWA_GROUP)]
        qs = jnp.concatenate(heads, axis=0)
        kj = k_all[:, j * SWA_HEAD_DIM:(j + 1) * SWA_HEAD_DIM]
        vj = v_all[:, j * SWA_HEAD_DIM:(j + 1) * SWA_HEAD_DIM]
        s = _dot_nt(qs, kj) * (SWA_HEAD_DIM ** -0.5)
        s = jnp.where(valid, s, NEG_INF)
        sink = sink_ref[j]
        m = jnp.maximum(jnp.max(s, axis=1, keepdims=True), sink)
        p = jnp.exp(s - m)
        den = jnp.sum(p, axis=1, keepdims=True) + jnp.exp(sink - m)
        o = _dot(p.astype(bf16), vj) / den
        outs.append(jnp.concatenate([o[g * tq:(g + 1) * tq] for g in range(SWA_GROUP)], axis=1))
    o_ref[...] = jnp.concatenate(outs, axis=1).astype(bf16)


def _swa(q, k_past, v_past, k_new, v_new, sinks, *, n_seq, tq):
    sink_rows = jnp.repeat(sinks.astype(f32).reshape(SWA_KV_HEADS, SWA_GROUP), tq, axis=1)
    sink_rows = sink_rows.reshape(SWA_KV_HEADS, SWA_GROUP * tq, 1)
    new = lambda w: pl.BlockSpec((tq, w), lambda b: (b, 0))
    past = pl.BlockSpec((WINDOW, KV_WIDTH), lambda b: (b, 0))
    return pl.pallas_call(
        functools.partial(_swa_kernel, tq=tq),
        out_shape=jax.ShapeDtypeStruct(q.shape, bf16),
        grid=(n_seq,),
        in_specs=[new(SWA_WIDTH), past, past, new(KV_WIDTH), new(KV_WIDTH),
                  pl.BlockSpec(sink_rows.shape, lambda b: (0, 0, 0))],
        out_specs=new(SWA_WIDTH),
        compiler_params=pltpu.CompilerParams(dimension_semantics=("parallel",),
                                             vmem_limit_bytes=VMEM_LIMIT),
        name="swa",
    )(q, k_past, v_past, k_new, v_new, sink_rows)


def _swa_t_kernel(q_ref, kp_ref, vp_ref, kc_ref, vc_ref, sink_ref, mk_ref, mq0_ref, mq_ref, o_ref, *, n_sub):
    tq = WINDOW
    hd = SWA_HEAD_DIM
    k_all = jnp.concatenate([kp_ref[...], kc_ref[...]], axis=0).astype(bf16)
    v_all = jnp.concatenate([vp_ref[...], vc_ref[...]], axis=0).astype(bf16)
    low_lanes = lax.broadcasted_iota(jnp.int32, (2 * WINDOW, KV_WIDTH), 1) < hd
    ones = jnp.ones((2 * WINDOW, KV_WIDTH), bf16)
    for sub in range(n_sub):
        keys = slice(sub * tq, sub * tq + 2 * WINDOW)
        toks = slice(sub * tq, (sub + 1) * tq)
        mq = (mq0_ref if sub == 0 else mq_ref)[0]
        for j in range(SWA_KV_HEADS):
            head = lambda g: slice((j * SWA_GROUP + g) * hd, (j * SWA_GROUP + g + 1) * hd)
            qs = jnp.concatenate([q_ref[head(g), toks] for g in range(SWA_GROUP)], axis=1)
            own = low_lanes if j == 0 else jnp.logical_not(low_lanes)
            k_aug = jnp.where(own, k_all[keys], mk_ref[j])
            v_aug = jnp.where(own, v_all[keys], ones)
            q_aug = jnp.concatenate([qs, mq] if j == 0 else [mq, qs], axis=0)
            s = _dot(k_aug, q_aug)
            sink = sink_ref[j]
            m = jnp.maximum(jnp.max(s, axis=0, keepdims=True), sink)
            p = jnp.exp2(s - m).astype(bf16)
            o_aug = _dot_tn(v_aug, p)
            pv, p_sum = (o_aug[:hd], o_aug[hd:hd + 1]) if j == 0 else (o_aug[hd:], o_aug[0:1])
            o = pv / (p_sum + jnp.exp2(sink - m))
            for g in range(SWA_GROUP):
                o_ref[head(g), toks] = o[:, g * tq:(g + 1) * tq].astype(bf16)


def _swa_t(q_t, k, v, sinks, *, n_seq, tiles, n_sub):
    tq = WINDOW
    steps = tiles // n_sub
    hd = SWA_HEAD_DIM
    sink_cols = jnp.repeat(sinks.astype(f32).reshape(SWA_KV_HEADS, SWA_GROUP) * LOG2_E, tq, axis=1)
    sink_cols = sink_cols.reshape(SWA_KV_HEADS, 1, SWA_GROUP * tq)
    n_kc = 2 * WINDOW // CHUNK
    key_chunk = jnp.arange(2 * WINDOW) // CHUNK
    lane = jnp.arange(KV_WIDTH)
    mk = jnp.stack([lane[None, :] == hd + key_chunk[:, None], lane[None, :] == key_chunk[:, None]]).astype(bf16)
    q_chunk = (jnp.arange(SWA_GROUP * tq) % tq) // CHUNK
    kc = jnp.arange(hd)[:, None]
    band = (kc >= q_chunk[None, :]) & (kc <= q_chunk[None, :] + WINDOW // CHUNK)
    visible = jnp.stack([band & (kc >= WINDOW // CHUNK), band]) | (kc >= n_kc)
    mq = jnp.where(visible, 0.0, NEG_INF).astype(bf16)
    mq0_spec = pl.BlockSpec((1, hd, SWA_GROUP * tq), lambda b, u: (jnp.minimum(u, 1), 0, 0))
    mq_spec = pl.BlockSpec((1, hd, SWA_GROUP * tq), lambda b, u: (1, 0, 0))
    qspec = pl.BlockSpec((SWA_WIDTH, n_sub * tq), lambda b, u: (0, b * steps + u))
    cur = pl.BlockSpec((n_sub * tq, KV_WIDTH), lambda b, u: (b * steps + u, 0))
    prev = pl.BlockSpec((WINDOW, KV_WIDTH), lambda b, u: (b * tiles + jnp.maximum(n_sub * u - 1, 0), 0))
    return pl.pallas_call(
        functools.partial(_swa_t_kernel, n_sub=n_sub),
        out_shape=jax.ShapeDtypeStruct(q_t.shape, bf16),
        grid=(n_seq, steps),
        in_specs=[qspec, prev, prev, cur, cur, pl.BlockSpec(sink_cols.shape, lambda b, u: (0, 0, 0)),
                  pl.BlockSpec(mk.shape, lambda b, u: (0, 0, 0)), mq0_spec, mq_spec],
        out_specs=qspec,
        compiler_params=pltpu.CompilerParams(dimension_semantics=("parallel", "arbitrary"),
                                             vmem_limit_bytes=VMEM_LIMIT),
        name="swa_t",
    )(q_t, k, v, k, v, sink_cols, mk, mq, mq)


def _gla_kernel(q_ref, k_ref, v_ref, lg_ref, r_ref, gh_ref, s0_ref, o_ref, s_ref, *, c, n_sub):
    @pl.when(pl.program_id(1) == 0)
    def _():
        s_ref[...] = s0_ref[...]

    row = lax.broadcasted_iota(jnp.int32, (c, GLA_DK), 0)
    causal = (lax.broadcasted_iota(jnp.int32, (c, c), 0) >= lax.broadcasted_iota(jnp.int32, (c, c), 1))
    for sub in range(n_sub):
        ts = slice(sub * c, (sub + 1) * c)
        for h in range(GLA_HEADS):
            ks = slice(h * GLA_DK, (h + 1) * GLA_DK)
            vs = slice(h * GLA_DV, (h + 1) * GLA_DV)
            b = lg_ref[ts, ks]
            step = 1
            while step < c:
                b = b + jnp.where(row >= step, pltpu.roll(b, step, 0), 0.0)
                step *= 2
            b_last = b[c - 1:c, :]
            q = q_ref[ts, ks].astype(f32)
            k = k_ref[ts, ks].astype(f32)
            v = v_ref[ts, vs]
            qd = (q * jnp.exp2(b)).astype(bf16)
            kd = (k * jnp.exp2(-b)).astype(bf16)
            kl = (k * jnp.exp2(b_last - b)).astype(bf16)
            a = jnp.where(causal, _dot_nt(qd, kd), 0.0)
            s = s_ref[0, h]
            o = _dot(qd, s.astype(bf16)) + _dot(a.astype(bf16), v)
            decay = jnp.broadcast_to(jnp.exp2(b_last), (GLA_DK, GLA_DK)).T
            s_ref[0, h] = s * jnp.concatenate([decay, decay], axis=1) + _dot_tn(kl, v)
            on = _rms(o) * gh_ref[...]
            o_ref[ts, vs] = (on * r_ref[ts, vs].astype(f32)).astype(bf16)


def _gla(gq, gk, gv, lg, r, g_head, s0, *, n_seq, c, n_sub, steps):
    rows = c * n_sub
    blk = lambda w: pl.BlockSpec((rows, w), lambda b, t: (b * steps + t, 0))
    state = pl.BlockSpec((1, GLA_HEADS, GLA_DK, GLA_DV), lambda b, t: (b, 0, 0, 0))
    return pl.pallas_call(
        functools.partial(_gla_kernel, c=c, n_sub=n_sub),
        out_shape=[jax.ShapeDtypeStruct(gv.shape, bf16),
                   jax.ShapeDtypeStruct((n_seq, GLA_HEADS, GLA_DK, GLA_DV), f32)],
        grid=(n_seq, steps),
        in_specs=[blk(GLA_QK_WIDTH), blk(GLA_QK_WIDTH), blk(GLA_V_WIDTH), blk(GLA_QK_WIDTH),
                  blk(GLA_V_WIDTH), pl.BlockSpec((1, GLA_DV), lambda b, t: (0, 0)), state],
        out_specs=[blk(GLA_V_WIDTH), state],
        compiler_params=pltpu.CompilerParams(dimension_semantics=("parallel", "arbitrary"),
                                             vmem_limit_bytes=VMEM_LIMIT),
        name="gla",
    )(gq, gk, gv, lg, r, g_head.reshape(1, GLA_DV), s0)


def _pack_pair(hb):
    w = hb.shape[1] // 2
    a = lax.bitcast_convert_type(hb[:, :w].astype(f32), u32)
    b = lax.bitcast_convert_type(hb[:, w:].astype(f32), u32)
    return a | (b >> 16)


def _unpack_pair(p):
    a = lax.bitcast_convert_type(p & jnp.uint32(0xFFFF0000), f32)
    b = lax.bitcast_convert_type(p << 16, f32)
    return a, b


def _load_row_tiles(ref, n):
    return jnp.concatenate([ref[pl.ds(s, n, stride=SUBLANES), :] for s in range(SUBLANES)], axis=1)


def _store_row_tiles(ref, val):
    n = val.shape[0]
    for s in range(SUBLANES):
        ref[pl.ds(s, n, stride=SUBLANES), :] = val[:, s * LANES:(s + 1) * LANES]


def _row_tile(ref, r):
    return ref.at[pl.ds(pl.multiple_of(r * SUBLANES, SUBLANES), SUBLANES), :]


def _outproj_kernel(*refs, n_alias, oa_transposed):
    (oa_ref, ob_ref, x_ref, gt_ref, sh_ref, sc_ref, g_ref, w_ref, wr_ref, br_ref) = refs[:10]
    x1_ref, h2_ref, route_ref, x1_s = refs[10 + n_alias:]

    @pl.when(pl.program_id(0) == 0)
    def _():
        x1_s[...] = jnp.zeros(x1_s.shape, x1_s.dtype)

    h2 = (_rms(x1_s[...]) * g_ref[...]) * (1.0 + _rows(sc_ref)) + _rows(sh_ref)
    hi = h2.astype(bf16)
    _store_row_tiles(h2_ref, _pack_pair(hi))
    logits = _dot(hi, wr_ref[...]) + br_ref[...]
    lane = lax.broadcasted_iota(jnp.int32, logits.shape, 1)
    lane_f = lane.astype(f32)
    neg = float("-inf")
    first = lambda hit: jnp.min(jnp.where(hit, lane_f, float(LANES)), axis=1, keepdims=True)
    lg_g = jnp.where(lane < N_GROUPS, logits, neg)
    g_max = jnp.max(lg_g, axis=1, keepdims=True)
    g_sel = first(lg_g == g_max)
    p_sel = 1.0 / jnp.sum(jnp.exp(lg_g - g_max), axis=1, keepdims=True)
    e_lane = lane - N_GROUPS
    in_group = (e_lane >= 0) & (e_lane < N_EXPERTS) & ((e_lane // EXPERTS_PER_GROUP).astype(f32) == g_sel)
    lg_e = jnp.where(in_group, logits, neg)
    v1 = jnp.max(lg_e, axis=1, keepdims=True)
    i1 = first(lg_e == v1)
    lg_e2 = jnp.where(lane_f == i1, neg, lg_e)
    v2 = jnp.max(lg_e2, axis=1, keepdims=True)
    i2 = first(lg_e2 == v2)
    e = jnp.exp(v2 - v1)
    g1 = p_sel / (1.0 + e)
    g2 = p_sel * e / (1.0 + e)
    rec = jnp.where(lane == ROUTE_E0, i1 - N_GROUPS, 0.0)
    rec = jnp.where(lane == ROUTE_E1, i2 - N_GROUPS, rec)
    rec = jnp.where(lane == ROUTE_G0, g1, rec)
    rec = jnp.where(lane == ROUTE_G1, g2, rec)
    route_ref[...] = rec

    oa_dot = _dot_tn if oa_transposed else _dot
    mix = oa_dot(oa_ref[...], w_ref[:SWA_WIDTH, :]) + _dot(ob_ref[...], w_ref[SWA_WIDTH:, :])
    x1 = x_ref[...] + _rows(gt_ref) * mix
    x1_ref[...] = x1
    x1_s[...] = x1


def _outproj(oa, ob, x2d, gt, sh, sc, g_ffn, w_out, w_route, b_route, *, tm, mod_specs,
             n_total, row0, bufs=None, fill_steps=0, oa_transposed=False):
    n = x2d.shape[0]
    tiles = n // tm
    blocks = tiles + fill_steps
    row = lambda w: pl.BlockSpec((tm, w), lambda i: (jnp.minimum(i, tiles - 1), 0))
    oa_spec = row(SWA_WIDTH)
    if oa_transposed:
        oa_spec = pl.BlockSpec((SWA_WIDTH, tm), lambda i: (0, jnp.minimum(i, tiles - 1)))
    head_blk = lambda i: row0 // tm + jnp.minimum(i, blocks - 1)
    tail_blk = lambda i: row0 // tm + jnp.maximum(i - 1, 0)
    out_x1 = pl.BlockSpec((tm, D_MODEL), lambda i: (head_blk(i), 0))
    out_tiles = pl.BlockSpec((tm * SUBLANES, LANES), lambda i: (tail_blk(i), 0))
    out_route = pl.BlockSpec((tm, LANES), lambda i: (tail_blk(i), 0))
    alias_in = list(bufs) if bufs is not None else []
    n_in = 10
    return pl.pallas_call(
        functools.partial(_outproj_kernel, n_alias=len(alias_in), oa_transposed=oa_transposed),
        out_shape=[jax.ShapeDtypeStruct((n_total, D_MODEL), f32),
                   jax.ShapeDtypeStruct((n_total * SUBLANES, LANES), u32),
                   jax.ShapeDtypeStruct((n_total, LANES), f32)],
        grid=(blocks + 1,),
        in_specs=[oa_spec, row(GLA_V_WIDTH), row(D_MODEL), mod_specs[0], mod_specs[1],
                  mod_specs[2], _resident((1, D_MODEL)), _resident(w_out.shape),
                  _resident(w_route.shape), _resident((1, LANES))]
                 + [pl.BlockSpec(memory_space=pl.ANY)] * len(alias_in),
        out_specs=[out_x1, out_tiles, out_route],
        scratch_shapes=[pltpu.VMEM((tm, D_MODEL), f32)],
        input_output_aliases={n_in + a: a for a in range(len(alias_in))},
        compiler_params=pltpu.CompilerParams(dimension_semantics=("arbitrary",),
                                             vmem_limit_bytes=VMEM_LIMIT),
        name="outproj",
    )(oa, ob, x2d, gt, sh, sc, g_ffn.reshape(1, D_MODEL), w_out, w_route, b_route, *alias_in)


def _rank_kernel(route_ref, pos_ref, pend_ref, rec_s, *, n_chunks, tb):
    n_e = N_EXPERTS
    expert = lax.broadcasted_iota(jnp.int32, (n_e, LANES), 0)
    expert_f = expert.astype(f32)
    earlier = (lax.broadcasted_iota(jnp.int32, (LANES, LANES), 0)
               < lax.broadcasted_iota(jnp.int32, (LANES, LANES), 1)).astype(bf16)

    def onehots(c):
        rec = rec_s[c]
        return expert_f == rec[ROUTE_E0:ROUTE_E0 + 1, :], expert_f == rec[ROUTE_E1:ROUTE_E1 + 1, :]

    def count(c, cnt):
        rec_s[c] = route_ref[pl.ds(pl.multiple_of(c * LANES, LANES), LANES), :].T[:SUBLANES, :]
        h0, h1 = onehots(c)
        return cnt + jnp.sum((h0 | h1).astype(f32), axis=1, keepdims=True)

    cnt = lax.fori_loop(0, n_chunks, count, jnp.zeros((n_e, 1), f32), unroll=4)
    padded = jnp.floor((cnt + (tb - 1.0)) / tb) * tb
    end = jnp.broadcast_to(padded, (n_e, LANES))
    step = 1
    while step < n_e:
        end = end + jnp.where(expert >= step, pltpu.roll(end, step, 0), 0.0)
        step *= 2
    pend_ref[0] = end.astype(jnp.int32)
    pend_ref[1] = jnp.broadcast_to(jnp.where(cnt > 0, cnt - (padded - tb), 0.0), (n_e, LANES)).astype(jnp.int32)

    def rank(c, base):
        h0, h1 = onehots(c)
        both = h0 | h1
        off = _dot(both.astype(bf16), earlier) + base
        pos_ref[0, pl.ds(c, 1), :] = jnp.sum(jnp.where(h0, off, 0.0), axis=0, keepdims=True).astype(jnp.int32)
        pos_ref[1, pl.ds(c, 1), :] = jnp.sum(jnp.where(h1, off, 0.0), axis=0, keepdims=True).astype(jnp.int32)
        return base + jnp.sum(both.astype(f32), axis=1, keepdims=True)

    lax.fori_loop(0, n_chunks, rank, (end - padded)[:, 0:1], unroll=4)


def _rank(route, *, tb):
    n = route.shape[0]
    n_chunks = n // LANES
    return pl.pallas_call(
        functools.partial(_rank_kernel, n_chunks=n_chunks, tb=tb),
        out_shape=[jax.ShapeDtypeStruct((TOP_K, n_chunks, LANES), jnp.int32),
                   jax.ShapeDtypeStruct((2, N_EXPERTS, LANES), jnp.int32)],
        in_specs=[_resident(route.shape)],
        scratch_shapes=[pltpu.VMEM((n_chunks, SUBLANES, LANES), f32)],
        compiler_params=pltpu.CompilerParams(vmem_limit_bytes=VMEM_LIMIT),
        name="rank",
    )(route)


def _dispatch_kernel(pos_ref, pend_ref, h2_ref, xs_hbm, stage, zbuf, sem, zsem, *, tm, tb, n_tot, steps):
    i = pl.program_id(0)
    slot = i % 2

    blk = tb * SUBLANES

    def wait_slot(s):
        for _ in range(TOP_K):
            pltpu.make_async_copy(stage.at[s], xs_hbm.at[pl.ds(0, tm * SUBLANES), :], sem.at[s]).wait()

    @pl.when(i == 0)
    def _():
        zbuf[...] = jnp.zeros(zbuf.shape, zbuf.dtype)

        def fill_block(b, start):
            cp = pltpu.make_async_copy(zbuf, xs_hbm.at[pl.ds(pl.multiple_of(b * blk, blk), blk), :], zsem)
            cp.start() if start else cp.wait()

        def fill(e, start):
            end = pend_ref[e]
            prev = jnp.where(e > 0, pend_ref[jnp.maximum(e - 1, 0)], 0)

            @pl.when(end > prev)
            def _():
                fill_block(end // tb - 1, start)

        fill_unused = fill_block

        first_unused = pend_ref[N_EXPERTS - 1] // tb
        n_blocks = xs_hbm.shape[0] // blk
        lax.fori_loop(0, N_EXPERTS, lambda e, c: (fill(e, True), c)[1], 0)
        lax.fori_loop(first_unused, n_blocks, lambda b, c: (fill_unused(b, True), c)[1], 0)
        lax.fori_loop(0, N_EXPERTS, lambda e, c: (fill(e, False), c)[1], 0)
        lax.fori_loop(first_unused, n_blocks, lambda b, c: (fill_unused(b, False), c)[1], 0)

    @pl.when(i >= 2)
    def _():
        wait_slot(slot)

    stage[slot] = h2_ref[...]

    def scatter(r, carry):
        for k in range(TOP_K):
            d = pos_ref[k * n_tot + i * tm + r]
            pltpu.make_async_copy(_row_tile(stage.at[slot], r), _row_tile(xs_hbm, d),
                                  sem.at[slot]).start(priority=k)
        return carry

    lax.fori_loop(0, tm, scatter, 0, unroll=8)

    @pl.when(i == steps - 1)
    def _():
        wait_slot(slot)
        if steps > 1:
            wait_slot(1 - slot)


def _dispatch(pos_flat, pend, h2p, *, tm, tb, n_blocks):
    n_tot = h2p.shape[0] // SUBLANES
    steps = n_tot // tm
    return pl.pallas_call(
        functools.partial(_dispatch_kernel, tm=tm, tb=tb, n_tot=n_tot, steps=steps),
        out_shape=jax.ShapeDtypeStruct((n_blocks * tb * SUBLANES, LANES), u32),
        grid_spec=pltpu.PrefetchScalarGridSpec(
            num_scalar_prefetch=2,
            grid=(steps,),
            in_specs=[pl.BlockSpec((tm * SUBLANES, LANES), lambda i, pos, pend: (i, 0))],
            out_specs=pl.BlockSpec(memory_space=pl.ANY),
            scratch_shapes=[pltpu.VMEM((2, tm * SUBLANES, LANES), u32),
                            pltpu.VMEM((tb * SUBLANES, LANES), u32),
                            pltpu.SemaphoreType.DMA((2,)), pltpu.SemaphoreType.DMA(())]),
        compiler_params=pltpu.CompilerParams(dimension_semantics=("arbitrary",),
                                             vmem_limit_bytes=VMEM_LIMIT),
        name="dispatch",
    )(pos_flat, pend, h2p)


def _expert_kernel(be_ref, ne_ref, hf_ref, nu_ref, xs_ref, wg_hbm, wu_hbm, wd_hbm, y_ref,
                   wg_s, wu_s, wd_s, wg_b, wu_b, wd_b, sem, *, tb):
    i = pl.program_id(0)
    used = i < nu_ref[0]
    e = be_ref[i]

    def fetch(expert):
        pairs = ((wg_hbm, wg_s), (wu_hbm, wu_s), (wd_hbm, wd_s))
        return [pltpu.make_async_copy(src.at[expert], dst, sem.at[n]) for n, (src, dst) in enumerate(pairs)]

    @pl.when(i == 0)
    def _():
        for cp in fetch(e):
            cp.start()

    @pl.when(used & ((i == 0) | (e != be_ref[jnp.maximum(i - 1, 0)])))
    def _():
        for cp in fetch(e):
            cp.wait()
        wg_b[...] = wg_s[...].astype(bf16)
        wu_b[...] = wu_s[...].astype(bf16)
        wd_b[...] = wd_s[...].astype(bf16)
        nxt = ne_ref[i]

        @pl.when(nxt != e)
        def _():
            for cp in fetch(nxt):
                cp.start()

    def ffn(rows):
        tiles = pl.ds(0, rows * SUBLANES)
        a, b = _unpack_pair(_load_row_tiles(xs_ref.at[tiles], rows))
        x = jnp.concatenate([a.astype(bf16), b.astype(bf16)], axis=1)
        g = _dot(x, wg_b[...])
        u = _dot(x, wu_b[...])
        y = _dot((_silu(g) * u).astype(bf16), wd_b[...])
        _store_row_tiles(y_ref.at[tiles], _pack_pair(y.astype(bf16)))

    half_full = hf_ref[i] == 1

    @pl.when(used & jnp.logical_not(half_full))
    def _():
        ffn(tb)

    @pl.when(used & half_full)
    def _():
        ffn(tb // 2)
        rest = pl.ds(tb // 2 * SUBLANES, tb // 2 * SUBLANES)
        y_ref[rest, :] = jnp.zeros((tb // 2 * SUBLANES, LANES), y_ref.dtype)

    @pl.when(jnp.logical_not(used))
    def _():
        y_ref[...] = jnp.zeros(y_ref.shape, y_ref.dtype)


def _experts(block_e, next_e, half_full, n_used, xs, w_eg, w_eu, w_ed, *, tb):
    n_blocks = block_e.shape[0]
    blk = (tb * SUBLANES, LANES)
    up, down = (D_MODEL, EXPERT_HIDDEN), (EXPERT_HIDDEN, D_MODEL)
    hbm = pl.BlockSpec(memory_space=pl.ANY)
    return pl.pallas_call(
        functools.partial(_expert_kernel, tb=tb),
        out_shape=jax.ShapeDtypeStruct(xs.shape, u32),
        grid_spec=pltpu.PrefetchScalarGridSpec(
            num_scalar_prefetch=4,
            grid=(n_blocks,),
            in_specs=[pl.BlockSpec(blk, lambda i, be, ne, hf, nu: (jnp.minimum(i, nu[0] - 1), 0)),
                      hbm, hbm, hbm],
            out_specs=pl.BlockSpec(blk, lambda i, be, ne, hf, nu: (i, 0)),
            scratch_shapes=[pltpu.VMEM(up, f32), pltpu.VMEM(up, f32), pltpu.VMEM(down, f32),
                            pltpu.VMEM(up, bf16), pltpu.VMEM(up, bf16), pltpu.VMEM(down, bf16),
                            pltpu.SemaphoreType.DMA((3,))]),
        compiler_params=pltpu.CompilerParams(dimension_semantics=("arbitrary",),
                                             vmem_limit_bytes=VMEM_LIMIT),
        name="experts",
    )(block_e, next_e, half_full, n_used, xs, w_eg, w_eu, w_ed)


def _combine_kernel(pos_ref, yb_hbm, x1_ref, route_ref, gt_ref, gf_ref, y_ref, ybuf, sem, *, tm):
    i = pl.program_id(0)
    slot = i % 2

    def start(blk, s):
        def body(r, carry):
            for k in range(TOP_K):
                t = pos_ref[(k * pl.num_programs(0) + blk) * tm + r]
                pltpu.make_async_copy(_row_tile(yb_hbm, t), _row_tile(ybuf.at[s, k], r),
                                      sem.at[s]).start(priority=k)
            return carry
        lax.fori_loop(0, tm, body, 0, unroll=8)

    @pl.when(i == 0)
    def _():
        start(0, 0)

    @pl.when(i + 1 < pl.num_programs(0))
    def _():
        start(i + 1, 1 - slot)

    for k in range(TOP_K):
        pltpu.make_async_copy(yb_hbm.at[pl.ds(0, tm * SUBLANES), :], ybuf.at[slot, k], sem.at[slot]).wait()
    route = route_ref[...]
    a0, b0 = _unpack_pair(_load_row_tiles(ybuf.at[slot, 0], tm))
    a1, b1 = _unpack_pair(_load_row_tiles(ybuf.at[slot, 1], tm))
    g0 = route[:, ROUTE_G0:ROUTE_G0 + 1]
    g1 = route[:, ROUTE_G1:ROUTE_G1 + 1]
    moe = jnp.concatenate([a0 * g0 + a1 * g1, b0 * g0 + b1 * g1], axis=1)
    x2 = x1_ref[...] + _rows(gt_ref) * moe
    y_ref[...] = _rms(x2) * gf_ref[...]


def _combine(pos_km, yb, x1, route, gt, g_final, *, tm, n, row0, gt_spec):
    blk0 = row0 // tm
    return pl.pallas_call(
        functools.partial(_combine_kernel, tm=tm),
        out_shape=jax.ShapeDtypeStruct((n, D_MODEL), f32),
        grid_spec=pltpu.PrefetchScalarGridSpec(
            num_scalar_prefetch=1,
            grid=(n // tm,),
            in_specs=[pl.BlockSpec(memory_space=pl.ANY),
                      pl.BlockSpec((tm, D_MODEL), lambda i, pos: (blk0 + i, 0)),
                      pl.BlockSpec((tm, LANES), lambda i, pos: (blk0 + i, 0)),
                      gt_spec,
                      pl.BlockSpec((1, D_MODEL), lambda i, pos: (0, 0))],
            out_specs=pl.BlockSpec((tm, D_MODEL), lambda i, pos: (i, 0)),
            scratch_shapes=[pltpu.VMEM((2, TOP_K, tm * SUBLANES, LANES), u32),
                            pltpu.SemaphoreType.DMA((2,))]),
        compiler_params=pltpu.CompilerParams(dimension_semantics=("arbitrary",),
                                             vmem_limit_bytes=VMEM_LIMIT),
        name="combine",
    )(pos_km, yb, x1, route, gt, g_final.reshape(1, D_MODEL))


def kernel(x_prompt, x_sample, cache_swa_k, cache_swa_v, state_gla, c_prompt, c_sample, g_mix_norm, g_ffn_norm, w_ada, b_ada, w_in, attn_sinks, w_gla_gate, b_gla_gate, g_gla_norm, w_out, w_router_group, b_router_group, w_router_expert, b_router_expert, w_expert_gate, w_expert_up, w_expert_down, g_final):
    depth = w_in.shape[0]
    assert depth == 1
    bp, t, d = x_prompt.shape
    bs, ts, _ = x_sample.shape
    n_p, n_s = bp * t, bs * ts
    n_tot = n_p + n_s
    assert n_tot % LANES == 0
    tm = 512
    to = 256
    tb = 512
    tc = 256
    td = LANES
    gla_c = 128
    gla_sub = 16
    swa_sub = 8

    w_in0 = w_in[0]
    w_rest = w_in0.astype(bf16)
    w_q = w_rest[:, :_C_K]
    w_q_t = w_q.T
    w_ab = jnp.pad(w_in0[:, _C_AB:], ((0, 0), (0, LANES - GLA_GATE_RANK))).astype(bf16)
    w_gate = jnp.pad(w_gla_gate[0], ((0, LANES - GLA_GATE_RANK), (0, 0))).astype(bf16)
    w_out_b = w_out[0].astype(bf16)
    n_r = N_GROUPS + N_EXPERTS
    w_r = jnp.pad(jnp.concatenate([w_router_group[0], w_router_expert[0]], axis=1),
                  ((0, 0), (0, LANES - n_r)))
    w_route = w_r.astype(bf16)
    b_route = jnp.pad(jnp.concatenate([b_router_group[0], b_router_expert[0]]),
                      (0, LANES - n_r)).reshape(1, LANES)

    c_all = jnp.concatenate([c_prompt, c_sample], axis=0)
    mod = _adaln(c_all, w_ada[0], b_ada[0])
    mod_tab = mod.reshape((bp + bs) * N_MOD, 1, d)
    mod_s = jnp.repeat(mod[bp:].reshape(bs, N_MOD, d), ts, axis=0)
    mod_s = [mod_s[:, m] for m in range(N_MOD)]
    pmod = lambda comp, tile, lag=0: _mod_spec(comp, t // tile, n_p // tile, lag)
    smod = pl.BlockSpec((n_s, d), lambda i: (0, 0))

    xp = x_prompt.reshape(n_p, d)
    xs = x_sample.reshape(n_s, d)
    proj_w = (w_rest, w_ab, w_gate, b_gla_gate[0])
    qp, kp, vp, gqp, gkp, gvp, rp, lgp = _proj(xp, mod_tab, mod_tab, g_mix_norm[0], w_q_t, *proj_w, tm=tm,
                                               mod_specs=(pmod(0, tm), pmod(1, tm)), q_transposed=True)
    qs, ks, vs, gqs, gks, gvs, rs, lgs = _proj(xs, mod_s[0], mod_s[1], g_mix_norm[0], w_q, *proj_w, tm=n_s,
                                               mod_specs=(smod, smod), q_transposed=False)

    sinks = attn_sinks[0]
    oap = _swa_t(qp, kp, vp, sinks, n_seq=bp, tiles=t // WINDOW, n_sub=swa_sub)
    ck = cache_swa_k[0].reshape(bs * WINDOW, KV_WIDTH)
    cv = cache_swa_v[0].reshape(bs * WINDOW, KV_WIDTH)
    oas = _swa(qs, ck, cv, ks, vs, sinks, n_seq=bs, tq=ts)
    s_zero = jnp.zeros((bp, GLA_HEADS, GLA_DK, GLA_DV), f32)
    obp, sp = _gla(gqp, gkp, gvp, lgp, rp, g_gla_norm[0], s_zero, n_seq=bp, c=gla_c, n_sub=gla_sub,
                   steps=t // (gla_c * gla_sub))
    obs, ss = _gla(gqs, gks, gvs, lgs, rs, g_gla_norm[0], state_gla[0], n_seq=bs, c=ts, n_sub=1, steps=1)

    out_w = (g_ffn_norm[0], w_out_b, w_route, b_route)
    bufs = _outproj(oap, obp, xp, mod_tab, mod_tab, mod_tab, *out_w, tm=to,
                    mod_specs=(pmod(2, to), pmod(3, to, 1), pmod(4, to, 1)), n_total=n_tot, row0=0,
                    fill_steps=-(-n_s // to), oa_transposed=True)
    x1, h2p, route = _outproj(oas, obs, xs, mod_s[2], mod_s[3], mod_s[4], *out_w, tm=n_s,
                              mod_specs=(smod, smod, smod), n_total=n_tot, row0=n_p, bufs=bufs)

    n_blocks = -(-(n_tot * TOP_K + N_EXPERTS * (tb - 1)) // tb)
    pos, pend_tab = _rank(route, tb=tb)
    pend, last_rows = pend_tab[0, :, 0], pend_tab[1, :, 0]
    block_row0 = jnp.arange(n_blocks, dtype=jnp.int32) * tb
    block_e = jnp.minimum(jnp.sum(pend[None, :] <= block_row0[:, None], axis=1), N_EXPERTS - 1).astype(jnp.int32)
    n_used = pend[N_EXPERTS - 1:] // tb
    experts = jnp.arange(N_EXPERTS, dtype=jnp.int32)
    has_rows = jnp.diff(pend, prepend=0) > 0
    later = jnp.where((experts[None, :] > experts[:, None]) & has_rows[None, :], experts[None, :], N_EXPERTS)
    next_with_rows = jnp.min(later, axis=1)
    next_with_rows = jnp.where(next_with_rows == N_EXPERTS, experts, next_with_rows)
    of_block = lambda per_expert: jnp.sum(jnp.where(block_e[:, None] == experts[None, :], per_expert[None, :], 0),
                                          axis=1).astype(jnp.int32)
    next_e = of_block(next_with_rows)
    half_full = ((block_row0 + tb == of_block(pend)) & (of_block(last_rows) <= tb // 2)).astype(jnp.int32)
    xsort = _dispatch(pos.reshape(-1), pend, h2p, tm=td, tb=tb, n_blocks=n_blocks)
    yb = _experts(block_e, next_e, half_full, n_used, xsort, w_expert_gate[0], w_expert_up[0],
                  w_expert_down[0], tb=tb)

    pos = pos.reshape(TOP_K, n_tot)
    pos_p = pos[:, :n_p].reshape(-1)
    pos_s = pos[:, n_p:].reshape(-1)
    gt_p = pl.BlockSpec((1, 1, d), lambda i, p: ((i // (t // tc)) * N_MOD + 5, 0, 0))
    gt_s = pl.BlockSpec((n_s, d), lambda i, p: (0, 0))
    y_p = _combine(pos_p, yb, x1, route, mod_tab, g_final, tm=tc, n=n_p, row0=0, gt_spec=gt_p)
    y_s = _combine(pos_s, yb, x1, route, mod_s[5], g_final, tm=n_s, n=n_s, row0=n_p, gt_spec=gt_s)

    kv_shape = (SWA_KV_HEADS, SWA_HEAD_DIM)
    last = lambda a: a.reshape(bp, t, KV_WIDTH)[:, t - WINDOW:, :].reshape(1, bp, WINDOW, *kv_shape)
    k_state_p, v_state_p = last(kp), last(vp)
    return (y_p.reshape(bp, t, d), y_s.reshape(bs, ts, d), k_state_p, v_state_p, sp[None],
            ks.reshape(bs, ts, *kv_shape)[None], vs.reshape(bs, ts, *kv_shape)[None], ss[None])
```

```python
import functools
import math

import jax
import jax.numpy as jnp
from jax import lax
from jax.experimental import pallas as pl
from jax.experimental.pallas import tpu as pltpu

f32 = jnp.float32
bf16 = jnp.bfloat16
u32 = jnp.uint32

D_MODEL = 2048
N_MOD = 6
EPS = 1e-6
NEG_INF = -1e30
LOG2_E = math.log2(math.e)

SWA_HEAD_DIM = 64
SWA_KV_HEADS = 2
SWA_GROUP = 8
SWA_WIDTH = SWA_KV_HEADS * SWA_GROUP * SWA_HEAD_DIM
KV_WIDTH = SWA_KV_HEADS * SWA_HEAD_DIM
WINDOW = 128
CHUNK = 64

GLA_HEADS = 4
GLA_DK = 128
GLA_DV = 256
GLA_QK_WIDTH = GLA_HEADS * GLA_DK
GLA_V_WIDTH = GLA_HEADS * GLA_DV
GLA_GATE_RANK = 16
GLA_GATE_NORM = 16.0

N_GROUPS = 4
EXPERTS_PER_GROUP = 8
N_EXPERTS = N_GROUPS * EXPERTS_PER_GROUP
TOP_K = 2
EXPERT_HIDDEN = D_MODEL // 4

_C_Q = 0
_C_K = _C_Q + SWA_WIDTH
_C_V = _C_K + KV_WIDTH
_C_GQ = _C_V + KV_WIDTH
_C_GK = _C_GQ + GLA_QK_WIDTH
_C_GV = _C_GK + GLA_QK_WIDTH
_C_R = _C_GV + GLA_V_WIDTH
_C_AB = _C_R + GLA_V_WIDTH

LANES = 128
SUBLANES = 8
PACKED_WIDTH = D_MODEL // 2
assert PACKED_WIDTH == SUBLANES * LANES
VMEM_LIMIT = 56 * 1024 * 1024

ROUTE_E0, ROUTE_E1, ROUTE_G0, ROUTE_G1 = 0, 1, 2, 3


def _dot(a, b):
    return jnp.dot(a, b, preferred_element_type=f32)


def _dot_nt(a, b):
    return lax.dot_general(a, b, (((1,), (1,)), ((), ())), preferred_element_type=f32)


def _dot_tn(a, b):
    return lax.dot_general(a, b, (((0,), (0,)), ((), ())), preferred_element_type=f32)


def _silu(x):
    return x / (1.0 + jnp.exp(-x))


def _rows(ref):
    v = ref[...]
    return v.reshape(v.shape[-2:])


def _rms(x):
    return x * lax.rsqrt(jnp.mean(x * x, axis=-1, keepdims=True) + EPS)


def _resident(shape):
    return pl.BlockSpec(shape, lambda *_: (0,) * len(shape), pipeline_mode=pl.Buffered(1))


def _mod_spec(comp, tiles_per_seq, n_tiles, lag=0):
    def index(i):
        tile = jnp.minimum(jnp.maximum(i - lag, 0), n_tiles - 1)
        return ((tile // tiles_per_seq) * N_MOD + comp, 0, 0)
    return pl.BlockSpec((1, 1, D_MODEL), index)


def _adaln_kernel(c_ref, w_ref, b_ref, o_ref):
    a = _silu(c_ref[...]).astype(bf16)
    o_ref[...] = _dot(a, w_ref[...].astype(bf16)) + b_ref[...]


def _adaln(c_all, w_ada, b_ada, tn=1024):
    r = c_all.shape[0]
    n = w_ada.shape[1]
    return pl.pallas_call(
        _adaln_kernel,
        out_shape=jax.ShapeDtypeStruct((r, n), f32),
        grid=(n // tn,),
        in_specs=[pl.BlockSpec((r, D_MODEL), lambda j: (0, 0)),
                  pl.BlockSpec((D_MODEL, tn), lambda j: (0, j)),
                  pl.BlockSpec((1, tn), lambda j: (0, j))],
        out_specs=pl.BlockSpec((r, tn), lambda j: (0, j)),
        compiler_params=pltpu.CompilerParams(dimension_semantics=("arbitrary",),
                                             vmem_limit_bytes=VMEM_LIMIT),
        name="adaln",
    )(c_all, w_ada, b_ada.reshape(1, n))


def _proj_kernel(x_ref, sh_ref, sc_ref, g_ref, wq_ref, w_ref, wab_ref, wg_ref, bg_ref,
                 q_ref, k_ref, v_ref, gq_ref, gk_ref, gv_ref, r_ref, lg_ref, *, q_transposed):
    h = (_rms(x_ref[...]) * g_ref[...]) * (1.0 + _rows(sc_ref)) + _rows(sh_ref)
    hb = h.astype(bf16)
    col = lambda c: c
    ab = _dot(hb, wab_ref[...])
    r_ref[...] = _silu(_dot(hb, w_ref[:, col(_C_R):col(_C_AB)])).astype(bf16)
    gq_ref[...] = (_dot(hb, w_ref[:, col(_C_GQ):col(_C_GK)]) * (GLA_DK ** -0.5)).astype(bf16)
    z = _dot(ab.astype(bf16), wg_ref[...]) + bg_ref[...]
    log_sig = jnp.minimum(z, 0.0) - jnp.log1p(jnp.exp(-jnp.abs(z)))
    lg_ref[...] = log_sig * (LOG2_E / GLA_GATE_NORM)
    gk_ref[...] = _dot(hb, w_ref[:, col(_C_GK):col(_C_GV)]).astype(bf16)
    gv_ref[...] = _dot(hb, w_ref[:, col(_C_GV):col(_C_R)]).astype(bf16)
    kv = _dot(hb, w_ref[:, col(_C_K):col(_C_GQ)])
    k_ref[...] = kv[:, :KV_WIDTH]
    v_ref[...] = kv[:, KV_WIDTH:]
    if q_transposed:
        q_ref[...] = (_dot_nt(wq_ref[...], hb) * (LOG2_E * SWA_HEAD_DIM ** -0.5)).astype(bf16)
    else:
        q_ref[...] = _dot(hb, wq_ref[...]).astype(bf16)


def _proj(x2d, sh, sc, g_mix, w_q, w_rest, w_ab, w_gate, b_gate, *, tm, mod_specs, q_transposed):
    n = x2d.shape[0]
    row = lambda w: pl.BlockSpec((tm, w), lambda i: (i, 0))
    outs = [(KV_WIDTH, f32), (KV_WIDTH, f32), (GLA_QK_WIDTH, bf16),
            (GLA_QK_WIDTH, bf16), (GLA_V_WIDTH, bf16), (GLA_V_WIDTH, bf16), (GLA_QK_WIDTH, f32)]
    if q_transposed:
        q_shape, q_spec = (SWA_WIDTH, n), pl.BlockSpec((SWA_WIDTH, tm), lambda i: (0, i))
    else:
        q_shape, q_spec = (n, SWA_WIDTH), row(SWA_WIDTH)
    return pl.pallas_call(
        functools.partial(_proj_kernel, q_transposed=q_transposed),
        out_shape=[jax.ShapeDtypeStruct(q_shape, bf16)]
                  + [jax.ShapeDtypeStruct((n, w), dt) for w, dt in outs],
        grid=(n // tm,),
        in_specs=[row(D_MODEL), mod_specs[0], mod_specs[1],
                  _resident((1, D_MODEL)), _resident(w_q.shape), _resident(w_rest.shape),
                  _resident(w_ab.shape), _resident(w_gate.shape), _resident((1, GLA_QK_WIDTH))],
        out_specs=[q_spec] + [row(w) for w, _ in outs],
        compiler_params=pltpu.CompilerParams(dimension_semantics=("parallel",),
                                             vmem_limit_bytes=VMEM_LIMIT),
        name="proj",
    )(x2d, sh, sc, g_mix.reshape(1, D_MODEL), w_q, w_rest, w_ab, w_gate, b_gate.reshape(1, -1))


def _swa_kernel(q_ref, kp_ref, vp_ref, kc_ref, vc_ref, sink_ref, o_ref, *, tq):
    pad = jnp.zeros((WINDOW - tq, KV_WIDTH), f32)
    k_all = jnp.concatenate([kp_ref[...], kc_ref[...], pad], axis=0).astype(bf16)
    v_all = jnp.concatenate([vp_ref[...], vc_ref[...], pad], axis=0).astype(bf16)
    rows, cols = SWA_GROUP * tq, 2 * WINDOW
    valid = lax.broadcasted_iota(jnp.int32, (rows, cols), 1) < WINDOW + tq
    q = q_ref[...]
    outs = []
    for j in range(SWA_KV_HEADS):
        heads = [q[:, (j * SWA_GROUP + g) * SWA_HEAD_DIM:(j * SWA_GROUP + g + 1) * SWA_HEAD_DIM]
                 for g in range(SWA_GROUP)]
        qs = jnp.concatenate(heads, axis=0)
        kj = k_all[:, j * SWA_HEAD_DIM:(j + 1) * SWA_HEAD_DIM]
        vj = v_all[:, j * SWA_HEAD_DIM:(j + 1) * SWA_HEAD_DIM]
        s = _dot_nt(qs, kj) * (SWA_HEAD_DIM ** -0.5)
        s = jnp.where(valid, s, NEG_INF)
        sink = sink_ref[j]
        m = jnp.maximum(jnp.max(s, axis=1, keepdims=True), sink)
        p = jnp.exp(s - m)
        den = jnp.sum(p, axis=1, keepdims=True) + jnp.exp(sink - m)
        o = _dot(p.astype(bf16), vj) / den
        outs.append(jnp.concatenate([o[g * tq:(g + 1) * tq] for g in range(SWA_GROUP)], axis=1))
    o_ref[...] = jnp.concatenate(outs, axis=1).astype(bf16)


def _swa(q, k_past, v_past, k_new, v_new, sinks, *, n_seq, tq):
    sink_rows = jnp.repeat(sinks.astype(f32).reshape(SWA_KV_HEADS, SWA_GROUP), tq, axis=1)
    sink_rows = sink_rows.reshape(SWA_KV_HEADS, SWA_GROUP * tq, 1)
    new = lambda w: pl.BlockSpec((tq, w), lambda b: (b, 0))
    past = pl.BlockSpec((WINDOW, KV_WIDTH), lambda b: (b, 0))
    return pl.pallas_call(
        functools.partial(_swa_kernel, tq=tq),
        out_shape=jax.ShapeDtypeStruct(q.shape, bf16),
        grid=(n_seq,),
        in_specs=[new(SWA_WIDTH), past, past, new(KV_WIDTH), new(KV_WIDTH),
                  pl.BlockSpec(sink_rows.shape, lambda b: (0, 0, 0))],
        out_specs=new(SWA_WIDTH),
        compiler_params=pltpu.CompilerParams(dimension_semantics=("parallel",),
                                             vmem_limit_bytes=VMEM_LIMIT),
        name="swa",
    )(q, k_past, v_past, k_new, v_new, sink_rows)


def _swa_t_kernel(q_ref, kp_ref, vp_ref, kc_ref, vc_ref, sink_ref, mk_ref, mq0_ref, mq_ref, o_ref, *, n_sub):
    tq = WINDOW
    hd = SWA_HEAD_DIM
    k_all = jnp.concatenate([kp_ref[...], kc_ref[...]], axis=0).astype(bf16)
    v_all = jnp.concatenate([vp_ref[...], vc_ref[...]], axis=0).astype(bf16)
    low_lanes = lax.broadcasted_iota(jnp.int32, (2 * WINDOW, KV_WIDTH), 1) < hd
    ones = jnp.ones((2 * WINDOW, KV_WIDTH), bf16)
    for sub in range(n_sub):
        keys = slice(sub * tq, sub * tq + 2 * WINDOW)
        toks = slice(sub * tq, (sub + 1) * tq)
        mq = (mq0_ref if sub == 0 else mq_ref)[0]
        for j in range(SWA_KV_HEADS):
            head = lambda g: slice((j * SWA_GROUP + g) * hd, (j * SWA_GROUP + g + 1) * hd)
            qs = jnp.concatenate([q_ref[head(g), toks] for g in range(SWA_GROUP)], axis=1)
            own = low_lanes if j == 0 else jnp.logical_not(low_lanes)
            k_aug = jnp.where(own, k_all[keys], mk_ref[j])
            v_aug = jnp.where(own, v_all[keys], ones)
            q_aug = jnp.concatenate([qs, mq] if j == 0 else [mq, qs], axis=0)
            s = _dot(k_aug, q_aug)
            sink = sink_ref[j]
            m = jnp.maximum(jnp.max(s, axis=0, keepdims=True), sink)
            p = jnp.exp2(s - m).astype(bf16)
            o_aug = _dot_tn(v_aug, p)
            pv, p_sum = (o_aug[:hd], o_aug[hd:hd + 1]) if j == 0 else (o_aug[hd:], o_aug[0:1])
            o = pv / (p_sum + jnp.exp2(sink - m))
            for g in range(SWA_GROUP):
                o_ref[head(g), toks] = o[:, g * tq:(g + 1) * tq].astype(bf16)


def _swa_t(q_t, k, v, sinks, *, n_seq, tiles, n_sub):
    tq = WINDOW
    steps = tiles // n_sub
    hd = SWA_HEAD_DIM
    sink_cols = jnp.repeat(sinks.astype(f32).reshape(SWA_KV_HEADS, SWA_GROUP) * LOG2_E, tq, axis=1)
    sink_cols = sink_cols.reshape(SWA_KV_HEADS, 1, SWA_GROUP * tq)
    n_kc = 2 * WINDOW // CHUNK
    key_chunk = jnp.arange(2 * WINDOW) // CHUNK
    lane = jnp.arange(KV_WIDTH)
    mk = jnp.stack([lane[None, :] == hd + key_chunk[:, None], lane[None, :] == key_chunk[:, None]]).astype(bf16)
    q_chunk = (jnp.arange(SWA_GROUP * tq) % tq) // CHUNK
    kc = jnp.arange(hd)[:, None]
    band = (kc >= q_chunk[None, :]) & (kc <= q_chunk[None, :] + WINDOW // CHUNK)
    visible = jnp.stack([band & (kc >= WINDOW // CHUNK), band]) | (kc >= n_kc)
    mq = jnp.where(visible, 0.0, NEG_INF).astype(bf16)
    mq0_spec = pl.BlockSpec((1, hd, SWA_GROUP * tq), lambda b, u: (jnp.minimum(u, 1), 0, 0))
    mq_spec = pl.BlockSpec((1, hd, SWA_GROUP * tq), lambda b, u: (1, 0, 0))
    qspec = pl.BlockSpec((SWA_WIDTH, n_sub * tq), lambda b, u: (0, b * steps + u))
    cur = pl.BlockSpec((n_sub * tq, KV_WIDTH), lambda b, u: (b * steps + u, 0))
    prev = pl.BlockSpec((WINDOW, KV_WIDTH), lambda b, u: (b * tiles + jnp.maximum(n_sub * u - 1, 0), 0))
    return pl.pallas_call(
        functools.partial(_swa_t_kernel, n_sub=n_sub),
        out_shape=jax.ShapeDtypeStruct(q_t.shape, bf16),
        grid=(n_seq, steps),
        in_specs=[qspec, prev, prev, cur, cur, pl.BlockSpec(sink_cols.shape, lambda b, u: (0, 0, 0)),
                  pl.BlockSpec(mk.shape, lambda b, u: (0, 0, 0)), mq0_spec, mq_spec],
        out_specs=qspec,
        compiler_params=pltpu.CompilerParams(dimension_semantics=("parallel", "arbitrary"),
                                             vmem_limit_bytes=VMEM_LIMIT),
        name="swa_t",
    )(q_t, k, v, k, v, sink_cols, mk, mq, mq)


def _gla_kernel(q_ref, k_ref, v_ref, lg_ref, r_ref, gh_ref, s0_ref, o_ref, s_ref, *, c, n_sub):
    @pl.when(pl.program_id(1) == 0)
    def _():
        s_ref[...] = s0_ref[...]

    row = lax.broadcasted_iota(jnp.int32, (c, GLA_DK), 0)
    causal = (lax.broadcasted_iota(jnp.int32, (c, c), 0) >= lax.broadcasted_iota(jnp.int32, (c, c), 1))
    for sub in range(n_sub):
        ts = slice(sub * c, (sub + 1) * c)
        for h in range(GLA_HEADS):
            ks = slice(h * GLA_DK, (h + 1) * GLA_DK)
            vs = slice(h * GLA_DV, (h + 1) * GLA_DV)
            b = lg_ref[ts, ks]
            step = 1
            while step < c:
                b = b + jnp.where(row >= step, pltpu.roll(b, step, 0), 0.0)
                step *= 2
            b_last = b[c - 1:c, :]
            q = q_ref[ts, ks].astype(f32)
            k = k_ref[ts, ks].astype(f32)
            v = v_ref[ts, vs]
            qd = (q * jnp.exp2(b)).astype(bf16)
            kd = (k * jnp.exp2(-b)).astype(bf16)
            kl = (k * jnp.exp2(b_last - b)).astype(bf16)
            a = jnp.where(causal, _dot_nt(qd, kd), 0.0)
            s = s_ref[0, h]
            o = _dot(qd, s.astype(bf16)) + _dot(a.astype(bf16), v)
            decay = jnp.broadcast_to(jnp.exp2(b_last), (GLA_DK, GLA_DK)).T
            s_ref[0, h] = s * jnp.concatenate([decay, decay], axis=1) + _dot_tn(kl, v)
            on = _rms(o) * gh_ref[...]
            o_ref[ts, vs] = (on * r_ref[ts, vs].astype(f32)).astype(bf16)


def _gla(gq, gk, gv, lg, r, g_head, s0, *, n_seq, c, n_sub, steps):
    rows = c * n_sub
    blk = lambda w: pl.BlockSpec((rows, w), lambda b, t: (b * steps + t, 0))
    state = pl.BlockSpec((1, GLA_HEADS, GLA_DK, GLA_DV), lambda b, t: (b, 0, 0, 0))
    return pl.pallas_call(
        functools.partial(_gla_kernel, c=c, n_sub=n_sub),
        out_shape=[jax.ShapeDtypeStruct(gv.shape, bf16),
                   jax.ShapeDtypeStruct((n_seq, GLA_HEADS, GLA_DK, GLA_DV), f32)],
        grid=(n_seq, steps),
        in_specs=[blk(GLA_QK_WIDTH), blk(GLA_QK_WIDTH), blk(GLA_V_WIDTH), blk(GLA_QK_WIDTH),
                  blk(GLA_V_WIDTH), pl.BlockSpec((1, GLA_DV), lambda b, t: (0, 0)), state],
        out_specs=[blk(GLA_V_WIDTH), state],
        compiler_params=pltpu.CompilerParams(dimension_semantics=("parallel", "arbitrary"),
                                             vmem_limit_bytes=VMEM_LIMIT),
        name="gla",
    )(gq, gk, gv, lg, r, g_head.reshape(1, GLA_DV), s0)


def _pack_pair(hb):
    w = hb.shape[1] // 2
    a = lax.bitcast_convert_type(hb[:, :w].astype(f32), u32)
    b = lax.bitcast_convert_type(hb[:, w:].astype(f32), u32)
    return a | (b >> 16)


def _unpack_pair(p):
    a = lax.bitcast_convert_type(p & jnp.uint32(0xFFFF0000), f32)
    b = lax.bitcast_convert_type(p << 16, f32)
    return a, b


def _load_row_tiles(ref, n):
    return jnp.concatenate([ref[pl.ds(s, n, stride=SUBLANES), :] for s in range(SUBLANES)], axis=1)


def _store_row_tiles(ref, val):
    n = val.shape[0]
    for s in range(SUBLANES):
        ref[pl.ds(s, n, stride=SUBLANES), :] = val[:, s * LANES:(s + 1) * LANES]


def _row_tile(ref, r):
    return ref.at[pl.ds(pl.multiple_of(r * SUBLANES, SUBLANES), SUBLANES), :]


def _outproj_kernel(*refs, n_alias, oa_transposed):
    (oa_ref, ob_ref, x_ref, gt_ref, sh_ref, sc_ref, g_ref, w_ref, wr_ref, br_ref) = refs[:10]
    x1_ref, h2_ref, route_ref, x1_s = refs[10 + n_alias:]

    @pl.when(pl.program_id(0) == 0)
    def _():
        x1_s[...] = jnp.zeros(x1_s.shape, x1_s.dtype)

    h2 = (_rms(x1_s[...]) * g_ref[...]) * (1.0 + _rows(sc_ref)) + _rows(sh_ref)
    hi = h2.astype(bf16)
    _store_row_tiles(h2_ref, _pack_pair(hi))
    logits = _dot(hi, wr_ref[...]) + br_ref[...]
    lane = lax.broadcasted_iota(jnp.int32, logits.shape, 1)
    lane_f = lane.astype(f32)
    neg = float("-inf")
    first = lambda hit: jnp.min(jnp.where(hit, lane_f, float(LANES)), axis=1, keepdims=True)
    lg_g = jnp.where(lane < N_GROUPS, logits, neg)
    g_max = jnp.max(lg_g, axis=1, keepdims=True)
    g_sel = first(lg_g == g_max)
    p_sel = 1.0 / jnp.sum(jnp.exp(lg_g - g_max), axis=1, keepdims=True)
    e_lane = lane - N_GROUPS
    in_group = (e_lane >= 0) & (e_lane < N_EXPERTS) & ((e_lane // EXPERTS_PER_GROUP).astype(f32) == g_sel)
    lg_e = jnp.where(in_group, logits, neg)
    v1 = jnp.max(lg_e, axis=1, keepdims=True)
    i1 = first(lg_e == v1)
    lg_e2 = jnp.where(lane_f == i1, neg, lg_e)
    v2 = jnp.max(lg_e2, axis=1, keepdims=True)
    i2 = first(lg_e2 == v2)
    e = jnp.exp(v2 - v1)
    g1 = p_sel / (1.0 + e)
    g2 = p_sel * e / (1.0 + e)
    rec = jnp.where(lane == ROUTE_E0, i1 - N_GROUPS, 0.0)
    rec = jnp.where(lane == ROUTE_E1, i2 - N_GROUPS, rec)
    rec = jnp.where(lane == ROUTE_G0, g1, rec)
    rec = jnp.where(lane == ROUTE_G1, g2, rec)
    route_ref[...] = rec

    oa_dot = _dot_tn if oa_transposed else _dot
    mix = oa_dot(oa_ref[...], w_ref[:SWA_WIDTH, :]) + _dot(ob_ref[...], w_ref[SWA_WIDTH:, :])
    x1 = x_ref[...] + _rows(gt_ref) * mix
    x1_ref[...] = x1
    x1_s[...] = x1


def _outproj(oa, ob, x2d, gt, sh, sc, g_ffn, w_out, w_route, b_route, *, tm, mod_specs,
             n_total, row0, bufs=None, fill_steps=0, oa_transposed=False):
    n = x2d.shape[0]
    tiles = n // tm
    blocks = tiles + fill_steps
    row = lambda w: pl.BlockSpec((tm, w), lambda i: (jnp.minimum(i, tiles - 1), 0))
    oa_spec = row(SWA_WIDTH)
    if oa_transposed:
        oa_spec = pl.BlockSpec((SWA_WIDTH, tm), lambda i: (0, jnp.minimum(i, tiles - 1)))
    head_blk = lambda i: row0 // tm + jnp.minimum(i, blocks - 1)
    tail_blk = lambda i: row0 // tm + jnp.maximum(i - 1, 0)
    out_x1 = pl.BlockSpec((tm, D_MODEL), lambda i: (head_blk(i), 0))
    out_tiles = pl.BlockSpec((tm * SUBLANES, LANES), lambda i: (tail_blk(i), 0))
    out_route = pl.BlockSpec((tm, LANES), lambda i: (tail_blk(i), 0))
    alias_in = list(bufs) if bufs is not None else []
    n_in = 10
    return pl.pallas_call(
        functools.partial(_outproj_kernel, n_alias=len(alias_in), oa_transposed=oa_transposed),
        out_shape=[jax.ShapeDtypeStruct((n_total, D_MODEL), f32),
                   jax.ShapeDtypeStruct((n_total * SUBLANES, LANES), u32),
                   jax.ShapeDtypeStruct((n_total, LANES), f32)],
        grid=(blocks + 1,),
        in_specs=[oa_spec, row(GLA_V_WIDTH), row(D_MODEL), mod_specs[0], mod_specs[1],
                  mod_specs[2], _resident((1, D_MODEL)), _resident(w_out.shape),
                  _resident(w_route.shape), _resident((1, LANES))]
                 + [pl.BlockSpec(memory_space=pl.ANY)] * len(alias_in),
        out_specs=[out_x1, out_tiles, out_route],
        scratch_shapes=[pltpu.VMEM((tm, D_MODEL), f32)],
        input_output_aliases={n_in + a: a for a in range(len(alias_in))},
        compiler_params=pltpu.CompilerParams(dimension_semantics=("arbitrary",),
                                             vmem_limit_bytes=VMEM_LIMIT),
        name="outproj",
    )(oa, ob, x2d, gt, sh, sc, g_ffn.reshape(1, D_MODEL), w_out, w_route, b_route, *alias_in)


def _rank_kernel(route_ref, pos_ref, pend_ref, rec_s, *, n_chunks, tb):
    n_e = N_EXPERTS
    expert = lax.broadcasted_iota(jnp.int32, (n_e, LANES), 0)
    expert_f = expert.astype(f32)
    earlier = (lax.broadcasted_iota(jnp.int32, (LANES, LANES), 0)
               < lax.broadcasted_iota(jnp.int32, (LANES, LANES), 1)).astype(bf16)

    def onehots(c):
        rec = rec_s[c]
        return expert_f == rec[ROUTE_E0:ROUTE_E0 + 1, :], expert_f == rec[ROUTE_E1:ROUTE_E1 + 1, :]

    def count(c, cnt):
        rec_s[c] = route_ref[pl.ds(pl.multiple_of(c * LANES, LANES), LANES), :].T[:SUBLANES, :]
        h0, h1 = onehots(c)
        return cnt + jnp.sum((h0 | h1).astype(f32), axis=1, keepdims=True)

    cnt = lax.fori_loop(0, n_chunks, count, jnp.zeros((n_e, 1), f32), unroll=4)
    padded = jnp.floor((cnt + (tb - 1.0)) / tb) * tb
    end = jnp.broadcast_to(padded, (n_e, LANES))
    step = 1
    while step < n_e:
        end = end + jnp.where(expert >= step, pltpu.roll(end, step, 0), 0.0)
        step *= 2
    pend_ref[0] = end.astype(jnp.int32)
    pend_ref[1] = jnp.broadcast_to(jnp.where(cnt > 0, cnt - (padded - tb), 0.0), (n_e, LANES)).astype(jnp.int32)

    def rank(c, base):
        h0, h1 = onehots(c)
        both = h0 | h1
        off = _dot(both.astype(bf16), earlier) + base
        pos_ref[0, pl.ds(c, 1), :] = jnp.sum(jnp.where(h0, off, 0.0), axis=0, keepdims=True).astype(jnp.int32)
        pos_ref[1, pl.ds(c, 1), :] = jnp.sum(jnp.where(h1, off, 0.0), axis=0, keepdims=True).astype(jnp.int32)
        return base + jnp.sum(both.astype(f32), axis=1, keepdims=True)

    lax.fori_loop(0, n_chunks, rank, (end - padded)[:, 0:1], unroll=4)


def _rank(route, *, tb):
    n = route.shape[0]
    n_chunks = n // LANES
    return pl.pallas_call(
        functools.partial(_rank_kernel, n_chunks=n_chunks, tb=tb),
        out_shape=[jax.ShapeDtypeStruct((TOP_K, n_chunks, LANES), jnp.int32),
                   jax.ShapeDtypeStruct((2, N_EXPERTS, LANES), jnp.int32)],
        in_specs=[_resident(route.shape)],
        scratch_shapes=[pltpu.VMEM((n_chunks, SUBLANES, LANES), f32)],
        compiler_params=pltpu.CompilerParams(vmem_limit_bytes=VMEM_LIMIT),
        name="rank",
    )(route)


def _dispatch_kernel(pos_ref, pend_ref, h2_ref, xs_hbm, stage, zbuf, sem, zsem, *, tm, tb, n_tot, steps):
    i = pl.program_id(0)
    slot = i % 2

    blk = tb * SUBLANES

    def wait_slot(s):
        for _ in range(TOP_K):
            pltpu.make_async_copy(stage.at[s], xs_hbm.at[pl.ds(0, tm * SUBLANES), :], sem.at[s]).wait()

    @pl.when(i == 0)
    def _():
        zbuf[...] = jnp.zeros(zbuf.shape, zbuf.dtype)

        def fill_block(b, start):
            cp = pltpu.make_async_copy(zbuf, xs_hbm.at[pl.ds(pl.multiple_of(b * blk, blk), blk), :], zsem)
            cp.start() if start else cp.wait()

        def fill(e, start):
            end = pend_ref[e]
            prev = jnp.where(e > 0, pend_ref[jnp.maximum(e - 1, 0)], 0)

            @pl.when(end > prev)
            def _():
                fill_block(end // tb - 1, start)

        fill_unused = fill_block

        first_unused = pend_ref[N_EXPERTS - 1] // tb
        n_blocks = xs_hbm.shape[0] // blk
        lax.fori_loop(0, N_EXPERTS, lambda e, c: (fill(e, True), c)[1], 0)
        lax.fori_loop(first_unused, n_blocks, lambda b, c: (fill_unused(b, True), c)[1], 0)
        lax.fori_loop(0, N_EXPERTS, lambda e, c: (fill(e, False), c)[1], 0)
        lax.fori_loop(first_unused, n_blocks, lambda b, c: (fill_unused(b, False), c)[1], 0)

    @pl.when(i >= 2)
    def _():
        wait_slot(slot)

    stage[slot] = h2_ref[...]

    def scatter(r, carry):
        for k in range(TOP_K):
            d = pos_ref[k * n_tot + i * tm + r]
            pltpu.make_async_copy(_row_tile(stage.at[slot], r), _row_tile(xs_hbm, d),
                                  sem.at[slot]).start(priority=k)
        return carry

    lax.fori_loop(0, tm, scatter, 0, unroll=8)

    @pl.when(i == steps - 1)
    def _():
        wait_slot(slot)
        if steps > 1:
            wait_slot(1 - slot)


def _dispatch(pos_flat, pend, h2p, *, tm, tb, n_blocks):
    n_tot = h2p.shape[0] // SUBLANES
    steps = n_tot // tm
    return pl.pallas_call(
        functools.partial(_dispatch_kernel, tm=tm, tb=tb, n_tot=n_tot, steps=steps),
        out_shape=jax.ShapeDtypeStruct((n_blocks * tb * SUBLANES, LANES), u32),
        grid_spec=pltpu.PrefetchScalarGridSpec(
            num_scalar_prefetch=2,
            grid=(steps,),
            in_specs=[pl.BlockSpec((tm * SUBLANES, LANES), lambda i, pos, pend: (i, 0))],
            out_specs=pl.BlockSpec(memory_space=pl.ANY),
            scratch_shapes=[pltpu.VMEM((2, tm * SUBLANES, LANES), u32),
                            pltpu.VMEM((tb * SUBLANES, LANES), u32),
                            pltpu.SemaphoreType.DMA((2,)), pltpu.SemaphoreType.DMA(())]),
        compiler_params=pltpu.CompilerParams(dimension_semantics=("arbitrary",),
                                             vmem_limit_bytes=VMEM_LIMIT),
        name="dispatch",
    )(pos_flat, pend, h2p)


def _expert_kernel(be_ref, ne_ref, hf_ref, nu_ref, xs_ref, wg_hbm, wu_hbm, wd_hbm, y_ref,
                   wg_s, wu_s, wd_s, wg_b, wu_b, wd_b, sem, *, tb):
    i = pl.program_id(0)
    used = i < nu_ref[0]
    e = be_ref[i]

    def fetch(expert):
        pairs = ((wg_hbm, wg_s), (wu_hbm, wu_s), (wd_hbm, wd_s))
        return [pltpu.make_async_copy(src.at[expert], dst, sem.at[n]) for n, (src, dst) in enumerate(pairs)]

    @pl.when(i == 0)
    def _():
        for cp in fetch(e):
            cp.start()

    @pl.when(used & ((i == 0) | (e != be_ref[jnp.maximum(i - 1, 0)])))
    def _():
        for cp in fetch(e):
            cp.wait()
        wg_b[...] = wg_s[...].astype(bf16)
        wu_b[...] = wu_s[...].astype(bf16)
        wd_b[...] = wd_s[...].astype(bf16)
        nxt = ne_ref[i]

        @pl.when(nxt != e)
        def _():
            for cp in fetch(nxt):
                cp.start()

    def ffn(rows):
        tiles = pl.ds(0, rows * SUBLANES)
        a, b = _unpack_pair(_load_row_tiles(xs_ref.at[tiles], rows))
        x = jnp.concatenate([a.astype(bf16), b.astype(bf16)], axis=1)
        g = _dot(x, wg_b[...])
        u = _dot(x, wu_b[...])
        y = _dot((_silu(g) * u).astype(bf16), wd_b[...])
        _store_row_tiles(y_ref.at[tiles], _pack_pair(y.astype(bf16)))

    half_full = hf_ref[i] == 1

    @pl.when(used & jnp.logical_not(half_full))
    def _():
        ffn(tb)

    @pl.when(used & half_full)
    def _():
        ffn(tb // 2)
        rest = pl.ds(tb // 2 * SUBLANES, tb // 2 * SUBLANES)
        y_ref[rest, :] = jnp.zeros((tb // 2 * SUBLANES, LANES), y_ref.dtype)

    @pl.when(jnp.logical_not(used))
    def _():
        y_ref[...] = jnp.zeros(y_ref.shape, y_ref.dtype)


def _experts(block_e, next_e, half_full, n_used, xs, w_eg, w_eu, w_ed, *, tb):
    n_blocks = block_e.shape[0]
    blk = (tb * SUBLANES, LANES)
    up, down = (D_MODEL, EXPERT_HIDDEN), (EXPERT_HIDDEN, D_MODEL)
    hbm = pl.BlockSpec(memory_space=pl.ANY)
    return pl.pallas_call(
        functools.partial(_expert_kernel, tb=tb),
        out_shape=jax.ShapeDtypeStruct(xs.shape, u32),
        grid_spec=pltpu.PrefetchScalarGridSpec(
            num_scalar_prefetch=4,
            grid=(n_blocks,),
            in_specs=[pl.BlockSpec(blk, lambda i, be, ne, hf, nu: (jnp.minimum(i, nu[0] - 1), 0)),
                      hbm, hbm, hbm],
            out_specs=pl.BlockSpec(blk, lambda i, be, ne, hf, nu: (i, 0)),
            scratch_shapes=[pltpu.VMEM(up, f32), pltpu.VMEM(up, f32), pltpu.VMEM(down, f32),
                            pltpu.VMEM(up, bf16), pltpu.VMEM(up, bf16), pltpu.VMEM(down, bf16),
                            pltpu.SemaphoreType.DMA((3,))]),
        compiler_params=pltpu.CompilerParams(dimension_semantics=("arbitrary",),
                                             vmem_limit_bytes=VMEM_LIMIT),
        name="experts",
    )(block_e, next_e, half_full, n_used, xs, w_eg, w_eu, w_ed)


def _combine_kernel(pos_ref, yb_hbm, x1_ref, route_ref, gt_ref, gf_ref, y_ref, ybuf, sem, *, tm):
    i = pl.program_id(0)
    slot = i % 2

    def start(blk, s):
        def body(r, carry):
            for k in range(TOP_K):
                t = pos_ref[(k * pl.num_programs(0) + blk) * tm + r]
                pltpu.make_async_copy(_row_tile(yb_hbm, t), _row_tile(ybuf.at[s, k], r),
                                      sem.at[s]).start(priority=k)
            return carry
        lax.fori_loop(0, tm, body, 0, unroll=8)

    @pl.when(i == 0)
    def _():
        start(0, 0)

    @pl.when(i + 1 < pl.num_programs(0))
    def _():
        start(i + 1, 1 - slot)

    for k in range(TOP_K):
        pltpu.make_async_copy(yb_hbm.at[pl.ds(0, tm * SUBLANES), :], ybuf.at[slot, k], sem.at[slot]).wait()
    route = route_ref[...]
    a0, b0 = _unpack_pair(_load_row_tiles(ybuf.at[slot, 0], tm))
    a1, b1 = _unpack_pair(_load_row_tiles(ybuf.at[slot, 1], tm))
    g0 = route[:, ROUTE_G0:ROUTE_G0 + 1]
    g1 = route[:, ROUTE_G1:ROUTE_G1 + 1]
    moe = jnp.concatenate([a0 * g0 + a1 * g1, b0 * g0 + b1 * g1], axis=1)
    x2 = x1_ref[...] + _rows(gt_ref) * moe
    y_ref[...] = _rms(x2) * gf_ref[...]


def _combine(pos_km, yb, x1, route, gt, g_final, *, tm, n, row0, gt_spec):
    blk0 = row0 // tm
    return pl.pallas_call(
        functools.partial(_combine_kernel, tm=tm),
        out_shape=jax.ShapeDtypeStruct((n, D_MODEL), f32),
        grid_spec=pltpu.PrefetchScalarGridSpec(
            num_scalar_prefetch=1,
            grid=(n // tm,),
            in_specs=[pl.BlockSpec(memory_space=pl.ANY),
                      pl.BlockSpec((tm, D_MODEL), lambda i, pos: (blk0 + i, 0)),
                      pl.BlockSpec((tm, LANES), lambda i, pos: (blk0 + i, 0)),
                      gt_spec,
                      pl.BlockSpec((1, D_MODEL), lambda i, pos: (0, 0))],
            out_specs=pl.BlockSpec((tm, D_MODEL), lambda i, pos: (i, 0)),
            scratch_shapes=[pltpu.VMEM((2, TOP_K, tm * SUBLANES, LANES), u32),
                            pltpu.SemaphoreType.DMA((2,))]),
        compiler_params=pltpu.CompilerParams(dimension_semantics=("arbitrary",),
                                             vmem_limit_bytes=VMEM_LIMIT),
        name="combine",
    )(pos_km, yb, x1, route, gt, g_final.reshape(1, D_MODEL))


def kernel(x_prompt, x_sample, cache_swa_k, cache_swa_v, state_gla, c_prompt, c_sample, g_mix_norm, g_ffn_norm, w_ada, b_ada, w_in, attn_sinks, w_gla_gate, b_gla_gate, g_gla_norm, w_out, w_router_group, b_router_group, w_router_expert, b_router_expert, w_expert_gate, w_expert_up, w_expert_down, g_final):
    depth = w_in.shape[0]
    assert depth == 1
    bp, t, d = x_prompt.shape
    bs, ts, _ = x_sample.shape
    n_p, n_s = bp * t, bs * ts
    n_tot = n_p + n_s
    assert n_tot % LANES == 0
    tm = 512
    to = 512
    tb = 512
    tc = 256
    td = LANES
    gla_c = 128
    gla_sub = 16
    swa_sub = 8

    w_in0 = w_in[0]
    w_rest = w_in0.astype(bf16)
    w_q = w_rest[:, :_C_K]
    w_q_t = w_q.T
    w_ab = jnp.pad(w_in0[:, _C_AB:], ((0, 0), (0, LANES - GLA_GATE_RANK))).astype(bf16)
    w_gate = jnp.pad(w_gla_gate[0], ((0, LANES - GLA_GATE_RANK), (0, 0))).astype(bf16)
    w_out_b = w_out[0].astype(bf16)
    n_r = N_GROUPS + N_EXPERTS
    w_r = jnp.pad(jnp.concatenate([w_router_group[0], w_router_expert[0]], axis=1),
                  ((0, 0), (0, LANES - n_r)))
    w_route = w_r.astype(bf16)
    b_route = jnp.pad(jnp.concatenate([b_router_group[0], b_router_expert[0]]),
                      (0, LANES - n_r)).reshape(1, LANES)

    c_all = jnp.concatenate([c_prompt, c_sample], axis=0)
    mod = _adaln(c_all, w_ada[0], b_ada[0])
    mod_tab = mod.reshape((bp + bs) * N_MOD, 1, d)
    mod_s = jnp.repeat(mod[bp:].reshape(bs, N_MOD, d), ts, axis=0)
    mod_s = [mod_s[:, m] for m in range(N_MOD)]
    pmod = lambda comp, tile, lag=0: _mod_spec(comp, t // tile, n_p // tile, lag)
    smod = pl.BlockSpec((n_s, d), lambda i: (0, 0))

    xp = x_prompt.reshape(n_p, d)
    xs = x_sample.reshape(n_s, d)
    proj_w = (w_rest, w_ab, w_gate, b_gla_gate[0])
    qp, kp, vp, gqp, gkp, gvp, rp, lgp = _proj(xp, mod_tab, mod_tab, g_mix_norm[0], w_q_t, *proj_w, tm=tm,
                                               mod_specs=(pmod(0, tm), pmod(1, tm)), q_transposed=True)
    qs, ks, vs, gqs, gks, gvs, rs, lgs = _proj(xs, mod_s[0], mod_s[1], g_mix_norm[0], w_q, *proj_w, tm=n_s,
                                               mod_specs=(smod, smod), q_transposed=False)

    sinks = attn_sinks[0]
    oap = _swa_t(qp, kp, vp, sinks, n_seq=bp, tiles=t // WINDOW, n_sub=swa_sub)
    ck = cache_swa_k[0].reshape(bs * WINDOW, KV_WIDTH)
    cv = cache_swa_v[0].reshape(bs * WINDOW, KV_WIDTH)
    oas = _swa(qs, ck, cv, ks, vs, sinks, n_seq=bs, tq=ts)
    s_zero = jnp.zeros((bp, GLA_HEADS, GLA_DK, GLA_DV), f32)
    obp, sp = _gla(gqp, gkp, gvp, lgp, rp, g_gla_norm[0], s_zero, n_seq=bp, c=gla_c, n_sub=gla_sub,
                   steps=t // (gla_c * gla_sub))
    obs, ss = _gla(gqs, gks, gvs, lgs, rs, g_gla_norm[0], state_gla[0], n_seq=bs, c=ts, n_sub=1, steps=1)

    out_w = (g_ffn_norm[0], w_out_b, w_route, b_route)
    bufs = _outproj(oap, obp, xp, mod_tab, mod_tab, mod_tab, *out_w, tm=to,
                    mod_specs=(pmod(2, to), pmod(3, to, 1), pmod(4, to, 1)), n_total=n_tot, row0=0,
                    fill_steps=-(-n_s // to), oa_transposed=True)
    x1, h2p, route = _outproj(oas, obs, xs, mod_s[2], mod_s[3], mod_s[4], *out_w, tm=n_s,
                              mod_specs=(smod, smod, smod), n_total=n_tot, row0=n_p, bufs=bufs)

    n_blocks = -(-(n_tot * TOP_K + N_EXPERTS * (tb - 1)) // tb)
    pos, pend_tab = _rank(route, tb=tb)
    pend, last_rows = pend_tab[0, :, 0], pend_tab[1, :, 0]
    block_row0 = jnp.arange(n_blocks, dtype=jnp.int32) * tb
    block_e = jnp.minimum(jnp.sum(pend[None, :] <= block_row0[:, None], axis=1), N_EXPERTS - 1).astype(jnp.int32)
    n_used = pend[N_EXPERTS - 1:] // tb
    experts = jnp.arange(N_EXPERTS, dtype=jnp.int32)
    has_rows = jnp.diff(pend, prepend=0) > 0
    later = jnp.where((experts[None, :] > experts[:, None]) & has_rows[None, :], experts[None, :], N_EXPERTS)
    next_with_rows = jnp.min(later, axis=1)
    next_with_rows = jnp.where(next_with_rows == N_EXPERTS, experts, next_with_rows)
    of_block = lambda per_expert: jnp.sum(jnp.where(block_e[:, None] == experts[None, :], per_expert[None, :], 0),
                                          axis=1).astype(jnp.int32)
    next_e = of_block(next_with_rows)
    half_full = ((block_row0 + tb == of_block(pend)) & (of_block(last_rows) <= tb // 2)).astype(jnp.int32)
    xsort = _dispatch(pos.reshape(-1), pend, h2p, tm=td, tb=tb, n_blocks=n_blocks)
    yb = _experts(block_e, next_e, half_full, n_used, xsort, w_expert_gate[0], w_expert_up[0],
                  w_expert_down[0], tb=tb)

    pos = pos.reshape(TOP_K, n_tot)
    pos_p = pos[:, :n_p].reshape(-1)
    pos_s = pos[:, n_p:].reshape(-1)
    gt_p = pl.BlockSpec((1, 1, d), lambda i, p: ((i // (t // tc)) * N_MOD + 5, 0, 0))
    gt_s = pl.BlockSpec((n_s, d), lambda i, p: (0, 0))
    y_p = _combine(pos_p, yb, x1, route, mod_tab, g_final, tm=tc, n=n_p, row0=0, gt_spec=gt_p)
    y_s = _combine(pos_s, yb, x1, route, mod_s[5], g_final, tm=n_s, n=n_s, row0=n_p, gt_spec=gt_s)

    kv_shape = (SWA_KV_HEADS, SWA_HEAD_DIM)
    last = lambda a: a.reshape(bp, t, KV_WIDTH)[:, t - WINDOW:, :].reshape(1, bp, WINDOW, *kv_shape)
    k_state_p, v_state_p = last(kp), last(vp)
    return (y_p.reshape(bp, t, d), y_s.reshape(bs, ts, d), k_state_p, v_state_p, sp[None],
            ks.reshape(bs, ts, *kv_shape)[None], vs.reshape(bs, ts, *kv_shape)[None], ss[None])
```

```python
import functools
import math

import jax
import jax.numpy as jnp
from jax import lax
from jax.experimental import pallas as pl
from jax.experimental.pallas import tpu as pltpu

f32 = jnp.float32
bf16 = jnp.bfloat16
u32 = jnp.uint32

D_MODEL = 2048
N_MOD = 6
EPS = 1e-6
NEG_INF = -1e30
LOG2_E = math.log2(math.e)

SWA_HEAD_DIM = 64
SWA_KV_HEADS = 2
SWA_GROUP = 8
SWA_WIDTH = SWA_KV_HEADS * SWA_GROUP * SWA_HEAD_DIM
KV_WIDTH = SWA_KV_HEADS * SWA_HEAD_DIM
WINDOW = 128
CHUNK = 64

GLA_HEADS = 4
GLA_DK = 128
GLA_DV = 256
GLA_QK_WIDTH = GLA_HEADS * GLA_DK
GLA_V_WIDTH = GLA_HEADS * GLA_DV
GLA_GATE_RANK = 16
GLA_GATE_NORM = 16.0

N_GROUPS = 4
EXPERTS_PER_GROUP = 8
N_EXPERTS = N_GROUPS * EXPERTS_PER_GROUP
TOP_K = 2
EXPERT_HIDDEN = D_MODEL // 4

_C_Q = 0
_C_K = _C_Q + SWA_WIDTH
_C_V = _C_K + KV_WIDTH
_C_GQ = _C_V + KV_WIDTH
_C_GK = _C_GQ + GLA_QK_WIDTH
_C_GV = _C_GK + GLA_QK_WIDTH
_C_R = _C_GV + GLA_V_WIDTH
_C_AB = _C_R + GLA_V_WIDTH

LANES = 128
SUBLANES = 8
PACKED_WIDTH = D_MODEL // 2
assert PACKED_WIDTH == SUBLANES * LANES
VMEM_LIMIT = 56 * 1024 * 1024

ROUTE_E0, ROUTE_E1, ROUTE_G0, ROUTE_G1 = 0, 1, 2, 3


def _dot(a, b):
    return jnp.dot(a, b, preferred_element_type=f32)


def _dot_nt(a, b):
    return lax.dot_general(a, b, (((1,), (1,)), ((), ())), preferred_element_type=f32)


def _dot_tn(a, b):
    return lax.dot_general(a, b, (((0,), (0,)), ((), ())), preferred_element_type=f32)


def _silu(x):
    return x / (1.0 + jnp.exp(-x))


def _rows(ref):
    v = ref[...]
    return v.reshape(v.shape[-2:])


def _rms(x):
    return x * lax.rsqrt(jnp.mean(x * x, axis=-1, keepdims=True) + EPS)


def _resident(shape):
    return pl.BlockSpec(shape, lambda *_: (0,) * len(shape), pipeline_mode=pl.Buffered(1))


def _mod_spec(comp, tiles_per_seq, n_tiles, lag=0):
    def index(i):
        tile = jnp.minimum(jnp.maximum(i - lag, 0), n_tiles - 1)
        return ((tile // tiles_per_seq) * N_MOD + comp, 0, 0)
    return pl.BlockSpec((1, 1, D_MODEL), index)


def _adaln_kernel(c_ref, w_ref, b_ref, o_ref):
    a = _silu(c_ref[...]).astype(bf16)
    o_ref[...] = _dot(a, w_ref[...].astype(bf16)) + b_ref[...]


def _adaln(c_all, w_ada, b_ada, tn=1024):
    r = c_all.shape[0]
    n = w_ada.shape[1]
    return pl.pallas_call(
        _adaln_kernel,
        out_shape=jax.ShapeDtypeStruct((r, n), f32),
        grid=(n // tn,),
        in_specs=[pl.BlockSpec((r, D_MODEL), lambda j: (0, 0)),
                  pl.BlockSpec((D_MODEL, tn), lambda j: (0, j)),
                  pl.BlockSpec((1, tn), lambda j: (0, j))],
        out_specs=pl.BlockSpec((r, tn), lambda j: (0, j)),
        compiler_params=pltpu.CompilerParams(dimension_semantics=("arbitrary",),
                                             vmem_limit_bytes=VMEM_LIMIT),
        name="adaln",
    )(c_all, w_ada, b_ada.reshape(1, n))


def _proj_kernel(x_ref, sh_ref, sc_ref, g_ref, wq_ref, w_ref, wab_ref, wg_ref, bg_ref,
                 q_ref, k_ref, v_ref, gq_ref, gk_ref, gv_ref, r_ref, lg_ref, *, q_transposed):
    h = (_rms(x_ref[...]) * g_ref[...]) * (1.0 + _rows(sc_ref)) + _rows(sh_ref)
    hb = h.astype(bf16)
    col = lambda c: c
    ab = _dot(hb, wab_ref[...])
    r_ref[...] = _silu(_dot(hb, w_ref[:, col(_C_R):col(_C_AB)])).astype(bf16)
    gq_ref[...] = (_dot(hb, w_ref[:, col(_C_GQ):col(_C_GK)]) * (GLA_DK ** -0.5)).astype(bf16)
    z = _dot(ab.astype(bf16), wg_ref[...]) + bg_ref[...]
    log_sig = jnp.minimum(z, 0.0) - jnp.log1p(jnp.exp(-jnp.abs(z)))
    lg_ref[...] = log_sig * (LOG2_E / GLA_GATE_NORM)
    gk_ref[...] = _dot(hb, w_ref[:, col(_C_GK):col(_C_GV)]).astype(bf16)
    gv_ref[...] = _dot(hb, w_ref[:, col(_C_GV):col(_C_R)]).astype(bf16)
    kv = _dot(hb, w_ref[:, col(_C_K):col(_C_GQ)])
    k_ref[...] = kv[:, :KV_WIDTH]
    v_ref[...] = kv[:, KV_WIDTH:]
    if q_transposed:
        q_ref[...] = (_dot_nt(wq_ref[...], hb) * (LOG2_E * SWA_HEAD_DIM ** -0.5)).astype(bf16)
    else:
        q_ref[...] = _dot(hb, wq_ref[...]).astype(bf16)


def _proj(x2d, sh, sc, g_mix, w_q, w_rest, w_ab, w_gate, b_gate, *, tm, mod_specs, q_transposed):
    n = x2d.shape[0]
    row = lambda w: pl.BlockSpec((tm, w), lambda i: (i, 0))
    outs = [(KV_WIDTH, f32), (KV_WIDTH, f32), (GLA_QK_WIDTH, bf16),
            (GLA_QK_WIDTH, bf16), (GLA_V_WIDTH, bf16), (GLA_V_WIDTH, bf16), (GLA_QK_WIDTH, f32)]
    if q_transposed:
        q_shape, q_spec = (SWA_WIDTH, n), pl.BlockSpec((SWA_WIDTH, tm), lambda i: (0, i))
    else:
        q_shape, q_spec = (n, SWA_WIDTH), row(SWA_WIDTH)
    return pl.pallas_call(
        functools.partial(_proj_kernel, q_transposed=q_transposed),
        out_shape=[jax.ShapeDtypeStruct(q_shape, bf16)]
                  + [jax.ShapeDtypeStruct((n, w), dt) for w, dt in outs],
        grid=(n // tm,),
        in_specs=[row(D_MODEL), mod_specs[0], mod_specs[1],
                  _resident((1, D_MODEL)), _resident(w_q.shape), _resident(w_rest.shape),
                  _resident(w_ab.shape), _resident(w_gate.shape), _resident((1, GLA_QK_WIDTH))],
        out_specs=[q_spec] + [row(w) for w, _ in outs],
        compiler_params=pltpu.CompilerParams(dimension_semantics=("parallel",),
                                             vmem_limit_bytes=VMEM_LIMIT),
        name="proj",
    )(x2d, sh, sc, g_mix.reshape(1, D_MODEL), w_q, w_rest, w_ab, w_gate, b_gate.reshape(1, -1))


def _swa_kernel(q_ref, kp_ref, vp_ref, kc_ref, vc_ref, sink_ref, o_ref, *, tq):
    pad = jnp.zeros((WINDOW - tq, KV_WIDTH), f32)
    k_all = jnp.concatenate([kp_ref[...], kc_ref[...], pad], axis=0).astype(bf16)
    v_all = jnp.concatenate([vp_ref[...], vc_ref[...], pad], axis=0).astype(bf16)
    rows, cols = SWA_GROUP * tq, 2 * WINDOW
    valid = lax.broadcasted_iota(jnp.int32, (rows, cols), 1) < WINDOW + tq
    q = q_ref[...]
    outs = []
    for j in range(SWA_KV_HEADS):
        heads = [q[:, (j * SWA_GROUP + g) * SWA_HEAD_DIM:(j * SWA_GROUP + g + 1) * SWA_HEAD_DIM]
                 for g in range(SWA_GROUP)]
        qs = jnp.concatenate(heads, axis=0)
        kj = k_all[:, j * SWA_HEAD_DIM:(j + 1) * SWA_HEAD_DIM]
        vj = v_all[:, j * SWA_HEAD_DIM:(j + 1) * SWA_HEAD_DIM]
        s = _dot_nt(qs, kj) * (SWA_HEAD_DIM ** -0.5)
        s = jnp.where(valid, s, NEG_INF)
        sink = sink_ref[j]
        m = jnp.maximum(jnp.max(s, axis=1, keepdims=True), sink)
        p = jnp.exp(s - m)
        den = jnp.sum(p, axis=1, keepdims=True) + jnp.exp(sink - m)
        o = _dot(p.astype(bf16), vj) / den
        outs.append(jnp.concatenate([o[g * tq:(g + 1) * tq] for g in range(SWA_GROUP)], axis=1))
    o_ref[...] = jnp.concatenate(outs, axis=1).astype(bf16)


def _swa(q, k_past, v_past, k_new, v_new, sinks, *, n_seq, tq):
    sink_rows = jnp.repeat(sinks.astype(f32).reshape(SWA_KV_HEADS, SWA_GROUP), tq, axis=1)
    sink_rows = sink_rows.reshape(SWA_KV_HEADS, SWA_GROUP * tq, 1)
    new = lambda w: pl.BlockSpec((tq, w), lambda b: (b, 0))
    past = pl.BlockSpec((WINDOW, KV_WIDTH), lambda b: (b, 0))
    return pl.pallas_call(
        functools.partial(_swa_kernel, tq=tq),
        out_shape=jax.ShapeDtypeStruct(q.shape, bf16),
        grid=(n_seq,),
        in_specs=[new(SWA_WIDTH), past, past, new(KV_WIDTH), new(KV_WIDTH),
                  pl.BlockSpec(sink_rows.shape, lambda b: (0, 0, 0))],
        out_specs=new(SWA_WIDTH),
        compiler_params=pltpu.CompilerParams(dimension_semantics=("parallel",),
                                             vmem_limit_bytes=VMEM_LIMIT),
        name="swa",
    )(q, k_past, v_past, k_new, v_new, sink_rows)


def _swa_t_kernel(q_ref, kp_ref, vp_ref, kc_ref, vc_ref, sink_ref, mk_ref, mq0_ref, mq_ref, o_ref, *, n_sub):
    tq = WINDOW
    hd = SWA_HEAD_DIM
    k_all = jnp.concatenate([kp_ref[...], kc_ref[...]], axis=0).astype(bf16)
    v_all = jnp.concatenate([vp_ref[...], vc_ref[...]], axis=0).astype(bf16)
    low_lanes = lax.broadcasted_iota(jnp.int32, (2 * WINDOW, KV_WIDTH), 1) < hd
    ones = jnp.ones((2 * WINDOW, KV_WIDTH), bf16)
    for sub in range(n_sub):
        keys = slice(sub * tq, sub * tq + 2 * WINDOW)
        toks = slice(sub * tq, (sub + 1) * tq)
        mq = (mq0_ref if sub == 0 else mq_ref)[0]
        for j in range(SWA_KV_HEADS):
            head = lambda g: slice((j * SWA_GROUP + g) * hd, (j * SWA_GROUP + g + 1) * hd)
            qs = jnp.concatenate([q_ref[head(g), toks] for g in range(SWA_GROUP)], axis=1)
            own = low_lanes if j == 0 else jnp.logical_not(low_lanes)
            k_aug = jnp.where(own, k_all[keys], mk_ref[j])
            v_aug = jnp.where(own, v_all[keys], ones)
            q_aug = jnp.concatenate([qs, mq] if j == 0 else [mq, qs], axis=0)
            s = _dot(k_aug, q_aug)
            sink = sink_ref[j]
            m = jnp.maximum(jnp.max(s, axis=0, keepdims=True), sink)
            p = jnp.exp2(s - m).astype(bf16)
            o_aug = _dot_tn(v_aug, p)
            pv, p_sum = (o_aug[:hd], o_aug[hd:hd + 1]) if j == 0 else (o_aug[hd:], o_aug[0:1])
            o = pv / (p_sum + jnp.exp2(sink - m))
            for g in range(SWA_GROUP):
                o_ref[head(g), toks] = o[:, g * tq:(g + 1) * tq].astype(bf16)


def _swa_t(q_t, k, v, sinks, *, n_seq, tiles, n_sub):
    tq = WINDOW
    steps = tiles // n_sub
    hd = SWA_HEAD_DIM
    sink_cols = jnp.repeat(sinks.astype(f32).reshape(SWA_KV_HEADS, SWA_GROUP) * LOG2_E, tq, axis=1)
    sink_cols = sink_cols.reshape(SWA_KV_HEADS, 1, SWA_GROUP * tq)
    n_kc = 2 * WINDOW // CHUNK
    key_chunk = jnp.arange(2 * WINDOW) // CHUNK
    lane = jnp.arange(KV_WIDTH)
    mk = jnp.stack([lane[None, :] == hd + key_chunk[:, None], lane[None, :] == key_chunk[:, None]]).astype(bf16)
    q_chunk = (jnp.arange(SWA_GROUP * tq) % tq) // CHUNK
    kc = jnp.arange(hd)[:, None]
    band = (kc >= q_chunk[None, :]) & (kc <= q_chunk[None, :] + WINDOW // CHUNK)
    visible = jnp.stack([band & (kc >= WINDOW // CHUNK), band]) | (kc >= n_kc)
    mq = jnp.where(visible, 0.0, NEG_INF).astype(bf16)
    mq0_spec = pl.BlockSpec((1, hd, SWA_GROUP * tq), lambda b, u: (jnp.minimum(u, 1), 0, 0))
    mq_spec = pl.BlockSpec((1, hd, SWA_GROUP * tq), lambda b, u: (1, 0, 0))
    qspec = pl.BlockSpec((SWA_WIDTH, n_sub * tq), lambda b, u: (0, b * steps + u))
    cur = pl.BlockSpec((n_sub * tq, KV_WIDTH), lambda b, u: (b * steps + u, 0))
    prev = pl.BlockSpec((WINDOW, KV_WIDTH), lambda b, u: (b * tiles + jnp.maximum(n_sub * u - 1, 0), 0))
    return pl.pallas_call(
        functools.partial(_swa_t_kernel, n_sub=n_sub),
        out_shape=jax.ShapeDtypeStruct(q_t.shape, bf16),
        grid=(n_seq, steps),
        in_specs=[qspec, prev, prev, cur, cur, pl.BlockSpec(sink_cols.shape, lambda b, u: (0, 0, 0)),
                  pl.BlockSpec(mk.shape, lambda b, u: (0, 0, 0)), mq0_spec, mq_spec],
        out_specs=qspec,
        compiler_params=pltpu.CompilerParams(dimension_semantics=("parallel", "arbitrary"),
                                             vmem_limit_bytes=VMEM_LIMIT),
        name="swa_t",
    )(q_t, k, v, k, v, sink_cols, mk, mq, mq)


def _gla_kernel(q_ref, k_ref, v_ref, lg_ref, r_ref, gh_ref, s0_ref, o_ref, s_ref, *, c, n_sub):
    @pl.when(pl.program_id(1) == 0)
    def _():
        s_ref[...] = s0_ref[...]

    row = lax.broadcasted_iota(jnp.int32, (c, GLA_DK), 0)
    causal = (lax.broadcasted_iota(jnp.int32, (c, c), 0) >= lax.broadcasted_iota(jnp.int32, (c, c), 1))
    for sub in range(n_sub):
        ts = slice(sub * c, (sub + 1) * c)
        for h in range(GLA_HEADS):
            ks = slice(h * GLA_DK, (h + 1) * GLA_DK)
            vs = slice(h * GLA_DV, (h + 1) * GLA_DV)
            b = lg_ref[ts, ks]
            step = 1
            while step < c:
                b = b + jnp.where(row >= step, pltpu.roll(b, step, 0), 0.0)
                step *= 2
            b_last = b[c - 1:c, :]
            q = q_ref[ts, ks].astype(f32)
            k = k_ref[ts, ks].astype(f32)
            v = v_ref[ts, vs]
            qd = (q * jnp.exp2(b)).astype(bf16)
            kd = (k * jnp.exp2(-b)).astype(bf16)
            kl = (k * jnp.exp2(b_last - b)).astype(bf16)
            a = jnp.where(causal, _dot_nt(qd, kd), 0.0)
            s = s_ref[0, h]
            o = _dot(qd, s.astype(bf16)) + _dot(a.astype(bf16), v)
            decay = jnp.broadcast_to(jnp.exp2(b_last), (GLA_DK, GLA_DK)).T
            s_ref[0, h] = s * jnp.concatenate([decay, decay], axis=1) + _dot_tn(kl, v)
            on = _rms(o) * gh_ref[...]
            o_ref[ts, vs] = (on * r_ref[ts, vs].astype(f32)).astype(bf16)


def _gla(gq, gk, gv, lg, r, g_head, s0, *, n_seq, c, n_sub, steps):
    rows = c * n_sub
    blk = lambda w: pl.BlockSpec((rows, w), lambda b, t: (b * steps + t, 0))
    state = pl.BlockSpec((1, GLA_HEADS, GLA_DK, GLA_DV), lambda b, t: (b, 0, 0, 0))
    return pl.pallas_call(
        functools.partial(_gla_kernel, c=c, n_sub=n_sub),
        out_shape=[jax.ShapeDtypeStruct(gv.shape, bf16),
                   jax.ShapeDtypeStruct((n_seq, GLA_HEADS, GLA_DK, GLA_DV), f32)],
        grid=(n_seq, steps),
        in_specs=[blk(GLA_QK_WIDTH), blk(GLA_QK_WIDTH), blk(GLA_V_WIDTH), blk(GLA_QK_WIDTH),
                  blk(GLA_V_WIDTH), pl.BlockSpec((1, GLA_DV), lambda b, t: (0, 0)), state],
        out_specs=[blk(GLA_V_WIDTH), state],
        compiler_params=pltpu.CompilerParams(dimension_semantics=("parallel", "arbitrary"),
                                             vmem_limit_bytes=VMEM_LIMIT),
        name="gla",
    )(gq, gk, gv, lg, r, g_head.reshape(1, GLA_DV), s0)


def _pack_pair(hb):
    w = hb.shape[1] // 2
    a = lax.bitcast_convert_type(hb[:, :w].astype(f32), u32)
    b = lax.bitcast_convert_type(hb[:, w:].astype(f32), u32)
    return a | (b >> 16)


def _unpack_pair(p):
    a = lax.bitcast_convert_type(p & jnp.uint32(0xFFFF0000), f32)
    b = lax.bitcast_convert_type(p << 16, f32)
    return a, b


def _load_row_tiles(ref, n):
    return jnp.concatenate([ref[pl.ds(s, n, stride=SUBLANES), :] for s in range(SUBLANES)], axis=1)


def _store_row_tiles(ref, val):
    n = val.shape[0]
    for s in range(SUBLANES):
        ref[pl.ds(s, n, stride=SUBLANES), :] = val[:, s * LANES:(s + 1) * LANES]


def _row_tile(ref, r):
    return ref.at[pl.ds(pl.multiple_of(r * SUBLANES, SUBLANES), SUBLANES), :]


def _outproj_kernel(*refs, n_alias, oa_transposed):
    (oa_ref, ob_ref, x_ref, gt_ref, sh_ref, sc_ref, g_ref, w_ref, wr_ref, br_ref) = refs[:10]
    x1_ref, h2_ref, route_ref, x1_s = refs[10 + n_alias:]

    @pl.when(pl.program_id(0) == 0)
    def _():
        x1_s[...] = jnp.zeros(x1_s.shape, x1_s.dtype)

    h2 = (_rms(x1_s[...]) * g_ref[...]) * (1.0 + _rows(sc_ref)) + _rows(sh_ref)
    hi = h2.astype(bf16)
    _store_row_tiles(h2_ref, _pack_pair(hi))
    logits = _dot(hi, wr_ref[...]) + br_ref[...]
    lane = lax.broadcasted_iota(jnp.int32, logits.shape, 1)
    lane_f = lane.astype(f32)
    neg = float("-inf")
    first = lambda hit: jnp.min(jnp.where(hit, lane_f, float(LANES)), axis=1, keepdims=True)
    lg_g = jnp.where(lane < N_GROUPS, logits, neg)
    g_max = jnp.max(lg_g, axis=1, keepdims=True)
    g_sel = first(lg_g == g_max)
    p_sel = 1.0 / jnp.sum(jnp.exp(lg_g - g_max), axis=1, keepdims=True)
    e_lane = lane - N_GROUPS
    in_group = (e_lane >= 0) & (e_lane < N_EXPERTS) & ((e_lane // EXPERTS_PER_GROUP).astype(f32) == g_sel)
    lg_e = jnp.where(in_group, logits, neg)
    v1 = jnp.max(lg_e, axis=1, keepdims=True)
    i1 = first(lg_e == v1)
    lg_e2 = jnp.where(lane_f == i1, neg, lg_e)
    v2 = jnp.max(lg_e2, axis=1, keepdims=True)
    i2 = first(lg_e2 == v2)
    e = jnp.exp(v2 - v1)
    g1 = p_sel / (1.0 + e)
    g2 = p_sel * e / (1.0 + e)
    rec = jnp.where(lane == ROUTE_E0, i1 - N_GROUPS, 0.0)
    rec = jnp.where(lane == ROUTE_E1, i2 - N_GROUPS, rec)
    rec = jnp.where(lane == ROUTE_G0, g1, rec)
    rec = jnp.where(lane == ROUTE_G1, g2, rec)
    route_ref[...] = rec

    oa_dot = _dot_tn if oa_transposed else _dot
    mix = oa_dot(oa_ref[...], w_ref[:SWA_WIDTH, :]) + _dot(ob_ref[...], w_ref[SWA_WIDTH:, :])
    x1 = x_ref[...] + _rows(gt_ref) * mix
    x1_ref[...] = x1
    x1_s[...] = x1


def _outproj(oa, ob, x2d, gt, sh, sc, g_ffn, w_out, w_route, b_route, *, tm, mod_specs,
             n_total, row0, bufs=None, fill_steps=0, oa_transposed=False):
    n = x2d.shape[0]
    tiles = n // tm
    blocks = tiles + fill_steps
    row = lambda w: pl.BlockSpec((tm, w), lambda i: (jnp.minimum(i, tiles - 1), 0))
    oa_spec = row(SWA_WIDTH)
    if oa_transposed:
        oa_spec = pl.BlockSpec((SWA_WIDTH, tm), lambda i: (0, jnp.minimum(i, tiles - 1)))
    head_blk = lambda i: row0 // tm + jnp.minimum(i, blocks - 1)
    tail_blk = lambda i: row0 // tm + jnp.maximum(i - 1, 0)
    out_x1 = pl.BlockSpec((tm, D_MODEL), lambda i: (head_blk(i), 0))
    out_tiles = pl.BlockSpec((tm * SUBLANES, LANES), lambda i: (tail_blk(i), 0))
    out_route = pl.BlockSpec((tm, LANES), lambda i: (tail_blk(i), 0))
    alias_in = list(bufs) if bufs is not None else []
    n_in = 10
    return pl.pallas_call(
        functools.partial(_outproj_kernel, n_alias=len(alias_in), oa_transposed=oa_transposed),
        out_shape=[jax.ShapeDtypeStruct((n_total, D_MODEL), f32),
                   jax.ShapeDtypeStruct((n_total * SUBLANES, LANES), u32),
                   jax.ShapeDtypeStruct((n_total, LANES), f32)],
        grid=(blocks + 1,),
        in_specs=[oa_spec, row(GLA_V_WIDTH), row(D_MODEL), mod_specs[0], mod_specs[1],
                  mod_specs[2], _resident((1, D_MODEL)), _resident(w_out.shape),
                  _resident(w_route.shape), _resident((1, LANES))]
                 + [pl.BlockSpec(memory_space=pl.ANY)] * len(alias_in),
        out_specs=[out_x1, out_tiles, out_route],
        scratch_shapes=[pltpu.VMEM((tm, D_MODEL), f32)],
        input_output_aliases={n_in + a: a for a in range(len(alias_in))},
        compiler_params=pltpu.CompilerParams(dimension_semantics=("arbitrary",),
                                             vmem_limit_bytes=VMEM_LIMIT),
        name="outproj",
    )(oa, ob, x2d, gt, sh, sc, g_ffn.reshape(1, D_MODEL), w_out, w_route, b_route, *alias_in)


def _rank_kernel(route_ref, pos_ref, pend_ref, rec_s, *, n_chunks, tb):
    n_e = N_EXPERTS
    expert = lax.broadcasted_iota(jnp.int32, (n_e, LANES), 0)
    expert_f = expert.astype(f32)
    earlier = (lax.broadcasted_iota(jnp.int32, (LANES, LANES), 0)
               < lax.broadcasted_iota(jnp.int32, (LANES, LANES), 1)).astype(bf16)

    def onehots(c):
        rec = rec_s[c]
        return expert_f == rec[ROUTE_E0:ROUTE_E0 + 1, :], expert_f == rec[ROUTE_E1:ROUTE_E1 + 1, :]

    def count(c, cnt):
        rec_s[c] = route_ref[pl.ds(pl.multiple_of(c * LANES, LANES), LANES), :].T[:SUBLANES, :]
        h0, h1 = onehots(c)
        return cnt + jnp.sum((h0 | h1).astype(f32), axis=1, keepdims=True)

    cnt = lax.fori_loop(0, n_chunks, count, jnp.zeros((n_e, 1), f32), unroll=4)
    padded = jnp.floor((cnt + (tb - 1.0)) / tb) * tb
    end = jnp.broadcast_to(padded, (n_e, LANES))
    step = 1
    while step < n_e:
        end = end + jnp.where(expert >= step, pltpu.roll(end, step, 0), 0.0)
        step *= 2
    pend_ref[0] = end.astype(jnp.int32)
    pend_ref[1] = jnp.broadcast_to(jnp.where(cnt > 0, cnt - (padded - tb), 0.0), (n_e, LANES)).astype(jnp.int32)

    def rank(c, base):
        h0, h1 = onehots(c)
        both = h0 | h1
        off = _dot(both.astype(bf16), earlier) + base
        pos_ref[0, pl.ds(c, 1), :] = jnp.sum(jnp.where(h0, off, 0.0), axis=0, keepdims=True).astype(jnp.int32)
        pos_ref[1, pl.ds(c, 1), :] = jnp.sum(jnp.where(h1, off, 0.0), axis=0, keepdims=True).astype(jnp.int32)
        return base + jnp.sum(both.astype(f32), axis=1, keepdims=True)

    lax.fori_loop(0, n_chunks, rank, (end - padded)[:, 0:1], unroll=4)


def _rank(route, *, tb):
    n = route.shape[0]
    n_chunks = n // LANES
    return pl.pallas_call(
        functools.partial(_rank_kernel, n_chunks=n_chunks, tb=tb),
        out_shape=[jax.ShapeDtypeStruct((TOP_K, n_chunks, LANES), jnp.int32),
                   jax.ShapeDtypeStruct((2, N_EXPERTS, LANES), jnp.int32)],
        in_specs=[_resident(route.shape)],
        scratch_shapes=[pltpu.VMEM((n_chunks, SUBLANES, LANES), f32)],
        compiler_params=pltpu.CompilerParams(vmem_limit_bytes=VMEM_LIMIT),
        name="rank",
    )(route)


def _dispatch_kernel(pos_ref, pend_ref, h2_ref, xs_hbm, stage, zbuf, sem, zsem, *, tm, tb, n_tot, steps):
    i = pl.program_id(0)
    slot = i % 2

    blk = tb * SUBLANES

    def wait_slot(s):
        for _ in range(TOP_K):
            pltpu.make_async_copy(stage.at[s], xs_hbm.at[pl.ds(0, tm * SUBLANES), :], sem.at[s]).wait()

    @pl.when(i == 0)
    def _():
        zbuf[...] = jnp.zeros(zbuf.shape, zbuf.dtype)

        def fill_block(b, start):
            cp = pltpu.make_async_copy(zbuf, xs_hbm.at[pl.ds(pl.multiple_of(b * blk, blk), blk), :], zsem)
            cp.start() if start else cp.wait()

        def fill(e, start):
            end = pend_ref[e]
            prev = jnp.where(e > 0, pend_ref[jnp.maximum(e - 1, 0)], 0)

            @pl.when(end > prev)
            def _():
                fill_block(end // tb - 1, start)

        fill_unused = fill_block

        first_unused = pend_ref[N_EXPERTS - 1] // tb
        n_blocks = xs_hbm.shape[0] // blk
        lax.fori_loop(0, N_EXPERTS, lambda e, c: (fill(e, True), c)[1], 0)
        lax.fori_loop(first_unused, n_blocks, lambda b, c: (fill_unused(b, True), c)[1], 0)
        lax.fori_loop(0, N_EXPERTS, lambda e, c: (fill(e, False), c)[1], 0)
        lax.fori_loop(first_unused, n_blocks, lambda b, c: (fill_unused(b, False), c)[1], 0)

    @pl.when(i >= 2)
    def _():
        wait_slot(slot)

    stage[slot] = h2_ref[...]

    def scatter(r, carry):
        for k in range(TOP_K):
            d = pos_ref[k * n_tot + i * tm + r]
            pltpu.make_async_copy(_row_tile(stage.at[slot], r), _row_tile(xs_hbm, d),
                                  sem.at[slot]).start(priority=k)
        return carry

    lax.fori_loop(0, tm, scatter, 0, unroll=8)

    @pl.when(i == steps - 1)
    def _():
        wait_slot(slot)
        if steps > 1:
            wait_slot(1 - slot)


def _dispatch(pos_flat, pend, h2p, *, tm, tb, n_blocks):
    n_tot = h2p.shape[0] // SUBLANES
    steps = n_tot // tm
    return pl.pallas_call(
        functools.partial(_dispatch_kernel, tm=tm, tb=tb, n_tot=n_tot, steps=steps),
        out_shape=jax.ShapeDtypeStruct((n_blocks * tb * SUBLANES, LANES), u32),
        grid_spec=pltpu.PrefetchScalarGridSpec(
            num_scalar_prefetch=2,
            grid=(steps,),
            in_specs=[pl.BlockSpec((tm * SUBLANES, LANES), lambda i, pos, pend: (i, 0))],
            out_specs=pl.BlockSpec(memory_space=pl.ANY),
            scratch_shapes=[pltpu.VMEM((2, tm * SUBLANES, LANES), u32),
                            pltpu.VMEM((tb * SUBLANES, LANES), u32),
                            pltpu.SemaphoreType.DMA((2,)), pltpu.SemaphoreType.DMA(())]),
        compiler_params=pltpu.CompilerParams(dimension_semantics=("arbitrary",),
                                             vmem_limit_bytes=VMEM_LIMIT),
        name="dispatch",
    )(pos_flat, pend, h2p)


def _expert_kernel(be_ref, ne_ref, hf_ref, nu_ref, xs_ref, wg_hbm, wu_hbm, wd_hbm, y_ref,
                   wg_s, wu_s, wd_s, wg_b, wu_b, wd_b, sem, *, tb):
    i = pl.program_id(0)
    used = i < nu_ref[0]
    e = be_ref[i]

    def fetch(expert):
        pairs = ((wg_hbm, wg_s), (wu_hbm, wu_s), (wd_hbm, wd_s))
        return [pltpu.make_async_copy(src.at[expert], dst, sem.at[n]) for n, (src, dst) in enumerate(pairs)]

    @pl.when(i == 0)
    def _():
        for cp in fetch(e):
            cp.start()

    @pl.when(used & ((i == 0) | (e != be_ref[jnp.maximum(i - 1, 0)])))
    def _():
        for cp in fetch(e):
            cp.wait()
        wg_b[...] = wg_s[...].astype(bf16)
        wu_b[...] = wu_s[...].astype(bf16)
        wd_b[...] = wd_s[...].astype(bf16)
        nxt = ne_ref[i]

        @pl.when(nxt != e)
        def _():
            for cp in fetch(nxt):
                cp.start()

    def ffn(rows):
        tiles = pl.ds(0, rows * SUBLANES)
        a, b = _unpack_pair(_load_row_tiles(xs_ref.at[tiles], rows))
        x = jnp.concatenate([a.astype(bf16), b.astype(bf16)], axis=1)
        g = _dot(x, wg_b[...])
        u = _dot(x, wu_b[...])
        y = _dot((_silu(g) * u).astype(bf16), wd_b[...])
        _store_row_tiles(y_ref.at[tiles], _pack_pair(y.astype(bf16)))

    half_full = hf_ref[i] == 1

    @pl.when(used & jnp.logical_not(half_full))
    def _():
        ffn(tb)

    @pl.when(used & half_full)
    def _():
        ffn(tb // 2)
        rest = pl.ds(tb // 2 * SUBLANES, tb // 2 * SUBLANES)
        y_ref[rest, :] = jnp.zeros((tb // 2 * SUBLANES, LANES), y_ref.dtype)

    @pl.when(jnp.logical_not(used))
    def _():
        y_ref[...] = jnp.zeros(y_ref.shape, y_ref.dtype)


def _experts(block_e, next_e, half_full, n_used, xs, w_eg, w_eu, w_ed, *, tb):
    n_blocks = block_e.shape[0]
    blk = (tb * SUBLANES, LANES)
    up, down = (D_MODEL, EXPERT_HIDDEN), (EXPERT_HIDDEN, D_MODEL)
    hbm = pl.BlockSpec(memory_space=pl.ANY)
    return pl.pallas_call(
        functools.partial(_expert_kernel, tb=tb),
        out_shape=jax.ShapeDtypeStruct(xs.shape, u32),
        grid_spec=pltpu.PrefetchScalarGridSpec(
            num_scalar_prefetch=4,
            grid=(n_blocks,),
            in_specs=[pl.BlockSpec(blk, lambda i, be, ne, hf, nu: (jnp.minimum(i, nu[0] - 1), 0)),
                      hbm, hbm, hbm],
            out_specs=pl.BlockSpec(blk, lambda i, be, ne, hf, nu: (i, 0)),
            scratch_shapes=[pltpu.VMEM(up, f32), pltpu.VMEM(up, f32), pltpu.VMEM(down, f32),
                            pltpu.VMEM(up, bf16), pltpu.VMEM(up, bf16), pltpu.VMEM(down, bf16),
                            pltpu.SemaphoreType.DMA((3,))]),
        compiler_params=pltpu.CompilerParams(dimension_semantics=("arbitrary",),
                                             vmem_limit_bytes=VMEM_LIMIT),
        name="experts",
    )(block_e, next_e, half_full, n_used, xs, w_eg, w_eu, w_ed)


def _combine_kernel(pos_ref, yb_hbm, x1_ref, route_ref, gt_ref, gf_ref, y_ref, ybuf, sem, *, tm):
    i = pl.program_id(0)
    slot = i % 2

    def start(blk, s):
        def body(r, carry):
            for k in range(TOP_K):
                t = pos_ref[(k * pl.num_programs(0) + blk) * tm + r]
                pltpu.make_async_copy(_row_tile(yb_hbm, t), _row_tile(ybuf.at[s, k], r),
                                      sem.at[s]).start(priority=k)
            return carry
        lax.fori_loop(0, tm, body, 0, unroll=8)

    @pl.when(i == 0)
    def _():
        start(0, 0)

    @pl.when(i + 1 < pl.num_programs(0))
    def _():
        start(i + 1, 1 - slot)

    for k in range(TOP_K):
        pltpu.make_async_copy(yb_hbm.at[pl.ds(0, tm * SUBLANES), :], ybuf.at[slot, k], sem.at[slot]).wait()
    route = route_ref[...]
    a0, b0 = _unpack_pair(_load_row_tiles(ybuf.at[slot, 0], tm))
    a1, b1 = _unpack_pair(_load_row_tiles(ybuf.at[slot, 1], tm))
    g0 = route[:, ROUTE_G0:ROUTE_G0 + 1]
    g1 = route[:, ROUTE_G1:ROUTE_G1 + 1]
    moe = jnp.concatenate([a0 * g0 + a1 * g1, b0 * g0 + b1 * g1], axis=1)
    x2 = x1_ref[...] + _rows(gt_ref) * moe
    y_ref[...] = _rms(x2) * gf_ref[...]


def _combine(pos_km, yb, x1, route, gt, g_final, *, tm, n, row0, gt_spec):
    blk0 = row0 // tm
    return pl.pallas_call(
        functools.partial(_combine_kernel, tm=tm),
        out_shape=jax.ShapeDtypeStruct((n, D_MODEL), f32),
        grid_spec=pltpu.PrefetchScalarGridSpec(
            num_scalar_prefetch=1,
            grid=(n // tm,),
            in_specs=[pl.BlockSpec(memory_space=pl.ANY),
                      pl.BlockSpec((tm, D_MODEL), lambda i, pos: (blk0 + i, 0)),
                      pl.BlockSpec((tm, LANES), lambda i, pos: (blk0 + i, 0)),
                      gt_spec,
                      pl.BlockSpec((1, D_MODEL), lambda i, pos: (0, 0))],
            out_specs=pl.BlockSpec((tm, D_MODEL), lambda i, pos: (i, 0)),
            scratch_shapes=[pltpu.VMEM((2, TOP_K, tm * SUBLANES, LANES), u32),
                            pltpu.SemaphoreType.DMA((2,))]),
        compiler_params=pltpu.CompilerParams(dimension_semantics=("arbitrary",),
                                             vmem_limit_bytes=VMEM_LIMIT),
        name="combine",
    )(pos_km, yb, x1, route, gt, g_final.reshape(1, D_MODEL))


def kernel(x_prompt, x_sample, cache_swa_k, cache_swa_v, state_gla, c_prompt, c_sample, g_mix_norm, g_ffn_norm, w_ada, b_ada, w_in, attn_sinks, w_gla_gate, b_gla_gate, g_gla_norm, w_out, w_router_group, b_router_group, w_router_expert, b_router_expert, w_expert_gate, w_expert_up, w_expert_down, g_final):
    depth = w_in.shape[0]
    assert depth == 1
    bp, t, d = x_prompt.shape
    bs, ts, _ = x_sample.shape
    n_p, n_s = bp * t, bs * ts
    n_tot = n_p + n_s
    assert n_tot % LANES == 0
    tm = 512
    to = 512
    tb = 512
    tc = 512
    td = LANES
    gla_c = 128
    gla_sub = 16
    swa_sub = 8

    w_in0 = w_in[0]
    w_rest = w_in0.astype(bf16)
    w_q = w_rest[:, :_C_K]
    w_q_t = w_q.T
    w_ab = jnp.pad(w_in0[:, _C_AB:], ((0, 0), (0, LANES - GLA_GATE_RANK))).astype(bf16)
    w_gate = jnp.pad(w_gla_gate[0], ((0, LANES - GLA_GATE_RANK), (0, 0))).astype(bf16)
    w_out_b = w_out[0].astype(bf16)
    n_r = N_GROUPS + N_EXPERTS
    w_r = jnp.pad(jnp.concatenate([w_router_group[0], w_router_expert[0]], axis=1),
                  ((0, 0), (0, LANES - n_r)))
    w_route = w_r.astype(bf16)
    b_route = jnp.pad(jnp.concatenate([b_router_group[0], b_router_expert[0]]),
                      (0, LANES - n_r)).reshape(1, LANES)

    c_all = jnp.concatenate([c_prompt, c_sample], axis=0)
    mod = _adaln(c_all, w_ada[0], b_ada[0])
    mod_tab = mod.reshape((bp + bs) * N_MOD, 1, d)
    mod_s = jnp.repeat(mod[bp:].reshape(bs, N_MOD, d), ts, axis=0)
    mod_s = [mod_s[:, m] for m in range(N_MOD)]
    pmod = lambda comp, tile, lag=0: _mod_spec(comp, t // tile, n_p // tile, lag)
    smod = pl.BlockSpec((n_s, d), lambda i: (0, 0))

    xp = x_prompt.reshape(n_p, d)
    xs = x_sample.reshape(n_s, d)
    proj_w = (w_rest, w_ab, w_gate, b_gla_gate[0])
    qp, kp, vp, gqp, gkp, gvp, rp, lgp = _proj(xp, mod_tab, mod_tab, g_mix_norm[0], w_q_t, *proj_w, tm=tm,
                                               mod_specs=(pmod(0, tm), pmod(1, tm)), q_transposed=True)
    qs, ks, vs, gqs, gks, gvs, rs, lgs = _proj(xs, mod_s[0], mod_s[1], g_mix_norm[0], w_q, *proj_w, tm=n_s,
                                               mod_specs=(smod, smod), q_transposed=False)

    sinks = attn_sinks[0]
    oap = _swa_t(qp, kp, vp, sinks, n_seq=bp, tiles=t // WINDOW, n_sub=swa_sub)
    ck = cache_swa_k[0].reshape(bs * WINDOW, KV_WIDTH)
    cv = cache_swa_v[0].reshape(bs * WINDOW, KV_WIDTH)
    oas = _swa(qs, ck, cv, ks, vs, sinks, n_seq=bs, tq=ts)
    s_zero = jnp.zeros((bp, GLA_HEADS, GLA_DK, GLA_DV), f32)
    obp, sp = _gla(gqp, gkp, gvp, lgp, rp, g_gla_norm[0], s_zero, n_seq=bp, c=gla_c, n_sub=gla_sub,
                   steps=t // (gla_c * gla_sub))
    obs, ss = _gla(gqs, gks, gvs, lgs, rs, g_gla_norm[0], state_gla[0], n_seq=bs, c=ts, n_sub=1, steps=1)

    out_w = (g_ffn_norm[0], w_out_b, w_route, b_route)
    bufs = _outproj(oap, obp, xp, mod_tab, mod_tab, mod_tab, *out_w, tm=to,
                    mod_specs=(pmod(2, to), pmod(3, to, 1), pmod(4, to, 1)), n_total=n_tot, row0=0,
                    fill_steps=-(-n_s // to), oa_transposed=True)
    x1, h2p, route = _outproj(oas, obs, xs, mod_s[2], mod_s[3], mod_s[4], *out_w, tm=n_s,
                              mod_specs=(smod, smod, smod), n_total=n_tot, row0=n_p, bufs=bufs)

    n_blocks = -(-(n_tot * TOP_K + N_EXPERTS * (tb - 1)) // tb)
    pos, pend_tab = _rank(route, tb=tb)
    pend, last_rows = pend_tab[0, :, 0], pend_tab[1, :, 0]
    block_row0 = jnp.arange(n_blocks, dtype=jnp.int32) * tb
    block_e = jnp.minimum(jnp.sum(pend[None, :] <= block_row0[:, None], axis=1), N_EXPERTS - 1).astype(jnp.int32)
    n_used = pend[N_EXPERTS - 1:] // tb
    experts = jnp.arange(N_EXPERTS, dtype=jnp.int32)
    has_rows = jnp.diff(pend, prepend=0) > 0
    later = jnp.where((experts[None, :] > experts[:, None]) & has_rows[None, :], experts[None, :], N_EXPERTS)
    next_with_rows = jnp.min(later, axis=1)
    next_with_rows = jnp.where(next_with_rows == N_EXPERTS, experts, next_with_rows)
    of_block = lambda per_expert: jnp.sum(jnp.where(block_e[:, None] == experts[None, :], per_expert[None, :], 0),
                                          axis=1).astype(jnp.int32)
    next_e = of_block(next_with_rows)
    half_full = ((block_row0 + tb == of_block(pend)) & (of_block(last_rows) <= tb // 2)).astype(jnp.int32)
    xsort = _dispatch(pos.reshape(-1), pend, h2p, tm=td, tb=tb, n_blocks=n_blocks)
    yb = _experts(block_e, next_e, half_full, n_used, xsort, w_expert_gate[0], w_expert_up[0],
                  w_expert_down[0], tb=tb)

    pos = pos.reshape(TOP_K, n_tot)
    pos_p = pos[:, :n_p].reshape(-1)
    pos_s = pos[:, n_p:].reshape(-1)
    gt_p = pl.BlockSpec((1, 1, d), lambda i, p: ((i // (t // tc)) * N_MOD + 5, 0, 0))
    gt_s = pl.BlockSpec((n_s, d), lambda i, p: (0, 0))
    y_p = _combine(pos_p, yb, x1, route, mod_tab, g_final, tm=tc, n=n_p, row0=0, gt_spec=gt_p)
    y_s = _combine(pos_s, yb, x1, route, mod_s[5], g_final, tm=n_s, n=n_s, row0=n_p, gt_spec=gt_s)

    kv_shape = (SWA_KV_HEADS, SWA_HEAD_DIM)
    last = lambda a: a.reshape(bp, t, KV_WIDTH)[:, t - WINDOW:, :].reshape(1, bp, WINDOW, *kv_shape)
    k_state_p, v_state_p = last(kp), last(vp)
    return (y_p.reshape(bp, t, d), y_s.reshape(bs, ts, d), k_state_p, v_state_p, sp[None],
            ks.reshape(bs, ts, *kv_shape)[None], vs.reshape(bs, ts, *kv_shape)[None], ss[None])
```

```python
import functools
import math

import jax
import jax.numpy as jnp
from jax import lax
from jax.experimental import pallas as pl
from jax.experimental.pallas import tpu as pltpu

f32 = jnp.float32
bf16 = jnp.bfloat16
u32 = jnp.uint32

D_MODEL = 2048
N_MOD = 6
EPS = 1e-6
NEG_INF = -1e30
LOG2_E = math.log2(math.e)

SWA_HEAD_DIM = 64
SWA_KV_HEADS = 2
SWA_GROUP = 8
SWA_WIDTH = SWA_KV_HEADS * SWA_GROUP * SWA_HEAD_DIM
KV_WIDTH = SWA_KV_HEADS * SWA_HEAD_DIM
WINDOW = 128
CHUNK = 64

GLA_HEADS = 4
GLA_DK = 128
GLA_DV = 256
GLA_QK_WIDTH = GLA_HEADS * GLA_DK
GLA_V_WIDTH = GLA_HEADS * GLA_DV
GLA_GATE_RANK = 16
GLA_GATE_NORM = 16.0

N_GROUPS = 4
EXPERTS_PER_GROUP = 8
N_EXPERTS = N_GROUPS * EXPERTS_PER_GROUP
TOP_K = 2
EXPERT_HIDDEN = D_MODEL // 4

_C_Q = 0
_C_K = _C_Q + SWA_WIDTH
_C_V = _C_K + KV_WIDTH
_C_GQ = _C_V + KV_WIDTH
_C_GK = _C_GQ + GLA_QK_WIDTH
_C_GV = _C_GK + GLA_QK_WIDTH
_C_R = _C_GV + GLA_V_WIDTH
_C_AB = _C_R + GLA_V_WIDTH

LANES = 128
SUBLANES = 8
PACKED_WIDTH = D_MODEL // 2
assert PACKED_WIDTH == SUBLANES * LANES
VMEM_LIMIT = 56 * 1024 * 1024

ROUTE_E0, ROUTE_E1, ROUTE_G0, ROUTE_G1 = 0, 1, 2, 3


def _dot(a, b):
    return jnp.dot(a, b, preferred_element_type=f32)


def _dot_nt(a, b):
    return lax.dot_general(a, b, (((1,), (1,)), ((), ())), preferred_element_type=f32)


def _dot_tn(a, b):
    return lax.dot_general(a, b, (((0,), (0,)), ((), ())), preferred_element_type=f32)


def _silu(x):
    return x / (1.0 + jnp.exp(-x))


def _rows(ref):
    v = ref[...]
    return v.reshape(v.shape[-2:])


def _rms(x):
    return x * lax.rsqrt(jnp.mean(x * x, axis=-1, keepdims=True) + EPS)


def _resident(shape):
    return pl.BlockSpec(shape, lambda *_: (0,) * len(shape), pipeline_mode=pl.Buffered(1))


def _mod_spec(comp, tiles_per_seq, n_tiles, lag=0):
    def index(i):
        tile = jnp.minimum(jnp.maximum(i - lag, 0), n_tiles - 1)
        return ((tile // tiles_per_seq) * N_MOD + comp, 0, 0)
    return pl.BlockSpec((1, 1, D_MODEL), index)


def _adaln_kernel(c_ref, w_ref, b_ref, o_ref):
    a = _silu(c_ref[...]).astype(bf16)
    o_ref[...] = _dot(a, w_ref[...].astype(bf16)) + b_ref[...]


def _adaln(c_all, w_ada, b_ada, tn=1024):
    r = c_all.shape[0]
    n = w_ada.shape[1]
    return pl.pallas_call(
        _adaln_kernel,
        out_shape=jax.ShapeDtypeStruct((r, n), f32),
        grid=(n // tn,),
        in_specs=[pl.BlockSpec((r, D_MODEL), lambda j: (0, 0)),
                  pl.BlockSpec((D_MODEL, tn), lambda j: (0, j)),
                  pl.BlockSpec((1, tn), lambda j: (0, j))],
        out_specs=pl.BlockSpec((r, tn), lambda j: (0, j)),
        compiler_params=pltpu.CompilerParams(dimension_semantics=("arbitrary",),
                                             vmem_limit_bytes=VMEM_LIMIT),
        name="adaln",
    )(c_all, w_ada, b_ada.reshape(1, n))


def _proj_kernel(x_ref, sh_ref, sc_ref, g_ref, wq_ref, w_ref, wab_ref, wg_ref, bg_ref,
                 q_ref, k_ref, v_ref, gq_ref, gk_ref, gv_ref, r_ref, lg_ref, *, q_transposed):
    h = (_rms(x_ref[...]) * g_ref[...]) * (1.0 + _rows(sc_ref)) + _rows(sh_ref)
    hb = h.astype(bf16)
    col = lambda c: c
    ab = _dot(hb, wab_ref[...])
    r_ref[...] = _silu(_dot(hb, w_ref[:, col(_C_R):col(_C_AB)])).astype(bf16)
    gq_ref[...] = (_dot(hb, w_ref[:, col(_C_GQ):col(_C_GK)]) * (GLA_DK ** -0.5)).astype(bf16)
    z = _dot(ab.astype(bf16), wg_ref[...]) + bg_ref[...]
    log_sig = jnp.minimum(z, 0.0) - jnp.log1p(jnp.exp(-jnp.abs(z)))
    lg_ref[...] = log_sig * (LOG2_E / GLA_GATE_NORM)
    gk_ref[...] = _dot(hb, w_ref[:, col(_C_GK):col(_C_GV)]).astype(bf16)
    gv_ref[...] = _dot(hb, w_ref[:, col(_C_GV):col(_C_R)]).astype(bf16)
    kv = _dot(hb, w_ref[:, col(_C_K):col(_C_GQ)])
    k_ref[...] = kv[:, :KV_WIDTH]
    v_ref[...] = kv[:, KV_WIDTH:]
    if q_transposed:
        q_ref[...] = (_dot_nt(wq_ref[...], hb) * (LOG2_E * SWA_HEAD_DIM ** -0.5)).astype(bf16)
    else:
        q_ref[...] = _dot(hb, wq_ref[...]).astype(bf16)


def _proj(x2d, sh, sc, g_mix, w_q, w_rest, w_ab, w_gate, b_gate, *, tm, mod_specs, q_transposed):
    n = x2d.shape[0]
    row = lambda w: pl.BlockSpec((tm, w), lambda i: (i, 0))
    outs = [(KV_WIDTH, f32), (KV_WIDTH, f32), (GLA_QK_WIDTH, bf16),
            (GLA_QK_WIDTH, bf16), (GLA_V_WIDTH, bf16), (GLA_V_WIDTH, bf16), (GLA_QK_WIDTH, f32)]
    if q_transposed:
        q_shape, q_spec = (SWA_WIDTH, n), pl.BlockSpec((SWA_WIDTH, tm), lambda i: (0, i))
    else:
        q_shape, q_spec = (n, SWA_WIDTH), row(SWA_WIDTH)
    return pl.pallas_call(
        functools.partial(_proj_kernel, q_transposed=q_transposed),
        out_shape=[jax.ShapeDtypeStruct(q_shape, bf16)]
                  + [jax.ShapeDtypeStruct((n, w), dt) for w, dt in outs],
        grid=(n // tm,),
        in_specs=[row(D_MODEL), mod_specs[0], mod_specs[1],
                  _resident((1, D_MODEL)), _resident(w_q.shape), _resident(w_rest.shape),
                  _resident(w_ab.shape), _resident(w_gate.shape), _resident((1, GLA_QK_WIDTH))],
        out_specs=[q_spec] + [row(w) for w, _ in outs],
        compiler_params=pltpu.CompilerParams(dimension_semantics=("parallel",),
                                             vmem_limit_bytes=VMEM_LIMIT),
        name="proj",
    )(x2d, sh, sc, g_mix.reshape(1, D_MODEL), w_q, w_rest, w_ab, w_gate, b_gate.reshape(1, -1))


def _swa_kernel(q_ref, kp_ref, vp_ref, kc_ref, vc_ref, sink_ref, o_ref, *, tq):
    pad = jnp.zeros((WINDOW - tq, KV_WIDTH), f32)
    k_all = jnp.concatenate([kp_ref[...], kc_ref[...], pad], axis=0).astype(bf16)
    v_all = jnp.concatenate([vp_ref[...], vc_ref[...], pad], axis=0).astype(bf16)
    rows, cols = SWA_GROUP * tq, 2 * WINDOW
    valid = lax.broadcasted_iota(jnp.int32, (rows, cols), 1) < WINDOW + tq
    q = q_ref[...]
    outs = []
    for j in range(SWA_KV_HEADS):
        heads = [q[:, (j * SWA_GROUP + g) * SWA_HEAD_DIM:(j * SWA_GROUP + g + 1) * SWA_HEAD_DIM]
                 for g in range(SWA_GROUP)]
        qs = jnp.concatenate(heads, axis=0)
        kj = k_all[:, j * SWA_HEAD_DIM:(j + 1) * SWA_HEAD_DIM]
        vj = v_all[:, j * SWA_HEAD_DIM:(j + 1) * SWA_HEAD_DIM]
        s = _dot_nt(qs, kj) * (SWA_HEAD_DIM ** -0.5)
        s = jnp.where(valid, s, NEG_INF)
        sink = sink_ref[j]
        m = jnp.maximum(jnp.max(s, axis=1, keepdims=True), sink)
        p = jnp.exp(s - m)
        den = jnp.sum(p, axis=1, keepdims=True) + jnp.exp(sink - m)
        o = _dot(p.astype(bf16), vj) / den
        outs.append(jnp.concatenate([o[g * tq:(g + 1) * tq] for g in range(SWA_GROUP)], axis=1))
    o_ref[...] = jnp.concatenate(outs, axis=1).astype(bf16)


def _swa(q, k_past, v_past, k_new, v_new, sinks, *, n_seq, tq):
    sink_rows = jnp.repeat(sinks.astype(f32).reshape(SWA_KV_HEADS, SWA_GROUP), tq, axis=1)
    sink_rows = sink_rows.reshape(SWA_KV_HEADS, SWA_GROUP * tq, 1)
    new = lambda w: pl.BlockSpec((tq, w), lambda b: (b, 0))
    past = pl.BlockSpec((WINDOW, KV_WIDTH), lambda b: (b, 0))
    return pl.pallas_call(
        functools.partial(_swa_kernel, tq=tq),
        out_shape=jax.ShapeDtypeStruct(q.shape, bf16),
        grid=(n_seq,),
        in_specs=[new(SWA_WIDTH), past, past, new(KV_WIDTH), new(KV_WIDTH),
                  pl.BlockSpec(sink_rows.shape, lambda b: (0, 0, 0))],
        out_specs=new(SWA_WIDTH),
        compiler_params=pltpu.CompilerParams(dimension_semantics=("parallel",),
                                             vmem_limit_bytes=VMEM_LIMIT),
        name="swa",
    )(q, k_past, v_past, k_new, v_new, sink_rows)


def _swa_t_kernel(q_ref, kp_ref, vp_ref, kc_ref, vc_ref, sink_ref, mk_ref, mq0_ref, mq_ref, o_ref, *, n_sub):
    tq = WINDOW
    hd = SWA_HEAD_DIM
    k_all = jnp.concatenate([kp_ref[...], kc_ref[...]], axis=0).astype(bf16)
    v_all = jnp.concatenate([vp_ref[...], vc_ref[...]], axis=0).astype(bf16)
    low_lanes = lax.broadcasted_iota(jnp.int32, (2 * WINDOW, KV_WIDTH), 1) < hd
    ones = jnp.ones((2 * WINDOW, KV_WIDTH), bf16)
    for sub in range(n_sub):
        keys = slice(sub * tq, sub * tq + 2 * WINDOW)
        toks = slice(sub * tq, (sub + 1) * tq)
        mq = (mq0_ref if sub == 0 else mq_ref)[0]
        for j in range(SWA_KV_HEADS):
            head = lambda g: slice((j * SWA_GROUP + g) * hd, (j * SWA_GROUP + g + 1) * hd)
            qs = jnp.concatenate([q_ref[head(g), toks] for g in range(SWA_GROUP)], axis=1)
            own = low_lanes if j == 0 else jnp.logical_not(low_lanes)
            k_aug = jnp.where(own, k_all[keys], mk_ref[j])
            v_aug = jnp.where(own, v_all[keys], ones)
            q_aug = jnp.concatenate([qs, mq] if j == 0 else [mq, qs], axis=0)
            s = _dot(k_aug, q_aug)
            sink = sink_ref[j]
            m = jnp.maximum(jnp.max(s, axis=0, keepdims=True), sink)
            p = jnp.exp2(s - m).astype(bf16)
            o_aug = _dot_tn(v_aug, p)
            pv, p_sum = (o_aug[:hd], o_aug[hd:hd + 1]) if j == 0 else (o_aug[hd:], o_aug[0:1])
            o = pv / (p_sum + jnp.exp2(sink - m))
            for g in range(SWA_GROUP):
                o_ref[head(g), toks] = o[:, g * tq:(g + 1) * tq].astype(bf16)


def _swa_t(q_t, k, v, sinks, *, n_seq, tiles, n_sub):
    tq = WINDOW
    steps = tiles // n_sub
    hd = SWA_HEAD_DIM
    sink_cols = jnp.repeat(sinks.astype(f32).reshape(SWA_KV_HEADS, SWA_GROUP) * LOG2_E, tq, axis=1)
    sink_cols = sink_cols.reshape(SWA_KV_HEADS, 1, SWA_GROUP * tq)
    n_kc = 2 * WINDOW // CHUNK
    key_chunk = jnp.arange(2 * WINDOW) // CHUNK
    lane = jnp.arange(KV_WIDTH)
    mk = jnp.stack([lane[None, :] == hd + key_chunk[:, None], lane[None, :] == key_chunk[:, None]]).astype(bf16)
    q_chunk = (jnp.arange(SWA_GROUP * tq) % tq) // CHUNK
    kc = jnp.arange(hd)[:, None]
    band = (kc >= q_chunk[None, :]) & (kc <= q_chunk[None, :] + WINDOW // CHUNK)
    visible = jnp.stack([band & (kc >= WINDOW // CHUNK), band]) | (kc >= n_kc)
    mq = jnp.where(visible, 0.0, NEG_INF).astype(bf16)
    mq0_spec = pl.BlockSpec((1, hd, SWA_GROUP * tq), lambda b, u: (jnp.minimum(u, 1), 0, 0))
    mq_spec = pl.BlockSpec((1, hd, SWA_GROUP * tq), lambda b, u: (1, 0, 0))
    qspec = pl.BlockSpec((SWA_WIDTH, n_sub * tq), lambda b, u: (0, b * steps + u))
    cur = pl.BlockSpec((n_sub * tq, KV_WIDTH), lambda b, u: (b * steps + u, 0))
    prev = pl.BlockSpec((WINDOW, KV_WIDTH), lambda b, u: (b * tiles + jnp.maximum(n_sub * u - 1, 0), 0))
    return pl.pallas_call(
        functools.partial(_swa_t_kernel, n_sub=n_sub),
        out_shape=jax.ShapeDtypeStruct(q_t.shape, bf16),
        grid=(n_seq, steps),
        in_specs=[qspec, prev, prev, cur, cur, pl.BlockSpec(sink_cols.shape, lambda b, u: (0, 0, 0)),
                  pl.BlockSpec(mk.shape, lambda b, u: (0, 0, 0)), mq0_spec, mq_spec],
        out_specs=qspec,
        compiler_params=pltpu.CompilerParams(dimension_semantics=("parallel", "arbitrary"),
                                             vmem_limit_bytes=VMEM_LIMIT),
        name="swa_t",
    )(q_t, k, v, k, v, sink_cols, mk, mq, mq)


def _gla_kernel(q_ref, k_ref, v_ref, lg_ref, r_ref, gh_ref, s0_ref, o_ref, s_ref, *, c, n_sub):
    @pl.when(pl.program_id(1) == 0)
    def _():
        s_ref[...] = s0_ref[...]

    row = lax.broadcasted_iota(jnp.int32, (c, GLA_DK), 0)
    causal = (lax.broadcasted_iota(jnp.int32, (c, c), 0) >= lax.broadcasted_iota(jnp.int32, (c, c), 1))
    for sub in range(n_sub):
        ts = slice(sub * c, (sub + 1) * c)
        for h in range(GLA_HEADS):
            ks = slice(h * GLA_DK, (h + 1) * GLA_DK)
            vs = slice(h * GLA_DV, (h + 1) * GLA_DV)
            b = lg_ref[ts, ks]
            step = 1
            while step < c:
                b = b + jnp.where(row >= step, pltpu.roll(b, step, 0), 0.0)
                step *= 2
            b_last = b[c - 1:c, :]
            q = q_ref[ts, ks].astype(f32)
            k = k_ref[ts, ks].astype(f32)
            v = v_ref[ts, vs]
            qd = (q * jnp.exp2(b)).astype(bf16)
            kd = (k * jnp.exp2(-b)).astype(bf16)
            kl = (k * jnp.exp2(b_last - b)).astype(bf16)
            a = jnp.where(causal, _dot_nt(qd, kd), 0.0)
            s = s_ref[0, h]
            o = _dot(qd, s.astype(bf16)) + _dot(a.astype(bf16), v)
            decay = jnp.broadcast_to(jnp.exp2(b_last), (GLA_DK, GLA_DK)).T
            s_ref[0, h] = s * jnp.concatenate([decay, decay], axis=1) + _dot_tn(kl, v)
            on = _rms(o) * gh_ref[...]
            o_ref[ts, vs] = (on * r_ref[ts, vs].astype(f32)).astype(bf16)


def _gla(gq, gk, gv, lg, r, g_head, s0, *, n_seq, c, n_sub, steps):
    rows = c * n_sub
    blk = lambda w: pl.BlockSpec((rows, w), lambda b, t: (b * steps + t, 0))
    state = pl.BlockSpec((1, GLA_HEADS, GLA_DK, GLA_DV), lambda b, t: (b, 0, 0, 0))
    return pl.pallas_call(
        functools.partial(_gla_kernel, c=c, n_sub=n_sub),
        out_shape=[jax.ShapeDtypeStruct(gv.shape, bf16),
                   jax.ShapeDtypeStruct((n_seq, GLA_HEADS, GLA_DK, GLA_DV), f32)],
        grid=(n_seq, steps),
        in_specs=[blk(GLA_QK_WIDTH), blk(GLA_QK_WIDTH), blk(GLA_V_WIDTH), blk(GLA_QK_WIDTH),
                  blk(GLA_V_WIDTH), pl.BlockSpec((1, GLA_DV), lambda b, t: (0, 0)), state],
        out_specs=[blk(GLA_V_WIDTH), state],
        compiler_params=pltpu.CompilerParams(dimension_semantics=("parallel", "arbitrary"),
                                             vmem_limit_bytes=VMEM_LIMIT),
        name="gla",
    )(gq, gk, gv, lg, r, g_head.reshape(1, GLA_DV), s0)


def _pack_pair(hb):
    w = hb.shape[1] // 2
    a = lax.bitcast_convert_type(hb[:, :w].astype(f32), u32)
    b = lax.bitcast_convert_type(hb[:, w:].astype(f32), u32)
    return a | (b >> 16)


def _unpack_pair(p):
    a = lax.bitcast_convert_type(p & jnp.uint32(0xFFFF0000), f32)
    b = lax.bitcast_convert_type(p << 16, f32)
    return a, b


def _load_row_tiles(ref, n):
    return jnp.concatenate([ref[pl.ds(s, n, stride=SUBLANES), :] for s in range(SUBLANES)], axis=1)


def _store_row_tiles(ref, val):
    n = val.shape[0]
    for s in range(SUBLANES):
        ref[pl.ds(s, n, stride=SUBLANES), :] = val[:, s * LANES:(s + 1) * LANES]


def _row_tile(ref, r):
    return ref.at[pl.ds(pl.multiple_of(r * SUBLANES, SUBLANES), SUBLANES), :]


def _outproj_kernel(*refs, n_alias, oa_transposed):
    (oa_ref, ob_ref, x_ref, gt_ref, sh_ref, sc_ref, g_ref, w_ref, wr_ref, br_ref) = refs[:10]
    x1_ref, h2_ref, route_ref, x1_s = refs[10 + n_alias:]

    @pl.when(pl.program_id(0) == 0)
    def _():
        x1_s[...] = jnp.zeros(x1_s.shape, x1_s.dtype)

    h2 = (_rms(x1_s[...]) * g_ref[...]) * (1.0 + _rows(sc_ref)) + _rows(sh_ref)
    hi = h2.astype(bf16)
    _store_row_tiles(h2_ref, _pack_pair(hi))
    logits = _dot(hi, wr_ref[...]) + br_ref[...]
    lane = lax.broadcasted_iota(jnp.int32, logits.shape, 1)
    lane_f = lane.astype(f32)
    neg = float("-inf")
    first = lambda hit: jnp.min(jnp.where(hit, lane_f, float(LANES)), axis=1, keepdims=True)
    lg_g = jnp.where(lane < N_GROUPS, logits, neg)
    g_max = jnp.max(lg_g, axis=1, keepdims=True)
    g_sel = first(lg_g == g_max)
    p_sel = 1.0 / jnp.sum(jnp.exp(lg_g - g_max), axis=1, keepdims=True)
    e_lane = lane - N_GROUPS
    in_group = (e_lane >= 0) & (e_lane < N_EXPERTS) & ((e_lane // EXPERTS_PER_GROUP).astype(f32) == g_sel)
    lg_e = jnp.where(in_group, logits, neg)
    v1 = jnp.max(lg_e, axis=1, keepdims=True)
    i1 = first(lg_e == v1)
    lg_e2 = jnp.where(lane_f == i1, neg, lg_e)
    v2 = jnp.max(lg_e2, axis=1, keepdims=True)
    i2 = first(lg_e2 == v2)
    e = jnp.exp(v2 - v1)
    g1 = p_sel / (1.0 + e)
    g2 = p_sel * e / (1.0 + e)
    rec = jnp.where(lane == ROUTE_E0, i1 - N_GROUPS, 0.0)
    rec = jnp.where(lane == ROUTE_E1, i2 - N_GROUPS, rec)
    rec = jnp.where(lane == ROUTE_G0, g1, rec)
    rec = jnp.where(lane == ROUTE_G1, g2, rec)
    route_ref[...] = rec

    oa_dot = _dot_tn if oa_transposed else _dot
    mix = oa_dot(oa_ref[...], w_ref[:SWA_WIDTH, :]) + _dot(ob_ref[...], w_ref[SWA_WIDTH:, :])
    x1 = x_ref[...] + _rows(gt_ref) * mix
    x1_ref[...] = x1
    x1_s[...] = x1


def _outproj(oa, ob, x2d, gt, sh, sc, g_ffn, w_out, w_route, b_route, *, tm, mod_specs,
             n_total, row0, bufs=None, fill_steps=0, oa_transposed=False):
    n = x2d.shape[0]
    tiles = n // tm
    blocks = tiles + fill_steps
    row = lambda w: pl.BlockSpec((tm, w), lambda i: (jnp.minimum(i, tiles - 1), 0))
    oa_spec = row(SWA_WIDTH)
    if oa_transposed:
        oa_spec = pl.BlockSpec((SWA_WIDTH, tm), lambda i: (0, jnp.minimum(i, tiles - 1)))
    head_blk = lambda i: row0 // tm + jnp.minimum(i, blocks - 1)
    tail_blk = lambda i: row0 // tm + jnp.maximum(i - 1, 0)
    out_x1 = pl.BlockSpec((tm, D_MODEL), lambda i: (head_blk(i), 0))
    out_tiles = pl.BlockSpec((tm * SUBLANES, LANES), lambda i: (tail_blk(i), 0))
    out_route = pl.BlockSpec((tm, LANES), lambda i: (tail_blk(i), 0))
    alias_in = list(bufs) if bufs is not None else []
    n_in = 10
    return pl.pallas_call(
        functools.partial(_outproj_kernel, n_alias=len(alias_in), oa_transposed=oa_transposed),
        out_shape=[jax.ShapeDtypeStruct((n_total, D_MODEL), f32),
                   jax.ShapeDtypeStruct((n_total * SUBLANES, LANES), u32),
                   jax.ShapeDtypeStruct((n_total, LANES), f32)],
        grid=(blocks + 1,),
        in_specs=[oa_spec, row(GLA_V_WIDTH), row(D_MODEL), mod_specs[0], mod_specs[1],
                  mod_specs[2], _resident((1, D_MODEL)), _resident(w_out.shape),
                  _resident(w_route.shape), _resident((1, LANES))]
                 + [pl.BlockSpec(memory_space=pl.ANY)] * len(alias_in),
        out_specs=[out_x1, out_tiles, out_route],
        scratch_shapes=[pltpu.VMEM((tm, D_MODEL), f32)],
        input_output_aliases={n_in + a: a for a in range(len(alias_in))},
        compiler_params=pltpu.CompilerParams(dimension_semantics=("arbitrary",),
                                             vmem_limit_bytes=VMEM_LIMIT),
        name="outproj",
    )(oa, ob, x2d, gt, sh, sc, g_ffn.reshape(1, D_MODEL), w_out, w_route, b_route, *alias_in)


def _rank_kernel(route_ref, pos_ref, pend_ref, rec_s, *, n_chunks, tb):
    n_e = N_EXPERTS
    expert = lax.broadcasted_iota(jnp.int32, (n_e, LANES), 0)
    expert_f = expert.astype(f32)
    earlier = (lax.broadcasted_iota(jnp.int32, (LANES, LANES), 0)
               < lax.broadcasted_iota(jnp.int32, (LANES, LANES), 1)).astype(bf16)

    def onehots(c):
        rec = rec_s[c]
        return expert_f == rec[ROUTE_E0:ROUTE_E0 + 1, :], expert_f == rec[ROUTE_E1:ROUTE_E1 + 1, :]

    def count(c, cnt):
        rec_s[c] = route_ref[pl.ds(pl.multiple_of(c * LANES, LANES), LANES), :].T[:SUBLANES, :]
        h0, h1 = onehots(c)
        return cnt + jnp.sum((h0 | h1).astype(f32), axis=1, keepdims=True)

    cnt = lax.fori_loop(0, n_chunks, count, jnp.zeros((n_e, 1), f32), unroll=4)
    padded = jnp.floor((cnt + (tb - 1.0)) / tb) * tb
    end = jnp.broadcast_to(padded, (n_e, LANES))
    step = 1
    while step < n_e:
        end = end + jnp.where(expert >= step, pltpu.roll(end, step, 0), 0.0)
        step *= 2
    pend_ref[0] = end.astype(jnp.int32)
    pend_ref[1] = jnp.broadcast_to(jnp.where(cnt > 0, cnt - (padded - tb), 0.0), (n_e, LANES)).astype(jnp.int32)

    def rank(c, base):
        h0, h1 = onehots(c)
        both = h0 | h1
        off = _dot(both.astype(bf16), earlier) + base
        pos_ref[0, pl.ds(c, 1), :] = jnp.sum(jnp.where(h0, off, 0.0), axis=0, keepdims=True).astype(jnp.int32)
        pos_ref[1, pl.ds(c, 1), :] = jnp.sum(jnp.where(h1, off, 0.0), axis=0, keepdims=True).astype(jnp.int32)
        return base + jnp.sum(both.astype(f32), axis=1, keepdims=True)

    lax.fori_loop(0, n_chunks, rank, (end - padded)[:, 0:1], unroll=4)


def _rank(route, *, tb):
    n = route.shape[0]
    n_chunks = n // LANES
    return pl.pallas_call(
        functools.partial(_rank_kernel, n_chunks=n_chunks, tb=tb),
        out_shape=[jax.ShapeDtypeStruct((TOP_K, n_chunks, LANES), jnp.int32),
                   jax.ShapeDtypeStruct((2, N_EXPERTS, LANES), jnp.int32)],
        in_specs=[_resident(route.shape)],
        scratch_shapes=[pltpu.VMEM((n_chunks, SUBLANES, LANES), f32)],
        compiler_params=pltpu.CompilerParams(vmem_limit_bytes=VMEM_LIMIT),
        name="rank",
    )(route)


def _dispatch_kernel(pos_ref, pend_ref, h2_ref, xs_hbm, stage, zbuf, sem, zsem, *, tm, tb, n_tot, steps):
    i = pl.program_id(0)
    slot = i % 2

    blk = tb * SUBLANES

    def wait_slot(s):
        for _ in range(TOP_K):
            pltpu.make_async_copy(stage.at[s], xs_hbm.at[pl.ds(0, tm * SUBLANES), :], sem.at[s]).wait()

    @pl.when(i == 0)
    def _():
        zbuf[...] = jnp.zeros(zbuf.shape, zbuf.dtype)

        def fill_block(b, start):
            cp = pltpu.make_async_copy(zbuf, xs_hbm.at[pl.ds(pl.multiple_of(b * blk, blk), blk), :], zsem)
            cp.start() if start else cp.wait()

        def fill(e, start):
            end = pend_ref[e]
            prev = jnp.where(e > 0, pend_ref[jnp.maximum(e - 1, 0)], 0)

            @pl.when(end > prev)
            def _():
                fill_block(end // tb - 1, start)

        fill_unused = fill_block

        first_unused = pend_ref[N_EXPERTS - 1] // tb
        n_blocks = xs_hbm.shape[0] // blk
        lax.fori_loop(0, N_EXPERTS, lambda e, c: (fill(e, True), c)[1], 0)
        lax.fori_loop(first_unused, n_blocks, lambda b, c: (fill_unused(b, True), c)[1], 0)
        lax.fori_loop(0, N_EXPERTS, lambda e, c: (fill(e, False), c)[1], 0)
        lax.fori_loop(first_unused, n_blocks, lambda b, c: (fill_unused(b, False), c)[1], 0)

    @pl.when(i >= 2)
    def _():
        wait_slot(slot)

    stage[slot] = h2_ref[...]

    def scatter(r, carry):
        for k in range(TOP_K):
            d = pos_ref[k * n_tot + i * tm + r]
            pltpu.make_async_copy(_row_tile(stage.at[slot], r), _row_tile(xs_hbm, d),
                                  sem.at[slot]).start(priority=k)
        return carry

    lax.fori_loop(0, tm, scatter, 0, unroll=8)

    @pl.when(i == steps - 1)
    def _():
        wait_slot(slot)
        if steps > 1:
            wait_slot(1 - slot)


def _dispatch(pos_flat, pend, h2p, *, tm, tb, n_blocks):
    n_tot = h2p.shape[0] // SUBLANES
    steps = n_tot // tm
    return pl.pallas_call(
        functools.partial(_dispatch_kernel, tm=tm, tb=tb, n_tot=n_tot, steps=steps),
        out_shape=jax.ShapeDtypeStruct((n_blocks * tb * SUBLANES, LANES), u32),
        grid_spec=pltpu.PrefetchScalarGridSpec(
            num_scalar_prefetch=2,
            grid=(steps,),
            in_specs=[pl.BlockSpec((tm * SUBLANES, LANES), lambda i, pos, pend: (i, 0))],
            out_specs=pl.BlockSpec(memory_space=pl.ANY),
            scratch_shapes=[pltpu.VMEM((2, tm * SUBLANES, LANES), u32),
                            pltpu.VMEM((tb * SUBLANES, LANES), u32),
                            pltpu.SemaphoreType.DMA((2,)), pltpu.SemaphoreType.DMA(())]),
        compiler_params=pltpu.CompilerParams(dimension_semantics=("arbitrary",),
                                             vmem_limit_bytes=VMEM_LIMIT),
        name="dispatch",
    )(pos_flat, pend, h2p)


def _expert_kernel(be_ref, ne_ref, hf_ref, nu_ref, xs_ref, wg_hbm, wu_hbm, wd_hbm, y_ref,
                   wg_s, wu_s, wd_s, wg_b, wu_b, wd_b, sem, *, tb):
    i = pl.program_id(0)
    used = i < nu_ref[0]
    e = be_ref[i]

    def fetch(expert):
        pairs = ((wg_hbm, wg_s), (wu_hbm, wu_s), (wd_hbm, wd_s))
        return [pltpu.make_async_copy(src.at[expert], dst, sem.at[n]) for n, (src, dst) in enumerate(pairs)]

    @pl.when(i == 0)
    def _():
        for cp in fetch(e):
            cp.start()

    @pl.when(used & ((i == 0) | (e != be_ref[jnp.maximum(i - 1, 0)])))
    def _():
        for cp in fetch(e):
            cp.wait()
        wg_b[...] = wg_s[...].astype(bf16)
        wu_b[...] = wu_s[...].astype(bf16)
        wd_b[...] = wd_s[...].astype(bf16)
        nxt = ne_ref[i]

        @pl.when(nxt != e)
        def _():
            for cp in fetch(nxt):
                cp.start()

    def ffn(rows):
        tiles = pl.ds(0, rows * SUBLANES)
        a, b = _unpack_pair(_load_row_tiles(xs_ref.at[tiles], rows))
        x = jnp.concatenate([a.astype(bf16), b.astype(bf16)], axis=1)
        g = _dot(x, wg_b[...])
        u = _dot(x, wu_b[...])
        y = _dot((_silu(g) * u).astype(bf16), wd_b[...])
        _store_row_tiles(y_ref.at[tiles], _pack_pair(y.astype(bf16)))

    half_full = hf_ref[i] == 1

    @pl.when(used & jnp.logical_not(half_full))
    def _():
        ffn(tb)

    @pl.when(used & half_full)
    def _():
        ffn(tb // 2)
        rest = pl.ds(tb // 2 * SUBLANES, tb // 2 * SUBLANES)
        y_ref[rest, :] = jnp.zeros((tb // 2 * SUBLANES, LANES), y_ref.dtype)

    @pl.when(jnp.logical_not(used))
    def _():
        y_ref[...] = jnp.zeros(y_ref.shape, y_ref.dtype)


def _experts(block_e, next_e, half_full, n_used, xs, w_eg, w_eu, w_ed, *, tb):
    n_blocks = block_e.shape[0]
    blk = (tb * SUBLANES, LANES)
    up, down = (D_MODEL, EXPERT_HIDDEN), (EXPERT_HIDDEN, D_MODEL)
    hbm = pl.BlockSpec(memory_space=pl.ANY)
    return pl.pallas_call(
        functools.partial(_expert_kernel, tb=tb),
        out_shape=jax.ShapeDtypeStruct(xs.shape, u32),
        grid_spec=pltpu.PrefetchScalarGridSpec(
            num_scalar_prefetch=4,
            grid=(n_blocks,),
            in_specs=[pl.BlockSpec(blk, lambda i, be, ne, hf, nu: (jnp.minimum(i, nu[0] - 1), 0)),
                      hbm, hbm, hbm],
            out_specs=pl.BlockSpec(blk, lambda i, be, ne, hf, nu: (i, 0)),
            scratch_shapes=[pltpu.VMEM(up, f32), pltpu.VMEM(up, f32), pltpu.VMEM(down, f32),
                            pltpu.VMEM(up, bf16), pltpu.VMEM(up, bf16), pltpu.VMEM(down, bf16),
                            pltpu.SemaphoreType.DMA((3,))]),
        compiler_params=pltpu.CompilerParams(dimension_semantics=("arbitrary",),
                                             vmem_limit_bytes=VMEM_LIMIT),
        name="experts",
    )(block_e, next_e, half_full, n_used, xs, w_eg, w_eu, w_ed)


def _combine_kernel(pos_ref, yb_hbm, x1_ref, route_ref, gt_ref, gf_ref, y_ref, ybuf, sem, *, tm):
    i = pl.program_id(0)
    slot = i % 2

    def start(blk, s):
        def body(r, carry):
            for k in range(TOP_K):
                t = pos_ref[(k * pl.num_programs(0) + blk) * tm + r]
                pltpu.make_async_copy(_row_tile(yb_hbm, t), _row_tile(ybuf.at[s, k], r),
                                      sem.at[s]).start(priority=k)
            return carry
        lax.fori_loop(0, tm, body, 0, unroll=8)

    @pl.when(i == 0)
    def _():
        start(0, 0)

    @pl.when(i + 1 < pl.num_programs(0))
    def _():
        start(i + 1, 1 - slot)

    for k in range(TOP_K):
        pltpu.make_async_copy(yb_hbm.at[pl.ds(0, tm * SUBLANES), :], ybuf.at[slot, k], sem.at[slot]).wait()
    route = route_ref[...]
    a0, b0 = _unpack_pair(_load_row_tiles(ybuf.at[slot, 0], tm))
    a1, b1 = _unpack_pair(_load_row_tiles(ybuf.at[slot, 1], tm))
    g0 = route[:, ROUTE_G0:ROUTE_G0 + 1]
    g1 = route[:, ROUTE_G1:ROUTE_G1 + 1]
    moe = jnp.concatenate([a0 * g0 + a1 * g1, b0 * g0 + b1 * g1], axis=1)
    x2 = x1_ref[...] + _rows(gt_ref) * moe
    y_ref[...] = _rms(x2) * gf_ref[...]


def _combine(pos_km, yb, x1, route, gt, g_final, *, tm, n, row0, gt_spec):
    blk0 = row0 // tm
    return pl.pallas_call(
        functools.partial(_combine_kernel, tm=tm),
        out_shape=jax.ShapeDtypeStruct((n, D_MODEL), f32),
        grid_spec=pltpu.PrefetchScalarGridSpec(
            num_scalar_prefetch=1,
            grid=(n // tm,),
            in_specs=[pl.BlockSpec(memory_space=pl.ANY),
                      pl.BlockSpec((tm, D_MODEL), lambda i, pos: (blk0 + i, 0)),
                      pl.BlockSpec((tm, LANES), lambda i, pos: (blk0 + i, 0)),
                      gt_spec,
                      pl.BlockSpec((1, D_MODEL), lambda i, pos: (0, 0))],
            out_specs=pl.BlockSpec((tm, D_MODEL), lambda i, pos: (i, 0)),
            scratch_shapes=[pltpu.VMEM((2, TOP_K, tm * SUBLANES, LANES), u32),
                            pltpu.SemaphoreType.DMA((2,))]),
        compiler_params=pltpu.CompilerParams(dimension_semantics=("arbitrary",),
                                             vmem_limit_bytes=VMEM_LIMIT),
        name="combine",
    )(pos_km, yb, x1, route, gt, g_final.reshape(1, D_MODEL))


def kernel(x_prompt, x_sample, cache_swa_k, cache_swa_v, state_gla, c_prompt, c_sample, g_mix_norm, g_ffn_norm, w_ada, b_ada, w_in, attn_sinks, w_gla_gate, b_gla_gate, g_gla_norm, w_out, w_router_group, b_router_group, w_router_expert, b_router_expert, w_expert_gate, w_expert_up, w_expert_down, g_final):
    depth = w_in.shape[0]
    assert depth == 1
    bp, t, d = x_prompt.shape
    bs, ts, _ = x_sample.shape
    n_p, n_s = bp * t, bs * ts
    n_tot = n_p + n_s
    assert n_tot % LANES == 0
    tm = 512
    to = 512
    tb = 512
    tc = 128
    td = LANES
    gla_c = 128
    gla_sub = 16
    swa_sub = 8

    w_in0 = w_in[0]
    w_rest = w_in0.astype(bf16)
    w_q = w_rest[:, :_C_K]
    w_q_t = w_q.T
    w_ab = jnp.pad(w_in0[:, _C_AB:], ((0, 0), (0, LANES - GLA_GATE_RANK))).astype(bf16)
    w_gate = jnp.pad(w_gla_gate[0], ((0, LANES - GLA_GATE_RANK), (0, 0))).astype(bf16)
    w_out_b = w_out[0].astype(bf16)
    n_r = N_GROUPS + N_EXPERTS
    w_r = jnp.pad(jnp.concatenate([w_router_group[0], w_router_expert[0]], axis=1),
                  ((0, 0), (0, LANES - n_r)))
    w_route = w_r.astype(bf16)
    b_route = jnp.pad(jnp.concatenate([b_router_group[0], b_router_expert[0]]),
                      (0, LANES - n_r)).reshape(1, LANES)

    c_all = jnp.concatenate([c_prompt, c_sample], axis=0)
    mod = _adaln(c_all, w_ada[0], b_ada[0])
    mod_tab = mod.reshape((bp + bs) * N_MOD, 1, d)
    mod_s = jnp.repeat(mod[bp:].reshape(bs, N_MOD, d), ts, axis=0)
    mod_s = [mod_s[:, m] for m in range(N_MOD)]
    pmod = lambda comp, tile, lag=0: _mod_spec(comp, t // tile, n_p // tile, lag)
    smod = pl.BlockSpec((n_s, d), lambda i: (0, 0))

    xp = x_prompt.reshape(n_p, d)
    xs = x_sample.reshape(n_s, d)
    proj_w = (w_rest, w_ab, w_gate, b_gla_gate[0])
    qp, kp, vp, gqp, gkp, gvp, rp, lgp = _proj(xp, mod_tab, mod_tab, g_mix_norm[0], w_q_t, *proj_w, tm=tm,
                                               mod_specs=(pmod(0, tm), pmod(1, tm)), q_transposed=True)
    qs, ks, vs, gqs, gks, gvs, rs, lgs = _proj(xs, mod_s[0], mod_s[1], g_mix_norm[0], w_q, *proj_w, tm=n_s,
                                               mod_specs=(smod, smod), q_transposed=False)

    sinks = attn_sinks[0]
    oap = _swa_t(qp, kp, vp, sinks, n_seq=bp, tiles=t // WINDOW, n_sub=swa_sub)
    ck = cache_swa_k[0].reshape(bs * WINDOW, KV_WIDTH)
    cv = cache_swa_v[0].reshape(bs * WINDOW, KV_WIDTH)
    oas = _swa(qs, ck, cv, ks, vs, sinks, n_seq=bs, tq=ts)
    s_zero = jnp.zeros((bp, GLA_HEADS, GLA_DK, GLA_DV), f32)
    obp, sp = _gla(gqp, gkp, gvp, lgp, rp, g_gla_norm[0], s_zero, n_seq=bp, c=gla_c, n_sub=gla_sub,
                   steps=t // (gla_c * gla_sub))
    obs, ss = _gla(gqs, gks, gvs, lgs, rs, g_gla_norm[0], state_gla[0], n_seq=bs, c=ts, n_sub=1, steps=1)

    out_w = (g_ffn_norm[0], w_out_b, w_route, b_route)
    bufs = _outproj(oap, obp, xp, mod_tab, mod_tab, mod_tab, *out_w, tm=to,
                    mod_specs=(pmod(2, to), pmod(3, to, 1), pmod(4, to, 1)), n_total=n_tot, row0=0,
                    fill_steps=-(-n_s // to), oa_transposed=True)
    x1, h2p, route = _outproj(oas, obs, xs, mod_s[2], mod_s[3], mod_s[4], *out_w, tm=n_s,
                              mod_specs=(smod, smod, smod), n_total=n_tot, row0=n_p, bufs=bufs)

    n_blocks = -(-(n_tot * TOP_K + N_EXPERTS * (tb - 1)) // tb)
    pos, pend_tab = _rank(route, tb=tb)
    pend, last_rows = pend_tab[0, :, 0], pend_tab[1, :, 0]
    block_row0 = jnp.arange(n_blocks, dtype=jnp.int32) * tb
    block_e = jnp.minimum(jnp.sum(pend[None, :] <= block_row0[:, None], axis=1), N_EXPERTS - 1).astype(jnp.int32)
    n_used = pend[N_EXPERTS - 1:] // tb
    experts = jnp.arange(N_EXPERTS, dtype=jnp.int32)
    has_rows = jnp.diff(pend, prepend=0) > 0
    later = jnp.where((experts[None, :] > experts[:, None]) & has_rows[None, :], experts[None, :], N_EXPERTS)
    next_with_rows = jnp.min(later, axis=1)
    next_with_rows = jnp.where(next_with_rows == N_EXPERTS, experts, next_with_rows)
    of_block = lambda per_expert: jnp.sum(jnp.where(block_e[:, None] == experts[None, :], per_expert[None, :], 0),
                                          axis=1).astype(jnp.int32)
    next_e = of_block(next_with_rows)
    half_full = ((block_row0 + tb == of_block(pend)) & (of_block(last_rows) <= tb // 2)).astype(jnp.int32)
    xsort = _dispatch(pos.reshape(-1), pend, h2p, tm=td, tb=tb, n_blocks=n_blocks)
    yb = _experts(block_e, next_e, half_full, n_used, xsort, w_expert_gate[0], w_expert_up[0],
                  w_expert_down[0], tb=tb)

    pos = pos.reshape(TOP_K, n_tot)
    pos_p = pos[:, :n_p].reshape(-1)
    pos_s = pos[:, n_p:].reshape(-1)
    gt_p = pl.BlockSpec((1, 1, d), lambda i, p: ((i // (t // tc)) * N_MOD + 5, 0, 0))
    gt_s = pl.BlockSpec((n_s, d), lambda i, p: (0, 0))
    y_p = _combine(pos_p, yb, x1, route, mod_tab, g_final, tm=tc, n=n_p, row0=0, gt_spec=gt_p)
    y_s = _combine(pos_s, yb, x1, route, mod_s[5], g_final, tm=n_s, n=n_s, row0=n_p, gt_spec=gt_s)

    kv_shape = (SWA_KV_HEADS, SWA_HEAD_DIM)
    last = lambda a: a.reshape(bp, t, KV_WIDTH)[:, t - WINDOW:, :].reshape(1, bp, WINDOW, *kv_shape)
    k_state_p, v_state_p = last(kp), last(vp)
    return (y_p.reshape(bp, t, d), y_s.reshape(bs, ts, d), k_state_p, v_state_p, sp[None],
            ks.reshape(bs, ts, *kv_shape)[None], vs.reshape(bs, ts, *kv_shape)[None], ss[None])
```

```python
import functools
import math

import jax
import jax.numpy as jnp
from jax import lax
from jax.experimental import pallas as pl
from jax.experimental.pallas import tpu as pltpu

f32 = jnp.float32
bf16 = jnp.bfloat16
u32 = jnp.uint32

D_MODEL = 2048
N_MOD = 6
EPS = 1e-6
NEG_INF = -1e30
LOG2_E = math.log2(math.e)

SWA_HEAD_DIM = 64
SWA_KV_HEADS = 2
SWA_GROUP = 8
SWA_WIDTH = SWA_KV_HEADS * SWA_GROUP * SWA_HEAD_DIM
KV_WIDTH = SWA_KV_HEADS * SWA_HEAD_DIM
WINDOW = 128
CHUNK = 64

GLA_HEADS = 4
GLA_DK = 128
GLA_DV = 256
GLA_QK_WIDTH = GLA_HEADS * GLA_DK
GLA_V_WIDTH = GLA_HEADS * GLA_DV
GLA_GATE_RANK = 16
GLA_GATE_NORM = 16.0

N_GROUPS = 4
EXPERTS_PER_GROUP = 8
N_EXPERTS = N_GROUPS * EXPERTS_PER_GROUP
TOP_K = 2
EXPERT_HIDDEN = D_MODEL // 4
XS_RING = 3

_C_Q = 0
_C_K = _C_Q + SWA_WIDTH
_C_V = _C_K + KV_WIDTH
_C_GQ = _C_V + KV_WIDTH
_C_GK = _C_GQ + GLA_QK_WIDTH
_C_GV = _C_GK + GLA_QK_WIDTH
_C_R = _C_GV + GLA_V_WIDTH
_C_AB = _C_R + GLA_V_WIDTH

LANES = 128
SUBLANES = 8
PACKED_WIDTH = D_MODEL // 2
assert PACKED_WIDTH == SUBLANES * LANES
VMEM_LIMIT = 56 * 1024 * 1024

ROUTE_E0, ROUTE_E1, ROUTE_G0, ROUTE_G1 = 0, 1, 2, 3


def _dot(a, b):
    return jnp.dot(a, b, preferred_element_type=f32)


def _dot_nt(a, b):
    return lax.dot_general(a, b, (((1,), (1,)), ((), ())), preferred_element_type=f32)


def _dot_tn(a, b):
    return lax.dot_general(a, b, (((0,), (0,)), ((), ())), preferred_element_type=f32)


def _silu(x):
    return x / (1.0 + jnp.exp(-x))


def _rows(ref):
    v = ref[...]
    return v.reshape(v.shape[-2:])


def _rms(x):
    return x * lax.rsqrt(jnp.mean(x * x, axis=-1, keepdims=True) + EPS)


def _resident(shape):
    return pl.BlockSpec(shape, lambda *_: (0,) * len(shape), pipeline_mode=pl.Buffered(1))


def _mod_spec(comp, tiles_per_seq, n_tiles, lag=0):
    def index(i):
        tile = jnp.minimum(jnp.maximum(i - lag, 0), n_tiles - 1)
        return ((tile // tiles_per_seq) * N_MOD + comp, 0, 0)
    return pl.BlockSpec((1, 1, D_MODEL), index)


def _adaln_kernel(c_ref, w_ref, b_ref, o_ref):
    a = _silu(c_ref[...]).astype(bf16)
    o_ref[...] = _dot(a, w_ref[...].astype(bf16)) + b_ref[...]


def _adaln(c_all, w_ada, b_ada, tn=1024):
    r = c_all.shape[0]
    n = w_ada.shape[1]
    return pl.pallas_call(
        _adaln_kernel,
        out_shape=jax.ShapeDtypeStruct((r, n), f32),
        grid=(n // tn,),
        in_specs=[pl.BlockSpec((r, D_MODEL), lambda j: (0, 0)),
                  pl.BlockSpec((D_MODEL, tn), lambda j: (0, j)),
                  pl.BlockSpec((1, tn), lambda j: (0, j))],
        out_specs=pl.BlockSpec((r, tn), lambda j: (0, j)),
        compiler_params=pltpu.CompilerParams(dimension_semantics=("arbitrary",),
                                             vmem_limit_bytes=VMEM_LIMIT),
        name="adaln",
    )(c_all, w_ada, b_ada.reshape(1, n))


def _proj_kernel(x_ref, sh_ref, sc_ref, g_ref, wq_ref, w_ref, wab_ref, wg_ref, bg_ref,
                 q_ref, k_ref, v_ref, gq_ref, gk_ref, gv_ref, r_ref, lg_ref, *, q_transposed):
    h = (_rms(x_ref[...]) * g_ref[...]) * (1.0 + _rows(sc_ref)) + _rows(sh_ref)
    hb = h.astype(bf16)
    col = lambda c: c
    ab = _dot(hb, wab_ref[...])
    r_ref[...] = _silu(_dot(hb, w_ref[:, col(_C_R):col(_C_AB)])).astype(bf16)
    gq_ref[...] = (_dot(hb, w_ref[:, col(_C_GQ):col(_C_GK)]) * (GLA_DK ** -0.5)).astype(bf16)
    z = _dot(ab.astype(bf16), wg_ref[...]) + bg_ref[...]
    log_sig = jnp.minimum(z, 0.0) - jnp.log1p(jnp.exp(-jnp.abs(z)))
    lg_ref[...] = log_sig * (LOG2_E / GLA_GATE_NORM)
    gk_ref[...] = _dot(hb, w_ref[:, col(_C_GK):col(_C_GV)]).astype(bf16)
    gv_ref[...] = _dot(hb, w_ref[:, col(_C_GV):col(_C_R)]).astype(bf16)
    kv = _dot(hb, w_ref[:, col(_C_K):col(_C_GQ)])
    k_ref[...] = kv[:, :KV_WIDTH]
    v_ref[...] = kv[:, KV_WIDTH:]
    if q_transposed:
        q_ref[...] = (_dot_nt(wq_ref[...], hb) * (LOG2_E * SWA_HEAD_DIM ** -0.5)).astype(bf16)
    else:
        q_ref[...] = _dot(hb, wq_ref[...]).astype(bf16)


def _proj(x2d, sh, sc, g_mix, w_q, w_rest, w_ab, w_gate, b_gate, *, tm, mod_specs, q_transposed):
    n = x2d.shape[0]
    row = lambda w: pl.BlockSpec((tm, w), lambda i: (i, 0))
    outs = [(KV_WIDTH, f32), (KV_WIDTH, f32), (GLA_QK_WIDTH, bf16),
            (GLA_QK_WIDTH, bf16), (GLA_V_WIDTH, bf16), (GLA_V_WIDTH, bf16), (GLA_QK_WIDTH, f32)]
    if q_transposed:
        q_shape, q_spec = (SWA_WIDTH, n), pl.BlockSpec((SWA_WIDTH, tm), lambda i: (0, i))
    else:
        q_shape, q_spec = (n, SWA_WIDTH), row(SWA_WIDTH)
    return pl.pallas_call(
        functools.partial(_proj_kernel, q_transposed=q_transposed),
        out_shape=[jax.ShapeDtypeStruct(q_shape, bf16)]
                  + [jax.ShapeDtypeStruct((n, w), dt) for w, dt in outs],
        grid=(n // tm,),
        in_specs=[row(D_MODEL), mod_specs[0], mod_specs[1],
                  _resident((1, D_MODEL)), _resident(w_q.shape), _resident(w_rest.shape),
                  _resident(w_ab.shape), _resident(w_gate.shape), _resident((1, GLA_QK_WIDTH))],
        out_specs=[q_spec] + [row(w) for w, _ in outs],
        compiler_params=pltpu.CompilerParams(dimension_semantics=("parallel",),
                                             vmem_limit_bytes=VMEM_LIMIT),
        name="proj",
    )(x2d, sh, sc, g_mix.reshape(1, D_MODEL), w_q, w_rest, w_ab, w_gate, b_gate.reshape(1, -1))


def _swa_kernel(q_ref, kp_ref, vp_ref, kc_ref, vc_ref, sink_ref, o_ref, *, tq):
    pad = jnp.zeros((WINDOW - tq, KV_WIDTH), f32)
    k_all = jnp.concatenate([kp_ref[...], kc_ref[...], pad], axis=0).astype(bf16)
    v_all = jnp.concatenate([vp_ref[...], vc_ref[...], pad], axis=0).astype(bf16)
    rows, cols = SWA_GROUP * tq, 2 * WINDOW
    valid = lax.broadcasted_iota(jnp.int32, (rows, cols), 1) < WINDOW + tq
    q = q_ref[...]
    outs = []
    for j in range(SWA_KV_HEADS):
        heads = [q[:, (j * SWA_GROUP + g) * SWA_HEAD_DIM:(j * SWA_GROUP + g + 1) * SWA_HEAD_DIM]
                 for g in range(SWA_GROUP)]
        qs = jnp.concatenate(heads, axis=0)
        kj = k_all[:, j * SWA_HEAD_DIM:(j + 1) * SWA_HEAD_DIM]
        vj = v_all[:, j * SWA_HEAD_DIM:(j + 1) * SWA_HEAD_DIM]
        s = _dot_nt(qs, kj) * (SWA_HEAD_DIM ** -0.5)
        s = jnp.where(valid, s, NEG_INF)
        sink = sink_ref[j]
        m = jnp.maximum(jnp.max(s, axis=1, keepdims=True), sink)
        p = jnp.exp(s - m)
        den = jnp.sum(p, axis=1, keepdims=True) + jnp.exp(sink - m)
        o = _dot(p.astype(bf16), vj) / den
        outs.append(jnp.concatenate([o[g * tq:(g + 1) * tq] for g in range(SWA_GROUP)], axis=1))
    o_ref[...] = jnp.concatenate(outs, axis=1).astype(bf16)


def _swa(q, k_past, v_past, k_new, v_new, sinks, *, n_seq, tq):
    sink_rows = jnp.repeat(sinks.astype(f32).reshape(SWA_KV_HEADS, SWA_GROUP), tq, axis=1)
    sink_rows = sink_rows.reshape(SWA_KV_HEADS, SWA_GROUP * tq, 1)
    new = lambda w: pl.BlockSpec((tq, w), lambda b: (b, 0))
    past = pl.BlockSpec((WINDOW, KV_WIDTH), lambda b: (b, 0))
    return pl.pallas_call(
        functools.partial(_swa_kernel, tq=tq),
        out_shape=jax.ShapeDtypeStruct(q.shape, bf16),
        grid=(n_seq,),
        in_specs=[new(SWA_WIDTH), past, past, new(KV_WIDTH), new(KV_WIDTH),
                  pl.BlockSpec(sink_rows.shape, lambda b: (0, 0, 0))],
        out_specs=new(SWA_WIDTH),
        compiler_params=pltpu.CompilerParams(dimension_semantics=("parallel",),
                                             vmem_limit_bytes=VMEM_LIMIT),
        name="swa",
    )(q, k_past, v_past, k_new, v_new, sink_rows)


def _swa_t_kernel(q_ref, kp_ref, vp_ref, kc_ref, vc_ref, sink_ref, mk_ref, mq0_ref, mq_ref, o_ref, *, n_sub):
    tq = WINDOW
    hd = SWA_HEAD_DIM
    k_all = jnp.concatenate([kp_ref[...], kc_ref[...]], axis=0).astype(bf16)
    v_all = jnp.concatenate([vp_ref[...], vc_ref[...]], axis=0).astype(bf16)
    low_lanes = lax.broadcasted_iota(jnp.int32, (2 * WINDOW, KV_WIDTH), 1) < hd
    ones = jnp.ones((2 * WINDOW, KV_WIDTH), bf16)
    for sub in range(n_sub):
        keys = slice(sub * tq, sub * tq + 2 * WINDOW)
        toks = slice(sub * tq, (sub + 1) * tq)
        mq = (mq0_ref if sub == 0 else mq_ref)[0]
        for j in range(SWA_KV_HEADS):
            head = lambda g: slice((j * SWA_GROUP + g) * hd, (j * SWA_GROUP + g + 1) * hd)
            qs = jnp.concatenate([q_ref[head(g), toks] for g in range(SWA_GROUP)], axis=1)
            own = low_lanes if j == 0 else jnp.logical_not(low_lanes)
            k_aug = jnp.where(own, k_all[keys], mk_ref[j])
            v_aug = jnp.where(own, v_all[keys], ones)
            q_aug = jnp.concatenate([qs, mq] if j == 0 else [mq, qs], axis=0)
            s = _dot(k_aug, q_aug)
            sink = sink_ref[j]
            m = jnp.maximum(jnp.max(s, axis=0, keepdims=True), sink)
            p = jnp.exp2(s - m).astype(bf16)
            o_aug = _dot_tn(v_aug, p)
            pv, p_sum = (o_aug[:hd], o_aug[hd:hd + 1]) if j == 0 else (o_aug[hd:], o_aug[0:1])
            o = pv / (p_sum + jnp.exp2(sink - m))
            for g in range(SWA_GROUP):
                o_ref[head(g), toks] = o[:, g * tq:(g + 1) * tq].astype(bf16)


def _swa_t(q_t, k, v, sinks, *, n_seq, tiles, n_sub):
    tq = WINDOW
    steps = tiles // n_sub
    hd = SWA_HEAD_DIM
    sink_cols = jnp.repeat(sinks.astype(f32).reshape(SWA_KV_HEADS, SWA_GROUP) * LOG2_E, tq, axis=1)
    sink_cols = sink_cols.reshape(SWA_KV_HEADS, 1, SWA_GROUP * tq)
    n_kc = 2 * WINDOW // CHUNK
    key_chunk = jnp.arange(2 * WINDOW) // CHUNK
    lane = jnp.arange(KV_WIDTH)
    mk = jnp.stack([lane[None, :] == hd + key_chunk[:, None], lane[None, :] == key_chunk[:, None]]).astype(bf16)
    q_chunk = (jnp.arange(SWA_GROUP * tq) % tq) // CHUNK
    kc = jnp.arange(hd)[:, None]
    band = (kc >= q_chunk[None, :]) & (kc <= q_chunk[None, :] + WINDOW // CHUNK)
    visible = jnp.stack([band & (kc >= WINDOW // CHUNK), band]) | (kc >= n_kc)
    mq = jnp.where(visible, 0.0, NEG_INF).astype(bf16)
    mq0_spec = pl.BlockSpec((1, hd, SWA_GROUP * tq), lambda b, u: (jnp.minimum(u, 1), 0, 0))
    mq_spec = pl.BlockSpec((1, hd, SWA_GROUP * tq), lambda b, u: (1, 0, 0))
    qspec = pl.BlockSpec((SWA_WIDTH, n_sub * tq), lambda b, u: (0, b * steps + u))
    cur = pl.BlockSpec((n_sub * tq, KV_WIDTH), lambda b, u: (b * steps + u, 0))
    prev = pl.BlockSpec((WINDOW, KV_WIDTH), lambda b, u: (b * tiles + jnp.maximum(n_sub * u - 1, 0), 0))
    return pl.pallas_call(
        functools.partial(_swa_t_kernel, n_sub=n_sub),
        out_shape=jax.ShapeDtypeStruct(q_t.shape, bf16),
        grid=(n_seq, steps),
        in_specs=[qspec, prev, prev, cur, cur, pl.BlockSpec(sink_cols.shape, lambda b, u: (0, 0, 0)),
                  pl.BlockSpec(mk.shape, lambda b, u: (0, 0, 0)), mq0_spec, mq_spec],
        out_specs=qspec,
        compiler_params=pltpu.CompilerParams(dimension_semantics=("parallel", "arbitrary"),
                                             vmem_limit_bytes=VMEM_LIMIT),
        name="swa_t",
    )(q_t, k, v, k, v, sink_cols, mk, mq, mq)


def _gla_kernel(q_ref, k_ref, v_ref, lg_ref, r_ref, gh_ref, s0_ref, o_ref, s_ref, *, c, n_sub):
    @pl.when(pl.program_id(1) == 0)
    def _():
        s_ref[...] = s0_ref[...]

    row = lax.broadcasted_iota(jnp.int32, (c, GLA_DK), 0)
    causal = (lax.broadcasted_iota(jnp.int32, (c, c), 0) >= lax.broadcasted_iota(jnp.int32, (c, c), 1))
    for sub in range(n_sub):
        ts = slice(sub * c, (sub + 1) * c)
        for h in range(GLA_HEADS):
            ks = slice(h * GLA_DK, (h + 1) * GLA_DK)
            vs = slice(h * GLA_DV, (h + 1) * GLA_DV)
            b = lg_ref[ts, ks]
            step = 1
            while step < c:
                b = b + jnp.where(row >= step, pltpu.roll(b, step, 0), 0.0)
                step *= 2
            b_last = b[c - 1:c, :]
            q = q_ref[ts, ks].astype(f32)
            k = k_ref[ts, ks].astype(f32)
            v = v_ref[ts, vs]
            qd = (q * jnp.exp2(b)).astype(bf16)
            kd = (k * jnp.exp2(-b)).astype(bf16)
            kl = (k * jnp.exp2(b_last - b)).astype(bf16)
            a = jnp.where(causal, _dot_nt(qd, kd), 0.0)
            s = s_ref[0, h]
            o = _dot(qd, s.astype(bf16)) + _dot(a.astype(bf16), v)
            decay = jnp.broadcast_to(jnp.exp2(b_last), (GLA_DK, GLA_DK)).T
            s_ref[0, h] = s * jnp.concatenate([decay, decay], axis=1) + _dot_tn(kl, v)
            on = _rms(o) * gh_ref[...]
            o_ref[ts, vs] = (on * r_ref[ts, vs].astype(f32)).astype(bf16)


def _gla(gq, gk, gv, lg, r, g_head, s0, *, n_seq, c, n_sub, steps):
    rows = c * n_sub
    blk = lambda w: pl.BlockSpec((rows, w), lambda b, t: (b * steps + t, 0))
    state = pl.BlockSpec((1, GLA_HEADS, GLA_DK, GLA_DV), lambda b, t: (b, 0, 0, 0))
    return pl.pallas_call(
        functools.partial(_gla_kernel, c=c, n_sub=n_sub),
        out_shape=[jax.ShapeDtypeStruct(gv.shape, bf16),
                   jax.ShapeDtypeStruct((n_seq, GLA_HEADS, GLA_DK, GLA_DV), f32)],
        grid=(n_seq, steps),
        in_specs=[blk(GLA_QK_WIDTH), blk(GLA_QK_WIDTH), blk(GLA_V_WIDTH), blk(GLA_QK_WIDTH),
                  blk(GLA_V_WIDTH), pl.BlockSpec((1, GLA_DV), lambda b, t: (0, 0)), state],
        out_specs=[blk(GLA_V_WIDTH), state],
        compiler_params=pltpu.CompilerParams(dimension_semantics=("parallel", "arbitrary"),
                                             vmem_limit_bytes=VMEM_LIMIT),
        name="gla",
    )(gq, gk, gv, lg, r, g_head.reshape(1, GLA_DV), s0)


def _pack_pair(hb):
    w = hb.shape[1] // 2
    a = lax.bitcast_convert_type(hb[:, :w].astype(f32), u32)
    b = lax.bitcast_convert_type(hb[:, w:].astype(f32), u32)
    return a | (b >> 16)


def _unpack_pair(p):
    a = lax.bitcast_convert_type(p & jnp.uint32(0xFFFF0000), f32)
    b = lax.bitcast_convert_type(p << 16, f32)
    return a, b


def _load_row_tiles(ref, n):
    return jnp.concatenate([ref[pl.ds(s, n, stride=SUBLANES), :] for s in range(SUBLANES)], axis=1)


def _store_row_tiles(ref, val):
    n = val.shape[0]
    for s in range(SUBLANES):
        ref[pl.ds(s, n, stride=SUBLANES), :] = val[:, s * LANES:(s + 1) * LANES]


def _row_tile(ref, r):
    return ref.at[pl.ds(pl.multiple_of(r * SUBLANES, SUBLANES), SUBLANES), :]


def _outproj_kernel(*refs, n_alias, oa_transposed):
    (oa_ref, ob_ref, x_ref, gt_ref, sh_ref, sc_ref, g_ref, w_ref, wr_ref, br_ref) = refs[:10]
    x1_ref, h2_ref, route_ref, x1_s = refs[10 + n_alias:]

    @pl.when(pl.program_id(0) == 0)
    def _():
        x1_s[...] = jnp.zeros(x1_s.shape, x1_s.dtype)

    h2 = (_rms(x1_s[...]) * g_ref[...]) * (1.0 + _rows(sc_ref)) + _rows(sh_ref)
    hi = h2.astype(bf16)
    _store_row_tiles(h2_ref, _pack_pair(hi))
    logits = _dot(hi, wr_ref[...]) + br_ref[...]
    lane = lax.broadcasted_iota(jnp.int32, logits.shape, 1)
    lane_f = lane.astype(f32)
    neg = float("-inf")
    first = lambda hit: jnp.min(jnp.where(hit, lane_f, float(LANES)), axis=1, keepdims=True)
    lg_g = jnp.where(lane < N_GROUPS, logits, neg)
    g_max = jnp.max(lg_g, axis=1, keepdims=True)
    g_sel = first(lg_g == g_max)
    p_sel = 1.0 / jnp.sum(jnp.exp(lg_g - g_max), axis=1, keepdims=True)
    e_lane = lane - N_GROUPS
    in_group = (e_lane >= 0) & (e_lane < N_EXPERTS) & ((e_lane // EXPERTS_PER_GROUP).astype(f32) == g_sel)
    lg_e = jnp.where(in_group, logits, neg)
    v1 = jnp.max(lg_e, axis=1, keepdims=True)
    i1 = first(lg_e == v1)
    lg_e2 = jnp.where(lane_f == i1, neg, lg_e)
    v2 = jnp.max(lg_e2, axis=1, keepdims=True)
    i2 = first(lg_e2 == v2)
    e = jnp.exp(v2 - v1)
    g1 = p_sel / (1.0 + e)
    g2 = p_sel * e / (1.0 + e)
    rec = jnp.where(lane == ROUTE_E0, i1 - N_GROUPS, 0.0)
    rec = jnp.where(lane == ROUTE_E1, i2 - N_GROUPS, rec)
    rec = jnp.where(lane == ROUTE_G0, g1, rec)
    rec = jnp.where(lane == ROUTE_G1, g2, rec)
    route_ref[...] = rec

    oa_dot = _dot_tn if oa_transposed else _dot
    mix = oa_dot(oa_ref[...], w_ref[:SWA_WIDTH, :]) + _dot(ob_ref[...], w_ref[SWA_WIDTH:, :])
    x1 = x_ref[...] + _rows(gt_ref) * mix
    x1_ref[...] = x1
    x1_s[...] = x1


def _outproj(oa, ob, x2d, gt, sh, sc, g_ffn, w_out, w_route, b_route, *, tm, mod_specs,
             n_total, row0, bufs=None, fill_steps=0, oa_transposed=False):
    n = x2d.shape[0]
    tiles = n // tm
    blocks = tiles + fill_steps
    row = lambda w: pl.BlockSpec((tm, w), lambda i: (jnp.minimum(i, tiles - 1), 0))
    oa_spec = row(SWA_WIDTH)
    if oa_transposed:
        oa_spec = pl.BlockSpec((SWA_WIDTH, tm), lambda i: (0, jnp.minimum(i, tiles - 1)))
    head_blk = lambda i: row0 // tm + jnp.minimum(i, blocks - 1)
    tail_blk = lambda i: row0 // tm + jnp.maximum(i - 1, 0)
    out_x1 = pl.BlockSpec((tm, D_MODEL), lambda i: (head_blk(i), 0))
    out_tiles = pl.BlockSpec((tm * SUBLANES, LANES), lambda i: (tail_blk(i), 0))
    out_route = pl.BlockSpec((tm, LANES), lambda i: (tail_blk(i), 0))
    alias_in = list(bufs) if bufs is not None else []
    n_in = 10
    return pl.pallas_call(
        functools.partial(_outproj_kernel, n_alias=len(alias_in), oa_transposed=oa_transposed),
        out_shape=[jax.ShapeDtypeStruct((n_total, D_MODEL), f32),
                   jax.ShapeDtypeStruct((n_total * SUBLANES, LANES), u32),
                   jax.ShapeDtypeStruct((n_total, LANES), f32)],
        grid=(blocks + 1,),
        in_specs=[oa_spec, row(GLA_V_WIDTH), row(D_MODEL), mod_specs[0], mod_specs[1],
                  mod_specs[2], _resident((1, D_MODEL)), _resident(w_out.shape),
                  _resident(w_route.shape), _resident((1, LANES))]
                 + [pl.BlockSpec(memory_space=pl.ANY)] * len(alias_in),
        out_specs=[out_x1, out_tiles, out_route],
        scratch_shapes=[pltpu.VMEM((tm, D_MODEL), f32)],
        input_output_aliases={n_in + a: a for a in range(len(alias_in))},
        compiler_params=pltpu.CompilerParams(dimension_semantics=("arbitrary",),
                                             vmem_limit_bytes=VMEM_LIMIT),
        name="outproj",
    )(oa, ob, x2d, gt, sh, sc, g_ffn.reshape(1, D_MODEL), w_out, w_route, b_route, *alias_in)


def _rank_kernel(route_ref, pos_ref, pend_ref, rec_s, *, n_chunks, tb):
    n_e = N_EXPERTS
    expert = lax.broadcasted_iota(jnp.int32, (n_e, LANES), 0)
    expert_f = expert.astype(f32)
    earlier = (lax.broadcasted_iota(jnp.int32, (LANES, LANES), 0)
               < lax.broadcasted_iota(jnp.int32, (LANES, LANES), 1)).astype(bf16)

    def onehots(c):
        rec = rec_s[c]
        return expert_f == rec[ROUTE_E0:ROUTE_E0 + 1, :], expert_f == rec[ROUTE_E1:ROUTE_E1 + 1, :]

    def count(c, cnt):
        rec_s[c] = route_ref[pl.ds(pl.multiple_of(c * LANES, LANES), LANES), :].T[:SUBLANES, :]
        h0, h1 = onehots(c)
        return cnt + jnp.sum((h0 | h1).astype(f32), axis=1, keepdims=True)

    cnt = lax.fori_loop(0, n_chunks, count, jnp.zeros((n_e, 1), f32), unroll=4)
    padded = jnp.floor((cnt + (tb - 1.0)) / tb) * tb
    end = jnp.broadcast_to(padded, (n_e, LANES))
    step = 1
    while step < n_e:
        end = end + jnp.where(expert >= step, pltpu.roll(end, step, 0), 0.0)
        step *= 2
    pend_ref[0] = end.astype(jnp.int32)
    pend_ref[1] = jnp.broadcast_to(jnp.where(cnt > 0, cnt - (padded - tb), 0.0), (n_e, LANES)).astype(jnp.int32)

    def rank(c, base):
        h0, h1 = onehots(c)
        both = h0 | h1
        off = _dot(both.astype(bf16), earlier) + base
        pos_ref[0, pl.ds(c, 1), :] = jnp.sum(jnp.where(h0, off, 0.0), axis=0, keepdims=True).astype(jnp.int32)
        pos_ref[1, pl.ds(c, 1), :] = jnp.sum(jnp.where(h1, off, 0.0), axis=0, keepdims=True).astype(jnp.int32)
        return base + jnp.sum(both.astype(f32), axis=1, keepdims=True)

    lax.fori_loop(0, n_chunks, rank, (end - padded)[:, 0:1], unroll=4)


def _rank(route, *, tb):
    n = route.shape[0]
    n_chunks = n // LANES
    return pl.pallas_call(
        functools.partial(_rank_kernel, n_chunks=n_chunks, tb=tb),
        out_shape=[jax.ShapeDtypeStruct((TOP_K, n_chunks, LANES), jnp.int32),
                   jax.ShapeDtypeStruct((2, N_EXPERTS, LANES), jnp.int32)],
        in_specs=[_resident(route.shape)],
        scratch_shapes=[pltpu.VMEM((n_chunks, SUBLANES, LANES), f32)],
        compiler_params=pltpu.CompilerParams(vmem_limit_bytes=VMEM_LIMIT),
        name="rank",
    )(route)


def _dispatch_kernel(pos_ref, pend_ref, h2_ref, xs_hbm, stage, zbuf, sem, zsem, *, tm, tb, n_tot, steps):
    i = pl.program_id(0)
    slot = i % 2

    blk = tb * SUBLANES

    def wait_slot(s):
        for _ in range(TOP_K):
            pltpu.make_async_copy(stage.at[s], xs_hbm.at[pl.ds(0, tm * SUBLANES), :], sem.at[s]).wait()

    @pl.when(i == 0)
    def _():
        zbuf[...] = jnp.zeros(zbuf.shape, zbuf.dtype)

        def fill_block(b, start):
            cp = pltpu.make_async_copy(zbuf, xs_hbm.at[pl.ds(pl.multiple_of(b * blk, blk), blk), :], zsem)
            cp.start() if start else cp.wait()

        def fill(e, start):
            end = pend_ref[e]
            prev = jnp.where(e > 0, pend_ref[jnp.maximum(e - 1, 0)], 0)

            @pl.when(end > prev)
            def _():
                fill_block(end // tb - 1, start)

        fill_unused = fill_block

        first_unused = pend_ref[N_EXPERTS - 1] // tb
        n_blocks = xs_hbm.shape[0] // blk
        lax.fori_loop(0, N_EXPERTS, lambda e, c: (fill(e, True), c)[1], 0)
        lax.fori_loop(first_unused, n_blocks, lambda b, c: (fill_unused(b, True), c)[1], 0)
        lax.fori_loop(0, N_EXPERTS, lambda e, c: (fill(e, False), c)[1], 0)
        lax.fori_loop(first_unused, n_blocks, lambda b, c: (fill_unused(b, False), c)[1], 0)

    @pl.when(i >= 2)
    def _():
        wait_slot(slot)

    stage[slot] = h2_ref[...]

    def scatter(r, carry):
        for k in range(TOP_K):
            d = pos_ref[k * n_tot + i * tm + r]
            pltpu.make_async_copy(_row_tile(stage.at[slot], r), _row_tile(xs_hbm, d),
                                  sem.at[slot]).start(priority=k)
        return carry

    lax.fori_loop(0, tm, scatter, 0, unroll=8)

    @pl.when(i == steps - 1)
    def _():
        wait_slot(slot)
        if steps > 1:
            wait_slot(1 - slot)


def _dispatch(pos_flat, pend, h2p, *, tm, tb, n_blocks):
    n_tot = h2p.shape[0] // SUBLANES
    steps = n_tot // tm
    return pl.pallas_call(
        functools.partial(_dispatch_kernel, tm=tm, tb=tb, n_tot=n_tot, steps=steps),
        out_shape=jax.ShapeDtypeStruct((n_blocks * tb * SUBLANES, LANES), u32),
        grid_spec=pltpu.PrefetchScalarGridSpec(
            num_scalar_prefetch=2,
            grid=(steps,),
            in_specs=[pl.BlockSpec((tm * SUBLANES, LANES), lambda i, pos, pend: (i, 0))],
            out_specs=pl.BlockSpec(memory_space=pl.ANY),
            scratch_shapes=[pltpu.VMEM((2, tm * SUBLANES, LANES), u32),
                            pltpu.VMEM((tb * SUBLANES, LANES), u32),
                            pltpu.SemaphoreType.DMA((2,)), pltpu.SemaphoreType.DMA(())]),
        compiler_params=pltpu.CompilerParams(dimension_semantics=("arbitrary",),
                                             vmem_limit_bytes=VMEM_LIMIT),
        name="dispatch",
    )(pos_flat, pend, h2p)


def _expert_kernel(be_ref, ne_ref, hf_ref, nu_ref, xs_hbm, wg_hbm, wu_hbm, wd_hbm, y_ref,
                   wg_s, wu_s, wd_s, wg_b, wu_b, wd_b, sem, xbuf, xsem, *, tb):
    i = pl.program_id(0)
    n_used = nu_ref[0]
    used = i < n_used
    e = be_ref[i]
    blk = tb * SUBLANES

    def row_block(b):
        src = xs_hbm.at[pl.ds(pl.multiple_of(b * blk, blk), blk), :]
        return pltpu.make_async_copy(src, xbuf.at[b % XS_RING], xsem.at[b % XS_RING])

    @pl.when(i == 0)
    def _():
        for b in range(XS_RING - 1):
            @pl.when(b < n_used)
            def _():
                row_block(b).start()

    @pl.when(i + XS_RING - 1 < n_used)
    def _():
        row_block(i + XS_RING - 1).start()

    @pl.when(used)
    def _():
        row_block(i).wait()

    xs_ref = xbuf.at[i % XS_RING]

    def fetch(expert):
        pairs = ((wg_hbm, wg_s), (wu_hbm, wu_s), (wd_hbm, wd_s))
        return [pltpu.make_async_copy(src.at[expert], dst, sem.at[n]) for n, (src, dst) in enumerate(pairs)]

    @pl.when(i == 0)
    def _():
        for cp in fetch(e):
            cp.start()

    @pl.when(used & ((i == 0) | (e != be_ref[jnp.maximum(i - 1, 0)])))
    def _():
        for cp in fetch(e):
            cp.wait()
        wg_b[...] = wg_s[...].astype(bf16)
        wu_b[...] = wu_s[...].astype(bf16)
        wd_b[...] = wd_s[...].astype(bf16)
        nxt = ne_ref[i]

        @pl.when(nxt != e)
        def _():
            for cp in fetch(nxt):
                cp.start()

    def ffn(rows):
        tiles = pl.ds(0, rows * SUBLANES)
        a, b = _unpack_pair(_load_row_tiles(xs_ref.at[tiles], rows))
        x = jnp.concatenate([a.astype(bf16), b.astype(bf16)], axis=1)
        g = _dot(x, wg_b[...])
        u = _dot(x, wu_b[...])
        y = _dot((_silu(g) * u).astype(bf16), wd_b[...])
        _store_row_tiles(y_ref.at[tiles], _pack_pair(y.astype(bf16)))

    half_full = hf_ref[i] == 1

    @pl.when(used & jnp.logical_not(half_full))
    def _():
        ffn(tb)

    @pl.when(used & half_full)
    def _():
        ffn(tb // 2)
        rest = pl.ds(tb // 2 * SUBLANES, tb // 2 * SUBLANES)
        y_ref[rest, :] = jnp.zeros((tb // 2 * SUBLANES, LANES), y_ref.dtype)

    @pl.when(jnp.logical_not(used))
    def _():
        y_ref[...] = jnp.zeros(y_ref.shape, y_ref.dtype)


def _experts(block_e, next_e, half_full, n_used, xs, w_eg, w_eu, w_ed, *, tb):
    n_blocks = block_e.shape[0]
    blk = (tb * SUBLANES, LANES)
    up, down = (D_MODEL, EXPERT_HIDDEN), (EXPERT_HIDDEN, D_MODEL)
    hbm = pl.BlockSpec(memory_space=pl.ANY)
    return pl.pallas_call(
        functools.partial(_expert_kernel, tb=tb),
        out_shape=jax.ShapeDtypeStruct(xs.shape, u32),
        grid_spec=pltpu.PrefetchScalarGridSpec(
            num_scalar_prefetch=4,
            grid=(n_blocks,),
            in_specs=[hbm,
                      hbm, hbm, hbm],
            out_specs=pl.BlockSpec(blk, lambda i, be, ne, hf, nu: (i, 0)),
            scratch_shapes=[pltpu.VMEM(up, f32), pltpu.VMEM(up, f32), pltpu.VMEM(down, f32),
                            pltpu.VMEM(up, bf16), pltpu.VMEM(up, bf16), pltpu.VMEM(down, bf16),
                            pltpu.SemaphoreType.DMA((3,)),
                            pltpu.VMEM((XS_RING,) + blk, u32), pltpu.SemaphoreType.DMA((XS_RING,))]),
        compiler_params=pltpu.CompilerParams(dimension_semantics=("arbitrary",),
                                             vmem_limit_bytes=VMEM_LIMIT),
        name="experts",
    )(block_e, next_e, half_full, n_used, xs, w_eg, w_eu, w_ed)


def _combine_kernel(pos_ref, yb_hbm, x1_ref, route_ref, gt_ref, gf_ref, y_ref, ybuf, sem, *, tm):
    i = pl.program_id(0)
    slot = i % 2

    def start(blk, s):
        def body(r, carry):
            for k in range(TOP_K):
                t = pos_ref[(k * pl.num_programs(0) + blk) * tm + r]
                pltpu.make_async_copy(_row_tile(yb_hbm, t), _row_tile(ybuf.at[s, k], r),
                                      sem.at[s]).start(priority=k)
            return carry
        lax.fori_loop(0, tm, body, 0, unroll=8)

    @pl.when(i == 0)
    def _():
        start(0, 0)

    @pl.when(i + 1 < pl.num_programs(0))
    def _():
        start(i + 1, 1 - slot)

    for k in range(TOP_K):
        pltpu.make_async_copy(yb_hbm.at[pl.ds(0, tm * SUBLANES), :], ybuf.at[slot, k], sem.at[slot]).wait()
    route = route_ref[...]
    a0, b0 = _unpack_pair(_load_row_tiles(ybuf.at[slot, 0], tm))
    a1, b1 = _unpack_pair(_load_row_tiles(ybuf.at[slot, 1], tm))
    g0 = route[:, ROUTE_G0:ROUTE_G0 + 1]
    g1 = route[:, ROUTE_G1:ROUTE_G1 + 1]
    moe = jnp.concatenate([a0 * g0 + a1 * g1, b0 * g0 + b1 * g1], axis=1)
    x2 = x1_ref[...] + _rows(gt_ref) * moe
    y_ref[...] = _rms(x2) * gf_ref[...]


def _combine(pos_km, yb, x1, route, gt, g_final, *, tm, n, row0, gt_spec):
    blk0 = row0 // tm
    return pl.pallas_call(
        functools.partial(_combine_kernel, tm=tm),
        out_shape=jax.ShapeDtypeStruct((n, D_MODEL), f32),
        grid_spec=pltpu.PrefetchScalarGridSpec(
            num_scalar_prefetch=1,
            grid=(n // tm,),
            in_specs=[pl.BlockSpec(memory_space=pl.ANY),
                      pl.BlockSpec((tm, D_MODEL), lambda i, pos: (blk0 + i, 0)),
                      pl.BlockSpec((tm, LANES), lambda i, pos: (blk0 + i, 0)),
                      gt_spec,
                      pl.BlockSpec((1, D_MODEL), lambda i, pos: (0, 0))],
            out_specs=pl.BlockSpec((tm, D_MODEL), lambda i, pos: (i, 0)),
            scratch_shapes=[pltpu.VMEM((2, TOP_K, tm * SUBLANES, LANES), u32),
                            pltpu.SemaphoreType.DMA((2,))]),
        compiler_params=pltpu.CompilerParams(dimension_semantics=("arbitrary",),
                                             vmem_limit_bytes=VMEM_LIMIT),
        name="combine",
    )(pos_km, yb, x1, route, gt, g_final.reshape(1, D_MODEL))


def kernel(x_prompt, x_sample, cache_swa_k, cache_swa_v, state_gla, c_prompt, c_sample, g_mix_norm, g_ffn_norm, w_ada, b_ada, w_in, attn_sinks, w_gla_gate, b_gla_gate, g_gla_norm, w_out, w_router_group, b_router_group, w_router_expert, b_router_expert, w_expert_gate, w_expert_up, w_expert_down, g_final):
    depth = w_in.shape[0]
    assert depth == 1
    bp, t, d = x_prompt.shape
    bs, ts, _ = x_sample.shape
    n_p, n_s = bp * t, bs * ts
    n_tot = n_p + n_s
    assert n_tot % LANES == 0
    tm = 512
    to = 512
    tb = 512
    tc = 256
    td = LANES
    gla_c = 128
    gla_sub = 16
    swa_sub = 8

    w_in0 = w_in[0]
    w_rest = w_in0.astype(bf16)
    w_q = w_rest[:, :_C_K]
    w_q_t = w_q.T
    w_ab = jnp.pad(w_in0[:, _C_AB:], ((0, 0), (0, LANES - GLA_GATE_RANK))).astype(bf16)
    w_gate = jnp.pad(w_gla_gate[0], ((0, LANES - GLA_GATE_RANK), (0, 0))).astype(bf16)
    w_out_b = w_out[0].astype(bf16)
    n_r = N_GROUPS + N_EXPERTS
    w_r = jnp.pad(jnp.concatenate([w_router_group[0], w_router_expert[0]], axis=1),
                  ((0, 0), (0, LANES - n_r)))
    w_route = w_r.astype(bf16)
    b_route = jnp.pad(jnp.concatenate([b_router_group[0], b_router_expert[0]]),
                      (0, LANES - n_r)).reshape(1, LANES)

    c_all = jnp.concatenate([c_prompt, c_sample], axis=0)
    mod = _adaln(c_all, w_ada[0], b_ada[0])
    mod_tab = mod.reshape((bp + bs) * N_MOD, 1, d)
    mod_s = jnp.repeat(mod[bp:].reshape(bs, N_MOD, d), ts, axis=0)
    mod_s = [mod_s[:, m] for m in range(N_MOD)]
    pmod = lambda comp, tile, lag=0: _mod_spec(comp, t // tile, n_p // tile, lag)
    smod = pl.BlockSpec((n_s, d), lambda i: (0, 0))

    xp = x_prompt.reshape(n_p, d)
    xs = x_sample.reshape(n_s, d)
    proj_w = (w_rest, w_ab, w_gate, b_gla_gate[0])
    qp, kp, vp, gqp, gkp, gvp, rp, lgp = _proj(xp, mod_tab, mod_tab, g_mix_norm[0], w_q_t, *proj_w, tm=tm,
                                               mod_specs=(pmod(0, tm), pmod(1, tm)), q_transposed=True)
    qs, ks, vs, gqs, gks, gvs, rs, lgs = _proj(xs, mod_s[0], mod_s[1], g_mix_norm[0], w_q, *proj_w, tm=n_s,
                                               mod_specs=(smod, smod), q_transposed=False)

    sinks = attn_sinks[0]
    oap = _swa_t(qp, kp, vp, sinks, n_seq=bp, tiles=t // WINDOW, n_sub=swa_sub)
    ck = cache_swa_k[0].reshape(bs * WINDOW, KV_WIDTH)
    cv = cache_swa_v[0].reshape(bs * WINDOW, KV_WIDTH)
    oas = _swa(qs, ck, cv, ks, vs, sinks, n_seq=bs, tq=ts)
    s_zero = jnp.zeros((bp, GLA_HEADS, GLA_DK, GLA_DV), f32)
    obp, sp = _gla(gqp, gkp, gvp, lgp, rp, g_gla_norm[0], s_zero, n_seq=bp, c=gla_c, n_sub=gla_sub,
                   steps=t // (gla_c * gla_sub))
    obs, ss = _gla(gqs, gks, gvs, lgs, rs, g_gla_norm[0], state_gla[0], n_seq=bs, c=ts, n_sub=1, steps=1)

    out_w = (g_ffn_norm[0], w_out_b, w_route, b_route)
    bufs = _outproj(oap, obp, xp, mod_tab, mod_tab, mod_tab, *out_w, tm=to,
                    mod_specs=(pmod(2, to), pmod(3, to, 1), pmod(4, to, 1)), n_total=n_tot, row0=0,
                    fill_steps=-(-n_s // to), oa_transposed=True)
    x1, h2p, route = _outproj(oas, obs, xs, mod_s[2], mod_s[3], mod_s[4], *out_w, tm=n_s,
                              mod_specs=(smod, smod, smod), n_total=n_tot, row0=n_p, bufs=bufs)

    n_blocks = -(-(n_tot * TOP_K + N_EXPERTS * (tb - 1)) // tb)
    pos, pend_tab = _rank(route, tb=tb)
    pend, last_rows = pend_tab[0, :, 0], pend_tab[1, :, 0]
    block_row0 = jnp.arange(n_blocks, dtype=jnp.int32) * tb
    block_e = jnp.minimum(jnp.sum(pend[None, :] <= block_row0[:, None], axis=1), N_EXPERTS - 1).astype(jnp.int32)
    n_used = pend[N_EXPERTS - 1:] // tb
    experts = jnp.arange(N_EXPERTS, dtype=jnp.int32)
    has_rows = jnp.diff(pend, prepend=0) > 0
    later = jnp.where((experts[None, :] > experts[:, None]) & has_rows[None, :], experts[None, :], N_EXPERTS)
    next_with_rows = jnp.min(later, axis=1)
    next_with_rows = jnp.where(next_with_rows == N_EXPERTS, experts, next_with_rows)
    of_block = lambda per_expert: jnp.sum(jnp.where(block_e[:, None] == experts[None, :], per_expert[None, :], 0),
                                          axis=1).astype(jnp.int32)
    next_e = of_block(next_with_rows)
    half_full = ((block_row0 + tb == of_block(pend)) & (of_block(last_rows) <= tb // 2)).astype(jnp.int32)
    xsort = _dispatch(pos.reshape(-1), pend, h2p, tm=td, tb=tb, n_blocks=n_blocks)
    yb = _experts(block_e, next_e, half_full, n_used, xsort, w_expert_gate[0], w_expert_up[0],
                  w_expert_down[0], tb=tb)

    pos = pos.reshape(TOP_K, n_tot)
    pos_p = pos[:, :n_p].reshape(-1)
    pos_s = pos[:, n_p:].reshape(-1)
    gt_p = pl.BlockSpec((1, 1, d), lambda i, p: ((i // (t // tc)) * N_MOD + 5, 0, 0))
    gt_s = pl.BlockSpec((n_s, d), lambda i, p: (0, 0))
    y_p = _combine(pos_p, yb, x1, route, mod_tab, g_final, tm=tc, n=n_p, row0=0, gt_spec=gt_p)
    y_s = _combine(pos_s, yb, x1, route, mod_s[5], g_final, tm=n_s, n=n_s, row0=n_p, gt_spec=gt_s)

    kv_shape = (SWA_KV_HEADS, SWA_HEAD_DIM)
    last = lambda a: a.reshape(bp, t, KV_WIDTH)[:, t - WINDOW:, :].reshape(1, bp, WINDOW, *kv_shape)
    k_state_p, v_state_p = last(kp), last(vp)
    return (y_p.reshape(bp, t, d), y_s.reshape(bs, ts, d), k_state_p, v_state_p, sp[None],
            ks.reshape(bs, ts, *kv_shape)[None], vs.reshape(bs, ts, *kv_shape)[None], ss[None])
```
